```python
import math
import jax, jax.numpy as jnp
from jax import lax
import numpy as np

D_MODEL = 2048
BATCH = 4
SEQ = 2048
DEPTH = 2
DEC_BATCH = 128
DEC_SEQ = 1
PAST_LEN = 16384
PAGE_SIZE = 128

MIX_GROUP = D_MODEL // 4
D_FF = 4 * D_MODEL
EPS = 1e-6
CONV_W = 4

GLA_HEADS = 4
GLA_DV = MIX_GROUP // GLA_HEADS
GLA_DK = GLA_DV // 2
GLA_RANK = 16
GLA_GATE_TEMP = 16.0
GLA_CHUNK = 16

LRU_WIDTH = MIX_GROUP
LRU_BLOCKS = 8
LRU_BLOCK = LRU_WIDTH // LRU_BLOCKS
LRU_C = 8.0

RWKV_HEAD = 64
RWKV_HEADS = MIX_GROUP // RWKV_HEAD
RWKV_W_RANK = 64
RWKV_A_RANK = 64
RWKV_G_RANK = 128
RWKV_DECAY_SCALE = math.exp(-0.5)
RWKV_LN_EPS = 64e-5

SSD_HEADDIM = 64
SSD_HEADS = MIX_GROUP // SSD_HEADDIM
SSD_GROUPS = 2
SSD_STATE = 128
SSD_CHUNK = 64
SSD_CONV_DIM = MIX_GROUP + 2 * SSD_GROUPS * SSD_STATE

GLA_COLS = 2 * GLA_HEADS * GLA_DK + 2 * MIX_GROUP + GLA_RANK
LRU_COLS = 2 * LRU_WIDTH
RWKV_COLS = 3 * MIX_GROUP + RWKV_W_RANK + RWKV_A_RANK + RWKV_G_RANK
SSD_COLS = MIX_GROUP + SSD_CONV_DIM + SSD_HEADS
PROJ_COLS = GLA_COLS + LRU_COLS + RWKV_COLS + SSD_COLS
SPLIT_IDX = (GLA_COLS, GLA_COLS + LRU_COLS, GLA_COLS + LRU_COLS + RWKV_COLS)

kernel_name = 'hybrid_gla_hawk_rwkv7_ssd_step'

F32 = jnp.float32


def rmsnorm(x, w):
    xf = x.astype(F32)
    y = xf * lax.rsqrt(jnp.mean(xf * xf, axis=-1, keepdims=True) + EPS)
    return (y * w.astype(F32)).astype(x.dtype)


def causal_conv(x, buf, w, b):
    T = x.shape[1]
    xp = jnp.concatenate([buf.astype(x.dtype), x], axis=1)
    y = b
    for i in range(CONV_W):
        y = y + xp[:, i:i + T] * w[i]
    return y, xp[:, T:]


def gla_chunked(q, k, v, g, s0):
    B, T, H, K = q.shape
    V = v.shape[-1]
    C = math.gcd(T, GLA_CHUNK)
    N = T // C
    q, k, g = (t.reshape(B, N, C, H, K) for t in (q, k, g))
    v = v.reshape(B, N, C, H, V)
    cum = jnp.cumsum(g, axis=2)
    mask = jnp.tril(jnp.ones((C, C), bool))[:, :, None, None]
    diff = cum[:, :, :, None] - cum[:, :, None, :]
    decay = jnp.exp(jnp.where(mask, diff, -jnp.inf))
    scores = jnp.einsum('bnlhk,bnshk,bnlshk->bnhls', q, k, decay)
    y_intra = jnp.einsum('bnhls,bnshv->bnlhv', scores, v)
    last = cum[:, :, -1]
    k_dec = k * jnp.exp(last[:, :, None] - cum)
    chunk_states = jnp.einsum('bnshk,bnshv->bnhkv', k_dec, v)

    def step(s, inp):
        dec, cs = inp
        return s * dec[..., None] + cs, s

    s_final, s_prev = lax.scan(step, s0.astype(F32),
                               (jnp.moveaxis(jnp.exp(last), 1, 0), jnp.moveaxis(chunk_states, 1, 0)))
    s_prev = jnp.moveaxis(s_prev, 0, 1)
    y_inter = jnp.einsum('bnlhk,bnhkv->bnlhv', q * jnp.exp(cum), s_prev)
    return (y_intra + y_inter).reshape(B, T, H, V), s_final


def rg_lru(x, gate_a, gate_i, lam, h0, fresh):
    log_a = -LRU_C * gate_a * jax.nn.softplus(-lam)
    a = jnp.exp(log_a)
    mult = jnp.sqrt(-jnp.expm1(2.0 * log_a))
    if fresh:
        mult = mult.at[:, 0].set(1.0)
    b = mult * gate_i * x
    b = b.at[:, 0].add(a[:, 0] * h0.astype(F32))

    def combine(left, right):
        a1, b1 = left
        a2, b2 = right
        return a1 * a2, a2 * b1 + b2

    _, h = lax.associative_scan(combine, (a, b), axis=1)
    return h, h[:, -1]


def rwkv7_scan(r, w, k, v, kk, a, s0):
    def step(s, inp):
        r_t, w_t, k_t, v_t, kk_t, a_t = inp
        sa = jnp.einsum('bhvk,bhk->bhv', s, kk_t)
        s = (s * w_t[:, :, None, :] - sa[..., None] * (kk_t * a_t)[:, :, None, :]
             + v_t[..., None] * k_t[:, :, None, :])
        return s, jnp.einsum('bhvk,bhk->bhv', s, r_t)

    xs = tuple(jnp.moveaxis(t, 1, 0) for t in (r, w, k, v, kk, a))
    s_final, y = lax.scan(step, s0.astype(F32), xs)
    return jnp.moveaxis(y, 0, 1), s_final


def ssd_chunked(x, dt, a_neg, bm, cm, s0):
    B, T, H, P = x.shape
    N = bm.shape[-1]
    C = math.gcd(T, SSD_CHUNK)
    NC = T // C
    rep = H // bm.shape[2]
    bh = jnp.repeat(bm, rep, axis=2).reshape(B, NC, C, H, N)
    ch = jnp.repeat(cm, rep, axis=2).reshape(B, NC, C, H, N)
    xdt = (x * dt[..., None]).reshape(B, NC, C, H, P)
    cum = jnp.cumsum((dt * a_neg).reshape(B, NC, C, H), axis=2)
    mask = jnp.tril(jnp.ones((C, C), bool))[:, :, None]
    seg = cum[:, :, :, None, :] - cum[:, :, None, :, :]
    Lmat = jnp.exp(jnp.where(mask, seg, -jnp.inf))
    scores = jnp.einsum('bclhn,bcshn->bclsh', ch, bh) * Lmat
    y_diag = jnp.einsum('bclsh,bcshp->bclhp', scores, xdt)
    last = cum[:, :, -1]
    states = jnp.einsum('bcshn,bcsh,bcshp->bchpn', bh, jnp.exp(last[:, :, None] - cum), xdt)

    def step(s, inp):
        dec, cs = inp
        return s * dec[..., None, None] + cs, s

    s_final, s_prev = lax.scan(step, s0.astype(F32),
                               (jnp.moveaxis(jnp.exp(last), 1, 0), jnp.moveaxis(states, 1, 0)))
    s_prev = jnp.moveaxis(s_prev, 0, 1)
    y_off = jnp.einsum('bclhn,bchpn,bclh->bclhp', ch, s_prev, jnp.exp(cum))
    return (y_diag + y_off).reshape(B, T, H, P), s_final


def gla_mixer(z, s0, w_gate2, b_gate, norm_w):
    B, T, _ = z.shape
    qk = GLA_HEADS * GLA_DK
    q, k, v, gate, a_down = jnp.split(z.astype(F32), [qk, 2 * qk, 2 * qk + MIX_GROUP, 2 * qk + 2 * MIX_GROUP], axis=-1)
    log_alpha = jax.nn.log_sigmoid(jnp.einsum('btr,rc->btc', a_down, w_gate2) + b_gate) / GLA_GATE_TEMP
    q = q.reshape(B, T, GLA_HEADS, GLA_DK) * (GLA_DK ** -0.5)
    k = k.reshape(B, T, GLA_HEADS, GLA_DK)
    v = v.reshape(B, T, GLA_HEADS, GLA_DV)
    log_alpha = log_alpha.reshape(B, T, GLA_HEADS, GLA_DK)
    o, s = gla_chunked(q, k, v, log_alpha, s0)
    o = rmsnorm(o, norm_w).reshape(B, T, MIX_GROUP)
    return (o * jax.nn.silu(gate)).astype(z.dtype), s


def hawk_mixer(z, h0, conv_buf, conv_w, conv_b, w_a, b_a, w_i, b_i, lam, fresh):
    B, T, _ = z.shape
    xb, gb = jnp.split(z, 2, axis=-1)
    xc, new_buf = causal_conv(xb, conv_buf, conv_w, conv_b)
    xc = xc.astype(F32)
    blk = xc.reshape(B, T, LRU_BLOCKS, LRU_BLOCK)
    r = jax.nn.sigmoid(jnp.einsum('btgi,gij->btgj', blk, w_a).reshape(B, T, LRU_WIDTH) + b_a)
    i = jax.nn.sigmoid(jnp.einsum('btgi,gij->btgj', blk, w_i).reshape(B, T, LRU_WIDTH) + b_i)
    h, h_last = rg_lru(xc, r, i, lam, h0, fresh)
    y = h * jax.nn.gelu(gb.astype(F32))
    return y.astype(z.dtype), h_last, new_buf


def rwkv_mixer(z, prev, s0, mu, w0, w2, a0, a2, g2, k_k, k_a, r_k, ln_w, ln_b):
    B, T, _ = z.shape
    zf = z.astype(F32)
    shifted = jnp.concatenate([prev[:, None].astype(F32), zf[:, :-1]], axis=1)
    zm = zf + (shifted - zf) * mu
    r, k, v, zw, za, zg = jnp.split(zm, [MIX_GROUP, 2 * MIX_GROUP, 3 * MIX_GROUP,
                                         3 * MIX_GROUP + RWKV_W_RANK,
                                         3 * MIX_GROUP + RWKV_W_RANK + RWKV_A_RANK], axis=-1)
    w = jnp.exp(-RWKV_DECAY_SCALE * jax.nn.sigmoid(w0 + jnp.einsum('btr,rc->btc', jnp.tanh(zw), w2)))
    a = jax.nn.sigmoid(a0 + jnp.einsum('btr,rc->btc', za, a2))
    g = jnp.einsum('btr,rc->btc', jax.nn.sigmoid(zg), g2)
    heads = lambda t: t.reshape(B, T, RWKV_HEADS, RWKV_HEAD)
    kk = heads(k * k_k)
    kk = kk / jnp.maximum(jnp.sqrt(jnp.sum(kk * kk, axis=-1, keepdims=True)), 1e-12)
    k = k * (1.0 + (a - 1.0) * k_a)
    rh, kh, vh, wh, ah = heads(r), heads(k), heads(v), heads(w), heads(a)
    y, s = rwkv7_scan(rh, wh, kh, vh, kk, ah, s0)
    mean = jnp.mean(y, axis=-1, keepdims=True)
    var = jnp.mean(jnp.square(y - mean), axis=-1, keepdims=True)
    yn = ((y - mean) * lax.rsqrt(var + RWKV_LN_EPS)).reshape(B, T, MIX_GROUP) * ln_w + ln_b
    bonus = (jnp.sum(rh * kh * r_k, axis=-1, keepdims=True) * vh).reshape(B, T, MIX_GROUP)
    out = (yn + bonus) * g
    return out.astype(z.dtype), s, z[:, -1]


def ssd_mixer(z, conv_buf, s0, conv_w, conv_b, dt_bias, a_log, d_skip, norm_w):
    B, T, _ = z.shape
    gate, xbc, dt_raw = jnp.split(z, [MIX_GROUP, MIX_GROUP + SSD_CONV_DIM], axis=-1)
    xbc_c, new_buf = causal_conv(xbc, conv_buf, conv_w, conv_b)
    xbc_c = jax.nn.silu(xbc_c.astype(F32))
    xs, bm, cm = jnp.split(xbc_c, [MIX_GROUP, MIX_GROUP + SSD_GROUPS * SSD_STATE], axis=-1)
    dt = jax.nn.softplus(dt_raw.astype(F32) + dt_bias)
    a_neg = -jnp.exp(a_log.astype(F32))
    xh = xs.reshape(B, T, SSD_HEADS, SSD_HEADDIM)
    y, s = ssd_chunked(xh, dt, a_neg, bm.reshape(B, T, SSD_GROUPS, SSD_STATE),
                       cm.reshape(B, T, SSD_GROUPS, SSD_STATE), s0)
    y = y + d_skip[:, None] * xh
    y = y.reshape(B, T, MIX_GROUP) * jax.nn.silu(gate.astype(F32))
    return rmsnorm(y, norm_w).astype(z.dtype), s, new_buf


def hybrid_layer(x, states, p, fresh):
    gla_s, lru_h, lru_buf, rwkv_prev, rwkv_s, ssd_buf, ssd_s = states
    h = rmsnorm(x, p['norm_mix_pre'])
    proj = jnp.einsum('btd,dc->btc', h, p['w_in'])
    z_gla, z_lru, z_rwkv, z_ssd = jnp.split(proj, SPLIT_IDX, axis=-1)
    o_gla, gla_s = gla_mixer(z_gla, gla_s, p['gla_w_gate2'], p['gla_b_gate'], p['gla_norm'])
    o_lru, lru_h, lru_buf = hawk_mixer(z_lru, lru_h, lru_buf, p['lru_conv_w'], p['lru_conv_b'],
                                       p['lru_w_a'], p['lru_b_a'], p['lru_w_i'], p['lru_b_i'],
                                       p['lru_lambda'], fresh)
    o_rwkv, rwkv_s, rwkv_prev = rwkv_mixer(z_rwkv, rwkv_prev, rwkv_s, p['rwkv_mu'], p['rwkv_w0'],
                                           p['rwkv_w2'], p['rwkv_a0'], p['rwkv_a2'], p['rwkv_g2'],
                                           p['rwkv_k_k'], p['rwkv_k_a'], p['rwkv_r_k'],
                                           p['rwkv_ln_w'], p['rwkv_ln_b'])
    o_ssd, ssd_s, ssd_buf = ssd_mixer(z_ssd, ssd_buf, ssd_s, p['ssd_conv_w'], p['ssd_conv_b'],
                                      p['ssd_dt_bias'], p['ssd_a_log'], p['ssd_d'], p['ssd_norm'])
    mix = jnp.concatenate([o_gla, o_lru, o_rwkv, o_ssd], axis=-1)
    x = x + rmsnorm(jnp.einsum('btm,md->btd', mix, p['w_out']), p['norm_mix_post'])
    h = rmsnorm(x, p['norm_mlp_pre'])
    u = jnp.square(jax.nn.relu(jnp.einsum('btd,df->btf', h, p['w_up'])))
    x = x + rmsnorm(jnp.einsum('btf,fd->btd', u, p['w_down']), p['norm_mlp_post'])
    return x, (gla_s, lru_h, lru_buf, rwkv_prev, rwkv_s, ssd_buf, ssd_s)


def zero_states(b, dtype):
    return (jnp.zeros((b, GLA_HEADS, GLA_DK, GLA_DV), F32),
            jnp.zeros((b, LRU_WIDTH), F32),
            jnp.zeros((b, CONV_W - 1, LRU_WIDTH), dtype),
            jnp.zeros((b, RWKV_COLS), dtype),
            jnp.zeros((b, RWKV_HEADS, RWKV_HEAD, RWKV_HEAD), F32),
            jnp.zeros((b, CONV_W - 1, SSD_CONV_DIM), dtype),
            jnp.zeros((b, SSD_HEADS, SSD_HEADDIM, SSD_STATE), F32))


def setup_inputs(seed: int = 0) -> dict:
    key = jax.random.key(seed)
    ks = iter(jax.random.split(key, 64))
    L = DEPTH

    def nrm(shape, s):
        return s * jax.random.normal(next(ks), shape, F32)

    def gain(shape):
        return 1.0 + nrm(shape, 0.05)

    lam_u = jax.random.uniform(next(ks), (L, LRU_WIDTH), F32, minval=0.9, maxval=0.999)
    dt0 = jnp.exp(jax.random.uniform(next(ks), (L, SSD_HEADS), F32,
                                     minval=math.log(1e-3), maxval=math.log(1e-1)))
    return {
        'x_prompt': nrm((BATCH, SEQ, D_MODEL), 1.0),
        'x_sample': nrm((DEC_BATCH, DEC_SEQ, D_MODEL), 1.0),
        'state_gla': nrm((L, DEC_BATCH, GLA_HEADS, GLA_DK, GLA_DV), 0.5),
        'state_lru': nrm((L, DEC_BATCH, LRU_WIDTH), 0.5),
        'cache_lru_conv': nrm((L, DEC_BATCH, CONV_W - 1, LRU_WIDTH), 1.0),
        'cache_rwkv_shift': nrm((L, DEC_BATCH, RWKV_COLS), 1.0),
        'state_rwkv': nrm((L, DEC_BATCH, RWKV_HEADS, RWKV_HEAD, RWKV_HEAD), 0.3),
        'cache_ssd_conv': nrm((L, DEC_BATCH, CONV_W - 1, SSD_CONV_DIM), 1.0),
        'state_ssd': nrm((L, DEC_BATCH, SSD_HEADS, SSD_HEADDIM, SSD_STATE), 0.1),
        'norm_mix_pre': gain((L, D_MODEL)),
        'norm_mix_post': gain((L, D_MODEL)),
        'norm_mlp_pre': gain((L, D_MODEL)),
        'norm_mlp_post': gain((L, D_MODEL)),
        'w_in': nrm((L, D_MODEL, PROJ_COLS), D_MODEL ** -0.5),
        'w_out': nrm((L, D_MODEL, D_MODEL), D_MODEL ** -0.5),
        'w_up': nrm((L, D_MODEL, D_FF), D_MODEL ** -0.5),
        'w_down': nrm((L, D_FF, D_MODEL), D_FF ** -0.5),
        'gla_w_gate2': nrm((L, GLA_RANK, GLA_HEADS * GLA_DK), GLA_RANK ** -0.5),
        'gla_b_gate': nrm((L, GLA_HEADS * GLA_DK), 0.1),
        'gla_norm': gain((L, GLA_DV)),
        'lru_conv_w': nrm((L, CONV_W, LRU_WIDTH), 0.5),
        'lru_conv_b': nrm((L, LRU_WIDTH), 0.01),
        'lru_w_a': nrm((L, LRU_BLOCKS, LRU_BLOCK, LRU_BLOCK), LRU_BLOCK ** -0.5),
        'lru_b_a': nrm((L, LRU_WIDTH), 0.01),
        'lru_w_i': nrm((L, LRU_BLOCKS, LRU_BLOCK, LRU_BLOCK), LRU_BLOCK ** -0.5),
        'lru_b_i': nrm((L, LRU_WIDTH), 0.01),
        'lru_lambda': jnp.log(lam_u) - jnp.log1p(-lam_u),
        'rwkv_mu': jax.random.uniform(next(ks), (L, RWKV_COLS), F32),
        'rwkv_w0': nrm((L, MIX_GROUP), 1.0),
        'rwkv_w2': nrm((L, RWKV_W_RANK, MIX_GROUP), 0.1),
        'rwkv_a0': nrm((L, MIX_GROUP), 0.1),
        'rwkv_a2': nrm((L, RWKV_A_RANK, MIX_GROUP), 0.1),
        'rwkv_g2': nrm((L, RWKV_G_RANK, MIX_GROUP), RWKV_G_RANK ** -0.5),
        'rwkv_k_k': 0.85 + nrm((L, MIX_GROUP), 0.05),
        'rwkv_k_a': 1.0 + nrm((L, MIX_GROUP), 0.1),
        'rwkv_r_k': nrm((L, RWKV_HEADS, RWKV_HEAD), 0.1),
        'rwkv_ln_w': gain((L, MIX_GROUP)),
        'rwkv_ln_b': nrm((L, MIX_GROUP), 0.01),
        'ssd_conv_w': nrm((L, CONV_W, SSD_CONV_DIM), 0.5),
        'ssd_conv_b': nrm((L, SSD_CONV_DIM), 0.01),
        'ssd_dt_bias': dt0 + jnp.log(-jnp.expm1(-dt0)),
        'ssd_a_log': jnp.log(jax.random.uniform(next(ks), (L, SSD_HEADS), F32, minval=1.0, maxval=16.0)),
        'ssd_d': 1.0 + nrm((L, SSD_HEADS), 0.1),
        'ssd_norm': gain((L, MIX_GROUP)),
    }


def reference(x_prompt, x_sample, state_gla, state_lru, cache_lru_conv, cache_rwkv_shift, state_rwkv,
              cache_ssd_conv, state_ssd, norm_mix_pre, norm_mix_post, norm_mlp_pre, norm_mlp_post,
              w_in, w_out, w_up, w_down, gla_w_gate2, gla_b_gate, gla_norm,
              lru_conv_w, lru_conv_b, lru_w_a, lru_b_a, lru_w_i, lru_b_i, lru_lambda,
              rwkv_mu, rwkv_w0, rwkv_w2, rwkv_a0, rwkv_a2, rwkv_g2, rwkv_k_k, rwkv_k_a, rwkv_r_k,
              rwkv_ln_w, rwkv_ln_b, ssd_conv_w, ssd_conv_b, ssd_dt_bias, ssd_a_log, ssd_d, ssd_norm):
    y_p, y_s = x_prompt, x_sample
    new_p, new_s = [], []
    for l in range(DEPTH):
        p = {'norm_mix_pre': norm_mix_pre[l], 'norm_mix_post': norm_mix_post[l],
             'norm_mlp_pre': norm_mlp_pre[l], 'norm_mlp_post': norm_mlp_post[l],
             'w_in': w_in[l], 'w_out': w_out[l], 'w_up': w_up[l], 'w_down': w_down[l],
             'gla_w_gate2': gla_w_gate2[l], 'gla_b_gate': gla_b_gate[l], 'gla_norm': gla_norm[l],
             'lru_conv_w': lru_conv_w[l], 'lru_conv_b': lru_conv_b[l], 'lru_w_a': lru_w_a[l],
             'lru_b_a': lru_b_a[l], 'lru_w_i': lru_w_i[l], 'lru_b_i': lru_b_i[l], 'lru_lambda': lru_lambda[l],
             'rwkv_mu': rwkv_mu[l], 'rwkv_w0': rwkv_w0[l], 'rwkv_w2': rwkv_w2[l], 'rwkv_a0': rwkv_a0[l],
             'rwkv_a2': rwkv_a2[l], 'rwkv_g2': rwkv_g2[l], 'rwkv_k_k': rwkv_k_k[l], 'rwkv_k_a': rwkv_k_a[l],
             'rwkv_r_k': rwkv_r_k[l], 'rwkv_ln_w': rwkv_ln_w[l], 'rwkv_ln_b': rwkv_ln_b[l],
             'ssd_conv_w': ssd_conv_w[l], 'ssd_conv_b': ssd_conv_b[l], 'ssd_dt_bias': ssd_dt_bias[l],
             'ssd_a_log': ssd_a_log[l], 'ssd_d': ssd_d[l], 'ssd_norm': ssd_norm[l]}
        y_p, st_p = hybrid_layer(y_p, zero_states(y_p.shape[0], y_p.dtype), p, True)
        st_in = (state_gla[l], state_lru[l], cache_lru_conv[l], cache_rwkv_shift[l], state_rwkv[l],
                 cache_ssd_conv[l], state_ssd[l])
        y_s, st_s = hybrid_layer(y_s, st_in, p, False)
        new_p.append(st_p)
        new_s.append(st_s)

    def stack(sts, i):
        return jnp.stack([s[i] for s in sts], axis=0)

    return (y_p, y_s,
            stack(new_p, 0), stack(new_s, 0),
            stack(new_p, 1), stack(new_s, 1),
            stack(new_p, 2), stack(new_s, 2),
            stack(new_p, 3), stack(new_s, 3),
            stack(new_p, 4), stack(new_s, 4),
            stack(new_p, 5), stack(new_s, 5),
            stack(new_p, 6), stack(new_s, 6))
```

```python
import functools
import math

import jax
import jax.numpy as jnp
from jax import lax
from jax.experimental import pallas as pl
from jax.experimental.pallas import tpu as pltpu

F32 = jnp.float32
BF16 = jnp.bfloat16

D_MODEL = 2048
D_FF = 4 * D_MODEL
DEPTH = 2
EPS = 1e-6
MIX = D_MODEL // 4
CONV_W = 4
GLA_H, GLA_DK, GLA_DV, GLA_RANK, GLA_TEMP, GLA_C = 4, 64, 128, 16, 16.0, 16
LRU_C = 8.0
RWKV_H, RWKV_D = 8, 64
RWKV_DECAY = math.exp(-0.5)
RWKV_LN_EPS = 64e-5
RWKV_C = 16
SSD_H, SSD_P, SSD_G, SSD_N, SSD_C = 8, 64, 2, 128, 64
GLA_COLS = 2 * GLA_H * GLA_DK + 2 * MIX + GLA_RANK
LRU_COLS = 2 * MIX
RWKV_COLS = 3 * MIX + 64 + 64 + 128
SSD_CONV_DIM = MIX + 2 * SSD_G * SSD_N
SSD_COLS = MIX + SSD_CONV_DIM + SSD_H

LANES = 128
SUBLANES = 8
TILE = 2 * LANES
GROUP_TILES = 7
T_GLA, T_RWKV, T_SSD, T_LRU = 0, 7, 14, 21
N_TILES = 25
VMEM_LIMIT = 56 * 1024 * 1024

NN = (((1,), (0,)), ((), ()))
NT = (((1,), (1,)), ((), ()))
TN = (((0,), (0,)), ((), ()))


def _split(x, n):
    if x.dtype == BF16:
        return [x]
    parts, r = [], x
    for i in range(n):
        p = r.astype(BF16)
        parts.append(p)
        if i + 1 < n:
            r = r - p.astype(F32)
    return parts


def _mm(a, b, dn=NN, pa=1, pb=1):
    aa, bb = _split(a, pa), _split(b, pb)
    acc = None
    for i, x in enumerate(aa):
        for j, y in enumerate(bb):
            if i + j >= max(len(aa), len(bb)):
                continue
            t = lax.dot_general(x, y, dn, preferred_element_type=F32)
            acc = t if acc is None else acc + t
    return acc


def _iota(shape, dim):
    return lax.broadcasted_iota(jnp.int32, shape, dim)


def _roll0(x, s):
    n = x.shape[0]
    s = s % n
    return x if s == 0 else pltpu.roll(x, s, 0)


def _rms(x, w):
    ms = jnp.mean(x * x, axis=-1, keepdims=True)
    return x * lax.rsqrt(ms + EPS) * w


def _sigmoid(x):
    return jax.nn.sigmoid(x)


def _silu(x):
    return x * jax.nn.sigmoid(x)


def _softplus(x):
    return jnp.maximum(x, 0.0) + jnp.log1p(jnp.exp(-jnp.abs(x)))


def _log_sigmoid(x):
    return jnp.minimum(x, 0.0) - jnp.log1p(jnp.exp(-jnp.abs(x)))


def _gelu_tanh(x):
    c = math.sqrt(2.0 / math.pi)
    return x * (0.5 * (1.0 + jnp.tanh(c * (x + 0.044715 * (x * x * x)))))


def _neg_expm1(x):
    return -jnp.tanh(0.5 * x) * (jnp.exp(x) + 1.0)


def _chunk_cumsum(x, chunk, pos):
    d = 1
    while d < chunk:
        x = x + jnp.where(pos >= d, _roll0(x, d), 0.0)
        d *= 2
    return x


def _chunk_last_bcast(x, chunk, pos):
    n = x.shape[0]
    y = jnp.where(pos == chunk - 1, x, 0.0)
    d = 1
    while d < chunk:
        y = y + _roll0(y, n - d)
        d *= 2
    return y


def _shift_rows(x, carry, i, row8):
    xs = _roll0(x, i)
    cs = _roll0(carry, i)
    top = jnp.where(row8 < i, cs, xs[:SUBLANES])
    return jnp.concatenate([top, xs[SUBLANES:]], axis=0)


def _causal_conv(x, carry, w_ref, b_ref):
    row8 = _iota((SUBLANES, x.shape[1]), 0)
    y = b_ref[...] + x * w_ref[CONV_W - 1:CONV_W, :]
    for i in range(1, CONV_W):
        y = y + _shift_rows(x, carry, i, row8) * w_ref[CONV_W - 1 - i:CONV_W - i, :]
    return y


def _params(*sem):
    return pltpu.CompilerParams(dimension_semantics=sem, vmem_limit_bytes=VMEM_LIMIT)


def _const_spec(shape):
    nd = len(shape)
    return pl.BlockSpec(shape, lambda *_: (0,) * nd)


PROJ_TILES_PER_STEP = 5


def _proj_kernel(x_ref, nw_ref, w_ref, o_ref, h_ref):
    @pl.when(pl.program_id(1) == 0)
    def _():
        h_ref[...] = _rms(x_ref[...], nw_ref[...]).astype(BF16)

    res = jnp.dot(h_ref[...], w_ref[...], preferred_element_type=F32)
    for j in range(PROJ_TILES_PER_STEP):
        o_ref[j] = res[:, j * TILE:(j + 1) * TILE]


def _proj(x2d, nw, w, tm):
    m = x2d.shape[0]
    tn = PROJ_TILES_PER_STEP * TILE
    return pl.pallas_call(
        _proj_kernel,
        grid=(m // tm, N_TILES // PROJ_TILES_PER_STEP),
        in_specs=[pl.BlockSpec((tm, D_MODEL), lambda i, j: (i, 0)),
                  _const_spec((1, D_MODEL)),
                  pl.BlockSpec((D_MODEL, tn), lambda i, j: (0, j))],
        out_specs=pl.BlockSpec((PROJ_TILES_PER_STEP, tm, TILE), lambda i, j: (j, i, 0)),
        out_shape=jax.ShapeDtypeStruct((N_TILES, m, TILE), F32),
        scratch_shapes=[pltpu.VMEM((tm, D_MODEL), BF16)],
        compiler_params=_params("parallel", "arbitrary"),
        name="norm_proj",
    )(x2d, nw, w)


def _outproj_kernel(m0, m1, m2, m3, w_ref, x_ref, n1_ref, n2_ref, x1_ref, h2_ref):
    acc = jnp.dot(m0[...], w_ref[0], preferred_element_type=F32)
    acc = acc + jnp.dot(m1[...], w_ref[1], preferred_element_type=F32)
    acc = acc + jnp.dot(m2[...], w_ref[2], preferred_element_type=F32)
    acc = acc + jnp.dot(m3[...], w_ref[3], preferred_element_type=F32)
    x1 = x_ref[...] + _rms(acc, n1_ref[...])
    x1_ref[...] = x1
    h2_ref[...] = _rms(x1, n2_ref[...]).astype(BF16)


def _outproj(mix, w_out, x2d, n1, n2, tm):
    m = x2d.shape[0]
    row = lambda i: (i, 0)
    return pl.pallas_call(
        _outproj_kernel,
        grid=(m // tm,),
        in_specs=[pl.BlockSpec((tm, MIX), row)] * 4 + [
            _const_spec((4, MIX, D_MODEL)), pl.BlockSpec((tm, D_MODEL), row),
            _const_spec((1, D_MODEL)), _const_spec((1, D_MODEL))],
        out_specs=[pl.BlockSpec((tm, D_MODEL), row), pl.BlockSpec((tm, D_MODEL), row)],
        out_shape=[jax.ShapeDtypeStruct((m, D_MODEL), F32), jax.ShapeDtypeStruct((m, D_MODEL), BF16)],
        compiler_params=_params("parallel"),
        name="out_proj",
    )(*mix, w_out, x2d, n1, n2)


def _mlp_kernel(h_ref, wu_ref, wd_ref, x1_ref, nw_ref, o_ref, acc_ref):
    f = pl.program_id(1)

    @pl.when(f == 0)
    def _():
        acc_ref[...] = jnp.zeros_like(acc_ref)

    u = jnp.maximum(jnp.dot(h_ref[...], wu_ref[...], preferred_element_type=F32), 0.0)
    acc_ref[...] += jnp.dot((u * u).astype(BF16), wd_ref[...], preferred_element_type=F32)

    @pl.when(f == pl.num_programs(1) - 1)
    def _():
        o_ref[...] = x1_ref[...] + _rms(acc_ref[...], nw_ref[...])


def _mlp(h2, w_up, w_down, x1, nw, tm, tf):
    m = h2.shape[0]
    return pl.pallas_call(
        _mlp_kernel,
        grid=(m // tm, D_FF // tf),
        in_specs=[pl.BlockSpec((tm, D_MODEL), lambda i, f: (i, 0)),
                  pl.BlockSpec((D_MODEL, tf), lambda i, f: (0, f)),
                  pl.BlockSpec((tf, D_MODEL), lambda i, f: (f, 0)),
                  pl.BlockSpec((tm, D_MODEL), lambda i, f: (i, 0)),
                  _const_spec((1, D_MODEL))],
        out_specs=pl.BlockSpec((tm, D_MODEL), lambda i, f: (i, 0)),
        out_shape=jax.ShapeDtypeStruct((m, D_MODEL), F32),
        scratch_shapes=[pltpu.VMEM((tm, D_MODEL), F32)],
        compiler_params=_params("parallel", "arbitrary"),
        name="mlp",
    )(h2, w_up, w_down, x1, nw)


GLA_TB = 128


def _gla_gates(ad, wg2_ref, bg_ref):
    x = _mm(ad, wg2_ref[...], pa=2) + bg_ref[...]
    return _log_sigmoid(x) * (1.0 / GLA_TEMP)


def _gla_kernel(z_ref, wg2_ref, bg_ref, nw_ref, ones_ref, o_ref, s_ref, st_ref, y_ref):
    tb = GLA_TB
    t = pl.program_id(1)

    @pl.when(t == 0)
    def _():
        st_ref[...] = jnp.zeros_like(st_ref)

    q = z_ref[0] * (GLA_DK ** -0.5)
    k = z_ref[1]
    v = jnp.concatenate([z_ref[2], z_ref[3]], axis=1)
    gate = jnp.concatenate([z_ref[4], z_ref[5]], axis=1)
    g = _gla_gates(z_ref[6], wg2_ref, bg_ref)
    pos = _iota((tb, TILE), 0) & (GLA_C - 1)
    cum = _chunk_cumsum(g, GLA_C, pos)

    y = None
    for j in range(GLA_C):
        if j == 0:
            term = q * k
            vj = v
        else:
            e = jnp.exp(jnp.where(pos >= j, cum - _roll0(cum, j), -jnp.inf))
            term = q * _roll0(k, j) * e
            vj = _roll0(v, j)
        sc = _mm(term, ones_ref[...], pa=2)
        y = sc * vj if y is None else y + sc * vj
    y_ref[...] = y

    qh = q * jnp.exp(cum)
    for c in range(tb // GLA_C):
        rows = slice(c * GLA_C, (c + 1) * GLA_C)
        cum_c = cum[rows]
        last = cum_c[GLA_C - 1:GLA_C]
        kt = k[rows] * jnp.exp(last - cum_c)
        dec = jnp.exp(last)
        for h in range(GLA_H):
            ks = slice(h * GLA_DK, (h + 1) * GLA_DK)
            vs = slice(h * GLA_DV, (h + 1) * GLA_DV)
            st = st_ref[h]
            y_ref[rows, vs] += _mm(qh[rows, ks], st, NT)
            st_ref[h] = st * dec[:, ks] + _mm(v[rows, vs], kt[:, ks], TN)

    yy = y_ref[...]
    outs = []
    for h in range(GLA_H):
        vs = slice(h * GLA_DV, (h + 1) * GLA_DV)
        outs.append(_rms(yy[:, vs], nw_ref[...]))
    o_ref[...] = (jnp.concatenate(outs, axis=1) * _silu(gate)).astype(BF16)

    @pl.when(t == pl.num_programs(1) - 1)
    def _():
        for h in range(GLA_H):
            s_ref[0, h] = st_ref[h].T


def _gla_prompt(proj, nb, nt_len, wg2, bg, nw, ones):
    tb = GLA_TB
    nt = nt_len // tb
    return pl.pallas_call(
        _gla_kernel,
        grid=(nb, nt),
        in_specs=[pl.BlockSpec((GROUP_TILES, tb, TILE), lambda b, t: (T_GLA // GROUP_TILES, b * nt + t, 0)),
                  _const_spec((TILE, TILE)), _const_spec((1, TILE)), _const_spec((1, GLA_DV)),
                  _const_spec((TILE, MIX))],
        out_specs=[pl.BlockSpec((tb, MIX), lambda b, t: (b * nt + t, 0)),
                   pl.BlockSpec((1, GLA_H, GLA_DK, GLA_DV), lambda b, t: (b, 0, 0, 0))],
        out_shape=[jax.ShapeDtypeStruct((nb * nt_len, MIX), BF16),
                   jax.ShapeDtypeStruct((nb, GLA_H, GLA_DK, GLA_DV), F32)],
        scratch_shapes=[pltpu.VMEM((GLA_H, GLA_DV, GLA_DK), F32), pltpu.VMEM((tb, MIX), F32)],
        compiler_params=_params("parallel", "arbitrary"),
        name="gla_prompt",
    )(proj, wg2, bg, nw, ones)


LRU_TB = 256


def _lru_gates(xc, wa_ref, ba_ref, wi_ref, bi_ref, lam_ref):
    ra, ri = [], []
    for j in range(2):
        xs = xc[:, j * TILE:(j + 1) * TILE]
        ra.append(_mm(xs, wa_ref[j], pa=2))
        ri.append(_mm(xs, wi_ref[j], pa=2))
    r = _sigmoid(jnp.concatenate(ra, axis=1) + ba_ref[...])
    i = _sigmoid(jnp.concatenate(ri, axis=1) + bi_ref[...])
    log_a = -LRU_C * r * _softplus(-lam_ref[...])
    a = jnp.exp(log_a)
    mult = jnp.sqrt(_neg_expm1(2.0 * log_a))
    return a, mult, i


def _lru_kernel(x0_ref, x1_ref, g0_ref, g1_ref, cw_ref, cb_ref, wa_ref, ba_ref, wi_ref, bi_ref, lam_ref,
                o_ref, h_ref, carry_ref, hc_ref):
    tb = LRU_TB
    t = pl.program_id(1)

    @pl.when(t == 0)
    def _():
        carry_ref[...] = jnp.zeros_like(carry_ref)
        hc_ref[...] = jnp.zeros_like(hc_ref)

    xb = jnp.concatenate([x0_ref[...], x1_ref[...]], axis=1)
    gb = jnp.concatenate([g0_ref[...], g1_ref[...]], axis=1)
    xc = _causal_conv(xb, carry_ref[...], cw_ref, cb_ref)
    carry_ref[...] = xb[tb - SUBLANES:]
    a, mult, i = _lru_gates(xc, wa_ref, ba_ref, wi_ref, bi_ref, lam_ref)
    row = _iota((tb, MIX), 0)
    mult = jnp.where((row == 0) & (t == 0), 1.0, mult)
    b = mult * i * xc
    d = 1
    while d < tb:
        m = row >= d
        b = jnp.where(m, a * _roll0(b, d) + b, b)
        a = jnp.where(m, a * _roll0(a, d), a)
        d *= 2
    h = b + a * hc_ref[0:1, :]
    hl = h[tb - 1:tb, :]
    hc_ref[...] = jnp.broadcast_to(hl, hc_ref.shape)
    o_ref[...] = (h * _gelu_tanh(gb)).astype(BF16)
    h_ref[0] = hl


def _lru_prompt(proj, nb, nt_len, cw, cb, wa, ba, wi, bi, lam):
    tb = LRU_TB
    nt = nt_len // tb

    def tile(j):
        return pl.BlockSpec((None, tb, TILE), lambda b, t: (T_LRU + j, b * nt + t, 0))

    return pl.pallas_call(
        _lru_kernel,
        grid=(nb, nt),
        in_specs=[tile(0), tile(1), tile(2), tile(3),
                  _const_spec((CONV_W, MIX)), _const_spec((1, MIX)),
                  _const_spec((2, TILE, TILE)), _const_spec((1, MIX)),
                  _const_spec((2, TILE, TILE)), _const_spec((1, MIX)), _const_spec((1, MIX))],
        out_specs=[pl.BlockSpec((tb, MIX), lambda b, t: (b * nt + t, 0)),
                   pl.BlockSpec((1, 1, MIX), lambda b, t: (b, 0, 0))],
        out_shape=[jax.ShapeDtypeStruct((nb * nt_len, MIX), BF16), jax.ShapeDtypeStruct((nb, 1, MIX), F32)],
        scratch_shapes=[pltpu.VMEM((SUBLANES, MIX), F32), pltpu.VMEM((SUBLANES, MIX), F32)],
        compiler_params=_params("parallel", "arbitrary"),
        name="lru_prompt",
    )(proj, proj, proj, proj, cw, cb, wa, ba, wi, bi, lam)


RWKV_TB = 128


def _seg_sum(x, seg_ref):
    outs = []
    for j in range(x.shape[1] // LANES):
        outs.append(_mm(x[:, j * LANES:(j + 1) * LANES], seg_ref[...], pa=3))
    return jnp.concatenate(outs, axis=1)


def _rwkv_pointwise(z, zs, mu_ref, w0_ref, w2_ref, a0_ref, a2_ref, g2_ref, kk_ref, ka_ref, seg_ref):
    zm = z + (zs - z) * mu_ref[...]
    r = zm[:, 0:MIX]
    k = zm[:, MIX:2 * MIX]
    v = zm[:, 2 * MIX:3 * MIX]
    zw = zm[:, 3 * MIX:3 * MIX + 64]
    za = zm[:, 3 * MIX + 64:3 * MIX + 128]
    zg = zm[:, 3 * MIX + 128:3 * MIX + 256]
    lw = -RWKV_DECAY * _sigmoid(w0_ref[...] + _mm(jnp.tanh(zw), w2_ref[...], pa=2))
    a = _sigmoid(a0_ref[...] + _mm(za, a2_ref[...], pa=2))
    g = _mm(_sigmoid(zg), g2_ref[...], pa=2)
    kk = k * kk_ref[...]
    kk = kk / jnp.maximum(jnp.sqrt(_seg_sum(kk * kk, seg_ref)), 1e-12)
    k = k * (1.0 + (a - 1.0) * ka_ref[...])
    return r, lw, k, v, kk, a, g


def _rwkv_finish(y, r, k, v, g, rk_ref, lnw_ref, lnb_ref, seg_ref):
    mean = _seg_sum(y, seg_ref) * (1.0 / RWKV_D)
    yc = y - mean
    var = _seg_sum(yc * yc, seg_ref) * (1.0 / RWKV_D)
    yn = yc * lax.rsqrt(var + RWKV_LN_EPS) * lnw_ref[...] + lnb_ref[...]
    bonus = _seg_sum(r * k * rk_ref[...], seg_ref) * v
    return (yn + bonus) * g


def _rwkv_kernel(z_ref, mu_ref, w0_ref, w2_ref, a0_ref, a2_ref, g2_ref, kk_ref, ka_ref, rk_ref, lnw_ref, lnb_ref,
                 seg_ref, o_ref, s_ref, prev_ref, st_ref, y_ref):
    tb = RWKV_TB
    cc = RWKV_C
    t = pl.program_id(1)

    @pl.when(t == 0)
    def _():
        prev_ref[...] = jnp.zeros_like(prev_ref)
        st_ref[...] = jnp.zeros_like(st_ref)

    z = jnp.concatenate([z_ref[j] for j in range(GROUP_TILES)], axis=1)
    zs = _shift_rows(z, prev_ref[...], 1, _iota((SUBLANES, RWKV_COLS), 0))
    prev_ref[...] = z[tb - SUBLANES:]
    r, lw, k, v, kk, a, g = _rwkv_pointwise(z, zs, mu_ref, w0_ref, w2_ref, a0_ref, a2_ref, g2_ref, kk_ref, ka_ref,
                                            seg_ref)
    b = kk * a

    pos = _iota((tb, MIX), 0) & (cc - 1)
    lg = _chunk_cumsum(lw, cc, pos)
    lg_end = _chunk_last_bcast(lg, cc, pos)
    gam = jnp.exp(lg)
    inv = jnp.exp(-lg)
    rg = r * gam
    kg = kk * jnp.exp(lg - lw)
    bi = b * inv
    ki = k * inv
    e_end = jnp.exp(lg_end - lg)
    bt = b * e_end
    kt = k * e_end
    g_end = jnp.exp(lg_end)

    ri = _iota((tb, tb), 0)
    ci = _iota((tb, tb), 1)
    same = (ri - (ri & (cc - 1))) == (ci - (ci & (cc - 1)))
    strict = same & (ci < ri)
    incl = same & (ci <= ri)
    eye = (ri == ci).astype(F32)
    eye_d = _iota((RWKV_D, RWKV_D), 0) == _iota((RWKV_D, RWKV_D), 1)

    for h in range(RWKV_H):
        ls = slice(h * RWKV_D, (h + 1) * RWKV_D)
        vh = v[:, ls]
        gmat = _mm(jnp.concatenate([kg[:, ls], rg[:, ls]], axis=0),
                   jnp.concatenate([bi[:, ls], ki[:, ls]], axis=0), NT, pa=2, pb=2)
        amat = jnp.where(strict, gmat[:tb, :tb], 0.0)
        bmat = jnp.where(strict, gmat[:tb, tb:], 0.0)
        pb_ = jnp.where(incl, gmat[tb:, :tb], 0.0)
        pk_ = jnp.where(incl, gmat[tb:, tb:], 0.0)
        x = eye - amat
        p = _mm(amat, amat, pa=2, pb=2)
        x = x + _mm(x, p, pa=2, pb=2)
        p = _mm(p, p, pa=2, pb=2)
        x = x + _mm(x, p, pa=2, pb=2)
        p = _mm(p, p, pa=2, pb=2)
        x = x + _mm(x, p, pa=2, pb=2)
        bv = _mm(bmat, vh, pa=2, pb=2)
        wu = -_mm(x, jnp.concatenate([kg[:, ls], bv], axis=1), pa=2, pb=2)
        qy = _mm(pb_, wu, pa=2, pb=2)
        qt = rg[:, ls] + qy[:, :RWKV_D]
        y0 = qy[:, RWKV_D:] + _mm(pk_, vh, pa=2, pb=2)
        w_ = wu[:, :RWKV_D]
        u0 = wu[:, RWKV_D:]
        s = st_ref[h]
        for c in range(tb // cc):
            rows = slice(c * cc, (c + 1) * cc)
            y_ref[rows, ls] = _mm(qt[rows], s, NT, pa=2, pb=2) + y0[rows]
            mc = jnp.where(eye_d, g_end[c * cc:c * cc + 1, ls], 0.0) + _mm(w_[rows], bt[rows, ls], TN, pa=2, pb=2)
            nc = _mm(jnp.concatenate([u0[rows], vh[rows]], axis=0),
                     jnp.concatenate([bt[rows, ls], kt[rows, ls]], axis=0), TN, pa=2, pb=2)
            s = _mm(s, mc, pa=2, pb=2) + nc
        st_ref[h] = s

    o_ref[...] = _rwkv_finish(y_ref[...], r, k, v, g, rk_ref, lnw_ref, lnb_ref, seg_ref).astype(BF16)

    @pl.when(t == pl.num_programs(1) - 1)
    def _():
        s_ref[0] = st_ref[...]


def _rwkv_prompt(proj, nb, nt_len, p):
    tb = RWKV_TB
    nt = nt_len // tb
    return pl.pallas_call(
        _rwkv_kernel,
        grid=(nb, nt),
        in_specs=[pl.BlockSpec((GROUP_TILES, tb, TILE), lambda b, t: (T_RWKV // GROUP_TILES, b * nt + t, 0)),
                  _const_spec((1, RWKV_COLS)), _const_spec((1, MIX)), _const_spec((64, MIX)),
                  _const_spec((1, MIX)), _const_spec((64, MIX)), _const_spec((128, MIX)),
                  _const_spec((1, MIX)), _const_spec((1, MIX)), _const_spec((1, MIX)),
                  _const_spec((1, MIX)), _const_spec((1, MIX)), _const_spec((LANES, LANES))],
        out_specs=[pl.BlockSpec((tb, MIX), lambda b, t: (b * nt + t, 0)),
                   pl.BlockSpec((1, RWKV_H, RWKV_D, RWKV_D), lambda b, t: (b, 0, 0, 0))],
        out_shape=[jax.ShapeDtypeStruct((nb * nt_len, MIX), BF16),
                   jax.ShapeDtypeStruct((nb, RWKV_H, RWKV_D, RWKV_D), F32)],
        scratch_shapes=[pltpu.VMEM((SUBLANES, RWKV_COLS), F32), pltpu.VMEM((RWKV_H, RWKV_D, RWKV_D), F32),
                        pltpu.VMEM((tb, MIX), F32)],
        compiler_params=_params("parallel", "arbitrary"),
        name="rwkv_prompt",
    )(proj, p["mu"], p["w0"], p["w2"], p["a0"], p["a2"], p["g2"], p["kk"], p["ka"], p["rk"], p["lnw"], p["lnb"],
      p["seg"])


SSD_TB = 256


def _ssd_pointwise(xbc_c, dt_raw, dtb_ref, aneg_ref):
    xbc = _silu(xbc_c)
    dt = _softplus(dt_raw + dtb_ref[...])
    return xbc, dt, dt * aneg_ref[...]


def _expand_heads(col8, width):
    n = col8.shape[0]
    return jnp.concatenate([jnp.broadcast_to(col8[:, h:h + 1], (n, width)) for h in range(SSD_H)], axis=1)


def _ssd_kernel(z_ref, cw_ref, cb_ref, dtb_ref, aneg_ref, dsk_ref, nw_ref, o_ref, s_ref, carry_ref, st_ref, y_ref):
    tb = SSD_TB
    cc = SSD_C
    t = pl.program_id(1)

    @pl.when(t == 0)
    def _():
        carry_ref[...] = jnp.zeros_like(carry_ref)
        st_ref[...] = jnp.zeros_like(st_ref)

    gate = jnp.concatenate([z_ref[0], z_ref[1]], axis=1)
    xbc_raw = jnp.concatenate([z_ref[2], z_ref[3], z_ref[4], z_ref[5]], axis=1)
    xbc_c = _causal_conv(xbc_raw, carry_ref[...], cw_ref, cb_ref)
    carry_ref[...] = xbc_raw[tb - SUBLANES:]
    xbc, dt, dta = _ssd_pointwise(xbc_c, z_ref[6], dtb_ref, aneg_ref)
    xs = xbc[:, :MIX]
    bm = xbc[:, MIX:MIX + TILE]
    cm = xbc[:, MIX + TILE:]
    pos = _iota((tb, TILE), 0) & (cc - 1)
    cum = _chunk_cumsum(dta, cc, pos)
    xdt = xs * _expand_heads(dt, SSD_P)
    tri = _iota((cc, cc), 0) >= _iota((cc, cc), 1)

    for c in range(tb // cc):
        rows = slice(c * cc, (c + 1) * cc)
        cum_c = cum[rows, :LANES]
        cum_t = cum_c.T
        for gi in range(SSD_G):
            ns = slice(gi * SSD_N, (gi + 1) * SSD_N)
            cg = cm[rows, ns]
            bg = bm[rows, ns]
            gmat = _mm(cg, bg, NT, pa=2, pb=2)
            for hh in range(SSD_H // SSD_G):
                h = gi * (SSD_H // SSD_G) + hh
                ps = slice(h * SSD_P, (h + 1) * SSD_P)
                col = cum_c[:, h:h + 1]
                lmat = jnp.exp(jnp.where(tri, col - cum_t[h:h + 1, :], -jnp.inf))
                xh = xdt[rows, ps]
                st = st_ref[h]
                yd = _mm(gmat * lmat, xh, pa=2, pb=2)
                yo = _mm(cg, st, NT, pa=2, pb=2) * jnp.exp(col)
                last = cum_c[cc - 1:cc, h:h + 1]
                st_ref[h] = st * jnp.exp(last) + _mm(xh * jnp.exp(last - col), bg, TN, pa=2, pb=2)
                y_ref[rows, ps] = yd + yo

    y = (y_ref[...] + dsk_ref[...] * xs) * _silu(gate)
    o_ref[...] = _rms(y, nw_ref[...]).astype(BF16)

    @pl.when(t == pl.num_programs(1) - 1)
    def _():
        s_ref[0] = st_ref[...]


def _ssd_prompt(proj, nb, nt_len, cw, cb, dtb, aneg, dsk, nw):
    tb = SSD_TB
    nt = nt_len // tb
    return pl.pallas_call(
        _ssd_kernel,
        grid=(nb, nt),
        in_specs=[pl.BlockSpec((GROUP_TILES, tb, TILE), lambda b, t: (T_SSD // GROUP_TILES, b * nt + t, 0)),
                  _const_spec((CONV_W, SSD_CONV_DIM)), _const_spec((1, SSD_CONV_DIM)),
                  _const_spec((1, TILE)), _const_spec((1, TILE)), _const_spec((1, MIX)), _const_spec((1, MIX))],
        out_specs=[pl.BlockSpec((tb, MIX), lambda b, t: (b * nt + t, 0)),
                   pl.BlockSpec((1, SSD_H, SSD_P, SSD_N), lambda b, t: (b, 0, 0, 0))],
        out_shape=[jax.ShapeDtypeStruct((nb * nt_len, MIX), BF16),
                   jax.ShapeDtypeStruct((nb, SSD_H, SSD_P, SSD_N), F32)],
        scratch_shapes=[pltpu.VMEM((SUBLANES, SSD_CONV_DIM), F32), pltpu.VMEM((SSD_H, SSD_P, SSD_N), F32),
                        pltpu.VMEM((tb, MIX), F32)],
        compiler_params=_params("parallel", "arbitrary"),
        name="ssd_prompt",
    )(proj, cw, cb, dtb, aneg, dsk, nw)


R_GLA_V, R_GLA_GATE = 0, 512
R_RW_W, R_RW_B, R_RW_K, R_RW_KK, R_RW_R, R_RW_V, R_RW_G = 1024, 1536, 2048, 2560, 3072, 3584, 4096
R_SSD_B, R_SSD_C, R_SSD_E, R_SSD_GATE, R_SSD_X = 4608, 4864, 5120, 6144, 6656
ROW_W = 7168
C_GLA_A, C_GLA_K, C_GLA_Q, C_RW_V, C_SSD_X = 0, 256, 512, 768, 1280
COL_W = 1792
DEC_RB = 8


def _dec_prep_kernel(z_ref, lconv_ref, lh_ref, rprev_ref, sconv_ref,
                     wg2_ref, bg_ref,
                     lcw_ref, lcb_ref, wa_ref, ba_ref, wi_ref, bi_ref, lam_ref,
                     mu_ref, w0_ref, w2_ref, a0_ref, a2_ref, g2_ref, kk_ref, ka_ref, seg_ref,
                     scw_ref, scb_ref, dtb_ref, aneg_ref,
                     rows_ref, colt_ref, olru_ref, lh_out_ref, lconv_out_ref, sconv_out_ref):
    n = rows_ref.shape[0]
    q = z_ref[T_GLA + 0] * (GLA_DK ** -0.5)
    k = z_ref[T_GLA + 1]
    g = _gla_gates(z_ref[T_GLA + 6], wg2_ref, bg_ref)
    rows_ref[:, R_GLA_V:R_GLA_V + TILE] = z_ref[T_GLA + 2]
    rows_ref[:, R_GLA_V + TILE:R_GLA_V + MIX] = z_ref[T_GLA + 3]
    rows_ref[:, R_GLA_GATE:R_GLA_GATE + TILE] = z_ref[T_GLA + 4]
    rows_ref[:, R_GLA_GATE + TILE:R_GLA_GATE + MIX] = z_ref[T_GLA + 5]
    cols = [jnp.exp(g), k, q]

    xb = jnp.concatenate([z_ref[T_LRU + 0], z_ref[T_LRU + 1]], axis=1)
    gb = jnp.concatenate([z_ref[T_LRU + 2], z_ref[T_LRU + 3]], axis=1)
    xc = lcb_ref[...] + xb * lcw_ref[CONV_W - 1:CONV_W, :]
    for i in range(CONV_W - 1):
        xc = xc + lconv_ref[:, i * MIX:(i + 1) * MIX] * lcw_ref[i:i + 1, :]
    a, mult, gi = _lru_gates(xc, wa_ref, ba_ref, wi_ref, bi_ref, lam_ref)
    h = a * lh_ref[...] + mult * gi * xc
    lh_out_ref[...] = h
    olru_ref[...] = (h * _gelu_tanh(gb)).astype(BF16)
    lconv_out_ref[:, 0:2 * MIX] = lconv_ref[:, MIX:3 * MIX]
    lconv_out_ref[:, 2 * MIX:3 * MIX] = xb

    z = jnp.concatenate([z_ref[T_RWKV + j] for j in range(GROUP_TILES)], axis=1)
    r, lw, kmod, v, kk, a7, g7 = _rwkv_pointwise(z, rprev_ref[...], mu_ref, w0_ref, w2_ref, a0_ref, a2_ref, g2_ref,
                                                 kk_ref, ka_ref, seg_ref)
    rows_ref[:, R_RW_W:R_RW_W + MIX] = jnp.exp(lw)
    rows_ref[:, R_RW_B:R_RW_B + MIX] = kk * a7
    rows_ref[:, R_RW_K:R_RW_K + MIX] = kmod
    rows_ref[:, R_RW_KK:R_RW_KK + MIX] = kk
    rows_ref[:, R_RW_R:R_RW_R + MIX] = r
    rows_ref[:, R_RW_V:R_RW_V + MIX] = v
    rows_ref[:, R_RW_G:R_RW_G + MIX] = g7
    cols.append(v)

    xbc_raw = jnp.concatenate([z_ref[T_SSD + 2], z_ref[T_SSD + 3], z_ref[T_SSD + 4], z_ref[T_SSD + 5]], axis=1)
    xbc_c = scb_ref[...] + xbc_raw * scw_ref[CONV_W - 1:CONV_W, :]
    for i in range(CONV_W - 1):
        xbc_c = xbc_c + sconv_ref[:, i * SSD_CONV_DIM:(i + 1) * SSD_CONV_DIM] * scw_ref[i:i + 1, :]
    xbc, dt, dta = _ssd_pointwise(xbc_c, z_ref[T_SSD + 6], dtb_ref, aneg_ref)
    xs = xbc[:, :MIX]
    rows_ref[:, R_SSD_B:R_SSD_B + TILE] = xbc[:, MIX:MIX + TILE]
    rows_ref[:, R_SSD_C:R_SSD_C + TILE] = xbc[:, MIX + TILE:]
    rows_ref[:, R_SSD_E:R_SSD_E + SSD_H * LANES] = _expand_heads(jnp.exp(dta), LANES)
    rows_ref[:, R_SSD_GATE:R_SSD_GATE + TILE] = z_ref[T_SSD + 0]
    rows_ref[:, R_SSD_GATE + TILE:R_SSD_GATE + MIX] = z_ref[T_SSD + 1]
    rows_ref[:, R_SSD_X:R_SSD_X + MIX] = xs
    cols.append(xs * _expand_heads(dt, SSD_P))
    sconv_out_ref[:, 0:2 * SSD_CONV_DIM] = sconv_ref[:, SSD_CONV_DIM:3 * SSD_CONV_DIM]
    sconv_out_ref[:, 2 * SSD_CONV_DIM:3 * SSD_CONV_DIM] = xbc_raw

    off = 0
    for cvec in cols:
        for j in range(cvec.shape[1] // LANES):
            colt_ref[off:off + LANES, :] = cvec[:, j * LANES:(j + 1) * LANES].T
            off += LANES


def _dec_prep(zdec, lconv, lh, rprev, sconv, pp):
    n = zdec.shape[1]
    args = [zdec, lconv, lh, rprev, sconv,
            pp["gla_wg2"], pp["gla_bg"],
            pp["lru_cw"], pp["lru_cb"], pp["lru_wa"], pp["lru_ba"], pp["lru_wi"], pp["lru_bi"], pp["lru_lam"],
            pp["rwkv"]["mu"], pp["rwkv"]["w0"], pp["rwkv"]["w2"], pp["rwkv"]["a0"], pp["rwkv"]["a2"],
            pp["rwkv"]["g2"], pp["rwkv"]["kk"], pp["rwkv"]["ka"], pp["rwkv"]["seg"],
            pp["ssd_cw"], pp["ssd_cb"], pp["ssd_dtb"], pp["ssd_aneg"]]
    return pl.pallas_call(
        _dec_prep_kernel,
        out_shape=[jax.ShapeDtypeStruct((n, ROW_W), F32), jax.ShapeDtypeStruct((COL_W, n), F32),
                   jax.ShapeDtypeStruct((n, MIX), BF16), jax.ShapeDtypeStruct((n, MIX), F32),
                   jax.ShapeDtypeStruct((n, 3 * MIX), F32), jax.ShapeDtypeStruct((n, 3 * SSD_CONV_DIM), F32)],
        compiler_params=pltpu.CompilerParams(vmem_limit_bytes=VMEM_LIMIT),
        name="decode_prep",
    )(*args)


def _dec_state_kernel(colt_ref, rows_ref, sg_ref, sr_ref, ss_ref,
                      sg_out, sr_out, ss_out, yg_ref, yrt_ref, yst_ref):
    pid = pl.program_id(0)
    n = colt_ref.shape[1]

    @pl.when(pid == 0)
    def _():
        yrt_ref[...] = jnp.zeros_like(yrt_ref)
        yst_ref[...] = jnp.zeros_like(yst_ref)

    lane = _iota((1, n), 1)

    def body(i, carry):
        sel = lane == pid * DEC_RB + i
        col = jnp.sum(jnp.where(sel, colt_ref[...], 0.0), axis=1, keepdims=True)
        rv = rows_ref[pl.ds(i, 1), :]
        ys = []
        for h in range(GLA_H):
            ks = slice(h * GLA_DK, (h + 1) * GLA_DK)
            al = col[C_GLA_A + h * GLA_DK:C_GLA_A + (h + 1) * GLA_DK]
            kc = col[C_GLA_K + h * GLA_DK:C_GLA_K + (h + 1) * GLA_DK]
            qc = col[C_GLA_Q + h * GLA_DK:C_GLA_Q + (h + 1) * GLA_DK]
            vr = rv[:, R_GLA_V + h * GLA_DV:R_GLA_V + (h + 1) * GLA_DV]
            s = al * sg_ref[i, h] + kc * vr
            sg_out[i, h] = s
            ys.append(jnp.sum(qc * s, axis=0, keepdims=True))
        yg_ref[pl.ds(i, 1), :] = jnp.concatenate(ys, axis=1)
        for h in range(RWKV_H):
            ls = slice(h * RWKV_D, (h + 1) * RWKV_D)
            w = rv[:, R_RW_W + h * RWKV_D:R_RW_W + (h + 1) * RWKV_D]
            b = rv[:, R_RW_B + h * RWKV_D:R_RW_B + (h + 1) * RWKV_D]
            k = rv[:, R_RW_K + h * RWKV_D:R_RW_K + (h + 1) * RWKV_D]
            kk = rv[:, R_RW_KK + h * RWKV_D:R_RW_KK + (h + 1) * RWKV_D]
            r = rv[:, R_RW_R + h * RWKV_D:R_RW_R + (h + 1) * RWKV_D]
            vc = col[C_RW_V + h * RWKV_D:C_RW_V + (h + 1) * RWKV_D]
            s = sr_ref[i, h]
            sa = jnp.sum(s * kk, axis=1, keepdims=True)
            s = s * w - sa * b + vc * k
            sr_out[i, h] = s
            yc = jnp.sum(s * r, axis=1, keepdims=True)
            yrt_ref[ls, :] = jnp.where(sel, yc, yrt_ref[ls, :])
        for h in range(SSD_H):
            ps = slice(h * SSD_P, (h + 1) * SSD_P)
            gi = h // (SSD_H // SSD_G)
            e = rv[:, R_SSD_E + h * LANES:R_SSD_E + (h + 1) * LANES]
            bn = rv[:, R_SSD_B + gi * SSD_N:R_SSD_B + (gi + 1) * SSD_N]
            cn = rv[:, R_SSD_C + gi * SSD_N:R_SSD_C + (gi + 1) * SSD_N]
            xc = col[C_SSD_X + h * SSD_P:C_SSD_X + (h + 1) * SSD_P]
            s = ss_ref[i, h] * e + xc * bn
            ss_out[i, h] = s
            yc = jnp.sum(s * cn, axis=1, keepdims=True)
            yst_ref[ps, :] = jnp.where(sel, yc, yst_ref[ps, :])
        return carry

    lax.fori_loop(0, DEC_RB, body, 0)


def _dec_state(colt, rows, sg, sr, ss):
    n = rows.shape[0]
    blk = lambda i: (i, 0, 0, 0)
    return pl.pallas_call(
        _dec_state_kernel,
        grid=(n // DEC_RB,),
        in_specs=[_const_spec((COL_W, n)), pl.BlockSpec((DEC_RB, ROW_W), lambda i: (i, 0)),
                  pl.BlockSpec((DEC_RB, GLA_H, GLA_DK, GLA_DV), blk),
                  pl.BlockSpec((DEC_RB, RWKV_H, RWKV_D, RWKV_D), blk),
                  pl.BlockSpec((DEC_RB, SSD_H, SSD_P, SSD_N), blk)],
        out_specs=[pl.BlockSpec((DEC_RB, GLA_H, GLA_DK, GLA_DV), blk),
                   pl.BlockSpec((DEC_RB, RWKV_H, RWKV_D, RWKV_D), blk),
                   pl.BlockSpec((DEC_RB, SSD_H, SSD_P, SSD_N), blk),
                   pl.BlockSpec((DEC_RB, MIX), lambda i: (i, 0)),
                   _const_spec((MIX, n)), _const_spec((MIX, n))],
        out_shape=[jax.ShapeDtypeStruct(sg.shape, F32), jax.ShapeDtypeStruct(sr.shape, F32),
                   jax.ShapeDtypeStruct(ss.shape, F32), jax.ShapeDtypeStruct((n, MIX), F32),
                   jax.ShapeDtypeStruct((MIX, n), F32), jax.ShapeDtypeStruct((MIX, n), F32)],
        compiler_params=_params("arbitrary"),
        name="decode_state",
    )(colt, rows, sg, sr, ss)


def _dec_finish_kernel(rows_ref, yg_ref, yrt_ref, yst_ref, gnw_ref, rk_ref, lnw_ref, lnb_ref, seg_ref,
                       dsk_ref, snw_ref, og_ref, or_ref, os_ref):
    outs = []
    for h in range(GLA_H):
        outs.append(_rms(yg_ref[:, h * GLA_DV:(h + 1) * GLA_DV], gnw_ref[...]))
    og_ref[...] = (jnp.concatenate(outs, axis=1) * _silu(rows_ref[:, R_GLA_GATE:R_GLA_GATE + MIX])).astype(BF16)
    yr = jnp.concatenate([yrt_ref[j * LANES:(j + 1) * LANES, :].T for j in range(MIX // LANES)], axis=1)
    or_ref[...] = _rwkv_finish(yr, rows_ref[:, R_RW_R:R_RW_R + MIX], rows_ref[:, R_RW_K:R_RW_K + MIX],
                               rows_ref[:, R_RW_V:R_RW_V + MIX], rows_ref[:, R_RW_G:R_RW_G + MIX],
                               rk_ref, lnw_ref, lnb_ref, seg_ref).astype(BF16)
    ysd = jnp.concatenate([yst_ref[j * LANES:(j + 1) * LANES, :].T for j in range(MIX // LANES)], axis=1)
    y = (ysd + dsk_ref[...] * rows_ref[:, R_SSD_X:R_SSD_X + MIX]) * _silu(rows_ref[:, R_SSD_GATE:R_SSD_GATE + MIX])
    os_ref[...] = _rms(y, snw_ref[...]).astype(BF16)


def _dec_finish(rows, yg, yrt, yst, pp):
    n = rows.shape[0]
    o = jax.ShapeDtypeStruct((n, MIX), BF16)
    return pl.pallas_call(
        _dec_finish_kernel,
        out_shape=[o, o, o],
        compiler_params=pltpu.CompilerParams(vmem_limit_bytes=VMEM_LIMIT),
        name="decode_finish",
    )(rows, yg, yrt, yst, pp["gla_nw"], pp["rwkv"]["rk"], pp["rwkv"]["lnw"], pp["rwkv"]["lnb"], pp["rwkv"]["seg"],
      pp["ssd_dsk"], pp["ssd_nw"])


def _block_diag(w4):
    out = jnp.zeros((TILE, TILE), w4.dtype)
    for j in range(4):
        out = out.at[j * 64:(j + 1) * 64, j * 64:(j + 1) * 64].set(w4[j])
    return out


def _prep_layer(l, w_in, w_out, w_up, w_down, norm_mix_pre, norm_mix_post, norm_mlp_pre, norm_mlp_post,
                gla_w_gate2, gla_b_gate, gla_norm, lru_conv_w, lru_conv_b, lru_w_a, lru_b_a, lru_w_i, lru_b_i,
                lru_lambda, rwkv_mu, rwkv_w0, rwkv_w2, rwkv_a0, rwkv_a2, rwkv_g2, rwkv_k_k, rwkv_k_a, rwkv_r_k,
                rwkv_ln_w, rwkv_ln_b, ssd_conv_w, ssd_conv_b, ssd_dt_bias, ssd_a_log, ssd_d, ssd_norm):
    row = lambda a: a.reshape(1, -1).astype(F32)
    w = w_in[l]
    o_lru = GLA_COLS
    o_rwkv = GLA_COLS + LRU_COLS
    o_ssd = o_rwkv + RWKV_COLS
    zpad = lambda n: jnp.zeros((D_MODEL, n), w.dtype)
    w_perm = jnp.concatenate([
        w[:, 0:GLA_COLS], zpad(GROUP_TILES * TILE - GLA_COLS),
        w[:, o_rwkv:o_rwkv + RWKV_COLS],
        w[:, o_ssd:o_ssd + SSD_COLS], zpad(GROUP_TILES * TILE - SSD_COLS),
        w[:, o_lru:o_lru + LRU_COLS]], axis=1).astype(BF16)
    ii = jnp.arange(TILE)[:, None]
    jj = jnp.arange(MIX)[None, :]
    seg_i = jnp.arange(LANES)
    pp = {
        "w_in": w_perm,
        "w_out": w_out[l].reshape(4, MIX, D_MODEL).astype(BF16),
        "w_up": w_up[l].astype(BF16),
        "w_down": w_down[l].astype(BF16),
        "n_mix_pre": row(norm_mix_pre[l]), "n_mix_post": row(norm_mix_post[l]),
        "n_mlp_pre": row(norm_mlp_pre[l]), "n_mlp_post": row(norm_mlp_post[l]),
        "gla_wg2": jnp.zeros((TILE, TILE), F32).at[:GLA_RANK, :].set(gla_w_gate2[l]).astype(BF16),
        "gla_bg": row(gla_b_gate[l]),
        "gla_nw": row(gla_norm[l]),
        "gla_ones": (ii // GLA_DK == jj // GLA_DV).astype(BF16),
        "lru_cw": lru_conv_w[l].astype(F32), "lru_cb": row(lru_conv_b[l]),
        "lru_wa": jnp.stack([_block_diag(lru_w_a[l, 0:4]), _block_diag(lru_w_a[l, 4:8])]).astype(BF16),
        "lru_wi": jnp.stack([_block_diag(lru_w_i[l, 0:4]), _block_diag(lru_w_i[l, 4:8])]).astype(BF16),
        "lru_ba": row(lru_b_a[l]), "lru_bi": row(lru_b_i[l]), "lru_lam": row(lru_lambda[l]),
        "rwkv": {
            "mu": row(rwkv_mu[l]), "w0": row(rwkv_w0[l]), "w2": rwkv_w2[l].astype(BF16),
            "a0": row(rwkv_a0[l]), "a2": rwkv_a2[l].astype(BF16), "g2": rwkv_g2[l].astype(BF16),
            "kk": row(rwkv_k_k[l]), "ka": row(rwkv_k_a[l]), "rk": row(rwkv_r_k[l]),
            "lnw": row(rwkv_ln_w[l]), "lnb": row(rwkv_ln_b[l]),
            "seg": (seg_i[:, None] // RWKV_D == seg_i[None, :] // RWKV_D).astype(BF16),
        },
        "ssd_cw": ssd_conv_w[l].astype(F32), "ssd_cb": row(ssd_conv_b[l]),
        "ssd_dtb": jnp.zeros((1, TILE), F32).at[0, :SSD_H].set(ssd_dt_bias[l]),
        "ssd_aneg": jnp.zeros((1, TILE), F32).at[0, :SSD_H].set(-jnp.exp(ssd_a_log[l].astype(F32))),
        "ssd_dsk": jnp.repeat(ssd_d[l].astype(F32), SSD_P).reshape(1, MIX),
        "ssd_nw": row(ssd_norm[l]),
    }
    return pp


def _tail_rows(proj, nb, nt_len, tile0, ntiles, nrows):
    t = proj[tile0:tile0 + ntiles].reshape(ntiles, nb, nt_len, TILE)[:, :, nt_len - nrows:, :]
    return jnp.transpose(t, (1, 2, 0, 3)).reshape(nb, nrows, ntiles * TILE)


def _layer_prompt(x2d, nb, nt_len, pp, tm_proj=512, tm_out=256, tm_mlp=512, tf=512):
    proj = _proj(x2d, pp["n_mix_pre"], pp["w_in"], tm_proj)
    o_gla, s_gla = _gla_prompt(proj, nb, nt_len, pp["gla_wg2"], pp["gla_bg"], pp["gla_nw"], pp["gla_ones"])
    o_lru, h_lru = _lru_prompt(proj, nb, nt_len, pp["lru_cw"], pp["lru_cb"], pp["lru_wa"], pp["lru_ba"],
                               pp["lru_wi"], pp["lru_bi"], pp["lru_lam"])
    o_rwkv, s_rwkv = _rwkv_prompt(proj, nb, nt_len, pp["rwkv"])
    o_ssd, s_ssd = _ssd_prompt(proj, nb, nt_len, pp["ssd_cw"], pp["ssd_cb"], pp["ssd_dtb"], pp["ssd_aneg"],
                               pp["ssd_dsk"], pp["ssd_nw"])
    x1, h2 = _outproj((o_gla, o_lru, o_rwkv, o_ssd), pp["w_out"], x2d, pp["n_mix_post"], pp["n_mlp_pre"], tm_out)
    x2 = _mlp(h2, pp["w_up"], pp["w_down"], x1, pp["n_mlp_post"], tm_mlp, tf)
    lru_conv = _tail_rows(proj, nb, nt_len, T_LRU, 2, CONV_W - 1)
    rwkv_shift = _tail_rows(proj, nb, nt_len, T_RWKV, GROUP_TILES, 1)[:, 0, :]
    ssd_conv = _tail_rows(proj, nb, nt_len, T_SSD + 2, 4, CONV_W - 1)
    states = (s_gla, h_lru[:, 0, :], lru_conv, rwkv_shift, s_rwkv, ssd_conv, s_ssd)
    return x2, states


def _layer_decode(x2d, st, pp, tf=512):
    sg, lh, lconv, rprev, sr, sconv, ss = st
    n = x2d.shape[0]
    proj = _proj(x2d, pp["n_mix_pre"], pp["w_in"], n)
    rows, colt, o_lru, lh_new, lconv_new, sconv_new = _dec_prep(
        proj, lconv.reshape(n, -1), lh, rprev, sconv.reshape(n, -1), pp)
    sg_new, sr_new, ss_new, yg, yrt, yst = _dec_state(colt, rows, sg, sr, ss)
    o_gla, o_rwkv, o_ssd = _dec_finish(rows, yg, yrt, yst, pp)
    x1, h2 = _outproj((o_gla, o_lru, o_rwkv, o_ssd), pp["w_out"], x2d, pp["n_mix_post"], pp["n_mlp_pre"], n)
    x2 = _mlp(h2, pp["w_up"], pp["w_down"], x1, pp["n_mlp_post"], n, tf)
    rshift_new = jnp.transpose(proj[T_RWKV:T_RWKV + GROUP_TILES], (1, 0, 2)).reshape(n, RWKV_COLS)
    states = (sg_new, lh_new, lconv_new.reshape(n, CONV_W - 1, MIX), rshift_new, sr_new,
              sconv_new.reshape(n, CONV_W - 1, SSD_CONV_DIM), ss_new)
    return x2, states


def kernel(x_prompt, x_sample, state_gla, state_lru, cache_lru_conv, cache_rwkv_shift, state_rwkv, cache_ssd_conv, state_ssd, norm_mix_pre, norm_mix_post, norm_mlp_pre, norm_mlp_post, w_in, w_out, w_up, w_down, gla_w_gate2, gla_b_gate, gla_norm, lru_conv_w, lru_conv_b, lru_w_a, lru_b_a, lru_w_i, lru_b_i, lru_lambda, rwkv_mu, rwkv_w0, rwkv_w2, rwkv_a0, rwkv_a2, rwkv_g2, rwkv_k_k, rwkv_k_a, rwkv_r_k, rwkv_ln_w, rwkv_ln_b, ssd_conv_w, ssd_conv_b, ssd_dt_bias, ssd_a_log, ssd_d, ssd_norm):
    nb, nt_len, _ = x_prompt.shape
    nd = x_sample.shape[0]
    yp = x_prompt.reshape(nb * nt_len, D_MODEL)
    ys = x_sample.reshape(nd, D_MODEL)
    new_p, new_s = [], []
    for l in range(DEPTH):
        pp = _prep_layer(l, w_in, w_out, w_up, w_down, norm_mix_pre, norm_mix_post, norm_mlp_pre, norm_mlp_post,
                         gla_w_gate2, gla_b_gate, gla_norm, lru_conv_w, lru_conv_b, lru_w_a, lru_b_a, lru_w_i,
                         lru_b_i, lru_lambda, rwkv_mu, rwkv_w0, rwkv_w2, rwkv_a0, rwkv_a2, rwkv_g2, rwkv_k_k,
                         rwkv_k_a, rwkv_r_k, rwkv_ln_w, rwkv_ln_b, ssd_conv_w, ssd_conv_b, ssd_dt_bias, ssd_a_log,
                         ssd_d, ssd_norm)
        yp, st_p = _layer_prompt(yp, nb, nt_len, pp)
        st_in = (state_gla[l], state_lru[l], cache_lru_conv[l], cache_rwkv_shift[l], state_rwkv[l],
                 cache_ssd_conv[l], state_ssd[l])
        ys, st_s = _layer_decode(ys, st_in, pp)
        new_p.append(st_p)
        new_s.append(st_s)

    def stack(sts, i):
        return jnp.stack([s[i] for s in sts], axis=0)

    return (yp.reshape(nb, nt_len, D_MODEL), ys.reshape(nd, 1, D_MODEL),
            stack(new_p, 0), stack(new_s, 0), stack(new_p, 1), stack(new_s, 1),
            stack(new_p, 2), stack(new_s, 2), stack(new_p, 3), stack(new_s, 3),
            stack(new_p, 4), stack(new_s, 4), stack(new_p, 5), stack(new_s, 5),
            stack(new_p, 6), stack(new_s, 6))
```

```python
import functools
import math

import jax
import jax.numpy as jnp
from jax import lax
from jax.experimental import pallas as pl
from jax.experimental.pallas import tpu as pltpu

F32 = jnp.float32
BF16 = jnp.bfloat16

D_MODEL = 2048
D_FF = 4 * D_MODEL
DEPTH = 2
EPS = 1e-6
MIX = D_MODEL // 4
CONV_W = 4
GLA_H, GLA_DK, GLA_DV, GLA_RANK, GLA_TEMP, GLA_C = 4, 64, 128, 16, 16.0, 16
LRU_C = 8.0
RWKV_H, RWKV_D = 8, 64
RWKV_DECAY = math.exp(-0.5)
RWKV_LN_EPS = 64e-5
RWKV_C = 16
SSD_H, SSD_P, SSD_G, SSD_N, SSD_C = 8, 64, 2, 128, 64
GLA_COLS = 2 * GLA_H * GLA_DK + 2 * MIX + GLA_RANK
LRU_COLS = 2 * MIX
RWKV_COLS = 3 * MIX + 64 + 64 + 128
SSD_CONV_DIM = MIX + 2 * SSD_G * SSD_N
SSD_COLS = MIX + SSD_CONV_DIM + SSD_H

LANES = 128
SUBLANES = 8
TILE = 2 * LANES
GROUP_TILES = 7
T_GLA, T_RWKV, T_SSD, T_LRU = 0, 7, 14, 21
N_TILES = 25
VMEM_LIMIT = 56 * 1024 * 1024

NN = (((1,), (0,)), ((), ()))
NT = (((1,), (1,)), ((), ()))
TN = (((0,), (0,)), ((), ()))


def _split(x, n):
    if x.dtype == BF16:
        return [x]
    parts, r = [], x
    for i in range(n):
        p = r.astype(BF16)
        parts.append(p)
        if i + 1 < n:
            r = r - p.astype(F32)
    return parts


def _mm(a, b, dn=NN, pa=1, pb=1):
    aa, bb = _split(a, pa), _split(b, pb)
    acc = None
    for i, x in enumerate(aa):
        for j, y in enumerate(bb):
            if i + j >= max(len(aa), len(bb)):
                continue
            t = lax.dot_general(x, y, dn, preferred_element_type=F32)
            acc = t if acc is None else acc + t
    return acc


def _iota(shape, dim):
    return lax.broadcasted_iota(jnp.int32, shape, dim)


def _roll0(x, s):
    n = x.shape[0]
    s = s % n
    return x if s == 0 else pltpu.roll(x, s, 0)


def _rms(x, w):
    ms = jnp.mean(x * x, axis=-1, keepdims=True)
    return x * lax.rsqrt(ms + EPS) * w


def _sigmoid(x):
    return jax.nn.sigmoid(x)


def _silu(x):
    return x * jax.nn.sigmoid(x)


def _softplus(x):
    return jnp.maximum(x, 0.0) + jnp.log1p(jnp.exp(-jnp.abs(x)))


def _log_sigmoid(x):
    return jnp.minimum(x, 0.0) - jnp.log1p(jnp.exp(-jnp.abs(x)))


def _gelu_tanh(x):
    c = math.sqrt(2.0 / math.pi)
    return x * (0.5 * (1.0 + jnp.tanh(c * (x + 0.044715 * (x * x * x)))))


def _neg_expm1(x):
    return -jnp.tanh(0.5 * x) * (jnp.exp(x) + 1.0)


def _chunk_cumsum(x, chunk, pos):
    d = 1
    while d < chunk:
        x = x + jnp.where(pos >= d, _roll0(x, d), 0.0)
        d *= 2
    return x


def _chunk_last_bcast(x, chunk, pos):
    n = x.shape[0]
    y = jnp.where(pos == chunk - 1, x, 0.0)
    d = 1
    while d < chunk:
        y = y + _roll0(y, n - d)
        d *= 2
    return y


def _shift_rows(x, carry, i, row8):
    xs = _roll0(x, i)
    cs = _roll0(carry, i)
    top = jnp.where(row8 < i, cs, xs[:SUBLANES])
    return jnp.concatenate([top, xs[SUBLANES:]], axis=0)


def _causal_conv(x, carry, w_ref, b_ref):
    row8 = _iota((SUBLANES, x.shape[1]), 0)
    y = b_ref[...] + x * w_ref[CONV_W - 1:CONV_W, :]
    for i in range(1, CONV_W):
        y = y + _shift_rows(x, carry, i, row8) * w_ref[CONV_W - 1 - i:CONV_W - i, :]
    return y


def _params(*sem):
    return pltpu.CompilerParams(dimension_semantics=sem, vmem_limit_bytes=VMEM_LIMIT)


def _const_spec(shape):
    nd = len(shape)
    return pl.BlockSpec(shape, lambda *_: (0,) * nd)


PROJ_TILES_PER_STEP = 5


def _proj_kernel(x_ref, nw_ref, w_ref, o_ref, h_ref):
    @pl.when(pl.program_id(1) == 0)
    def _():
        h_ref[...] = _rms(x_ref[...], nw_ref[...]).astype(BF16)

    res = jnp.dot(h_ref[...], w_ref[...], preferred_element_type=F32)
    for j in range(PROJ_TILES_PER_STEP):
        o_ref[j] = res[:, j * TILE:(j + 1) * TILE]


def _proj(x2d, nw, w, l, tm):
    m = x2d.shape[0]
    tn = PROJ_TILES_PER_STEP * TILE
    return pl.pallas_call(
        _proj_kernel,
        grid=(m // tm, N_TILES // PROJ_TILES_PER_STEP),
        in_specs=[pl.BlockSpec((tm, D_MODEL), lambda i, j: (i, 0)),
                  _const_spec((1, D_MODEL)),
                  pl.BlockSpec((None, D_MODEL, tn), lambda i, j: (l, 0, j))],
        out_specs=pl.BlockSpec((PROJ_TILES_PER_STEP, tm, TILE), lambda i, j: (j, i, 0)),
        out_shape=jax.ShapeDtypeStruct((N_TILES, m, TILE), F32),
        scratch_shapes=[pltpu.VMEM((tm, D_MODEL), BF16)],
        compiler_params=_params("parallel", "arbitrary"),
        name="norm_proj",
    )(x2d, nw, w)


def _outproj_kernel(m0, m1, m2, m3, w_ref, x_ref, n1_ref, n2_ref, x1_ref, h2_ref):
    acc = jnp.dot(m0[...], w_ref[0], preferred_element_type=F32)
    acc = acc + jnp.dot(m1[...], w_ref[1], preferred_element_type=F32)
    acc = acc + jnp.dot(m2[...], w_ref[2], preferred_element_type=F32)
    acc = acc + jnp.dot(m3[...], w_ref[3], preferred_element_type=F32)
    x1 = x_ref[...] + _rms(acc, n1_ref[...])
    x1_ref[...] = x1
    h2_ref[...] = _rms(x1, n2_ref[...]).astype(BF16)


def _outproj(mix, w_out, l, x2d, n1, n2, tm):
    m = x2d.shape[0]
    row = lambda i: (i, 0)
    return pl.pallas_call(
        _outproj_kernel,
        grid=(m // tm,),
        in_specs=[pl.BlockSpec((tm, MIX), row)] * 4 + [
            pl.BlockSpec((None, 4, MIX, D_MODEL), lambda i: (l, 0, 0, 0)), pl.BlockSpec((tm, D_MODEL), row),
            _const_spec((1, D_MODEL)), _const_spec((1, D_MODEL))],
        out_specs=[pl.BlockSpec((tm, D_MODEL), row), pl.BlockSpec((tm, D_MODEL), row)],
        out_shape=[jax.ShapeDtypeStruct((m, D_MODEL), F32), jax.ShapeDtypeStruct((m, D_MODEL), BF16)],
        compiler_params=_params("parallel"),
        name="out_proj",
    )(*mix, w_out, x2d, n1, n2)


def _mlp_kernel(h_ref, wu_ref, wd_ref, x1_ref, nw_ref, o_ref, acc_ref):
    f = pl.program_id(1)

    @pl.when(f == 0)
    def _():
        acc_ref[...] = jnp.zeros_like(acc_ref)

    u = jnp.maximum(jnp.dot(h_ref[...], wu_ref[...], preferred_element_type=F32), 0.0)
    acc_ref[...] += jnp.dot((u * u).astype(BF16), wd_ref[...], preferred_element_type=F32)

    @pl.when(f == pl.num_programs(1) - 1)
    def _():
        o_ref[...] = x1_ref[...] + _rms(acc_ref[...], nw_ref[...])


def _mlp(h2, w_up, w_down, l, x1, nw, tm, tf):
    m = h2.shape[0]
    return pl.pallas_call(
        _mlp_kernel,
        grid=(m // tm, D_FF // tf),
        in_specs=[pl.BlockSpec((tm, D_MODEL), lambda i, f: (i, 0)),
                  pl.BlockSpec((None, D_MODEL, tf), lambda i, f: (l, 0, f)),
                  pl.BlockSpec((None, tf, D_MODEL), lambda i, f: (l, f, 0)),
                  pl.BlockSpec((tm, D_MODEL), lambda i, f: (i, 0)),
                  _const_spec((1, D_MODEL))],
        out_specs=pl.BlockSpec((tm, D_MODEL), lambda i, f: (i, 0)),
        out_shape=jax.ShapeDtypeStruct((m, D_MODEL), F32),
        scratch_shapes=[pltpu.VMEM((tm, D_MODEL), F32)],
        compiler_params=_params("parallel", "arbitrary"),
        name="mlp",
    )(h2, w_up, w_down, x1, nw)


GLA_TB = 128


def _gla_gates(ad, wg2_ref, bg_ref):
    x = _mm(ad, wg2_ref[...], pa=2) + bg_ref[...]
    return _log_sigmoid(x) * (1.0 / GLA_TEMP)


def _gla_kernel(z_ref, wg2_ref, bg_ref, nw_ref, ones_ref, o_ref, s_ref, st_ref, y_ref):
    tb = GLA_TB
    t = pl.program_id(1)

    @pl.when(t == 0)
    def _():
        st_ref[...] = jnp.zeros_like(st_ref)

    q = z_ref[0] * (GLA_DK ** -0.5)
    k = z_ref[1]
    v = jnp.concatenate([z_ref[2], z_ref[3]], axis=1)
    gate = jnp.concatenate([z_ref[4], z_ref[5]], axis=1)
    g = _gla_gates(z_ref[6], wg2_ref, bg_ref)
    pos = _iota((tb, TILE), 0) & (GLA_C - 1)
    cum = _chunk_cumsum(g, GLA_C, pos)

    y = None
    for j in range(GLA_C):
        if j == 0:
            term = q * k
            vj = v
        else:
            e = jnp.exp(jnp.where(pos >= j, cum - _roll0(cum, j), -jnp.inf))
            term = q * _roll0(k, j) * e
            vj = _roll0(v, j)
        sc = _mm(term, ones_ref[...], pa=2)
        y = sc * vj if y is None else y + sc * vj
    y_ref[...] = y

    qh = q * jnp.exp(cum)
    for c in range(tb // GLA_C):
        rows = slice(c * GLA_C, (c + 1) * GLA_C)
        cum_c = cum[rows]
        last = cum_c[GLA_C - 1:GLA_C]
        kt = k[rows] * jnp.exp(last - cum_c)
        dec = jnp.exp(last)
        for h in range(GLA_H):
            ks = slice(h * GLA_DK, (h + 1) * GLA_DK)
            vs = slice(h * GLA_DV, (h + 1) * GLA_DV)
            st = st_ref[h]
            y_ref[rows, vs] += _mm(qh[rows, ks], st, NT)
            st_ref[h] = st * dec[:, ks] + _mm(v[rows, vs], kt[:, ks], TN)

    yy = y_ref[...]
    outs = []
    for h in range(GLA_H):
        vs = slice(h * GLA_DV, (h + 1) * GLA_DV)
        outs.append(_rms(yy[:, vs], nw_ref[...]))
    o_ref[...] = (jnp.concatenate(outs, axis=1) * _silu(gate)).astype(BF16)

    @pl.when(t == pl.num_programs(1) - 1)
    def _():
        for h in range(GLA_H):
            s_ref[0, h] = st_ref[h].T


def _gla_prompt(proj, nb, nt_len, wg2, bg, nw, ones):
    tb = GLA_TB
    nt = nt_len // tb
    return pl.pallas_call(
        _gla_kernel,
        grid=(nb, nt),
        in_specs=[pl.BlockSpec((GROUP_TILES, tb, TILE), lambda b, t: (T_GLA // GROUP_TILES, b * nt + t, 0)),
                  _const_spec((TILE, TILE)), _const_spec((1, TILE)), _const_spec((1, GLA_DV)),
                  _const_spec((TILE, MIX))],
        out_specs=[pl.BlockSpec((tb, MIX), lambda b, t: (b * nt + t, 0)),
                   pl.BlockSpec((1, GLA_H, GLA_DK, GLA_DV), lambda b, t: (b, 0, 0, 0))],
        out_shape=[jax.ShapeDtypeStruct((nb * nt_len, MIX), BF16),
                   jax.ShapeDtypeStruct((nb, GLA_H, GLA_DK, GLA_DV), F32)],
        scratch_shapes=[pltpu.VMEM((GLA_H, GLA_DV, GLA_DK), F32), pltpu.VMEM((tb, MIX), F32)],
        compiler_params=_params("parallel", "arbitrary"),
        name="gla_prompt",
    )(proj, wg2, bg, nw, ones)


LRU_TB = 256


def _lru_gates(xc, wa_ref, ba_ref, wi_ref, bi_ref, lam_ref):
    ra, ri = [], []
    for j in range(2):
        xs = xc[:, j * TILE:(j + 1) * TILE]
        ra.append(_mm(xs, wa_ref[j], pa=2))
        ri.append(_mm(xs, wi_ref[j], pa=2))
    r = _sigmoid(jnp.concatenate(ra, axis=1) + ba_ref[...])
    i = _sigmoid(jnp.concatenate(ri, axis=1) + bi_ref[...])
    log_a = -LRU_C * r * _softplus(-lam_ref[...])
    a = jnp.exp(log_a)
    mult = jnp.sqrt(_neg_expm1(2.0 * log_a))
    return a, mult, i


def _lru_kernel(x0_ref, x1_ref, g0_ref, g1_ref, cw_ref, cb_ref, wa_ref, ba_ref, wi_ref, bi_ref, lam_ref,
                o_ref, h_ref, tail_ref, carry_ref, hc_ref):
    tb = LRU_TB
    t = pl.program_id(1)

    @pl.when(t == 0)
    def _():
        carry_ref[...] = jnp.zeros_like(carry_ref)
        hc_ref[...] = jnp.zeros_like(hc_ref)

    xb = jnp.concatenate([x0_ref[...], x1_ref[...]], axis=1)
    gb = jnp.concatenate([g0_ref[...], g1_ref[...]], axis=1)
    xc = _causal_conv(xb, carry_ref[...], cw_ref, cb_ref)
    carry_ref[...] = xb[tb - SUBLANES:]
    a, mult, i = _lru_gates(xc, wa_ref, ba_ref, wi_ref, bi_ref, lam_ref)
    row = _iota((tb, MIX), 0)
    mult = jnp.where((row == 0) & (t == 0), 1.0, mult)
    b = mult * i * xc
    d = 1
    while d < tb:
        m = row >= d
        b = jnp.where(m, a * _roll0(b, d) + b, b)
        a = jnp.where(m, a * _roll0(a, d), a)
        d *= 2
    h = b + a * hc_ref[0:1, :]
    hl = h[tb - 1:tb, :]
    hc_ref[...] = jnp.broadcast_to(hl, hc_ref.shape)
    o_ref[...] = (h * _gelu_tanh(gb)).astype(BF16)
    h_ref[0] = hl
    tail_ref[0] = xb[tb - SUBLANES:]


def _lru_prompt(proj, nb, nt_len, cw, cb, wa, ba, wi, bi, lam):
    tb = LRU_TB
    nt = nt_len // tb

    def tile(j):
        return pl.BlockSpec((None, tb, TILE), lambda b, t: (T_LRU + j, b * nt + t, 0))

    return pl.pallas_call(
        _lru_kernel,
        grid=(nb, nt),
        in_specs=[tile(0), tile(1), tile(2), tile(3),
                  _const_spec((CONV_W, MIX)), _const_spec((1, MIX)),
                  _const_spec((2, TILE, TILE)), _const_spec((1, MIX)),
                  _const_spec((2, TILE, TILE)), _const_spec((1, MIX)), _const_spec((1, MIX))],
        out_specs=[pl.BlockSpec((tb, MIX), lambda b, t: (b * nt + t, 0)),
                   pl.BlockSpec((1, 1, MIX), lambda b, t: (b, 0, 0)),
                   pl.BlockSpec((1, SUBLANES, MIX), lambda b, t: (b, 0, 0))],
        out_shape=[jax.ShapeDtypeStruct((nb * nt_len, MIX), BF16), jax.ShapeDtypeStruct((nb, 1, MIX), F32),
                   jax.ShapeDtypeStruct((nb, SUBLANES, MIX), F32)],
        scratch_shapes=[pltpu.VMEM((SUBLANES, MIX), F32), pltpu.VMEM((SUBLANES, MIX), F32)],
        compiler_params=_params("parallel", "arbitrary"),
        name="lru_prompt",
    )(proj, proj, proj, proj, cw, cb, wa, ba, wi, bi, lam)


RWKV_TB = 128
RWKV_P = {"g": (1, 1), "neu": (1, 1), "app": (1, 1), "chunk": (1, 1)}


def _seg_sum(x, seg_ref):
    outs = []
    for j in range(x.shape[1] // LANES):
        outs.append(_mm(x[:, j * LANES:(j + 1) * LANES], seg_ref[...], pa=3))
    return jnp.concatenate(outs, axis=1)


def _rwkv_pointwise(z, zs, mu_ref, w0_ref, w2_ref, a0_ref, a2_ref, g2_ref, kk_ref, ka_ref, seg_ref):
    zm = z + (zs - z) * mu_ref[...]
    r = zm[:, 0:MIX]
    k = zm[:, MIX:2 * MIX]
    v = zm[:, 2 * MIX:3 * MIX]
    zw = zm[:, 3 * MIX:3 * MIX + 64]
    za = zm[:, 3 * MIX + 64:3 * MIX + 128]
    zg = zm[:, 3 * MIX + 128:3 * MIX + 256]
    lw = -RWKV_DECAY * _sigmoid(w0_ref[...] + _mm(jnp.tanh(zw), w2_ref[...], pa=2))
    a = _sigmoid(a0_ref[...] + _mm(za, a2_ref[...], pa=2))
    g = _mm(_sigmoid(zg), g2_ref[...], pa=2)
    kk = k * kk_ref[...]
    kk = kk / jnp.maximum(jnp.sqrt(_seg_sum(kk * kk, seg_ref)), 1e-12)
    k = k * (1.0 + (a - 1.0) * ka_ref[...])
    return r, lw, k, v, kk, a, g


def _rwkv_finish(y, r, k, v, g, rk_ref, lnw_ref, lnb_ref, seg_ref):
    mean = _seg_sum(y, seg_ref) * (1.0 / RWKV_D)
    yc = y - mean
    var = _seg_sum(yc * yc, seg_ref) * (1.0 / RWKV_D)
    yn = yc * lax.rsqrt(var + RWKV_LN_EPS) * lnw_ref[...] + lnb_ref[...]
    bonus = _seg_sum(r * k * rk_ref[...], seg_ref) * v
    return (yn + bonus) * g


def _rwkv_kernel(z_ref, mu_ref, w0_ref, w2_ref, a0_ref, a2_ref, g2_ref, kk_ref, ka_ref, rk_ref, lnw_ref, lnb_ref,
                 seg_ref, o_ref, s_ref, tail_ref, prev_ref, st_ref, y_ref):
    tb = RWKV_TB
    cc = RWKV_C
    t = pl.program_id(1)

    @pl.when(t == 0)
    def _():
        prev_ref[...] = jnp.zeros_like(prev_ref)
        st_ref[...] = jnp.zeros_like(st_ref)

    z = jnp.concatenate([z_ref[j] for j in range(GROUP_TILES)], axis=1)
    zs = _shift_rows(z, prev_ref[...], 1, _iota((SUBLANES, RWKV_COLS), 0))
    prev_ref[...] = z[tb - SUBLANES:]
    tail_ref[0] = z[tb - SUBLANES:]
    r, lw, k, v, kk, a, g = _rwkv_pointwise(z, zs, mu_ref, w0_ref, w2_ref, a0_ref, a2_ref, g2_ref, kk_ref, ka_ref,
                                            seg_ref)
    b = kk * a

    pos = _iota((tb, MIX), 0) & (cc - 1)
    lg = _chunk_cumsum(lw, cc, pos)
    lg_end = _chunk_last_bcast(lg, cc, pos)
    gam = jnp.exp(lg)
    inv = jnp.exp(-lg)
    rg = r * gam
    kg = kk * jnp.exp(lg - lw)
    bi = b * inv
    ki = k * inv
    e_end = jnp.exp(lg_end - lg)
    bt = b * e_end
    kt = k * e_end
    g_end = jnp.exp(lg_end)

    ri = _iota((tb, tb), 0)
    ci = _iota((tb, tb), 1)
    same = (ri - (ri & (cc - 1))) == (ci - (ci & (cc - 1)))
    strict = same & (ci < ri)
    incl = same & (ci <= ri)
    eye = (ri == ci).astype(F32)
    eye_d = _iota((RWKV_D, RWKV_D), 0) == _iota((RWKV_D, RWKV_D), 1)

    hs = [slice(h * RWKV_D, (h + 1) * RWKV_D) for h in range(RWKV_H)]
    mg, mn, ma, mc_ = RWKV_P["g"], RWKV_P["neu"], RWKV_P["app"], RWKV_P["chunk"]
    vh = [v[:, ls] for ls in hs]
    gmat = [_mm(jnp.concatenate([kg[:, ls], rg[:, ls]], axis=0),
                jnp.concatenate([bi[:, ls], ki[:, ls]], axis=0), NT, *mg) for ls in hs]
    amat = [jnp.where(strict, gm[:tb, :tb], 0.0) for gm in gmat]
    bmat = [jnp.where(strict, gm[:tb, tb:], 0.0) for gm in gmat]
    pb_ = [jnp.where(incl, gm[tb:, :tb], 0.0) for gm in gmat]
    pk_ = [jnp.where(incl, gm[tb:, tb:], 0.0) for gm in gmat]
    x = [eye - a_ for a_ in amat]
    p = [_mm(a_, a_, NN, *mn) for a_ in amat]
    bv = [_mm(bm_, v_, NN, *ma) for bm_, v_ in zip(bmat, vh)]
    x = [x_ + _mm(x_, p_, NN, *mn) for x_, p_ in zip(x, p)]
    p = [_mm(p_, p_, NN, *mn) for p_ in p]
    x = [x_ + _mm(x_, p_, NN, *mn) for x_, p_ in zip(x, p)]
    p = [_mm(p_, p_, NN, *mn) for p_ in p]
    x = [x_ + _mm(x_, p_, NN, *mn) for x_, p_ in zip(x, p)]
    wu = [-_mm(x_, jnp.concatenate([kg[:, ls], bv_], axis=1), NN, *ma) for x_, ls, bv_ in zip(x, hs, bv)]
    qy = [_mm(pb, wu_, NN, *ma) for pb, wu_ in zip(pb_, wu)]
    pkv = [_mm(pk, v_, NN, *ma) for pk, v_ in zip(pk_, vh)]
    qt = [rg[:, ls] + qy_[:, :RWKV_D] for ls, qy_ in zip(hs, qy)]
    y0 = [qy_[:, RWKV_D:] + pkv_ for qy_, pkv_ in zip(qy, pkv)]
    s = [st_ref[h] for h in range(RWKV_H)]
    for c in range(tb // cc):
        rows = slice(c * cc, (c + 1) * cc)
        mcs = [jnp.where(eye_d, g_end[c * cc:c * cc + 1, ls], 0.0)
               + _mm(wu_[rows, :RWKV_D], bt[rows, ls], TN, *mc_) for ls, wu_ in zip(hs, wu)]
        ncs = [_mm(jnp.concatenate([wu_[rows, RWKV_D:], v_[rows]], axis=0),
                   jnp.concatenate([bt[rows, ls], kt[rows, ls]], axis=0), TN, *mc_) for ls, wu_, v_ in zip(hs, wu, vh)]
        for h, ls in enumerate(hs):
            y_ref[rows, ls] = _mm(qt[h][rows], s[h], NT, *mc_) + y0[h][rows]
        s = [_mm(s_, m_, NN, *mc_) + n_ for s_, m_, n_ in zip(s, mcs, ncs)]
    for h in range(RWKV_H):
        st_ref[h] = s[h]

    o_ref[...] = _rwkv_finish(y_ref[...], r, k, v, g, rk_ref, lnw_ref, lnb_ref, seg_ref).astype(BF16)

    @pl.when(t == pl.num_programs(1) - 1)
    def _():
        s_ref[0] = st_ref[...]


def _rwkv_prompt(proj, nb, nt_len, p):
    tb = RWKV_TB
    nt = nt_len // tb
    return pl.pallas_call(
        _rwkv_kernel,
        grid=(nb, nt),
        in_specs=[pl.BlockSpec((GROUP_TILES, tb, TILE), lambda b, t: (T_RWKV // GROUP_TILES, b * nt + t, 0)),
                  _const_spec((1, RWKV_COLS)), _const_spec((1, MIX)), _const_spec((64, MIX)),
                  _const_spec((1, MIX)), _const_spec((64, MIX)), _const_spec((128, MIX)),
                  _const_spec((1, MIX)), _const_spec((1, MIX)), _const_spec((1, MIX)),
                  _const_spec((1, MIX)), _const_spec((1, MIX)), _const_spec((LANES, LANES))],
        out_specs=[pl.BlockSpec((tb, MIX), lambda b, t: (b * nt + t, 0)),
                   pl.BlockSpec((1, RWKV_H, RWKV_D, RWKV_D), lambda b, t: (b, 0, 0, 0)),
                   pl.BlockSpec((1, SUBLANES, RWKV_COLS), lambda b, t: (b, 0, 0))],
        out_shape=[jax.ShapeDtypeStruct((nb * nt_len, MIX), BF16),
                   jax.ShapeDtypeStruct((nb, RWKV_H, RWKV_D, RWKV_D), F32),
                   jax.ShapeDtypeStruct((nb, SUBLANES, RWKV_COLS), F32)],
        scratch_shapes=[pltpu.VMEM((SUBLANES, RWKV_COLS), F32), pltpu.VMEM((RWKV_H, RWKV_D, RWKV_D), F32),
                        pltpu.VMEM((tb, MIX), F32)],
        compiler_params=_params("parallel", "arbitrary"),
        name="rwkv_prompt",
    )(proj, p["mu"], p["w0"], p["w2"], p["a0"], p["a2"], p["g2"], p["kk"], p["ka"], p["rk"], p["lnw"], p["lnb"],
      p["seg"])


SSD_TB = 256


def _ssd_pointwise(xbc_c, dt_raw, dtb_ref, aneg_ref):
    xbc = _silu(xbc_c)
    dt = _softplus(dt_raw + dtb_ref[...])
    return xbc, dt, dt * aneg_ref[...]


def _expand_heads(col8, width):
    n = col8.shape[0]
    return jnp.concatenate([jnp.broadcast_to(col8[:, h:h + 1], (n, width)) for h in range(SSD_H)], axis=1)


def _ssd_kernel(z_ref, cw_ref, cb_ref, dtb_ref, aneg_ref, dsk_ref, nw_ref, o_ref, s_ref, tail_ref, carry_ref, st_ref,
                y_ref):
    tb = SSD_TB
    cc = SSD_C
    t = pl.program_id(1)

    @pl.when(t == 0)
    def _():
        carry_ref[...] = jnp.zeros_like(carry_ref)
        st_ref[...] = jnp.zeros_like(st_ref)

    gate = jnp.concatenate([z_ref[0], z_ref[1]], axis=1)
    xbc_raw = jnp.concatenate([z_ref[2], z_ref[3], z_ref[4], z_ref[5]], axis=1)
    xbc_c = _causal_conv(xbc_raw, carry_ref[...], cw_ref, cb_ref)
    carry_ref[...] = xbc_raw[tb - SUBLANES:]
    tail_ref[0] = xbc_raw[tb - SUBLANES:]
    xbc, dt, dta = _ssd_pointwise(xbc_c, z_ref[6], dtb_ref, aneg_ref)
    xs = xbc[:, :MIX]
    bm = xbc[:, MIX:MIX + TILE]
    cm = xbc[:, MIX + TILE:]
    pos = _iota((tb, TILE), 0) & (cc - 1)
    cum = _chunk_cumsum(dta, cc, pos)
    xdt = xs * _expand_heads(dt, SSD_P)
    tri = _iota((cc, cc), 0) >= _iota((cc, cc), 1)

    for c in range(tb // cc):
        rows = slice(c * cc, (c + 1) * cc)
        cum_c = cum[rows, :LANES]
        cum_t = cum_c.T
        for gi in range(SSD_G):
            ns = slice(gi * SSD_N, (gi + 1) * SSD_N)
            cg = cm[rows, ns]
            bg = bm[rows, ns]
            gmat = _mm(cg, bg, NT, pa=2, pb=2)
            for hh in range(SSD_H // SSD_G):
                h = gi * (SSD_H // SSD_G) + hh
                ps = slice(h * SSD_P, (h + 1) * SSD_P)
                col = cum_c[:, h:h + 1]
                lmat = jnp.exp(jnp.where(tri, col - cum_t[h:h + 1, :], -jnp.inf))
                xh = xdt[rows, ps]
                st = st_ref[h]
                yd = _mm(gmat * lmat, xh, pa=2, pb=2)
                yo = _mm(cg, st, NT, pa=2, pb=2) * jnp.exp(col)
                last = cum_c[cc - 1:cc, h:h + 1]
                st_ref[h] = st * jnp.exp(last) + _mm(xh * jnp.exp(last - col), bg, TN, pa=2, pb=2)
                y_ref[rows, ps] = yd + yo

    y = (y_ref[...] + dsk_ref[...] * xs) * _silu(gate)
    o_ref[...] = _rms(y, nw_ref[...]).astype(BF16)

    @pl.when(t == pl.num_programs(1) - 1)
    def _():
        s_ref[0] = st_ref[...]


def _ssd_prompt(proj, nb, nt_len, cw, cb, dtb, aneg, dsk, nw):
    tb = SSD_TB
    nt = nt_len // tb
    return pl.pallas_call(
        _ssd_kernel,
        grid=(nb, nt),
        in_specs=[pl.BlockSpec((GROUP_TILES, tb, TILE), lambda b, t: (T_SSD // GROUP_TILES, b * nt + t, 0)),
                  _const_spec((CONV_W, SSD_CONV_DIM)), _const_spec((1, SSD_CONV_DIM)),
                  _const_spec((1, TILE)), _const_spec((1, TILE)), _const_spec((1, MIX)), _const_spec((1, MIX))],
        out_specs=[pl.BlockSpec((tb, MIX), lambda b, t: (b * nt + t, 0)),
                   pl.BlockSpec((1, SSD_H, SSD_P, SSD_N), lambda b, t: (b, 0, 0, 0)),
                   pl.BlockSpec((1, SUBLANES, SSD_CONV_DIM), lambda b, t: (b, 0, 0))],
        out_shape=[jax.ShapeDtypeStruct((nb * nt_len, MIX), BF16),
                   jax.ShapeDtypeStruct((nb, SSD_H, SSD_P, SSD_N), F32),
                   jax.ShapeDtypeStruct((nb, SUBLANES, SSD_CONV_DIM), F32)],
        scratch_shapes=[pltpu.VMEM((SUBLANES, SSD_CONV_DIM), F32), pltpu.VMEM((SSD_H, SSD_P, SSD_N), F32),
                        pltpu.VMEM((tb, MIX), F32)],
        compiler_params=_params("parallel", "arbitrary"),
        name="ssd_prompt",
    )(proj, cw, cb, dtb, aneg, dsk, nw)


R_GLA_V, R_GLA_GATE = 0, 512
R_RW_W, R_RW_B, R_RW_K, R_RW_KK, R_RW_R, R_RW_V, R_RW_G = 1024, 1536, 2048, 2560, 3072, 3584, 4096
R_SSD_B, R_SSD_C, R_SSD_E, R_SSD_GATE, R_SSD_X = 4608, 4864, 5120, 6144, 6656
ROW_W = 7168
C_GLA_A, C_GLA_K, C_GLA_Q, C_RW_V, C_SSD_X = 0, 256, 512, 768, 1280
COL_W = 1792
DEC_RB = 8


def _dec_prep_kernel(z_ref, lconv_ref, lh_ref, rprev_ref, sconv_ref,
                     wg2_ref, bg_ref,
                     lcw_ref, lcb_ref, wa_ref, ba_ref, wi_ref, bi_ref, lam_ref,
                     mu_ref, w0_ref, w2_ref, a0_ref, a2_ref, g2_ref, kk_ref, ka_ref, seg_ref,
                     scw_ref, scb_ref, dtb_ref, aneg_ref,
                     rows_ref, colt_ref, olru_ref, lh_out_ref, lconv_out_ref, sconv_out_ref):
    n = rows_ref.shape[0]
    q = z_ref[T_GLA + 0] * (GLA_DK ** -0.5)
    k = z_ref[T_GLA + 1]
    g = _gla_gates(z_ref[T_GLA + 6], wg2_ref, bg_ref)
    rows_ref[:, R_GLA_V:R_GLA_V + TILE] = z_ref[T_GLA + 2]
    rows_ref[:, R_GLA_V + TILE:R_GLA_V + MIX] = z_ref[T_GLA + 3]
    rows_ref[:, R_GLA_GATE:R_GLA_GATE + TILE] = z_ref[T_GLA + 4]
    rows_ref[:, R_GLA_GATE + TILE:R_GLA_GATE + MIX] = z_ref[T_GLA + 5]
    cols = [jnp.exp(g), k, q]

    xb = jnp.concatenate([z_ref[T_LRU + 0], z_ref[T_LRU + 1]], axis=1)
    gb = jnp.concatenate([z_ref[T_LRU + 2], z_ref[T_LRU + 3]], axis=1)
    xc = lcb_ref[...] + xb * lcw_ref[CONV_W - 1:CONV_W, :]
    for i in range(CONV_W - 1):
        xc = xc + lconv_ref[:, i * MIX:(i + 1) * MIX] * lcw_ref[i:i + 1, :]
    a, mult, gi = _lru_gates(xc, wa_ref, ba_ref, wi_ref, bi_ref, lam_ref)
    h = a * lh_ref[...] + mult * gi * xc
    lh_out_ref[...] = h
    olru_ref[...] = (h * _gelu_tanh(gb)).astype(BF16)
    lconv_out_ref[:, 0:2 * MIX] = lconv_ref[:, MIX:3 * MIX]
    lconv_out_ref[:, 2 * MIX:3 * MIX] = xb

    z = jnp.concatenate([z_ref[T_RWKV + j] for j in range(GROUP_TILES)], axis=1)
    r, lw, kmod, v, kk, a7, g7 = _rwkv_pointwise(z, rprev_ref[...], mu_ref, w0_ref, w2_ref, a0_ref, a2_ref, g2_ref,
                                                 kk_ref, ka_ref, seg_ref)
    rows_ref[:, R_RW_W:R_RW_W + MIX] = jnp.exp(lw)
    rows_ref[:, R_RW_B:R_RW_B + MIX] = kk * a7
    rows_ref[:, R_RW_K:R_RW_K + MIX] = kmod
    rows_ref[:, R_RW_KK:R_RW_KK + MIX] = kk
    rows_ref[:, R_RW_R:R_RW_R + MIX] = r
    rows_ref[:, R_RW_V:R_RW_V + MIX] = v
    rows_ref[:, R_RW_G:R_RW_G + MIX] = g7
    cols.append(v)

    xbc_raw = jnp.concatenate([z_ref[T_SSD + 2], z_ref[T_SSD + 3], z_ref[T_SSD + 4], z_ref[T_SSD + 5]], axis=1)
    xbc_c = scb_ref[...] + xbc_raw * scw_ref[CONV_W - 1:CONV_W, :]
    for i in range(CONV_W - 1):
        xbc_c = xbc_c + sconv_ref[:, i * SSD_CONV_DIM:(i + 1) * SSD_CONV_DIM] * scw_ref[i:i + 1, :]
    xbc, dt, dta = _ssd_pointwise(xbc_c, z_ref[T_SSD + 6], dtb_ref, aneg_ref)
    xs = xbc[:, :MIX]
    rows_ref[:, R_SSD_B:R_SSD_B + TILE] = xbc[:, MIX:MIX + TILE]
    rows_ref[:, R_SSD_C:R_SSD_C + TILE] = xbc[:, MIX + TILE:]
    rows_ref[:, R_SSD_E:R_SSD_E + SSD_H * LANES] = _expand_heads(jnp.exp(dta), LANES)
    rows_ref[:, R_SSD_GATE:R_SSD_GATE + TILE] = z_ref[T_SSD + 0]
    rows_ref[:, R_SSD_GATE + TILE:R_SSD_GATE + MIX] = z_ref[T_SSD + 1]
    rows_ref[:, R_SSD_X:R_SSD_X + MIX] = xs
    cols.append(xs * _expand_heads(dt, SSD_P))
    sconv_out_ref[:, 0:2 * SSD_CONV_DIM] = sconv_ref[:, SSD_CONV_DIM:3 * SSD_CONV_DIM]
    sconv_out_ref[:, 2 * SSD_CONV_DIM:3 * SSD_CONV_DIM] = xbc_raw

    off = 0
    for cvec in cols:
        for j in range(cvec.shape[1] // LANES):
            colt_ref[off:off + LANES, :] = cvec[:, j * LANES:(j + 1) * LANES].T
            off += LANES


def _dec_prep(zdec, lconv, lh, rprev, sconv, pp):
    n = zdec.shape[1]
    args = [zdec, lconv, lh, rprev, sconv,
            pp["gla_wg2"], pp["gla_bg"],
            pp["lru_cw"], pp["lru_cb"], pp["lru_wa"], pp["lru_ba"], pp["lru_wi"], pp["lru_bi"], pp["lru_lam"],
            pp["rwkv"]["mu"], pp["rwkv"]["w0"], pp["rwkv"]["w2"], pp["rwkv"]["a0"], pp["rwkv"]["a2"],
            pp["rwkv"]["g2"], pp["rwkv"]["kk"], pp["rwkv"]["ka"], pp["rwkv"]["seg"],
            pp["ssd_cw"], pp["ssd_cb"], pp["ssd_dtb"], pp["ssd_aneg"]]
    return pl.pallas_call(
        _dec_prep_kernel,
        out_shape=[jax.ShapeDtypeStruct((n, ROW_W), F32), jax.ShapeDtypeStruct((COL_W, n), F32),
                   jax.ShapeDtypeStruct((n, MIX), BF16), jax.ShapeDtypeStruct((n, MIX), F32),
                   jax.ShapeDtypeStruct((n, 3 * MIX), F32), jax.ShapeDtypeStruct((n, 3 * SSD_CONV_DIM), F32)],
        compiler_params=pltpu.CompilerParams(vmem_limit_bytes=VMEM_LIMIT),
        name="decode_prep",
    )(*args)


def _dec_state_kernel(colt_ref, rows_ref, sg_ref, sr_ref, ss_ref, *rest):
    sg_out, sr_out, ss_out, yg_ref, yrt_ref, yst_ref = rest[-6:]
    pid = pl.program_id(0)
    n = colt_ref.shape[1]

    @pl.when(pid == 0)
    def _():
        yrt_ref[...] = jnp.zeros_like(yrt_ref)
        yst_ref[...] = jnp.zeros_like(yst_ref)

    lane = _iota((1, n), 1)

    def body(i, carry):
        sel = lane == pid * DEC_RB + i
        col = jnp.sum(jnp.where(sel, colt_ref[...], 0.0), axis=1, keepdims=True)
        rv = rows_ref[pl.ds(i, 1), :]
        ys = []
        for h in range(GLA_H):
            ks = slice(h * GLA_DK, (h + 1) * GLA_DK)
            al = col[C_GLA_A + h * GLA_DK:C_GLA_A + (h + 1) * GLA_DK]
            kc = col[C_GLA_K + h * GLA_DK:C_GLA_K + (h + 1) * GLA_DK]
            qc = col[C_GLA_Q + h * GLA_DK:C_GLA_Q + (h + 1) * GLA_DK]
            vr = rv[:, R_GLA_V + h * GLA_DV:R_GLA_V + (h + 1) * GLA_DV]
            s = al * sg_ref[i, h] + kc * vr
            sg_out[i, h] = s
            ys.append(jnp.sum(qc * s, axis=0, keepdims=True))
        yg_ref[pl.ds(i, 1), :] = jnp.concatenate(ys, axis=1)
        for h in range(RWKV_H):
            ls = slice(h * RWKV_D, (h + 1) * RWKV_D)
            w = rv[:, R_RW_W + h * RWKV_D:R_RW_W + (h + 1) * RWKV_D]
            b = rv[:, R_RW_B + h * RWKV_D:R_RW_B + (h + 1) * RWKV_D]
            k = rv[:, R_RW_K + h * RWKV_D:R_RW_K + (h + 1) * RWKV_D]
            kk = rv[:, R_RW_KK + h * RWKV_D:R_RW_KK + (h + 1) * RWKV_D]
            r = rv[:, R_RW_R + h * RWKV_D:R_RW_R + (h + 1) * RWKV_D]
            vc = col[C_RW_V + h * RWKV_D:C_RW_V + (h + 1) * RWKV_D]
            s = sr_ref[i, h]
            sa = jnp.sum(s * kk, axis=1, keepdims=True)
            s = s * w - sa * b + vc * k
            sr_out[i, h] = s
            yc = jnp.sum(s * r, axis=1, keepdims=True)
            yrt_ref[ls, :] = jnp.where(sel, yc, yrt_ref[ls, :])
        for h in range(SSD_H):
            ps = slice(h * SSD_P, (h + 1) * SSD_P)
            gi = h // (SSD_H // SSD_G)
            e = rv[:, R_SSD_E + h * LANES:R_SSD_E + (h + 1) * LANES]
            bn = rv[:, R_SSD_B + gi * SSD_N:R_SSD_B + (gi + 1) * SSD_N]
            cn = rv[:, R_SSD_C + gi * SSD_N:R_SSD_C + (gi + 1) * SSD_N]
            xc = col[C_SSD_X + h * SSD_P:C_SSD_X + (h + 1) * SSD_P]
            s = ss_ref[i, h] * e + xc * bn
            ss_out[i, h] = s
            yc = jnp.sum(s * cn, axis=1, keepdims=True)
            yst_ref[ps, :] = jnp.where(sel, yc, yst_ref[ps, :])
        return carry

    lax.fori_loop(0, DEC_RB, body, 0)


def _dec_state(colt, rows, sg, sr, ss, l, prev):
    n = rows.shape[0]
    blk = lambda i: (l, i, 0, 0, 0)
    specs = [pl.BlockSpec((None, DEC_RB, GLA_H, GLA_DK, GLA_DV), blk),
             pl.BlockSpec((None, DEC_RB, RWKV_H, RWKV_D, RWKV_D), blk),
             pl.BlockSpec((None, DEC_RB, SSD_H, SSD_P, SSD_N), blk)]
    extra = [] if prev is None else list(prev)
    return pl.pallas_call(
        _dec_state_kernel,
        grid=(n // DEC_RB,),
        in_specs=[_const_spec((COL_W, n)), pl.BlockSpec((DEC_RB, ROW_W), lambda i: (i, 0))] + specs
        + [pl.BlockSpec(memory_space=pl.ANY)] * len(extra),
        out_specs=specs + [pl.BlockSpec((DEC_RB, MIX), lambda i: (i, 0)),
                           _const_spec((MIX, n)), _const_spec((MIX, n))],
        out_shape=[jax.ShapeDtypeStruct(sg.shape, F32), jax.ShapeDtypeStruct(sr.shape, F32),
                   jax.ShapeDtypeStruct(ss.shape, F32), jax.ShapeDtypeStruct((n, MIX), F32),
                   jax.ShapeDtypeStruct((MIX, n), F32), jax.ShapeDtypeStruct((MIX, n), F32)],
        input_output_aliases={5 + j: j for j in range(len(extra))},
        compiler_params=_params("arbitrary"),
        name="decode_state",
    )(colt, rows, sg, sr, ss, *extra)


def _dec_finish_kernel(rows_ref, yg_ref, yrt_ref, yst_ref, gnw_ref, rk_ref, lnw_ref, lnb_ref, seg_ref,
                       dsk_ref, snw_ref, og_ref, or_ref, os_ref):
    outs = []
    for h in range(GLA_H):
        outs.append(_rms(yg_ref[:, h * GLA_DV:(h + 1) * GLA_DV], gnw_ref[...]))
    og_ref[...] = (jnp.concatenate(outs, axis=1) * _silu(rows_ref[:, R_GLA_GATE:R_GLA_GATE + MIX])).astype(BF16)
    yr = jnp.concatenate([yrt_ref[j * LANES:(j + 1) * LANES, :].T for j in range(MIX // LANES)], axis=1)
    or_ref[...] = _rwkv_finish(yr, rows_ref[:, R_RW_R:R_RW_R + MIX], rows_ref[:, R_RW_K:R_RW_K + MIX],
                               rows_ref[:, R_RW_V:R_RW_V + MIX], rows_ref[:, R_RW_G:R_RW_G + MIX],
                               rk_ref, lnw_ref, lnb_ref, seg_ref).astype(BF16)
    ysd = jnp.concatenate([yst_ref[j * LANES:(j + 1) * LANES, :].T for j in range(MIX // LANES)], axis=1)
    y = (ysd + dsk_ref[...] * rows_ref[:, R_SSD_X:R_SSD_X + MIX]) * _silu(rows_ref[:, R_SSD_GATE:R_SSD_GATE + MIX])
    os_ref[...] = _rms(y, snw_ref[...]).astype(BF16)


def _dec_finish(rows, yg, yrt, yst, pp):
    n = rows.shape[0]
    o = jax.ShapeDtypeStruct((n, MIX), BF16)
    return pl.pallas_call(
        _dec_finish_kernel,
        out_shape=[o, o, o],
        compiler_params=pltpu.CompilerParams(vmem_limit_bytes=VMEM_LIMIT),
        name="decode_finish",
    )(rows, yg, yrt, yst, pp["gla_nw"], pp["rwkv"]["rk"], pp["rwkv"]["lnw"], pp["rwkv"]["lnb"], pp["rwkv"]["seg"],
      pp["ssd_dsk"], pp["ssd_nw"])


def _block_diag(w4):
    out = jnp.zeros((TILE, TILE), w4.dtype)
    for j in range(4):
        out = out.at[j * 64:(j + 1) * 64, j * 64:(j + 1) * 64].set(w4[j])
    return out


def _prep_weights(w_in, w_out, w_up, w_down):
    o_lru = GLA_COLS
    o_rwkv = GLA_COLS + LRU_COLS
    o_ssd = o_rwkv + RWKV_COLS
    zpad = lambda n: jnp.zeros((DEPTH, D_MODEL, n), w_in.dtype)
    w_perm = jnp.concatenate([
        w_in[:, :, 0:GLA_COLS], zpad(GROUP_TILES * TILE - GLA_COLS),
        w_in[:, :, o_rwkv:o_rwkv + RWKV_COLS],
        w_in[:, :, o_ssd:o_ssd + SSD_COLS], zpad(GROUP_TILES * TILE - SSD_COLS),
        w_in[:, :, o_lru:o_lru + LRU_COLS]], axis=2).astype(BF16)
    return {"w_in": w_perm,
            "w_out": w_out.reshape(DEPTH, 4, MIX, D_MODEL).astype(BF16),
            "w_up": w_up.astype(BF16),
            "w_down": w_down.astype(BF16)}


def _prep_layer(l, norm_mix_pre, norm_mix_post, norm_mlp_pre, norm_mlp_post,
                gla_w_gate2, gla_b_gate, gla_norm, lru_conv_w, lru_conv_b, lru_w_a, lru_b_a, lru_w_i, lru_b_i,
                lru_lambda, rwkv_mu, rwkv_w0, rwkv_w2, rwkv_a0, rwkv_a2, rwkv_g2, rwkv_k_k, rwkv_k_a, rwkv_r_k,
                rwkv_ln_w, rwkv_ln_b, ssd_conv_w, ssd_conv_b, ssd_dt_bias, ssd_a_log, ssd_d, ssd_norm):
    row = lambda a: a.reshape(1, -1).astype(F32)
    ii = jnp.arange(TILE)[:, None]
    jj = jnp.arange(MIX)[None, :]
    seg_i = jnp.arange(LANES)
    pp = {
        "n_mix_pre": row(norm_mix_pre[l]), "n_mix_post": row(norm_mix_post[l]),
        "n_mlp_pre": row(norm_mlp_pre[l]), "n_mlp_post": row(norm_mlp_post[l]),
        "gla_wg2": jnp.zeros((TILE, TILE), F32).at[:GLA_RANK, :].set(gla_w_gate2[l]).astype(BF16),
        "gla_bg": row(gla_b_gate[l]),
        "gla_nw": row(gla_norm[l]),
        "gla_ones": (ii // GLA_DK == jj // GLA_DV).astype(BF16),
        "lru_cw": lru_conv_w[l].astype(F32), "lru_cb": row(lru_conv_b[l]),
        "lru_wa": jnp.stack([_block_diag(lru_w_a[l, 0:4]), _block_diag(lru_w_a[l, 4:8])]).astype(BF16),
        "lru_wi": jnp.stack([_block_diag(lru_w_i[l, 0:4]), _block_diag(lru_w_i[l, 4:8])]).astype(BF16),
        "lru_ba": row(lru_b_a[l]), "lru_bi": row(lru_b_i[l]), "lru_lam": row(lru_lambda[l]),
        "rwkv": {
            "mu": row(rwkv_mu[l]), "w0": row(rwkv_w0[l]), "w2": rwkv_w2[l].astype(BF16),
            "a0": row(rwkv_a0[l]), "a2": rwkv_a2[l].astype(BF16), "g2": rwkv_g2[l].astype(BF16),
            "kk": row(rwkv_k_k[l]), "ka": row(rwkv_k_a[l]), "rk": row(rwkv_r_k[l]),
            "lnw": row(rwkv_ln_w[l]), "lnb": row(rwkv_ln_b[l]),
            "seg": (seg_i[:, None] // RWKV_D == seg_i[None, :] // RWKV_D).astype(BF16),
        },
        "ssd_cw": ssd_conv_w[l].astype(F32), "ssd_cb": row(ssd_conv_b[l]),
        "ssd_dtb": jnp.zeros((1, TILE), F32).at[0, :SSD_H].set(ssd_dt_bias[l]),
        "ssd_aneg": jnp.zeros((1, TILE), F32).at[0, :SSD_H].set(-jnp.exp(ssd_a_log[l].astype(F32))),
        "ssd_dsk": jnp.repeat(ssd_d[l].astype(F32), SSD_P).reshape(1, MIX),
        "ssd_nw": row(ssd_norm[l]),
    }
    return pp


def _layer_prompt(x2d, nb, nt_len, wts, l, pp, tm_proj=512, tm_out=256, tm_mlp=512, tf=512):
    proj = _proj(x2d, pp["n_mix_pre"], wts["w_in"], l, tm_proj)
    o_gla, s_gla = _gla_prompt(proj, nb, nt_len, pp["gla_wg2"], pp["gla_bg"], pp["gla_nw"], pp["gla_ones"])
    o_lru, h_lru, lru_tail = _lru_prompt(proj, nb, nt_len, pp["lru_cw"], pp["lru_cb"], pp["lru_wa"], pp["lru_ba"],
                                         pp["lru_wi"], pp["lru_bi"], pp["lru_lam"])
    o_rwkv, s_rwkv, rwkv_tail = _rwkv_prompt(proj, nb, nt_len, pp["rwkv"])
    o_ssd, s_ssd, ssd_tail = _ssd_prompt(proj, nb, nt_len, pp["ssd_cw"], pp["ssd_cb"], pp["ssd_dtb"],
                                         pp["ssd_aneg"], pp["ssd_dsk"], pp["ssd_nw"])
    x1, h2 = _outproj((o_gla, o_lru, o_rwkv, o_ssd), wts["w_out"], l, x2d, pp["n_mix_post"], pp["n_mlp_pre"], tm_out)
    x2 = _mlp(h2, wts["w_up"], wts["w_down"], l, x1, pp["n_mlp_post"], tm_mlp, tf)
    keep = SUBLANES - (CONV_W - 1)
    states = (s_gla, h_lru[:, 0, :], lru_tail[:, keep:, :], rwkv_tail[:, SUBLANES - 1, :], s_rwkv,
              ssd_tail[:, keep:, :], s_ssd)
    return x2, states


def _layer_decode(x2d, big, small, wts, l, pp, prev, tf=512):
    sg, sr, ss = big
    lh, lconv, rprev, sconv = small
    n = x2d.shape[0]
    proj = _proj(x2d, pp["n_mix_pre"], wts["w_in"], l, n)
    rows, colt, o_lru, lh_new, lconv_new, sconv_new = _dec_prep(
        proj, lconv.reshape(n, -1), lh, rprev, sconv.reshape(n, -1), pp)
    sg_new, sr_new, ss_new, yg, yrt, yst = _dec_state(colt, rows, sg, sr, ss, l, prev)
    o_gla, o_rwkv, o_ssd = _dec_finish(rows, yg, yrt, yst, pp)
    x1, h2 = _outproj((o_gla, o_lru, o_rwkv, o_ssd), wts["w_out"], l, x2d, pp["n_mix_post"], pp["n_mlp_pre"], n)
    x2 = _mlp(h2, wts["w_up"], wts["w_down"], l, x1, pp["n_mlp_post"], n, tf)
    rshift_new = jnp.transpose(proj[T_RWKV:T_RWKV + GROUP_TILES], (1, 0, 2)).reshape(n, RWKV_COLS)
    small_new = (lh_new, lconv_new.reshape(n, CONV_W - 1, MIX), rshift_new,
                 sconv_new.reshape(n, CONV_W - 1, SSD_CONV_DIM))
    return x2, (sg_new, sr_new, ss_new), small_new


def kernel(x_prompt, x_sample, state_gla, state_lru, cache_lru_conv, cache_rwkv_shift, state_rwkv, cache_ssd_conv, state_ssd, norm_mix_pre, norm_mix_post, norm_mlp_pre, norm_mlp_post, w_in, w_out, w_up, w_down, gla_w_gate2, gla_b_gate, gla_norm, lru_conv_w, lru_conv_b, lru_w_a, lru_b_a, lru_w_i, lru_b_i, lru_lambda, rwkv_mu, rwkv_w0, rwkv_w2, rwkv_a0, rwkv_a2, rwkv_g2, rwkv_k_k, rwkv_k_a, rwkv_r_k, rwkv_ln_w, rwkv_ln_b, ssd_conv_w, ssd_conv_b, ssd_dt_bias, ssd_a_log, ssd_d, ssd_norm):
    nb, nt_len, _ = x_prompt.shape
    nd = x_sample.shape[0]
    yp = x_prompt.reshape(nb * nt_len, D_MODEL)
    ys = x_sample.reshape(nd, D_MODEL)
    wts = _prep_weights(w_in, w_out, w_up, w_down)
    big = (state_gla, state_rwkv, state_ssd)
    new_p, small_s, big_s = [], [], None
    for l in range(DEPTH):
        pp = _prep_layer(l, norm_mix_pre, norm_mix_post, norm_mlp_pre, norm_mlp_post,
                         gla_w_gate2, gla_b_gate, gla_norm, lru_conv_w, lru_conv_b, lru_w_a, lru_b_a, lru_w_i,
                         lru_b_i, lru_lambda, rwkv_mu, rwkv_w0, rwkv_w2, rwkv_a0, rwkv_a2, rwkv_g2, rwkv_k_k,
                         rwkv_k_a, rwkv_r_k, rwkv_ln_w, rwkv_ln_b, ssd_conv_w, ssd_conv_b, ssd_dt_bias, ssd_a_log,
                         ssd_d, ssd_norm)
        yp, st_p = _layer_prompt(yp, nb, nt_len, wts, l, pp)
        small = (state_lru[l], cache_lru_conv[l], cache_rwkv_shift[l], cache_ssd_conv[l])
        ys, big_s, sm = _layer_decode(ys, big, small, wts, l, pp, big_s)
        new_p.append(st_p)
        small_s.append(sm)

    def stack(sts, i):
        return jnp.stack([s[i] for s in sts], axis=0)

    gla_s, rwkv_s, ssd_s = big_s
    return (yp.reshape(nb, nt_len, D_MODEL), ys.reshape(nd, 1, D_MODEL),
            stack(new_p, 0), gla_s, stack(new_p, 1), stack(small_s, 0),
            stack(new_p, 2), stack(small_s, 1), stack(new_p, 3), stack(small_s, 2),
            stack(new_p, 4), rwkv_s, stack(new_p, 5), stack(small_s, 3),
            stack(new_p, 6), ssd_s)
```

```python
import functools
import math

import jax
import jax.numpy as jnp
from jax import lax
from jax.experimental import pallas as pl
from jax.experimental.pallas import tpu as pltpu

F32 = jnp.float32
BF16 = jnp.bfloat16

D_MODEL = 2048
D_FF = 4 * D_MODEL
DEPTH = 2
EPS = 1e-6
MIX = D_MODEL // 4
CONV_W = 4
GLA_H, GLA_DK, GLA_DV, GLA_RANK, GLA_TEMP, GLA_C = 4, 64, 128, 16, 16.0, 16
LRU_C = 8.0
RWKV_H, RWKV_D = 8, 64
RWKV_DECAY = math.exp(-0.5)
RWKV_LN_EPS = 64e-5
RWKV_C = 16
SSD_H, SSD_P, SSD_G, SSD_N, SSD_C = 8, 64, 2, 128, 64
GLA_COLS = 2 * GLA_H * GLA_DK + 2 * MIX + GLA_RANK
LRU_COLS = 2 * MIX
RWKV_COLS = 3 * MIX + 64 + 64 + 128
SSD_CONV_DIM = MIX + 2 * SSD_G * SSD_N
SSD_COLS = MIX + SSD_CONV_DIM + SSD_H

LANES = 128
SUBLANES = 8
TILE = 2 * LANES
GROUP_TILES = 7
T_GLA, T_RWKV, T_SSD, T_LRU = 0, 7, 14, 21
N_TILES = 25
VMEM_LIMIT = 56 * 1024 * 1024

NN = (((1,), (0,)), ((), ()))
NT = (((1,), (1,)), ((), ()))
TN = (((0,), (0,)), ((), ()))


def _split(x, n):
    if x.dtype == BF16:
        return [x]
    parts, r = [], x
    for i in range(n):
        p = r.astype(BF16)
        parts.append(p)
        if i + 1 < n:
            r = r - p.astype(F32)
    return parts


def _mm(a, b, dn=NN, pa=1, pb=1):
    aa, bb = _split(a, pa), _split(b, pb)
    acc = None
    for i, x in enumerate(aa):
        for j, y in enumerate(bb):
            if i + j >= max(len(aa), len(bb)):
                continue
            t = lax.dot_general(x, y, dn, preferred_element_type=F32)
            acc = t if acc is None else acc + t
    return acc


def _iota(shape, dim):
    return lax.broadcasted_iota(jnp.int32, shape, dim)


def _roll0(x, s):
    n = x.shape[0]
    s = s % n
    return x if s == 0 else pltpu.roll(x, s, 0)


def _rms(x, w):
    ms = jnp.mean(x * x, axis=-1, keepdims=True)
    return x * lax.rsqrt(ms + EPS) * w


def _sigmoid(x):
    return jax.nn.sigmoid(x)


def _silu(x):
    return x * jax.nn.sigmoid(x)


def _softplus(x):
    return jnp.maximum(x, 0.0) + jnp.log1p(jnp.exp(-jnp.abs(x)))


def _log_sigmoid(x):
    return jnp.minimum(x, 0.0) - jnp.log1p(jnp.exp(-jnp.abs(x)))


def _gelu_tanh(x):
    c = math.sqrt(2.0 / math.pi)
    return x * (0.5 * (1.0 + jnp.tanh(c * (x + 0.044715 * (x * x * x)))))


def _neg_expm1(x):
    return -jnp.tanh(0.5 * x) * (jnp.exp(x) + 1.0)


def _chunk_cumsum(x, chunk, pos):
    d = 1
    while d < chunk:
        x = x + jnp.where(pos >= d, _roll0(x, d), 0.0)
        d *= 2
    return x


def _chunk_last_bcast(x, chunk, pos):
    n = x.shape[0]
    y = jnp.where(pos == chunk - 1, x, 0.0)
    d = 1
    while d < chunk:
        y = y + _roll0(y, n - d)
        d *= 2
    return y


def _shift_rows(x, carry, i, row8):
    xs = _roll0(x, i)
    cs = _roll0(carry, i)
    top = jnp.where(row8 < i, cs, xs[:SUBLANES])
    return jnp.concatenate([top, xs[SUBLANES:]], axis=0)


def _causal_conv(x, carry, w_ref, b_ref):
    row8 = _iota((SUBLANES, x.shape[1]), 0)
    y = b_ref[...] + x * w_ref[CONV_W - 1:CONV_W, :]
    for i in range(1, CONV_W):
        y = y + _shift_rows(x, carry, i, row8) * w_ref[CONV_W - 1 - i:CONV_W - i, :]
    return y


def _params(*sem):
    return pltpu.CompilerParams(dimension_semantics=sem, vmem_limit_bytes=VMEM_LIMIT)


def _const_spec(shape):
    nd = len(shape)
    return pl.BlockSpec(shape, lambda *_: (0,) * nd)


PROJ_TILES_PER_STEP = 5


def _proj_kernel(x_ref, nw_ref, w_ref, o_ref, h_ref):
    @pl.when(pl.program_id(1) == 0)
    def _():
        h_ref[...] = _rms(x_ref[...], nw_ref[...]).astype(BF16)

    res = lax.dot_general(h_ref[...], w_ref[...], NT, preferred_element_type=F32)
    for j in range(PROJ_TILES_PER_STEP):
        o_ref[j] = res[:, j * TILE:(j + 1) * TILE]


def _proj(x2d, nw, w, l, tm):
    m = x2d.shape[0]
    tn = PROJ_TILES_PER_STEP * TILE
    return pl.pallas_call(
        _proj_kernel,
        grid=(m // tm, N_TILES // PROJ_TILES_PER_STEP),
        in_specs=[pl.BlockSpec((tm, D_MODEL), lambda i, j: (i, 0)),
                  _const_spec((1, D_MODEL)),
                  pl.BlockSpec((None, tn, D_MODEL), lambda i, j: (l, j, 0))],
        out_specs=pl.BlockSpec((PROJ_TILES_PER_STEP, tm, TILE), lambda i, j: (j, i, 0)),
        out_shape=jax.ShapeDtypeStruct((N_TILES, m, TILE), F32),
        scratch_shapes=[pltpu.VMEM((tm, D_MODEL), BF16)],
        compiler_params=_params("parallel", "arbitrary"),
        name="norm_proj",
    )(x2d, nw, w)


def _outproj_kernel(m0, m1, m2, m3, w_ref, x_ref, n1_ref, n2_ref, x1_ref, h2_ref):
    acc = jnp.dot(m0[...], w_ref[0], preferred_element_type=F32)
    acc = acc + jnp.dot(m1[...], w_ref[1], preferred_element_type=F32)
    acc = acc + jnp.dot(m2[...], w_ref[2], preferred_element_type=F32)
    acc = acc + jnp.dot(m3[...], w_ref[3], preferred_element_type=F32)
    x1 = x_ref[...] + _rms(acc, n1_ref[...])
    x1_ref[...] = x1
    h2_ref[...] = _rms(x1, n2_ref[...]).astype(BF16)


def _outproj(mix, w_out, l, x2d, n1, n2, tm):
    m = x2d.shape[0]
    row = lambda i: (i, 0)
    return pl.pallas_call(
        _outproj_kernel,
        grid=(m // tm,),
        in_specs=[pl.BlockSpec((tm, MIX), row)] * 4 + [
            pl.BlockSpec((None, 4, MIX, D_MODEL), lambda i: (l, 0, 0, 0)), pl.BlockSpec((tm, D_MODEL), row),
            _const_spec((1, D_MODEL)), _const_spec((1, D_MODEL))],
        out_specs=[pl.BlockSpec((tm, D_MODEL), row), pl.BlockSpec((tm, D_MODEL), row)],
        out_shape=[jax.ShapeDtypeStruct((m, D_MODEL), F32), jax.ShapeDtypeStruct((m, D_MODEL), BF16)],
        compiler_params=_params("parallel"),
        name="out_proj",
    )(*mix, w_out, x2d, n1, n2)


def _mlp_kernel(h_ref, wu_ref, wd_ref, x1_ref, nw_ref, o_ref, acc_ref):
    f = pl.program_id(1)

    @pl.when(f == 0)
    def _():
        acc_ref[...] = jnp.zeros_like(acc_ref)

    u = jnp.maximum(jnp.dot(h_ref[...], wu_ref[...], preferred_element_type=F32), 0.0)
    acc_ref[...] += jnp.dot((u * u).astype(BF16), wd_ref[...], preferred_element_type=F32)

    @pl.when(f == pl.num_programs(1) - 1)
    def _():
        o_ref[...] = x1_ref[...] + _rms(acc_ref[...], nw_ref[...])


def _mlp(h2, w_up, w_down, l, x1, nw, tm, tf):
    m = h2.shape[0]
    return pl.pallas_call(
        _mlp_kernel,
        grid=(m // tm, D_FF // tf),
        in_specs=[pl.BlockSpec((tm, D_MODEL), lambda i, f: (i, 0)),
                  pl.BlockSpec((None, D_MODEL, tf), lambda i, f: (l, 0, f)),
                  pl.BlockSpec((None, tf, D_MODEL), lambda i, f: (l, f, 0)),
                  pl.BlockSpec((tm, D_MODEL), lambda i, f: (i, 0)),
                  _const_spec((1, D_MODEL))],
        out_specs=pl.BlockSpec((tm, D_MODEL), lambda i, f: (i, 0)),
        out_shape=jax.ShapeDtypeStruct((m, D_MODEL), F32),
        scratch_shapes=[pltpu.VMEM((tm, D_MODEL), F32)],
        compiler_params=_params("parallel", "arbitrary"),
        name="mlp",
    )(h2, w_up, w_down, x1, nw)


GLA_TB = 128


def _gla_gates(ad, wg2_ref, bg_ref):
    x = _mm(ad, wg2_ref[...], pa=2) + bg_ref[...]
    return _log_sigmoid(x) * (1.0 / GLA_TEMP)


def _gla_kernel(z_ref, wg2_ref, bg_ref, nw_ref, ones_ref, o_ref, s_ref, st_ref, y_ref):
    tb = GLA_TB
    t = pl.program_id(1)

    @pl.when(t == 0)
    def _():
        st_ref[...] = jnp.zeros_like(st_ref)

    q = z_ref[0] * (GLA_DK ** -0.5)
    k = z_ref[1]
    v = jnp.concatenate([z_ref[2], z_ref[3]], axis=1)
    gate = jnp.concatenate([z_ref[4], z_ref[5]], axis=1)
    g = _gla_gates(z_ref[6], wg2_ref, bg_ref)
    pos = _iota((tb, TILE), 0) & (GLA_C - 1)
    cum = _chunk_cumsum(g, GLA_C, pos)

    y = None
    for j in range(GLA_C):
        if j == 0:
            term = q * k
            vj = v
        else:
            e = jnp.exp(jnp.where(pos >= j, cum - _roll0(cum, j), -jnp.inf))
            term = q * _roll0(k, j) * e
            vj = _roll0(v, j)
        sc = _mm(term, ones_ref[...])
        y = sc * vj if y is None else y + sc * vj
    y_ref[...] = y

    qh = q * jnp.exp(cum)
    for c in range(tb // GLA_C):
        rows = slice(c * GLA_C, (c + 1) * GLA_C)
        cum_c = cum[rows]
        last = cum_c[GLA_C - 1:GLA_C]
        kt = k[rows] * jnp.exp(last - cum_c)
        dec = jnp.exp(last)
        for h in range(GLA_H):
            ks = slice(h * GLA_DK, (h + 1) * GLA_DK)
            vs = slice(h * GLA_DV, (h + 1) * GLA_DV)
            st = st_ref[h]
            y_ref[rows, vs] += _mm(qh[rows, ks], st, NT)
            st_ref[h] = st * dec[:, ks] + _mm(v[rows, vs], kt[:, ks], TN)

    yy = y_ref[...]
    outs = []
    for h in range(GLA_H):
        vs = slice(h * GLA_DV, (h + 1) * GLA_DV)
        outs.append(_rms(yy[:, vs], nw_ref[...]))
    o_ref[...] = (jnp.concatenate(outs, axis=1) * _silu(gate)).astype(BF16)

    @pl.when(t == pl.num_programs(1) - 1)
    def _():
        for h in range(GLA_H):
            s_ref[0, h] = st_ref[h].T


def _gla_prompt(proj, nb, nt_len, wg2, bg, nw, ones):
    tb = GLA_TB
    nt = nt_len // tb
    return pl.pallas_call(
        _gla_kernel,
        grid=(nb, nt),
        in_specs=[pl.BlockSpec((GROUP_TILES, tb, TILE), lambda b, t: (T_GLA // GROUP_TILES, b * nt + t, 0)),
                  _const_spec((TILE, TILE)), _const_spec((1, TILE)), _const_spec((1, GLA_DV)),
                  _const_spec((TILE, MIX))],
        out_specs=[pl.BlockSpec((tb, MIX), lambda b, t: (b * nt + t, 0)),
                   pl.BlockSpec((1, GLA_H, GLA_DK, GLA_DV), lambda b, t: (b, 0, 0, 0))],
        out_shape=[jax.ShapeDtypeStruct((nb * nt_len, MIX), BF16),
                   jax.ShapeDtypeStruct((nb, GLA_H, GLA_DK, GLA_DV), F32)],
        scratch_shapes=[pltpu.VMEM((GLA_H, GLA_DV, GLA_DK), F32), pltpu.VMEM((tb, MIX), F32)],
        compiler_params=_params("parallel", "arbitrary"),
        name="gla_prompt",
    )(proj, wg2, bg, nw, ones)


LRU_TB = 256


def _lru_gates(xc, wa_ref, ba_ref, wi_ref, bi_ref, lam_ref):
    ra, ri = [], []
    for j in range(2):
        xs = xc[:, j * TILE:(j + 1) * TILE]
        ra.append(_mm(xs, wa_ref[j], pa=2))
        ri.append(_mm(xs, wi_ref[j], pa=2))
    r = _sigmoid(jnp.concatenate(ra, axis=1) + ba_ref[...])
    i = _sigmoid(jnp.concatenate(ri, axis=1) + bi_ref[...])
    log_a = -LRU_C * r * _softplus(-lam_ref[...])
    a = jnp.exp(log_a)
    mult = jnp.sqrt(_neg_expm1(2.0 * log_a))
    return a, mult, i


def _lru_kernel(x0_ref, x1_ref, g0_ref, g1_ref, cw_ref, cb_ref, wa_ref, ba_ref, wi_ref, bi_ref, lam_ref,
                o_ref, h_ref, tail_ref, carry_ref, hc_ref):
    tb = LRU_TB
    t = pl.program_id(1)

    @pl.when(t == 0)
    def _():
        carry_ref[...] = jnp.zeros_like(carry_ref)
        hc_ref[...] = jnp.zeros_like(hc_ref)

    xb = jnp.concatenate([x0_ref[...], x1_ref[...]], axis=1)
    gb = jnp.concatenate([g0_ref[...], g1_ref[...]], axis=1)
    xc = _causal_conv(xb, carry_ref[...], cw_ref, cb_ref)
    carry_ref[...] = xb[tb - SUBLANES:]
    a, mult, i = _lru_gates(xc, wa_ref, ba_ref, wi_ref, bi_ref, lam_ref)
    row = _iota((tb, MIX), 0)
    mult = jnp.where((row == 0) & (t == 0), 1.0, mult)
    b = mult * i * xc
    d = 1
    while d < tb:
        m = row >= d
        b = jnp.where(m, a * _roll0(b, d) + b, b)
        a = jnp.where(m, a * _roll0(a, d), a)
        d *= 2
    h = b + a * hc_ref[0:1, :]
    hl = h[tb - 1:tb, :]
    hc_ref[...] = jnp.broadcast_to(hl, hc_ref.shape)
    o_ref[...] = (h * _gelu_tanh(gb)).astype(BF16)
    h_ref[0] = hl
    tail_ref[0] = xb[tb - SUBLANES:]


def _lru_prompt(proj, nb, nt_len, cw, cb, wa, ba, wi, bi, lam):
    tb = LRU_TB
    nt = nt_len // tb

    def tile(j):
        return pl.BlockSpec((None, tb, TILE), lambda b, t: (T_LRU + j, b * nt + t, 0))

    return pl.pallas_call(
        _lru_kernel,
        grid=(nb, nt),
        in_specs=[tile(0), tile(1), tile(2), tile(3),
                  _const_spec((CONV_W, MIX)), _const_spec((1, MIX)),
                  _const_spec((2, TILE, TILE)), _const_spec((1, MIX)),
                  _const_spec((2, TILE, TILE)), _const_spec((1, MIX)), _const_spec((1, MIX))],
        out_specs=[pl.BlockSpec((tb, MIX), lambda b, t: (b * nt + t, 0)),
                   pl.BlockSpec((1, 1, MIX), lambda b, t: (b, 0, 0)),
                   pl.BlockSpec((1, SUBLANES, MIX), lambda b, t: (b, 0, 0))],
        out_shape=[jax.ShapeDtypeStruct((nb * nt_len, MIX), BF16), jax.ShapeDtypeStruct((nb, 1, MIX), F32),
                   jax.ShapeDtypeStruct((nb, SUBLANES, MIX), F32)],
        scratch_shapes=[pltpu.VMEM((SUBLANES, MIX), F32), pltpu.VMEM((SUBLANES, MIX), F32)],
        compiler_params=_params("parallel", "arbitrary"),
        name="lru_prompt",
    )(proj, proj, proj, proj, cw, cb, wa, ba, wi, bi, lam)


RWKV_TB = 128
RWKV_P = {"g": (1, 1), "neu": (1, 1), "app": (1, 1), "chunk": (1, 1)}


def _seg_sum(x, seg_ref):
    outs = []
    for j in range(x.shape[1] // LANES):
        outs.append(_mm(x[:, j * LANES:(j + 1) * LANES], seg_ref[...], pa=2))
    return jnp.concatenate(outs, axis=1)


def _rwkv_pointwise(z, zs, mu_ref, w0_ref, w2_ref, a0_ref, a2_ref, g2_ref, kk_ref, ka_ref, seg_ref):
    zm = z + (zs - z) * mu_ref[...]
    r = zm[:, 0:MIX]
    k = zm[:, MIX:2 * MIX]
    v = zm[:, 2 * MIX:3 * MIX]
    zw = zm[:, 3 * MIX:3 * MIX + 64]
    za = zm[:, 3 * MIX + 64:3 * MIX + 128]
    zg = zm[:, 3 * MIX + 128:3 * MIX + 256]
    lw = -RWKV_DECAY * _sigmoid(w0_ref[...] + _mm(jnp.tanh(zw), w2_ref[...], pa=2))
    a = _sigmoid(a0_ref[...] + _mm(za, a2_ref[...], pa=2))
    g = _mm(_sigmoid(zg), g2_ref[...], pa=2)
    kk = k * kk_ref[...]
    kk = kk / jnp.maximum(jnp.sqrt(_seg_sum(kk * kk, seg_ref)), 1e-12)
    k = k * (1.0 + (a - 1.0) * ka_ref[...])
    return r, lw, k, v, kk, a, g


def _rwkv_finish(y, r, k, v, g, rk_ref, lnw_ref, lnb_ref, seg_ref):
    mean = _seg_sum(y, seg_ref) * (1.0 / RWKV_D)
    yc = y - mean
    var = _seg_sum(yc * yc, seg_ref) * (1.0 / RWKV_D)
    yn = yc * lax.rsqrt(var + RWKV_LN_EPS) * lnw_ref[...] + lnb_ref[...]
    bonus = _seg_sum(r * k * rk_ref[...], seg_ref) * v
    return (yn + bonus) * g


def _rwkv_kernel(z_ref, mu_ref, w0_ref, w2_ref, a0_ref, a2_ref, g2_ref, kk_ref, ka_ref, rk_ref, lnw_ref, lnb_ref,
                 seg_ref, o_ref, s_ref, tail_ref, prev_ref, st_ref, y_ref):
    tb = RWKV_TB
    cc = RWKV_C
    t = pl.program_id(1)

    @pl.when(t == 0)
    def _():
        prev_ref[...] = jnp.zeros_like(prev_ref)
        st_ref[...] = jnp.zeros_like(st_ref)

    z = jnp.concatenate([z_ref[j] for j in range(GROUP_TILES)], axis=1)
    zs = _shift_rows(z, prev_ref[...], 1, _iota((SUBLANES, RWKV_COLS), 0))
    prev_ref[...] = z[tb - SUBLANES:]
    tail_ref[0] = z[tb - SUBLANES:]
    r, lw, k, v, kk, a, g = _rwkv_pointwise(z, zs, mu_ref, w0_ref, w2_ref, a0_ref, a2_ref, g2_ref, kk_ref, ka_ref,
                                            seg_ref)
    b = kk * a

    pos = _iota((tb, MIX), 0) & (cc - 1)
    lg = _chunk_cumsum(lw, cc, pos)
    lg_end = _chunk_last_bcast(lg, cc, pos)
    gam = jnp.exp(lg)
    inv = jnp.exp(-lg)
    rg = r * gam
    kg = kk * jnp.exp(lg - lw)
    bi = b * inv
    ki = k * inv
    e_end = jnp.exp(lg_end - lg)
    bt = b * e_end
    kt = k * e_end
    g_end = jnp.exp(lg_end)

    ri = _iota((tb, tb), 0)
    ci = _iota((tb, tb), 1)
    same = (ri - (ri & (cc - 1))) == (ci - (ci & (cc - 1)))
    strict = same & (ci < ri)
    incl = same & (ci <= ri)
    eye = (ri == ci).astype(F32)
    eye_d = _iota((RWKV_D, RWKV_D), 0) == _iota((RWKV_D, RWKV_D), 1)

    hs = [slice(h * RWKV_D, (h + 1) * RWKV_D) for h in range(RWKV_H)]
    mg, mn, ma, mc_ = RWKV_P["g"], RWKV_P["neu"], RWKV_P["app"], RWKV_P["chunk"]
    vh = [v[:, ls] for ls in hs]
    gmat = [_mm(jnp.concatenate([kg[:, ls], rg[:, ls]], axis=0),
                jnp.concatenate([bi[:, ls], ki[:, ls]], axis=0), NT, *mg) for ls in hs]
    amat = [jnp.where(strict, gm[:tb, :tb], 0.0) for gm in gmat]
    bmat = [jnp.where(strict, gm[:tb, tb:], 0.0) for gm in gmat]
    pb_ = [jnp.where(incl, gm[tb:, :tb], 0.0) for gm in gmat]
    pk_ = [jnp.where(incl, gm[tb:, tb:], 0.0) for gm in gmat]
    x = [eye - a_ for a_ in amat]
    p = [_mm(a_, a_, NN, *mn) for a_ in amat]
    bv = [_mm(bm_, v_, NN, *ma) for bm_, v_ in zip(bmat, vh)]
    x = [x_ + _mm(x_, p_, NN, *mn) for x_, p_ in zip(x, p)]
    p = [_mm(p_, p_, NN, *mn) for p_ in p]
    x = [x_ + _mm(x_, p_, NN, *mn) for x_, p_ in zip(x, p)]
    p = [_mm(p_, p_, NN, *mn) for p_ in p]
    x = [x_ + _mm(x_, p_, NN, *mn) for x_, p_ in zip(x, p)]
    wu = [-_mm(x_, jnp.concatenate([kg[:, ls], bv_], axis=1), NN, *ma) for x_, ls, bv_ in zip(x, hs, bv)]
    qy = [_mm(pb, wu_, NN, *ma) for pb, wu_ in zip(pb_, wu)]
    pkv = [_mm(pk, v_, NN, *ma) for pk, v_ in zip(pk_, vh)]
    qt = [rg[:, ls] + qy_[:, :RWKV_D] for ls, qy_ in zip(hs, qy)]
    y0 = [qy_[:, RWKV_D:] + pkv_ for qy_, pkv_ in zip(qy, pkv)]
    s = [st_ref[h] for h in range(RWKV_H)]
    for c in range(tb // cc):
        rows = slice(c * cc, (c + 1) * cc)
        mcs = [jnp.where(eye_d, g_end[c * cc:c * cc + 1, ls], 0.0)
               + _mm(wu_[rows, :RWKV_D], bt[rows, ls], TN, *mc_) for ls, wu_ in zip(hs, wu)]
        ncs = [_mm(jnp.concatenate([wu_[rows, RWKV_D:], v_[rows]], axis=0),
                   jnp.concatenate([bt[rows, ls], kt[rows, ls]], axis=0), TN, *mc_) for ls, wu_, v_ in zip(hs, wu, vh)]
        for h, ls in enumerate(hs):
            y_ref[rows, ls] = _mm(qt[h][rows], s[h], NT, *mc_) + y0[h][rows]
        s = [_mm(s_, m_, NN, *mc_) + n_ for s_, m_, n_ in zip(s, mcs, ncs)]
    for h in range(RWKV_H):
        st_ref[h] = s[h]

    o_ref[...] = _rwkv_finish(y_ref[...], r, k, v, g, rk_ref, lnw_ref, lnb_ref, seg_ref).astype(BF16)

    @pl.when(t == pl.num_programs(1) - 1)
    def _():
        s_ref[0] = st_ref[...]


def _rwkv_prompt(proj, nb, nt_len, p):
    tb = RWKV_TB
    nt = nt_len // tb
    return pl.pallas_call(
        _rwkv_kernel,
        grid=(nb, nt),
        in_specs=[pl.BlockSpec((GROUP_TILES, tb, TILE), lambda b, t: (T_RWKV // GROUP_TILES, b * nt + t, 0)),
                  _const_spec((1, RWKV_COLS)), _const_spec((1, MIX)), _const_spec((64, MIX)),
                  _const_spec((1, MIX)), _const_spec((64, MIX)), _const_spec((128, MIX)),
                  _const_spec((1, MIX)), _const_spec((1, MIX)), _const_spec((1, MIX)),
                  _const_spec((1, MIX)), _const_spec((1, MIX)), _const_spec((LANES, LANES))],
        out_specs=[pl.BlockSpec((tb, MIX), lambda b, t: (b * nt + t, 0)),
                   pl.BlockSpec((1, RWKV_H, RWKV_D, RWKV_D), lambda b, t: (b, 0, 0, 0)),
                   pl.BlockSpec((1, SUBLANES, RWKV_COLS), lambda b, t: (b, 0, 0))],
        out_shape=[jax.ShapeDtypeStruct((nb * nt_len, MIX), BF16),
                   jax.ShapeDtypeStruct((nb, RWKV_H, RWKV_D, RWKV_D), F32),
                   jax.ShapeDtypeStruct((nb, SUBLANES, RWKV_COLS), F32)],
        scratch_shapes=[pltpu.VMEM((SUBLANES, RWKV_COLS), F32), pltpu.VMEM((RWKV_H, RWKV_D, RWKV_D), F32),
                        pltpu.VMEM((tb, MIX), F32)],
        compiler_params=_params("parallel", "arbitrary"),
        name="rwkv_prompt",
    )(proj, p["mu"], p["w0"], p["w2"], p["a0"], p["a2"], p["g2"], p["kk"], p["ka"], p["rk"], p["lnw"], p["lnb"],
      p["seg"])


SSD_TB = 256


def _ssd_pointwise(xbc_c, dt_raw, dtb_ref, aneg_ref):
    xbc = _silu(xbc_c)
    dt = _softplus(dt_raw + dtb_ref[...])
    return xbc, dt, dt * aneg_ref[...]


def _expand_heads(col8, width):
    n = col8.shape[0]
    return jnp.concatenate([jnp.broadcast_to(col8[:, h:h + 1], (n, width)) for h in range(SSD_H)], axis=1)


def _ssd_kernel(z_ref, cw_ref, cb_ref, dtb_ref, aneg_ref, dsk_ref, nw_ref, o_ref, s_ref, tail_ref, carry_ref, st_ref,
                y_ref):
    tb = SSD_TB
    cc = SSD_C
    t = pl.program_id(1)

    @pl.when(t == 0)
    def _():
        carry_ref[...] = jnp.zeros_like(carry_ref)
        st_ref[...] = jnp.zeros_like(st_ref)

    gate = jnp.concatenate([z_ref[0], z_ref[1]], axis=1)
    xbc_raw = jnp.concatenate([z_ref[2], z_ref[3], z_ref[4], z_ref[5]], axis=1)
    xbc_c = _causal_conv(xbc_raw, carry_ref[...], cw_ref, cb_ref)
    carry_ref[...] = xbc_raw[tb - SUBLANES:]
    tail_ref[0] = xbc_raw[tb - SUBLANES:]
    xbc, dt, dta = _ssd_pointwise(xbc_c, z_ref[6], dtb_ref, aneg_ref)
    xs = xbc[:, :MIX]
    bm = xbc[:, MIX:MIX + TILE]
    cm = xbc[:, MIX + TILE:]
    pos = _iota((tb, TILE), 0) & (cc - 1)
    cum = _chunk_cumsum(dta, cc, pos)
    xdt = xs * _expand_heads(dt, SSD_P)
    tri = _iota((cc, cc), 0) >= _iota((cc, cc), 1)

    for c in range(tb // cc):
        rows = slice(c * cc, (c + 1) * cc)
        cum_c = cum[rows, :LANES]
        cum_t = cum_c.T
        hg = SSD_H // SSD_G
        for gi in range(SSD_G):
            ns = slice(gi * SSD_N, (gi + 1) * SSD_N)
            gs = slice(gi * hg * SSD_P, (gi + 1) * hg * SSD_P)
            cg = cm[rows, ns]
            bg = bm[rows, ns]
            gmat = _mm(cg, bg, NT)
            st = st_ref[gs, :]
            yo = _mm(cg, st, NT)
            yds, xds, decs = [], [], []
            for hh in range(hg):
                h = gi * hg + hh
                ps = slice(h * SSD_P, (h + 1) * SSD_P)
                col = cum_c[:, h:h + 1]
                lmat = jnp.exp(jnp.where(tri, col - cum_t[h:h + 1, :], -jnp.inf))
                xh = xdt[rows, ps]
                last = cum_c[cc - 1:cc, h:h + 1]
                yds.append(_mm(gmat * lmat, xh) + yo[:, hh * SSD_P:(hh + 1) * SSD_P] * jnp.exp(col))
                xds.append(xh * jnp.exp(last - col))
                decs.append(jnp.broadcast_to(jnp.exp(last), (SSD_P, SSD_N)))
            y_ref[rows, gs] = jnp.concatenate(yds, axis=1)
            st_ref[gs, :] = st * jnp.concatenate(decs, axis=0) + _mm(jnp.concatenate(xds, axis=1), bg, TN)

    y = (y_ref[...] + dsk_ref[...] * xs) * _silu(gate)
    o_ref[...] = _rms(y, nw_ref[...]).astype(BF16)

    @pl.when(t == pl.num_programs(1) - 1)
    def _():
        for h in range(SSD_H):
            s_ref[0, h] = st_ref[h * SSD_P:(h + 1) * SSD_P, :]


def _ssd_prompt(proj, nb, nt_len, cw, cb, dtb, aneg, dsk, nw):
    tb = SSD_TB
    nt = nt_len // tb
    return pl.pallas_call(
        _ssd_kernel,
        grid=(nb, nt),
        in_specs=[pl.BlockSpec((GROUP_TILES, tb, TILE), lambda b, t: (T_SSD // GROUP_TILES, b * nt + t, 0)),
                  _const_spec((CONV_W, SSD_CONV_DIM)), _const_spec((1, SSD_CONV_DIM)),
                  _const_spec((1, TILE)), _const_spec((1, TILE)), _const_spec((1, MIX)), _const_spec((1, MIX))],
        out_specs=[pl.BlockSpec((tb, MIX), lambda b, t: (b * nt + t, 0)),
                   pl.BlockSpec((1, SSD_H, SSD_P, SSD_N), lambda b, t: (b, 0, 0, 0)),
                   pl.BlockSpec((1, SUBLANES, SSD_CONV_DIM), lambda b, t: (b, 0, 0))],
        out_shape=[jax.ShapeDtypeStruct((nb * nt_len, MIX), BF16),
                   jax.ShapeDtypeStruct((nb, SSD_H, SSD_P, SSD_N), F32),
                   jax.ShapeDtypeStruct((nb, SUBLANES, SSD_CONV_DIM), F32)],
        scratch_shapes=[pltpu.VMEM((SUBLANES, SSD_CONV_DIM), F32), pltpu.VMEM((SSD_H * SSD_P, SSD_N), F32),
                        pltpu.VMEM((tb, MIX), F32)],
        compiler_params=_params("parallel", "arbitrary"),
        name="ssd_prompt",
    )(proj, cw, cb, dtb, aneg, dsk, nw)


R_GLA_V, R_GLA_GATE, R_GLA_Q = 0, 512, 1024
R_RW_K, R_RW_R, R_RW_V, R_RW_G = 1280, 1792, 2304, 2816
R_SSD_B, R_SSD_C, R_SSD_E, R_SSD_GATE, R_SSD_X = 3328, 3584, 3840, 4864, 5376
ROW_W = 5888
C_GLA_A, C_GLA_K, C_SSD_X = 0, 256, 512
COL_W = 1024
T_RW_W, T_RW_B, T_RW_K, T_RW_KK, T_RW_R, T_RW_V = 0, 512, 1024, 1536, 2048, 2560
RWT_W = 3072
DEC_RB = 8


def _dec_prep_kernel(z_ref, lconv_ref, lh_ref, rprev_ref, sconv_ref,
                     wg2_ref, bg_ref,
                     lcw_ref, lcb_ref, wa_ref, ba_ref, wi_ref, bi_ref, lam_ref,
                     mu_ref, w0_ref, w2_ref, a0_ref, a2_ref, g2_ref, kk_ref, ka_ref, seg_ref,
                     scw_ref, scb_ref, dtb_ref, aneg_ref,
                     rows_ref, colt_ref, rwt_ref, olru_ref, lh_out_ref, lconv_out_ref, sconv_out_ref):
    k = z_ref[T_GLA + 1]
    g = _gla_gates(z_ref[T_GLA + 6], wg2_ref, bg_ref)
    rows_ref[:, R_GLA_V:R_GLA_V + TILE] = z_ref[T_GLA + 2]
    rows_ref[:, R_GLA_V + TILE:R_GLA_V + MIX] = z_ref[T_GLA + 3]
    rows_ref[:, R_GLA_GATE:R_GLA_GATE + TILE] = z_ref[T_GLA + 4]
    rows_ref[:, R_GLA_GATE + TILE:R_GLA_GATE + MIX] = z_ref[T_GLA + 5]
    rows_ref[:, R_GLA_Q:R_GLA_Q + TILE] = z_ref[T_GLA + 0] * (GLA_DK ** -0.5)
    cols = [jnp.exp(g), k]

    xb = jnp.concatenate([z_ref[T_LRU + 0], z_ref[T_LRU + 1]], axis=1)
    gb = jnp.concatenate([z_ref[T_LRU + 2], z_ref[T_LRU + 3]], axis=1)
    xc = lcb_ref[...] + xb * lcw_ref[CONV_W - 1:CONV_W, :]
    for i in range(CONV_W - 1):
        xc = xc + lconv_ref[:, i * MIX:(i + 1) * MIX] * lcw_ref[i:i + 1, :]
    a, mult, gi = _lru_gates(xc, wa_ref, ba_ref, wi_ref, bi_ref, lam_ref)
    h = a * lh_ref[...] + mult * gi * xc
    lh_out_ref[...] = h
    olru_ref[...] = (h * _gelu_tanh(gb)).astype(BF16)
    lconv_out_ref[:, 0:2 * MIX] = lconv_ref[:, MIX:3 * MIX]
    lconv_out_ref[:, 2 * MIX:3 * MIX] = xb

    z = jnp.concatenate([z_ref[T_RWKV + j] for j in range(GROUP_TILES)], axis=1)
    r, lw, kmod, v, kk, a7, g7 = _rwkv_pointwise(z, rprev_ref[...], mu_ref, w0_ref, w2_ref, a0_ref, a2_ref, g2_ref,
                                                 kk_ref, ka_ref, seg_ref)
    rows_ref[:, R_RW_K:R_RW_K + MIX] = kmod
    rows_ref[:, R_RW_R:R_RW_R + MIX] = r
    rows_ref[:, R_RW_V:R_RW_V + MIX] = v
    rows_ref[:, R_RW_G:R_RW_G + MIX] = g7
    off = 0
    for vec in (jnp.exp(lw), kk * a7, kmod, kk, r, v):
        for j in range(MIX // LANES):
            rwt_ref[off:off + LANES, :] = vec[:, j * LANES:(j + 1) * LANES].T
            off += LANES

    xbc_raw = jnp.concatenate([z_ref[T_SSD + 2], z_ref[T_SSD + 3], z_ref[T_SSD + 4], z_ref[T_SSD + 5]], axis=1)
    xbc_c = scb_ref[...] + xbc_raw * scw_ref[CONV_W - 1:CONV_W, :]
    for i in range(CONV_W - 1):
        xbc_c = xbc_c + sconv_ref[:, i * SSD_CONV_DIM:(i + 1) * SSD_CONV_DIM] * scw_ref[i:i + 1, :]
    xbc, dt, dta = _ssd_pointwise(xbc_c, z_ref[T_SSD + 6], dtb_ref, aneg_ref)
    xs = xbc[:, :MIX]
    rows_ref[:, R_SSD_B:R_SSD_B + TILE] = xbc[:, MIX:MIX + TILE]
    rows_ref[:, R_SSD_C:R_SSD_C + TILE] = xbc[:, MIX + TILE:]
    rows_ref[:, R_SSD_E:R_SSD_E + SSD_H * LANES] = _expand_heads(jnp.exp(dta), LANES)
    rows_ref[:, R_SSD_GATE:R_SSD_GATE + TILE] = z_ref[T_SSD + 0]
    rows_ref[:, R_SSD_GATE + TILE:R_SSD_GATE + MIX] = z_ref[T_SSD + 1]
    rows_ref[:, R_SSD_X:R_SSD_X + MIX] = xs
    cols.append(xs * _expand_heads(dt, SSD_P))
    sconv_out_ref[:, 0:2 * SSD_CONV_DIM] = sconv_ref[:, SSD_CONV_DIM:3 * SSD_CONV_DIM]
    sconv_out_ref[:, 2 * SSD_CONV_DIM:3 * SSD_CONV_DIM] = xbc_raw

    off = 0
    for cvec in cols:
        for j in range(cvec.shape[1] // LANES):
            colt_ref[off:off + LANES, :] = cvec[:, j * LANES:(j + 1) * LANES].T
            off += LANES


def _dec_prep(zdec, lconv, lh, rprev, sconv, pp):
    n = zdec.shape[1]
    args = [zdec, lconv, lh, rprev, sconv,
            pp["gla_wg2"], pp["gla_bg"],
            pp["lru_cw"], pp["lru_cb"], pp["lru_wa"], pp["lru_ba"], pp["lru_wi"], pp["lru_bi"], pp["lru_lam"],
            pp["rwkv"]["mu"], pp["rwkv"]["w0"], pp["rwkv"]["w2"], pp["rwkv"]["a0"], pp["rwkv"]["a2"],
            pp["rwkv"]["g2"], pp["rwkv"]["kk"], pp["rwkv"]["ka"], pp["rwkv"]["seg"],
            pp["ssd_cw"], pp["ssd_cb"], pp["ssd_dtb"], pp["ssd_aneg"]]
    return pl.pallas_call(
        _dec_prep_kernel,
        out_shape=[jax.ShapeDtypeStruct((n, ROW_W), F32), jax.ShapeDtypeStruct((COL_W, n), F32),
                   jax.ShapeDtypeStruct((RWT_W, n), F32),
                   jax.ShapeDtypeStruct((n, MIX), BF16), jax.ShapeDtypeStruct((n, MIX), F32),
                   jax.ShapeDtypeStruct((n, 3 * MIX), F32), jax.ShapeDtypeStruct((n, 3 * SSD_CONV_DIM), F32)],
        compiler_params=pltpu.CompilerParams(vmem_limit_bytes=VMEM_LIMIT),
        name="decode_prep",
    )(*args)


def _dec_rwkv_kernel(w_ref, b_ref, k_ref, kk_ref, r_ref, v_ref, s_ref, *rest):
    s_out, y_ref = rest[-2:]
    w = w_ref[...]
    b = b_ref[...]
    k = k_ref[...]
    kk = kk_ref[...]
    r = r_ref[...]

    def body(vi, carry):
        s = s_ref[vi]
        sa = jnp.sum(s * kk, axis=0, keepdims=True)
        s = s * w - sa * b + v_ref[pl.ds(vi, 1), :] * k
        s_out[vi] = s
        y_ref[pl.ds(vi, 1), :] = jnp.sum(s * r, axis=0, keepdims=True)
        return carry

    lax.fori_loop(0, RWKV_D, body, 0, unroll=8)


def _dec_rwkv(rwt, sr_t, l, prev):
    n = rwt.shape[1]
    hb = MIX // RWKV_D

    def vec(off):
        return pl.BlockSpec((RWKV_D, n), lambda h: (off // RWKV_D + h, 0))

    sspec = pl.BlockSpec((None, None, RWKV_D, RWKV_D, n), lambda h: (l, h, 0, 0, 0))
    extra = [] if prev is None else [prev]
    return pl.pallas_call(
        _dec_rwkv_kernel,
        grid=(hb,),
        in_specs=[vec(T_RW_W), vec(T_RW_B), vec(T_RW_K), vec(T_RW_KK), vec(T_RW_R), vec(T_RW_V), sspec]
        + [pl.BlockSpec(memory_space=pl.ANY)] * len(extra),
        out_specs=[sspec, pl.BlockSpec((RWKV_D, n), lambda h: (h, 0))],
        out_shape=[jax.ShapeDtypeStruct(sr_t.shape, F32), jax.ShapeDtypeStruct((MIX, n), F32)],
        input_output_aliases={7: 0} if extra else {},
        compiler_params=_params("parallel"),
        name="decode_rwkv",
    )(rwt, rwt, rwt, rwt, rwt, rwt, sr_t, *extra)


def _dec_state_kernel(colt_ref, rows_ref, sg_ref, ss_ref, *rest):
    sg_out, ss_out, yg_ref, ys_ref = rest[-4:]
    pid = pl.program_id(0)
    n = colt_ref.shape[1]
    lane = _iota((1, n), 1)
    rid = _iota((DEC_RB, LANES), 0)
    rows = rows_ref[...]
    hg = SSD_H // SSD_G
    yg = [jnp.zeros((DEC_RB, GLA_DV), F32) for _ in range(GLA_H)]
    ys = [jnp.zeros((DEC_RB, SSD_P), F32) for _ in range(SSD_H)]
    for i in range(DEC_RB):
        sel = lane == pid * DEC_RB + i
        col = jnp.sum(jnp.where(sel, colt_ref[...], 0.0), axis=1, keepdims=True)
        rv = rows[i:i + 1, :]
        mine = rid == i
        for h in range(GLA_H):
            al = col[C_GLA_A + h * GLA_DK:C_GLA_A + (h + 1) * GLA_DK]
            kc = col[C_GLA_K + h * GLA_DK:C_GLA_K + (h + 1) * GLA_DK]
            vr = rv[:, R_GLA_V + h * GLA_DV:R_GLA_V + (h + 1) * GLA_DV]
            s = al * sg_ref[i, h] + kc * vr
            sg_out[i, h] = s
            q8 = rows[:, R_GLA_Q + h * GLA_DK:R_GLA_Q + (h + 1) * GLA_DK]
            yg[h] = jnp.where(mine, _mm(q8, s, NN, pa=2, pb=2), yg[h])
        for h in range(SSD_H):
            gi = h // hg
            e = rv[:, R_SSD_E + h * LANES:R_SSD_E + (h + 1) * LANES]
            bn = rv[:, R_SSD_B + gi * SSD_N:R_SSD_B + (gi + 1) * SSD_N]
            xc = col[C_SSD_X + h * SSD_P:C_SSD_X + (h + 1) * SSD_P]
            s = ss_ref[i, h] * e + xc * bn
            ss_out[i, h] = s
            c8 = rows[:, R_SSD_C + gi * SSD_N:R_SSD_C + (gi + 1) * SSD_N]
            ys[h] = jnp.where(mine[:, :SSD_P], _mm(c8, s, NT, pa=2, pb=2), ys[h])
    yg_ref[...] = jnp.concatenate(yg, axis=1)
    ys_ref[...] = jnp.concatenate(ys, axis=1)


def _dec_state(colt, rows, sg, ss, l, prev):
    n = rows.shape[0]
    blk = lambda i: (l, i, 0, 0, 0)
    specs = [pl.BlockSpec((None, DEC_RB, GLA_H, GLA_DK, GLA_DV), blk),
             pl.BlockSpec((None, DEC_RB, SSD_H, SSD_P, SSD_N), blk)]
    extra = [] if prev is None else list(prev)
    rowspec = pl.BlockSpec((DEC_RB, MIX), lambda i: (i, 0))
    return pl.pallas_call(
        _dec_state_kernel,
        grid=(n // DEC_RB,),
        in_specs=[_const_spec((COL_W, n)), pl.BlockSpec((DEC_RB, ROW_W), lambda i: (i, 0))] + specs
        + [pl.BlockSpec(memory_space=pl.ANY)] * len(extra),
        out_specs=specs + [rowspec, rowspec],
        out_shape=[jax.ShapeDtypeStruct(sg.shape, F32), jax.ShapeDtypeStruct(ss.shape, F32),
                   jax.ShapeDtypeStruct((n, MIX), F32), jax.ShapeDtypeStruct((n, MIX), F32)],
        input_output_aliases={4 + j: j for j in range(len(extra))},
        compiler_params=_params("parallel"),
        name="decode_state",
    )(colt, rows, sg, ss, *extra)


def _dec_finish_kernel(rows_ref, yg_ref, yrt_ref, ys_ref, gnw_ref, rk_ref, lnw_ref, lnb_ref, seg_ref,
                       dsk_ref, snw_ref, og_ref, or_ref, os_ref):
    outs = []
    for h in range(GLA_H):
        outs.append(_rms(yg_ref[:, h * GLA_DV:(h + 1) * GLA_DV], gnw_ref[...]))
    og_ref[...] = (jnp.concatenate(outs, axis=1) * _silu(rows_ref[:, R_GLA_GATE:R_GLA_GATE + MIX])).astype(BF16)
    yr = jnp.concatenate([yrt_ref[j * LANES:(j + 1) * LANES, :].T for j in range(MIX // LANES)], axis=1)
    or_ref[...] = _rwkv_finish(yr, rows_ref[:, R_RW_R:R_RW_R + MIX], rows_ref[:, R_RW_K:R_RW_K + MIX],
                               rows_ref[:, R_RW_V:R_RW_V + MIX], rows_ref[:, R_RW_G:R_RW_G + MIX],
                               rk_ref, lnw_ref, lnb_ref, seg_ref).astype(BF16)
    y = ((ys_ref[...] + dsk_ref[...] * rows_ref[:, R_SSD_X:R_SSD_X + MIX])
         * _silu(rows_ref[:, R_SSD_GATE:R_SSD_GATE + MIX]))
    os_ref[...] = _rms(y, snw_ref[...]).astype(BF16)


def _dec_finish(rows, yg, yrt, ys, pp):
    n = rows.shape[0]
    o = jax.ShapeDtypeStruct((n, MIX), BF16)
    return pl.pallas_call(
        _dec_finish_kernel,
        out_shape=[o, o, o],
        compiler_params=pltpu.CompilerParams(vmem_limit_bytes=VMEM_LIMIT),
        name="decode_finish",
    )(rows, yg, yrt, ys, pp["gla_nw"], pp["rwkv"]["rk"], pp["rwkv"]["lnw"], pp["rwkv"]["lnb"], pp["rwkv"]["seg"],
      pp["ssd_dsk"], pp["ssd_nw"])


def _block_diag(w4):
    out = jnp.zeros((TILE, TILE), w4.dtype)
    for j in range(4):
        out = out.at[j * 64:(j + 1) * 64, j * 64:(j + 1) * 64].set(w4[j])
    return out


def _prep_weights(w_in, w_out, w_up, w_down):
    o_lru = GLA_COLS
    o_rwkv = GLA_COLS + LRU_COLS
    o_ssd = o_rwkv + RWKV_COLS
    wt = jnp.transpose(w_in, (0, 2, 1))
    zpad = lambda n: jnp.zeros((DEPTH, n, D_MODEL), w_in.dtype)
    w_perm = jnp.concatenate([
        wt[:, 0:GLA_COLS], zpad(GROUP_TILES * TILE - GLA_COLS),
        wt[:, o_rwkv:o_rwkv + RWKV_COLS],
        wt[:, o_ssd:o_ssd + SSD_COLS], zpad(GROUP_TILES * TILE - SSD_COLS),
        wt[:, o_lru:o_lru + LRU_COLS]], axis=1).astype(BF16)
    return {"w_in": w_perm,
            "w_out": w_out.reshape(DEPTH, 4, MIX, D_MODEL).astype(BF16),
            "w_up": w_up.astype(BF16),
            "w_down": w_down.astype(BF16)}


def _prep_layer(l, norm_mix_pre, norm_mix_post, norm_mlp_pre, norm_mlp_post,
                gla_w_gate2, gla_b_gate, gla_norm, lru_conv_w, lru_conv_b, lru_w_a, lru_b_a, lru_w_i, lru_b_i,
                lru_lambda, rwkv_mu, rwkv_w0, rwkv_w2, rwkv_a0, rwkv_a2, rwkv_g2, rwkv_k_k, rwkv_k_a, rwkv_r_k,
                rwkv_ln_w, rwkv_ln_b, ssd_conv_w, ssd_conv_b, ssd_dt_bias, ssd_a_log, ssd_d, ssd_norm):
    row = lambda a: a.reshape(1, -1).astype(F32)
    ii = jnp.arange(TILE)[:, None]
    jj = jnp.arange(MIX)[None, :]
    seg_i = jnp.arange(LANES)
    pp = {
        "n_mix_pre": row(norm_mix_pre[l]), "n_mix_post": row(norm_mix_post[l]),
        "n_mlp_pre": row(norm_mlp_pre[l]), "n_mlp_post": row(norm_mlp_post[l]),
        "gla_wg2": jnp.zeros((TILE, TILE), F32).at[:GLA_RANK, :].set(gla_w_gate2[l]).astype(BF16),
        "gla_bg": row(gla_b_gate[l]),
        "gla_nw": row(gla_norm[l]),
        "gla_ones": (ii // GLA_DK == jj // GLA_DV).astype(BF16),
        "lru_cw": lru_conv_w[l].astype(F32), "lru_cb": row(lru_conv_b[l]),
        "lru_wa": jnp.stack([_block_diag(lru_w_a[l, 0:4]), _block_diag(lru_w_a[l, 4:8])]).astype(BF16),
        "lru_wi": jnp.stack([_block_diag(lru_w_i[l, 0:4]), _block_diag(lru_w_i[l, 4:8])]).astype(BF16),
        "lru_ba": row(lru_b_a[l]), "lru_bi": row(lru_b_i[l]), "lru_lam": row(lru_lambda[l]),
        "rwkv": {
            "mu": row(rwkv_mu[l]), "w0": row(rwkv_w0[l]), "w2": rwkv_w2[l].astype(BF16),
            "a0": row(rwkv_a0[l]), "a2": rwkv_a2[l].astype(BF16), "g2": rwkv_g2[l].astype(BF16),
            "kk": row(rwkv_k_k[l]), "ka": row(rwkv_k_a[l]), "rk": row(rwkv_r_k[l]),
            "lnw": row(rwkv_ln_w[l]), "lnb": row(rwkv_ln_b[l]),
            "seg": (seg_i[:, None] // RWKV_D == seg_i[None, :] // RWKV_D).astype(BF16),
        },
        "ssd_cw": ssd_conv_w[l].astype(F32), "ssd_cb": row(ssd_conv_b[l]),
        "ssd_dtb": jnp.zeros((1, TILE), F32).at[0, :SSD_H].set(ssd_dt_bias[l]),
        "ssd_aneg": jnp.zeros((1, TILE), F32).at[0, :SSD_H].set(-jnp.exp(ssd_a_log[l].astype(F32))),
        "ssd_dsk": jnp.repeat(ssd_d[l].astype(F32), SSD_P).reshape(1, MIX),
        "ssd_nw": row(ssd_norm[l]),
    }
    return pp


def _layer_prompt(x2d, nb, nt_len, wts, l, pp, tm_proj=1024, tm_out=512, tm_mlp=512, tf=1024):
    proj = _proj(x2d, pp["n_mix_pre"], wts["w_in"], l, tm_proj)
    o_gla, s_gla = _gla_prompt(proj, nb, nt_len, pp["gla_wg2"], pp["gla_bg"], pp["gla_nw"], pp["gla_ones"])
    o_lru, h_lru, lru_tail = _lru_prompt(proj, nb, nt_len, pp["lru_cw"], pp["lru_cb"], pp["lru_wa"], pp["lru_ba"],
                                         pp["lru_wi"], pp["lru_bi"], pp["lru_lam"])
    o_rwkv, s_rwkv, rwkv_tail = _rwkv_prompt(proj, nb, nt_len, pp["rwkv"])
    o_ssd, s_ssd, ssd_tail = _ssd_prompt(proj, nb, nt_len, pp["ssd_cw"], pp["ssd_cb"], pp["ssd_dtb"],
                                         pp["ssd_aneg"], pp["ssd_dsk"], pp["ssd_nw"])
    x1, h2 = _outproj((o_gla, o_lru, o_rwkv, o_ssd), wts["w_out"], l, x2d, pp["n_mix_post"], pp["n_mlp_pre"], tm_out)
    x2 = _mlp(h2, wts["w_up"], wts["w_down"], l, x1, pp["n_mlp_post"], tm_mlp, tf)
    keep = SUBLANES - (CONV_W - 1)
    states = (s_gla, h_lru[:, 0, :], lru_tail[:, keep:, :], rwkv_tail[:, SUBLANES - 1, :], s_rwkv,
              ssd_tail[:, keep:, :], s_ssd)
    return x2, states


def _layer_decode(x2d, big, small, wts, l, pp, prev, tf=1024):
    sg, sr_t, ss = big
    lh, lconv, rprev, sconv = small
    n = x2d.shape[0]
    proj = _proj(x2d, pp["n_mix_pre"], wts["w_in"], l, n)
    rows, colt, rwt, o_lru, lh_new, lconv_new, sconv_new = _dec_prep(
        proj, lconv.reshape(n, -1), lh, rprev, sconv.reshape(n, -1), pp)
    sr_new, yrt = _dec_rwkv(rwt, sr_t, l, None if prev is None else prev[1])
    sg_new, ss_new, yg, ys = _dec_state(colt, rows, sg, ss, l, None if prev is None else (prev[0], prev[2]))
    o_gla, o_rwkv, o_ssd = _dec_finish(rows, yg, yrt, ys, pp)
    x1, h2 = _outproj((o_gla, o_lru, o_rwkv, o_ssd), wts["w_out"], l, x2d, pp["n_mix_post"], pp["n_mlp_pre"], n)
    x2 = _mlp(h2, wts["w_up"], wts["w_down"], l, x1, pp["n_mlp_post"], n, tf)
    rshift_new = jnp.transpose(proj[T_RWKV:T_RWKV + GROUP_TILES], (1, 0, 2)).reshape(n, RWKV_COLS)
    small_new = (lh_new, lconv_new.reshape(n, CONV_W - 1, MIX), rshift_new,
                 sconv_new.reshape(n, CONV_W - 1, SSD_CONV_DIM))
    return x2, (sg_new, sr_new, ss_new), small_new


def kernel(x_prompt, x_sample, state_gla, state_lru, cache_lru_conv, cache_rwkv_shift, state_rwkv, cache_ssd_conv, state_ssd, norm_mix_pre, norm_mix_post, norm_mlp_pre, norm_mlp_post, w_in, w_out, w_up, w_down, gla_w_gate2, gla_b_gate, gla_norm, lru_conv_w, lru_conv_b, lru_w_a, lru_b_a, lru_w_i, lru_b_i, lru_lambda, rwkv_mu, rwkv_w0, rwkv_w2, rwkv_a0, rwkv_a2, rwkv_g2, rwkv_k_k, rwkv_k_a, rwkv_r_k, rwkv_ln_w, rwkv_ln_b, ssd_conv_w, ssd_conv_b, ssd_dt_bias, ssd_a_log, ssd_d, ssd_norm):
    nb, nt_len, _ = x_prompt.shape
    nd = x_sample.shape[0]
    yp = x_prompt.reshape(nb * nt_len, D_MODEL)
    ys = x_sample.reshape(nd, D_MODEL)
    wts = _prep_weights(w_in, w_out, w_up, w_down)
    big = (state_gla, jnp.transpose(state_rwkv, (0, 2, 3, 4, 1)), state_ssd)
    new_p, small_s, big_s = [], [], None
    for l in range(DEPTH):
        pp = _prep_layer(l, norm_mix_pre, norm_mix_post, norm_mlp_pre, norm_mlp_post,
                         gla_w_gate2, gla_b_gate, gla_norm, lru_conv_w, lru_conv_b, lru_w_a, lru_b_a, lru_w_i,
                         lru_b_i, lru_lambda, rwkv_mu, rwkv_w0, rwkv_w2, rwkv_a0, rwkv_a2, rwkv_g2, rwkv_k_k,
                         rwkv_k_a, rwkv_r_k, rwkv_ln_w, rwkv_ln_b, ssd_conv_w, ssd_conv_b, ssd_dt_bias, ssd_a_log,
                         ssd_d, ssd_norm)
        yp, st_p = _layer_prompt(yp, nb, nt_len, wts, l, pp)
        small = (state_lru[l], cache_lru_conv[l], cache_rwkv_shift[l], cache_ssd_conv[l])
        ys, big_s, sm = _layer_decode(ys, big, small, wts, l, pp, big_s)
        new_p.append(st_p)
        small_s.append(sm)

    def stack(sts, i):
        return jnp.stack([s[i] for s in sts], axis=0)

    gla_s, rwkv_s, ssd_s = big_s
    rwkv_s = jnp.transpose(rwkv_s, (0, 4, 1, 2, 3))
    return (yp.reshape(nb, nt_len, D_MODEL), ys.reshape(nd, 1, D_MODEL),
            stack(new_p, 0), gla_s, stack(new_p, 1), stack(small_s, 0),
            stack(new_p, 2), stack(small_s, 1), stack(new_p, 3), stack(small_s, 2),
            stack(new_p, 4), rwkv_s, stack(new_p, 5), stack(small_s, 3),
            stack(new_p, 6), ssd_s)
```

```python
import functools
import math

import jax
import jax.numpy as jnp
from jax import lax
from jax.experimental import pallas as pl
from jax.experimental.pallas import tpu as pltpu

F32 = jnp.float32
BF16 = jnp.bfloat16

D_MODEL = 2048
D_FF = 4 * D_MODEL
DEPTH = 2
EPS = 1e-6
MIX = D_MODEL // 4
CONV_W = 4
GLA_H, GLA_DK, GLA_DV, GLA_RANK, GLA_TEMP, GLA_C = 4, 64, 128, 16, 16.0, 16
LRU_C = 8.0
RWKV_H, RWKV_D = 8, 64
RWKV_DECAY = math.exp(-0.5)
RWKV_LN_EPS = 64e-5
RWKV_C = 32
SSD_H, SSD_P, SSD_G, SSD_N, SSD_C = 8, 64, 2, 128, 64
GLA_COLS = 2 * GLA_H * GLA_DK + 2 * MIX + GLA_RANK
LRU_COLS = 2 * MIX
RWKV_COLS = 3 * MIX + 64 + 64 + 128
SSD_CONV_DIM = MIX + 2 * SSD_G * SSD_N
SSD_COLS = MIX + SSD_CONV_DIM + SSD_H

LANES = 128
SUBLANES = 8
TILE = 2 * LANES
GROUP_TILES = 7
T_GLA, T_RWKV, T_SSD, T_LRU = 0, 7, 14, 21
N_TILES = 25
VMEM_LIMIT = 56 * 1024 * 1024

NN = (((1,), (0,)), ((), ()))
NT = (((1,), (1,)), ((), ()))
TN = (((0,), (0,)), ((), ()))


def _split(x, n):
    if x.dtype == BF16:
        return [x]
    parts, r = [], x
    for i in range(n):
        p = r.astype(BF16)
        parts.append(p)
        if i + 1 < n:
            r = r - p.astype(F32)
    return parts


def _mm(a, b, dn=NN, pa=1, pb=1):
    aa, bb = _split(a, pa), _split(b, pb)
    acc = None
    for i, x in enumerate(aa):
        for j, y in enumerate(bb):
            if i + j >= max(len(aa), len(bb)):
                continue
            t = lax.dot_general(x, y, dn, preferred_element_type=F32)
            acc = t if acc is None else acc + t
    return acc


def _iota(shape, dim):
    return lax.broadcasted_iota(jnp.int32, shape, dim)


def _roll0(x, s):
    n = x.shape[0]
    s = s % n
    return x if s == 0 else pltpu.roll(x, s, 0)


def _rms(x, w):
    ms = jnp.mean(x * x, axis=-1, keepdims=True)
    return x * lax.rsqrt(ms + EPS) * w


def _sigmoid(x):
    return jax.nn.sigmoid(x)


def _silu(x):
    return x * jax.nn.sigmoid(x)


def _softplus(x):
    return jnp.maximum(x, 0.0) + jnp.log1p(jnp.exp(-jnp.abs(x)))


def _log_sigmoid(x):
    return jnp.minimum(x, 0.0) - jnp.log1p(jnp.exp(-jnp.abs(x)))


def _gelu_tanh(x):
    c = math.sqrt(2.0 / math.pi)
    return x * (0.5 * (1.0 + jnp.tanh(c * (x + 0.044715 * (x * x * x)))))


def _neg_expm1(x):
    return -jnp.tanh(0.5 * x) * (jnp.exp(x) + 1.0)


def _chunk_cumsum(x, chunk, pos):
    d = 1
    while d < chunk:
        x = x + jnp.where(pos >= d, _roll0(x, d), 0.0)
        d *= 2
    return x


def _chunk_last_bcast(x, chunk, pos):
    n = x.shape[0]
    y = jnp.where(pos == chunk - 1, x, 0.0)
    d = 1
    while d < chunk:
        y = y + _roll0(y, n - d)
        d *= 2
    return y


def _shift_rows(x, carry, i, row8):
    xs = _roll0(x, i)
    cs = _roll0(carry, i)
    top = jnp.where(row8 < i, cs, xs[:SUBLANES])
    return jnp.concatenate([top, xs[SUBLANES:]], axis=0)


def _causal_conv(x, carry, w_ref, b_ref):
    row8 = _iota((SUBLANES, x.shape[1]), 0)
    y = b_ref[...] + x * w_ref[CONV_W - 1:CONV_W, :]
    for i in range(1, CONV_W):
        y = y + _shift_rows(x, carry, i, row8) * w_ref[CONV_W - 1 - i:CONV_W - i, :]
    return y


def _params(*sem):
    return pltpu.CompilerParams(dimension_semantics=sem, vmem_limit_bytes=VMEM_LIMIT)


def _const_spec(shape):
    nd = len(shape)
    return pl.BlockSpec(shape, lambda *_: (0,) * nd)


PROJ_TILES_PER_STEP = 5


def _proj_kernel(x_ref, nw_ref, w_ref, o_ref, h_ref):
    @pl.when(pl.program_id(1) == 0)
    def _():
        h_ref[...] = _rms(x_ref[...], nw_ref[...]).astype(BF16)

    res = lax.dot_general(h_ref[...], w_ref[...], NT, preferred_element_type=F32)
    for j in range(PROJ_TILES_PER_STEP):
        o_ref[j] = res[:, j * TILE:(j + 1) * TILE]


def _proj(x2d, nw, w, l, tm):
    m = x2d.shape[0]
    tn = PROJ_TILES_PER_STEP * TILE
    return pl.pallas_call(
        _proj_kernel,
        grid=(m // tm, N_TILES // PROJ_TILES_PER_STEP),
        in_specs=[pl.BlockSpec((tm, D_MODEL), lambda i, j: (i, 0)),
                  _const_spec((1, D_MODEL)),
                  pl.BlockSpec((None, tn, D_MODEL), lambda i, j: (l, j, 0))],
        out_specs=pl.BlockSpec((PROJ_TILES_PER_STEP, tm, TILE), lambda i, j: (j, i, 0)),
        out_shape=jax.ShapeDtypeStruct((N_TILES, m, TILE), F32),
        scratch_shapes=[pltpu.VMEM((tm, D_MODEL), BF16)],
        compiler_params=_params("parallel", "arbitrary"),
        name="norm_proj",
    )(x2d, nw, w)


OUTPROJ_SUB_ROWS = 128

def _outproj_kernel(m0, m1, m2, m3, w_ref, x_ref, n1_ref, n2_ref, x1_ref, h2_ref):
    tm = x_ref.shape[0]
    sub = min(tm, OUTPROJ_SUB_ROWS)
    for r0 in range(0, tm, sub):
        rs = slice(r0, r0 + sub)
        mix = jnp.concatenate([m0[rs, :], m1[rs, :], m2[rs, :], m3[rs, :]], axis=1)
        acc = jnp.dot(mix, w_ref[...], preferred_element_type=F32)
        x1 = x_ref[rs, :] + _rms(acc, n1_ref[...])
        x1_ref[rs, :] = x1
        h2_ref[rs, :] = _rms(x1, n2_ref[...]).astype(BF16)


def _outproj(mix, w_out, l, x2d, n1, n2, tm):
    m = x2d.shape[0]
    row = lambda i: (i, 0)
    return pl.pallas_call(
        _outproj_kernel,
        grid=(m // tm,),
        in_specs=[pl.BlockSpec((tm, MIX), row)] * 4 + [
            pl.BlockSpec((None, D_MODEL, D_MODEL), lambda i: (l, 0, 0)), pl.BlockSpec((tm, D_MODEL), row),
            _const_spec((1, D_MODEL)), _const_spec((1, D_MODEL))],
        out_specs=[pl.BlockSpec((tm, D_MODEL), row), pl.BlockSpec((tm, D_MODEL), row)],
        out_shape=[jax.ShapeDtypeStruct((m, D_MODEL), F32), jax.ShapeDtypeStruct((m, D_MODEL), BF16)],
        compiler_params=_params("parallel"),
        name="out_proj",
    )(*mix, w_out, x2d, n1, n2)


def _mlp_kernel(h_ref, wu_ref, wd_ref, x1_ref, nw_ref, o_ref, acc_ref):
    f = pl.program_id(1)

    @pl.when(f == 0)
    def _():
        acc_ref[...] = jnp.zeros_like(acc_ref)

    u = jnp.maximum(jnp.dot(h_ref[...], wu_ref[...], preferred_element_type=F32), 0.0)
    acc_ref[...] += jnp.dot((u * u).astype(BF16), wd_ref[...], preferred_element_type=F32)

    @pl.when(f == pl.num_programs(1) - 1)
    def _():
        o_ref[...] = x1_ref[...] + _rms(acc_ref[...], nw_ref[...])


def _mlp(h2, w_up, w_down, l, x1, nw, tm, tf):
    m = h2.shape[0]
    return pl.pallas_call(
        _mlp_kernel,
        grid=(m // tm, D_FF // tf),
        in_specs=[pl.BlockSpec((tm, D_MODEL), lambda i, f: (i, 0)),
                  pl.BlockSpec((None, D_MODEL, tf), lambda i, f: (l, 0, f)),
                  pl.BlockSpec((None, tf, D_MODEL), lambda i, f: (l, f, 0)),
                  pl.BlockSpec((tm, D_MODEL), lambda i, f: (i, 0)),
                  _const_spec((1, D_MODEL))],
        out_specs=pl.BlockSpec((tm, D_MODEL), lambda i, f: (i, 0)),
        out_shape=jax.ShapeDtypeStruct((m, D_MODEL), F32),
        scratch_shapes=[pltpu.VMEM((tm, D_MODEL), F32)],
        compiler_params=_params("parallel", "arbitrary"),
        name="mlp",
    )(h2, w_up, w_down, x1, nw)


GLA_TB = 128


def _gla_gates(ad, wg2_ref, bg_ref):
    x = _mm(ad, wg2_ref[...], pa=2) + bg_ref[...]
    return _log_sigmoid(x) * (1.0 / GLA_TEMP)


def _gla_kernel(z_ref, wg2_ref, bg_ref, nw_ref, ones_ref, o_ref, s_ref, st_ref, y_ref):
    tb = GLA_TB
    t = pl.program_id(1)

    @pl.when(t == 0)
    def _():
        st_ref[...] = jnp.zeros_like(st_ref)

    q = z_ref[0] * (GLA_DK ** -0.5)
    k = z_ref[1]
    v = jnp.concatenate([z_ref[2], z_ref[3]], axis=1)
    gate = jnp.concatenate([z_ref[4], z_ref[5]], axis=1)
    g = _gla_gates(z_ref[6], wg2_ref, bg_ref)
    pos = _iota((tb, TILE), 0) & (GLA_C - 1)
    cum = _chunk_cumsum(g, GLA_C, pos)

    y = None
    for j in range(GLA_C):
        if j == 0:
            term = q * k
            vj = v
        else:
            e = jnp.exp(jnp.where(pos >= j, cum - _roll0(cum, j), -jnp.inf))
            term = q * _roll0(k, j) * e
            vj = _roll0(v, j)
        sc = _mm(term, ones_ref[...])
        y = sc * vj if y is None else y + sc * vj
    y_ref[...] = y

    qh = q * jnp.exp(cum)
    for c in range(tb // GLA_C):
        rows = slice(c * GLA_C, (c + 1) * GLA_C)
        cum_c = cum[rows]
        last = cum_c[GLA_C - 1:GLA_C]
        kt = k[rows] * jnp.exp(last - cum_c)
        dec = jnp.exp(last)
        for h in range(GLA_H):
            ks = slice(h * GLA_DK, (h + 1) * GLA_DK)
            vs = slice(h * GLA_DV, (h + 1) * GLA_DV)
            st = st_ref[h]
            y_ref[rows, vs] += _mm(qh[rows, ks], st, NT)
            st_ref[h] = st * dec[:, ks] + _mm(v[rows, vs], kt[:, ks], TN)

    yy = y_ref[...]
    outs = []
    for h in range(GLA_H):
        vs = slice(h * GLA_DV, (h + 1) * GLA_DV)
        outs.append(_rms(yy[:, vs], nw_ref[...]))
    o_ref[...] = (jnp.concatenate(outs, axis=1) * _silu(gate)).astype(BF16)

    @pl.when(t == pl.num_programs(1) - 1)
    def _():
        for h in range(GLA_H):
            s_ref[0, h] = st_ref[h].T


def _gla_prompt(proj, nb, nt_len, wg2, bg, nw, ones):
    tb = GLA_TB
    nt = nt_len // tb
    return pl.pallas_call(
        _gla_kernel,
        grid=(nb, nt),
        in_specs=[pl.BlockSpec((GROUP_TILES, tb, TILE), lambda b, t: (T_GLA // GROUP_TILES, b * nt + t, 0)),
                  _const_spec((TILE, TILE)), _const_spec((1, TILE)), _const_spec((1, GLA_DV)),
                  _const_spec((TILE, MIX))],
        out_specs=[pl.BlockSpec((tb, MIX), lambda b, t: (b * nt + t, 0)),
                   pl.BlockSpec((1, GLA_H, GLA_DK, GLA_DV), lambda b, t: (b, 0, 0, 0))],
        out_shape=[jax.ShapeDtypeStruct((nb * nt_len, MIX), BF16),
                   jax.ShapeDtypeStruct((nb, GLA_H, GLA_DK, GLA_DV), F32)],
        scratch_shapes=[pltpu.VMEM((GLA_H, GLA_DV, GLA_DK), F32), pltpu.VMEM((tb, MIX), F32)],
        compiler_params=_params("parallel", "arbitrary"),
        name="gla_prompt",
    )(proj, wg2, bg, nw, ones)


LRU_TB = 256


def _lru_gates(xc, wa_ref, ba_ref, wi_ref, bi_ref, lam_ref):
    ra, ri = [], []
    for j in range(2):
        xs = xc[:, j * TILE:(j + 1) * TILE]
        ra.append(_mm(xs, wa_ref[j], pa=2))
        ri.append(_mm(xs, wi_ref[j], pa=2))
    r = _sigmoid(jnp.concatenate(ra, axis=1) + ba_ref[...])
    i = _sigmoid(jnp.concatenate(ri, axis=1) + bi_ref[...])
    log_a = -LRU_C * r * _softplus(-lam_ref[...])
    a = jnp.exp(log_a)
    mult = jnp.sqrt(_neg_expm1(2.0 * log_a))
    return a, mult, i


def _lru_kernel(x0_ref, x1_ref, g0_ref, g1_ref, cw_ref, cb_ref, wa_ref, ba_ref, wi_ref, bi_ref, lam_ref,
                o_ref, h_ref, tail_ref, carry_ref, hc_ref):
    tb = LRU_TB
    t = pl.program_id(1)

    @pl.when(t == 0)
    def _():
        carry_ref[...] = jnp.zeros_like(carry_ref)
        hc_ref[...] = jnp.zeros_like(hc_ref)

    xb = jnp.concatenate([x0_ref[...], x1_ref[...]], axis=1)
    gb = jnp.concatenate([g0_ref[...], g1_ref[...]], axis=1)
    xc = _causal_conv(xb, carry_ref[...], cw_ref, cb_ref)
    carry_ref[...] = xb[tb - SUBLANES:]
    a, mult, i = _lru_gates(xc, wa_ref, ba_ref, wi_ref, bi_ref, lam_ref)
    row = _iota((tb, MIX), 0)
    mult = jnp.where((row == 0) & (t == 0), 1.0, mult)
    b = mult * i * xc
    d = 1
    while d < tb:
        m = row >= d
        b = jnp.where(m, a * _roll0(b, d) + b, b)
        a = jnp.where(m, a * _roll0(a, d), a)
        d *= 2
    h = b + a * hc_ref[0:1, :]
    hl = h[tb - 1:tb, :]
    hc_ref[...] = jnp.broadcast_to(hl, hc_ref.shape)
    o_ref[...] = (h * _gelu_tanh(gb)).astype(BF16)
    h_ref[0] = hl
    tail_ref[0] = xb[tb - SUBLANES:]


def _lru_prompt(proj, nb, nt_len, cw, cb, wa, ba, wi, bi, lam):
    tb = LRU_TB
    nt = nt_len // tb

    def tile(j):
        return pl.BlockSpec((None, tb, TILE), lambda b, t: (T_LRU + j, b * nt + t, 0))

    return pl.pallas_call(
        _lru_kernel,
        grid=(nb, nt),
        in_specs=[tile(0), tile(1), tile(2), tile(3),
                  _const_spec((CONV_W, MIX)), _const_spec((1, MIX)),
                  _const_spec((2, TILE, TILE)), _const_spec((1, MIX)),
                  _const_spec((2, TILE, TILE)), _const_spec((1, MIX)), _const_spec((1, MIX))],
        out_specs=[pl.BlockSpec((tb, MIX), lambda b, t: (b * nt + t, 0)),
                   pl.BlockSpec((1, 1, MIX), lambda b, t: (b, 0, 0)),
                   pl.BlockSpec((1, SUBLANES, MIX), lambda b, t: (b, 0, 0))],
        out_shape=[jax.ShapeDtypeStruct((nb * nt_len, MIX), BF16), jax.ShapeDtypeStruct((nb, 1, MIX), F32),
                   jax.ShapeDtypeStruct((nb, SUBLANES, MIX), F32)],
        scratch_shapes=[pltpu.VMEM((SUBLANES, MIX), F32), pltpu.VMEM((SUBLANES, MIX), F32)],
        compiler_params=_params("parallel", "arbitrary"),
        name="lru_prompt",
    )(proj, proj, proj, proj, cw, cb, wa, ba, wi, bi, lam)


RWKV_TB = 128
RWKV_P = {"g": (1, 1), "neu": (1, 1), "app": (1, 1), "chunk": (1, 1)}


def _seg_sum(x, seg_ref):
    outs = []
    for j in range(x.shape[1] // LANES):
        outs.append(_mm(x[:, j * LANES:(j + 1) * LANES], seg_ref[...], pa=2))
    return jnp.concatenate(outs, axis=1)


def _rwkv_pointwise(z, zs, mu_ref, w0_ref, w2_ref, a0_ref, a2_ref, g2_ref, kk_ref, ka_ref, seg_ref):
    zm = z + (zs - z) * mu_ref[...]
    r = zm[:, 0:MIX]
    k = zm[:, MIX:2 * MIX]
    v = zm[:, 2 * MIX:3 * MIX]
    zw = zm[:, 3 * MIX:3 * MIX + 64]
    za = zm[:, 3 * MIX + 64:3 * MIX + 128]
    zg = zm[:, 3 * MIX + 128:3 * MIX + 256]
    lw = -RWKV_DECAY * _sigmoid(w0_ref[...] + _mm(jnp.tanh(zw), w2_ref[...], pa=2))
    a = _sigmoid(a0_ref[...] + _mm(za, a2_ref[...], pa=2))
    g = _mm(_sigmoid(zg), g2_ref[...], pa=2)
    kk = k * kk_ref[...]
    kk = kk / jnp.maximum(jnp.sqrt(_seg_sum(kk * kk, seg_ref)), 1e-12)
    k = k * (1.0 + (a - 1.0) * ka_ref[...])
    return r, lw, k, v, kk, a, g


def _rwkv_finish(y, r, k, v, g, rk_ref, lnw_ref, lnb_ref, seg_ref):
    mean = _seg_sum(y, seg_ref) * (1.0 / RWKV_D)
    yc = y - mean
    var = _seg_sum(yc * yc, seg_ref) * (1.0 / RWKV_D)
    yn = yc * lax.rsqrt(var + RWKV_LN_EPS) * lnw_ref[...] + lnb_ref[...]
    bonus = _seg_sum(r * k * rk_ref[...], seg_ref) * v
    return (yn + bonus) * g


def _rwkv_kernel(z_ref, mu_ref, w0_ref, w2_ref, a0_ref, a2_ref, g2_ref, kk_ref, ka_ref, rk_ref, lnw_ref, lnb_ref,
                 seg_ref, o_ref, s_ref, tail_ref, prev_ref, st_ref, y_ref, scat_ref):
    tb = RWKV_TB
    cc = RWKV_C
    t = pl.program_id(1)

    @pl.when(t == 0)
    def _():
        prev_ref[...] = jnp.zeros_like(prev_ref)
        st_ref[...] = jnp.zeros_like(st_ref)

    z = jnp.concatenate([z_ref[j] for j in range(GROUP_TILES)], axis=1)
    zs = _shift_rows(z, prev_ref[...], 1, _iota((SUBLANES, RWKV_COLS), 0))
    prev_ref[...] = z[tb - SUBLANES:]
    tail_ref[0] = z[tb - SUBLANES:]
    r, lw, k, v, kk, a, g = _rwkv_pointwise(z, zs, mu_ref, w0_ref, w2_ref, a0_ref, a2_ref, g2_ref, kk_ref, ka_ref,
                                            seg_ref)
    b = kk * a

    pos = _iota((tb, MIX), 0) & (cc - 1)
    lg = _chunk_cumsum(lw, cc, pos)
    lg_end = _chunk_last_bcast(lg, cc, pos)
    gam = jnp.exp(lg)
    inv = jnp.exp(-lg)
    rg = r * gam
    kg = kk * jnp.exp(lg - lw)
    bi = b * inv
    ki = k * inv
    e_end = jnp.exp(lg_end - lg)
    bt = b * e_end
    kt = k * e_end
    g_end = jnp.exp(lg_end)

    ri = _iota((tb, tb), 0)
    ci = _iota((tb, tb), 1)
    same = (ri - (ri & (cc - 1))) == (ci - (ci & (cc - 1)))
    strict = same & (ci < ri)
    incl = same & (ci <= ri)
    eye = (ri == ci).astype(F32)
    eye_d = _iota((RWKV_D, RWKV_D), 0) == _iota((RWKV_D, RWKV_D), 1)

    hs = [slice(h * RWKV_D, (h + 1) * RWKV_D) for h in range(RWKV_H)]
    mg, mn, ma, mc_ = RWKV_P["g"], RWKV_P["neu"], RWKV_P["app"], RWKV_P["chunk"]
    vh = [v[:, ls] for ls in hs]
    gmat = [_mm(jnp.concatenate([kg[:, ls], rg[:, ls]], axis=0),
                jnp.concatenate([bi[:, ls], ki[:, ls]], axis=0), NT, *mg) for ls in hs]
    amat = [jnp.where(strict, gm[:tb, :tb], 0.0) for gm in gmat]
    bmat = [jnp.where(strict, gm[:tb, tb:], 0.0) for gm in gmat]
    pb_ = [jnp.where(incl, gm[tb:, :tb], 0.0) for gm in gmat]
    pk_ = [jnp.where(incl, gm[tb:, tb:], 0.0) for gm in gmat]
    bv = [_mm(bm_, v_, NN, *ma) for bm_, v_ in zip(bmat, vh)]

    def lower_left(b):
        rb = ri & (b - 1)
        cb = ci & (b - 1)
        return ((ri - rb) == (ci - cb)) & (rb >= b // 2) & (cb < b // 2)

    x = [eye - jnp.where(lower_left(2), a_, 0.0) for a_ in amat]
    b = 4
    while b <= cc:
        m = lower_left(b)
        tx = [_mm(x_, jnp.where(m, a_, 0.0), NN, *mn) for x_, a_ in zip(x, amat)]
        x = [x_ - _mm(tx_, x_, NN, *mn) for x_, tx_ in zip(x, tx)]
        b *= 2
    wu = [-_mm(x_, jnp.concatenate([kg[:, ls], bv_], axis=1), NN, *ma) for x_, ls, bv_ in zip(x, hs, bv)]
    low = _iota((tb, LANES), 1) < RWKV_D
    zero = jnp.zeros((tb, LANES), F32)

    def dup(slab, odd):
        sw = pltpu.roll(slab, RWKV_D, 1)
        return jnp.where(low, sw, slab) if odd else jnp.where(low, slab, sw)

    def upper(slab, odd):
        return jnp.where(low, zero, slab if odd else pltpu.roll(slab, RWKV_D, 1))

    pair = [slice((h // 2) * LANES, (h // 2 + 1) * LANES) for h in range(RWKV_H)]
    rg2 = [dup(rg[:, ps], h % 2) for h, ps in enumerate(pair)]
    ov = [upper(v[:, ps], h % 2) for h, ps in enumerate(pair)]
    ww = [jnp.where(low, wu_, pltpu.roll(wu_, RWKV_D, 1)) for wu_ in wu]
    qy = [_mm(pb, jnp.concatenate([ww_, wu_], axis=1), NN, *ma) for pb, ww_, wu_ in zip(pb_, ww, wu)]
    pkv = [_mm(pk, v_, NN, *ma) for pk, v_ in zip(pk_, vh)]
    qt2 = [rg2_ + qy_[:, :LANES] for rg2_, qy_ in zip(rg2, qy)]
    y0 = [qy_[:, LANES + RWKV_D:] + pkv_ for qy_, pkv_ in zip(qy, pkv)]
    nch = tb // cc
    s = [st_ref[h] for h in range(RWKV_H)]
    for c in range(nch):
        rows = slice(c * cc, (c + 1) * cc)
        for h in range(RWKV_H):
            scat_ref[h, :, c * RWKV_D:(c + 1) * RWKV_D] = s[h]
        tr = [_mm(jnp.concatenate([wu_[rows], ov_[rows]], axis=0),
                  jnp.concatenate([bt[rows, ls], kt[rows, ls]], axis=0), TN, *mc_) for ls, wu_, ov_ in zip(hs, wu, ov)]
        s = [_mm(s_, jnp.where(eye_d, g_end[c * cc:c * cc + 1, ls], 0.0) + t_[:RWKV_D], NN, *mc_) + t_[RWKV_D:]
             for s_, t_, ls in zip(s, tr, hs)]
    r5 = _iota((tb, nch * RWKV_D), 0)
    c5 = _iota((tb, nch * RWKV_D), 1)
    own = (r5 - (r5 & (cc - 1))) * (RWKV_D // cc) == c5 - (c5 & (RWKV_D - 1))
    for h, ls in enumerate(hs):
        st_ref[h] = s[h]
        qexp = jnp.where(own, jnp.concatenate([qt2[h]] * (nch * RWKV_D // LANES), axis=1), 0.0)
        y_ref[:, ls] = _mm(qexp, scat_ref[h], NT, *mc_) + y0[h]

    o_ref[...] = _rwkv_finish(y_ref[...], r, k, v, g, rk_ref, lnw_ref, lnb_ref, seg_ref).astype(BF16)

    @pl.when(t == pl.num_programs(1) - 1)
    def _():
        s_ref[0] = st_ref[...]


def _rwkv_prompt(proj, nb, nt_len, p):
    tb = RWKV_TB
    nt = nt_len // tb
    return pl.pallas_call(
        _rwkv_kernel,
        grid=(nb, nt),
        in_specs=[pl.BlockSpec((GROUP_TILES, tb, TILE), lambda b, t: (T_RWKV // GROUP_TILES, b * nt + t, 0)),
                  _const_spec((1, RWKV_COLS)), _const_spec((1, MIX)), _const_spec((64, MIX)),
                  _const_spec((1, MIX)), _const_spec((64, MIX)), _const_spec((128, MIX)),
                  _const_spec((1, MIX)), _const_spec((1, MIX)), _const_spec((1, MIX)),
                  _const_spec((1, MIX)), _const_spec((1, MIX)), _const_spec((LANES, LANES))],
        out_specs=[pl.BlockSpec((tb, MIX), lambda b, t: (b * nt + t, 0)),
                   pl.BlockSpec((1, RWKV_H, RWKV_D, RWKV_D), lambda b, t: (b, 0, 0, 0)),
                   pl.BlockSpec((1, SUBLANES, RWKV_COLS), lambda b, t: (b, 0, 0))],
        out_shape=[jax.ShapeDtypeStruct((nb * nt_len, MIX), BF16),
                   jax.ShapeDtypeStruct((nb, RWKV_H, RWKV_D, RWKV_D), F32),
                   jax.ShapeDtypeStruct((nb, SUBLANES, RWKV_COLS), F32)],
        scratch_shapes=[pltpu.VMEM((SUBLANES, RWKV_COLS), F32), pltpu.VMEM((RWKV_H, RWKV_D, RWKV_D), F32),
                        pltpu.VMEM((tb, MIX), F32), pltpu.VMEM((RWKV_H, RWKV_D, (tb // RWKV_C) * RWKV_D), F32)],
        compiler_params=_params("parallel", "arbitrary"),
        name="rwkv_prompt",
    )(proj, p["mu"], p["w0"], p["w2"], p["a0"], p["a2"], p["g2"], p["kk"], p["ka"], p["rk"], p["lnw"], p["lnb"],
      p["seg"])


SSD_TB = 256


def _ssd_pointwise(xbc_c, dt_raw, dtb_ref, aneg_ref):
    xbc = _silu(xbc_c)
    dt = _softplus(dt_raw + dtb_ref[...])
    return xbc, dt, dt * aneg_ref[...]


def _expand_heads(col8, width):
    n = col8.shape[0]
    return jnp.concatenate([jnp.broadcast_to(col8[:, h:h + 1], (n, width)) for h in range(SSD_H)], axis=1)


def _ssd_kernel(z_ref, cw_ref, cb_ref, dtb_ref, aneg_ref, dsk_ref, nw_ref, o_ref, s_ref, tail_ref, carry_ref, st_ref,
                y_ref):
    tb = SSD_TB
    cc = SSD_C
    t = pl.program_id(1)

    @pl.when(t == 0)
    def _():
        carry_ref[...] = jnp.zeros_like(carry_ref)
        st_ref[...] = jnp.zeros_like(st_ref)

    gate = jnp.concatenate([z_ref[0], z_ref[1]], axis=1)
    xbc_raw = jnp.concatenate([z_ref[2], z_ref[3], z_ref[4], z_ref[5]], axis=1)
    xbc_c = _causal_conv(xbc_raw, carry_ref[...], cw_ref, cb_ref)
    carry_ref[...] = xbc_raw[tb - SUBLANES:]
    tail_ref[0] = xbc_raw[tb - SUBLANES:]
    xbc, dt, dta = _ssd_pointwise(xbc_c, z_ref[6], dtb_ref, aneg_ref)
    xs = xbc[:, :MIX]
    bm = xbc[:, MIX:MIX + TILE]
    cm = xbc[:, MIX + TILE:]
    pos = _iota((tb, TILE), 0) & (cc - 1)
    cum = _chunk_cumsum(dta, cc, pos)
    xdt = xs * _expand_heads(dt, SSD_P)
    tri = _iota((cc, cc), 0) >= _iota((cc, cc), 1)

    hg = SSD_H // SSD_G
    nch = tb // cc
    pre = []
    for c in range(nch):
        rows = slice(c * cc, (c + 1) * cc)
        cum_c = cum[rows, :LANES]
        cum_t = cum_c.T
        for gi in range(SSD_G):
            ns = slice(gi * SSD_N, (gi + 1) * SSD_N)
            cg = cm[rows, ns]
            bg = bm[rows, ns]
            gmat = _mm(cg, bg, NT)
            yds, xds, decs, ecs = [], [], [], []
            for hh in range(hg):
                h = gi * hg + hh
                col = cum_c[:, h:h + 1]
                lmat = jnp.exp(jnp.where(tri, col - cum_t[h:h + 1, :], -jnp.inf))
                xh = xdt[rows, h * SSD_P:(h + 1) * SSD_P]
                last = cum_c[cc - 1:cc, h:h + 1]
                yds.append(_mm(gmat * lmat, xh))
                xds.append(xh * jnp.exp(last - col))
                decs.append(jnp.broadcast_to(jnp.exp(last), (SSD_P, SSD_N)))
                ecs.append(jnp.broadcast_to(jnp.exp(col), (cc, SSD_P)))
            pre.append((cg, jnp.concatenate(yds, axis=1), jnp.concatenate(ecs, axis=1),
                        jnp.concatenate(decs, axis=0), _mm(jnp.concatenate(xds, axis=1), bg, TN)))
    for c in range(nch):
        rows = slice(c * cc, (c + 1) * cc)
        for gi in range(SSD_G):
            gs = slice(gi * hg * SSD_P, (gi + 1) * hg * SSD_P)
            cg, yd, ec, dec, ds = pre[c * SSD_G + gi]
            st = st_ref[gs, :]
            y_ref[rows, gs] = yd + _mm(cg, st, NT) * ec
            st_ref[gs, :] = st * dec + ds

    y = (y_ref[...] + dsk_ref[...] * xs) * _silu(gate)
    o_ref[...] = _rms(y, nw_ref[...]).astype(BF16)

    @pl.when(t == pl.num_programs(1) - 1)
    def _():
        for h in range(SSD_H):
            s_ref[0, h] = st_ref[h * SSD_P:(h + 1) * SSD_P, :]


def _ssd_prompt(proj, nb, nt_len, cw, cb, dtb, aneg, dsk, nw):
    tb = SSD_TB
    nt = nt_len // tb
    return pl.pallas_call(
        _ssd_kernel,
        grid=(nb, nt),
        in_specs=[pl.BlockSpec((GROUP_TILES, tb, TILE), lambda b, t: (T_SSD // GROUP_TILES, b * nt + t, 0)),
                  _const_spec((CONV_W, SSD_CONV_DIM)), _const_spec((1, SSD_CONV_DIM)),
                  _const_spec((1, TILE)), _const_spec((1, TILE)), _const_spec((1, MIX)), _const_spec((1, MIX))],
        out_specs=[pl.BlockSpec((tb, MIX), lambda b, t: (b * nt + t, 0)),
                   pl.BlockSpec((1, SSD_H, SSD_P, SSD_N), lambda b, t: (b, 0, 0, 0)),
                   pl.BlockSpec((1, SUBLANES, SSD_CONV_DIM), lambda b, t: (b, 0, 0))],
        out_shape=[jax.ShapeDtypeStruct((nb * nt_len, MIX), BF16),
                   jax.ShapeDtypeStruct((nb, SSD_H, SSD_P, SSD_N), F32),
                   jax.ShapeDtypeStruct((nb, SUBLANES, SSD_CONV_DIM), F32)],
        scratch_shapes=[pltpu.VMEM((SUBLANES, SSD_CONV_DIM), F32), pltpu.VMEM((SSD_H * SSD_P, SSD_N), F32),
                        pltpu.VMEM((tb, MIX), F32)],
        compiler_params=_params("parallel", "arbitrary"),
        name="ssd_prompt",
    )(proj, cw, cb, dtb, aneg, dsk, nw)


R_GLA_V, R_GLA_GATE, R_GLA_Q = 0, 512, 1024
R_RW_K, R_RW_R, R_RW_V, R_RW_G = 1280, 1792, 2304, 2816
R_SSD_B, R_SSD_C, R_SSD_E, R_SSD_GATE, R_SSD_X = 3328, 3584, 3840, 4864, 5376
ROW_W = 5888
C_GLA_A, C_GLA_K, C_SSD_X = 0, 256, 512
COL_W = 1024
T_RW_W, T_RW_B, T_RW_K, T_RW_KK, T_RW_R, T_RW_V = 0, 512, 1024, 1536, 2048, 2560
RWT_W = 3072
DEC_RB = 8


def _dec_prep_kernel(z_ref, lconv_ref, lh_ref, rprev_ref, sconv_ref,
                     wg2_ref, bg_ref,
                     lcw_ref, lcb_ref, wa_ref, ba_ref, wi_ref, bi_ref, lam_ref,
                     mu_ref, w0_ref, w2_ref, a0_ref, a2_ref, g2_ref, kk_ref, ka_ref, seg_ref,
                     scw_ref, scb_ref, dtb_ref, aneg_ref,
                     rows_ref, colt_ref, rwt_ref, olru_ref, lh_out_ref, lconv_out_ref, sconv_out_ref):
    k = z_ref[T_GLA + 1]
    g = _gla_gates(z_ref[T_GLA + 6], wg2_ref, bg_ref)
    rows_ref[:, R_GLA_V:R_GLA_V + TILE] = z_ref[T_GLA + 2]
    rows_ref[:, R_GLA_V + TILE:R_GLA_V + MIX] = z_ref[T_GLA + 3]
    rows_ref[:, R_GLA_GATE:R_GLA_GATE + TILE] = z_ref[T_GLA + 4]
    rows_ref[:, R_GLA_GATE + TILE:R_GLA_GATE + MIX] = z_ref[T_GLA + 5]
    rows_ref[:, R_GLA_Q:R_GLA_Q + TILE] = z_ref[T_GLA + 0] * (GLA_DK ** -0.5)
    cols = [jnp.exp(g), k]

    xb = jnp.concatenate([z_ref[T_LRU + 0], z_ref[T_LRU + 1]], axis=1)
    gb = jnp.concatenate([z_ref[T_LRU + 2], z_ref[T_LRU + 3]], axis=1)
    xc = lcb_ref[...] + xb * lcw_ref[CONV_W - 1:CONV_W, :]
    for i in range(CONV_W - 1):
        xc = xc + lconv_ref[:, i * MIX:(i + 1) * MIX] * lcw_ref[i:i + 1, :]
    a, mult, gi = _lru_gates(xc, wa_ref, ba_ref, wi_ref, bi_ref, lam_ref)
    h = a * lh_ref[...] + mult * gi * xc
    lh_out_ref[...] = h
    olru_ref[...] = (h * _gelu_tanh(gb)).astype(BF16)
    lconv_out_ref[:, 0:2 * MIX] = lconv_ref[:, MIX:3 * MIX]
    lconv_out_ref[:, 2 * MIX:3 * MIX] = xb

    z = jnp.concatenate([z_ref[T_RWKV + j] for j in range(GROUP_TILES)], axis=1)
    r, lw, kmod, v, kk, a7, g7 = _rwkv_pointwise(z, rprev_ref[...], mu_ref, w0_ref, w2_ref, a0_ref, a2_ref, g2_ref,
                                                 kk_ref, ka_ref, seg_ref)
    rows_ref[:, R_RW_K:R_RW_K + MIX] = kmod
    rows_ref[:, R_RW_R:R_RW_R + MIX] = r
    rows_ref[:, R_RW_V:R_RW_V + MIX] = v
    rows_ref[:, R_RW_G:R_RW_G + MIX] = g7
    off = 0
    for vec in (jnp.exp(lw), kk * a7, kmod, kk, r, v):
        for j in range(MIX // LANES):
            rwt_ref[off:off + LANES, :] = vec[:, j * LANES:(j + 1) * LANES].T
            off += LANES

    xbc_raw = jnp.concatenate([z_ref[T_SSD + 2], z_ref[T_SSD + 3], z_ref[T_SSD + 4], z_ref[T_SSD + 5]], axis=1)
    xbc_c = scb_ref[...] + xbc_raw * scw_ref[CONV_W - 1:CONV_W, :]
    for i in range(CONV_W - 1):
        xbc_c = xbc_c + sconv_ref[:, i * SSD_CONV_DIM:(i + 1) * SSD_CONV_DIM] * scw_ref[i:i + 1, :]
    xbc, dt, dta = _ssd_pointwise(xbc_c, z_ref[T_SSD + 6], dtb_ref, aneg_ref)
    xs = xbc[:, :MIX]
    rows_ref[:, R_SSD_B:R_SSD_B + TILE] = xbc[:, MIX:MIX + TILE]
    rows_ref[:, R_SSD_C:R_SSD_C + TILE] = xbc[:, MIX + TILE:]
    rows_ref[:, R_SSD_E:R_SSD_E + SSD_H * LANES] = _expand_heads(jnp.exp(dta), LANES)
    rows_ref[:, R_SSD_GATE:R_SSD_GATE + TILE] = z_ref[T_SSD + 0]
    rows_ref[:, R_SSD_GATE + TILE:R_SSD_GATE + MIX] = z_ref[T_SSD + 1]
    rows_ref[:, R_SSD_X:R_SSD_X + MIX] = xs
    cols.append(xs * _expand_heads(dt, SSD_P))
    sconv_out_ref[:, 0:2 * SSD_CONV_DIM] = sconv_ref[:, SSD_CONV_DIM:3 * SSD_CONV_DIM]
    sconv_out_ref[:, 2 * SSD_CONV_DIM:3 * SSD_CONV_DIM] = xbc_raw

    off = 0
    for cvec in cols:
        for j in range(cvec.shape[1] // LANES):
            colt_ref[off:off + LANES, :] = cvec[:, j * LANES:(j + 1) * LANES].T
            off += LANES


def _dec_prep(zdec, lconv, lh, rprev, sconv, pp):
    n = zdec.shape[1]
    args = [zdec, lconv, lh, rprev, sconv,
            pp["gla_wg2"], pp["gla_bg"],
            pp["lru_cw"], pp["lru_cb"], pp["lru_wa"], pp["lru_ba"], pp["lru_wi"], pp["lru_bi"], pp["lru_lam"],
            pp["rwkv"]["mu"], pp["rwkv"]["w0"], pp["rwkv"]["w2"], pp["rwkv"]["a0"], pp["rwkv"]["a2"],
            pp["rwkv"]["g2"], pp["rwkv"]["kk"], pp["rwkv"]["ka"], pp["rwkv"]["seg"],
            pp["ssd_cw"], pp["ssd_cb"], pp["ssd_dtb"], pp["ssd_aneg"]]
    return pl.pallas_call(
        _dec_prep_kernel,
        out_shape=[jax.ShapeDtypeStruct((n, ROW_W), F32), jax.ShapeDtypeStruct((COL_W, n), F32),
                   jax.ShapeDtypeStruct((RWT_W, n), F32),
                   jax.ShapeDtypeStruct((n, MIX), BF16), jax.ShapeDtypeStruct((n, MIX), F32),
                   jax.ShapeDtypeStruct((n, 3 * MIX), F32), jax.ShapeDtypeStruct((n, 3 * SSD_CONV_DIM), F32)],
        compiler_params=pltpu.CompilerParams(vmem_limit_bytes=VMEM_LIMIT),
        name="decode_prep",
    )(*args)


def _dec_rwkv_kernel(w_ref, b_ref, k_ref, kk_ref, r_ref, v_ref, s_ref, *rest):
    s_out, y_ref = rest[-2:]
    w = w_ref[...]
    b = b_ref[...]
    k = k_ref[...]
    kk = kk_ref[...]
    r = r_ref[...]

    def body(vi, carry):
        s = s_ref[vi]
        sa = jnp.sum(s * kk, axis=0, keepdims=True)
        s = s * w - sa * b + v_ref[pl.ds(vi, 1), :] * k
        s_out[vi] = s
        y_ref[pl.ds(vi, 1), :] = jnp.sum(s * r, axis=0, keepdims=True)
        return carry

    lax.fori_loop(0, RWKV_D, body, 0, unroll=8)


def _dec_rwkv(rwt, sr_t, l, prev):
    n = rwt.shape[1]
    hb = MIX // RWKV_D

    def vec(off):
        return pl.BlockSpec((RWKV_D, n), lambda h: (off // RWKV_D + h, 0))

    sspec = pl.BlockSpec((None, None, RWKV_D, RWKV_D, n), lambda h: (l, h, 0, 0, 0))
    extra = [] if prev is None else [prev]
    return pl.pallas_call(
        _dec_rwkv_kernel,
        grid=(hb,),
        in_specs=[vec(T_RW_W), vec(T_RW_B), vec(T_RW_K), vec(T_RW_KK), vec(T_RW_R), vec(T_RW_V), sspec]
        + [pl.BlockSpec(memory_space=pl.ANY)] * len(extra),
        out_specs=[sspec, pl.BlockSpec((RWKV_D, n), lambda h: (h, 0))],
        out_shape=[jax.ShapeDtypeStruct(sr_t.shape, F32), jax.ShapeDtypeStruct((MIX, n), F32)],
        input_output_aliases={7: 0} if extra else {},
        compiler_params=_params("parallel"),
        name="decode_rwkv",
    )(rwt, rwt, rwt, rwt, rwt, rwt, sr_t, *extra)


def _dec_state_kernel(colt_ref, rows_ref, sg_ref, ss_ref, *rest):
    sg_out, ss_out, yg_ref, ys_ref = rest[-4:]
    pid = pl.program_id(0)
    n = colt_ref.shape[1]
    lane = _iota((1, n), 1)
    rid = _iota((DEC_RB, LANES), 0)
    rows = rows_ref[...]
    hg = SSD_H // SSD_G
    yg = [jnp.zeros((DEC_RB, GLA_DV), F32) for _ in range(GLA_H)]
    ys = [jnp.zeros((DEC_RB, SSD_P), F32) for _ in range(SSD_H)]
    for i in range(DEC_RB):
        sel = lane == pid * DEC_RB + i
        col = jnp.sum(jnp.where(sel, colt_ref[...], 0.0), axis=1, keepdims=True)
        rv = rows[i:i + 1, :]
        mine = rid == i
        for h in range(GLA_H):
            al = col[C_GLA_A + h * GLA_DK:C_GLA_A + (h + 1) * GLA_DK]
            kc = col[C_GLA_K + h * GLA_DK:C_GLA_K + (h + 1) * GLA_DK]
            vr = rv[:, R_GLA_V + h * GLA_DV:R_GLA_V + (h + 1) * GLA_DV]
            s = al * sg_ref[i, h] + kc * vr
            sg_out[i, h] = s
            q8 = rows[:, R_GLA_Q + h * GLA_DK:R_GLA_Q + (h + 1) * GLA_DK]
            yg[h] = jnp.where(mine, _mm(q8, s, NN, pa=2, pb=2), yg[h])
        for h in range(SSD_H):
            gi = h // hg
            e = rv[:, R_SSD_E + h * LANES:R_SSD_E + (h + 1) * LANES]
            bn = rv[:, R_SSD_B + gi * SSD_N:R_SSD_B + (gi + 1) * SSD_N]
            xc = col[C_SSD_X + h * SSD_P:C_SSD_X + (h + 1) * SSD_P]
            s = ss_ref[i, h] * e + xc * bn
            ss_out[i, h] = s
            c8 = rows[:, R_SSD_C + gi * SSD_N:R_SSD_C + (gi + 1) * SSD_N]
            ys[h] = jnp.where(mine[:, :SSD_P], _mm(c8, s, NT, pa=2, pb=2), ys[h])
    yg_ref[...] = jnp.concatenate(yg, axis=1)
    ys_ref[...] = jnp.concatenate(ys, axis=1)


def _dec_state(colt, rows, sg, ss, l, prev):
    n = rows.shape[0]
    blk = lambda i: (l, i, 0, 0, 0)
    specs = [pl.BlockSpec((None, DEC_RB, GLA_H, GLA_DK, GLA_DV), blk),
             pl.BlockSpec((None, DEC_RB, SSD_H, SSD_P, SSD_N), blk)]
    extra = [] if prev is None else list(prev)
    rowspec = pl.BlockSpec((DEC_RB, MIX), lambda i: (i, 0))
    return pl.pallas_call(
        _dec_state_kernel,
        grid=(n // DEC_RB,),
        in_specs=[_const_spec((COL_W, n)), pl.BlockSpec((DEC_RB, ROW_W), lambda i: (i, 0))] + specs
        + [pl.BlockSpec(memory_space=pl.ANY)] * len(extra),
        out_specs=specs + [rowspec, rowspec],
        out_shape=[jax.ShapeDtypeStruct(sg.shape, F32), jax.ShapeDtypeStruct(ss.shape, F32),
                   jax.ShapeDtypeStruct((n, MIX), F32), jax.ShapeDtypeStruct((n, MIX), F32)],
        input_output_aliases={4 + j: j for j in range(len(extra))},
        compiler_params=_params("parallel"),
        name="decode_state",
    )(colt, rows, sg, ss, *extra)


def _dec_finish_kernel(rows_ref, yg_ref, yrt_ref, ys_ref, gnw_ref, rk_ref, lnw_ref, lnb_ref, seg_ref,
                       dsk_ref, snw_ref, og_ref, or_ref, os_ref):
    outs = []
    for h in range(GLA_H):
        outs.append(_rms(yg_ref[:, h * GLA_DV:(h + 1) * GLA_DV], gnw_ref[...]))
    og_ref[...] = (jnp.concatenate(outs, axis=1) * _silu(rows_ref[:, R_GLA_GATE:R_GLA_GATE + MIX])).astype(BF16)
    yr = jnp.concatenate([yrt_ref[j * LANES:(j + 1) * LANES, :].T for j in range(MIX // LANES)], axis=1)
    or_ref[...] = _rwkv_finish(yr, rows_ref[:, R_RW_R:R_RW_R + MIX], rows_ref[:, R_RW_K:R_RW_K + MIX],
                               rows_ref[:, R_RW_V:R_RW_V + MIX], rows_ref[:, R_RW_G:R_RW_G + MIX],
                               rk_ref, lnw_ref, lnb_ref, seg_ref).astype(BF16)
    y = ((ys_ref[...] + dsk_ref[...] * rows_ref[:, R_SSD_X:R_SSD_X + MIX])
         * _silu(rows_ref[:, R_SSD_GATE:R_SSD_GATE + MIX]))
    os_ref[...] = _rms(y, snw_ref[...]).astype(BF16)


def _dec_finish(rows, yg, yrt, ys, pp):
    n = rows.shape[0]
    o = jax.ShapeDtypeStruct((n, MIX), BF16)
    return pl.pallas_call(
        _dec_finish_kernel,
        out_shape=[o, o, o],
        compiler_params=pltpu.CompilerParams(vmem_limit_bytes=VMEM_LIMIT),
        name="decode_finish",
    )(rows, yg, yrt, ys, pp["gla_nw"], pp["rwkv"]["rk"], pp["rwkv"]["lnw"], pp["rwkv"]["lnb"], pp["rwkv"]["seg"],
      pp["ssd_dsk"], pp["ssd_nw"])


def _block_diag(w4):
    out = jnp.zeros((TILE, TILE), w4.dtype)
    for j in range(4):
        out = out.at[j * 64:(j + 1) * 64, j * 64:(j + 1) * 64].set(w4[j])
    return out


def _prep_weights(w_in, w_out, w_up, w_down):
    o_lru = GLA_COLS
    o_rwkv = GLA_COLS + LRU_COLS
    o_ssd = o_rwkv + RWKV_COLS
    wt = jnp.transpose(w_in, (0, 2, 1))
    zpad = lambda n: jnp.zeros((DEPTH, n, D_MODEL), w_in.dtype)
    w_perm = jnp.concatenate([
        wt[:, 0:GLA_COLS], zpad(GROUP_TILES * TILE - GLA_COLS),
        wt[:, o_rwkv:o_rwkv + RWKV_COLS],
        wt[:, o_ssd:o_ssd + SSD_COLS], zpad(GROUP_TILES * TILE - SSD_COLS),
        wt[:, o_lru:o_lru + LRU_COLS]], axis=1).astype(BF16)
    return {"w_in": w_perm,
            "w_out": w_out.astype(BF16),
            "w_up": w_up.astype(BF16),
            "w_down": w_down.astype(BF16)}


def _prep_layer(l, norm_mix_pre, norm_mix_post, norm_mlp_pre, norm_mlp_post,
                gla_w_gate2, gla_b_gate, gla_norm, lru_conv_w, lru_conv_b, lru_w_a, lru_b_a, lru_w_i, lru_b_i,
                lru_lambda, rwkv_mu, rwkv_w0, rwkv_w2, rwkv_a0, rwkv_a2, rwkv_g2, rwkv_k_k, rwkv_k_a, rwkv_r_k,
                rwkv_ln_w, rwkv_ln_b, ssd_conv_w, ssd_conv_b, ssd_dt_bias, ssd_a_log, ssd_d, ssd_norm):
    row = lambda a: a.reshape(1, -1).astype(F32)
    ii = jnp.arange(TILE)[:, None]
    jj = jnp.arange(MIX)[None, :]
    seg_i = jnp.arange(LANES)
    pp = {
        "n_mix_pre": row(norm_mix_pre[l]), "n_mix_post": row(norm_mix_post[l]),
        "n_mlp_pre": row(norm_mlp_pre[l]), "n_mlp_post": row(norm_mlp_post[l]),
        "gla_wg2": jnp.zeros((TILE, TILE), F32).at[:GLA_RANK, :].set(gla_w_gate2[l]).astype(BF16),
        "gla_bg": row(gla_b_gate[l]),
        "gla_nw": row(gla_norm[l]),
        "gla_ones": (ii // GLA_DK == jj // GLA_DV).astype(BF16),
        "lru_cw": lru_conv_w[l].astype(F32), "lru_cb": row(lru_conv_b[l]),
        "lru_wa": jnp.stack([_block_diag(lru_w_a[l, 0:4]), _block_diag(lru_w_a[l, 4:8])]).astype(BF16),
        "lru_wi": jnp.stack([_block_diag(lru_w_i[l, 0:4]), _block_diag(lru_w_i[l, 4:8])]).astype(BF16),
        "lru_ba": row(lru_b_a[l]), "lru_bi": row(lru_b_i[l]), "lru_lam": row(lru_lambda[l]),
        "rwkv": {
            "mu": row(rwkv_mu[l]), "w0": row(rwkv_w0[l]), "w2": rwkv_w2[l].astype(BF16),
            "a0": row(rwkv_a0[l]), "a2": rwkv_a2[l].astype(BF16), "g2": rwkv_g2[l].astype(BF16),
            "kk": row(rwkv_k_k[l]), "ka": row(rwkv_k_a[l]), "rk": row(rwkv_r_k[l]),
            "lnw": row(rwkv_ln_w[l]), "lnb": row(rwkv_ln_b[l]),
            "seg": (seg_i[:, None] // RWKV_D == seg_i[None, :] // RWKV_D).astype(BF16),
        },
        "ssd_cw": ssd_conv_w[l].astype(F32), "ssd_cb": row(ssd_conv_b[l]),
        "ssd_dtb": jnp.zeros((1, TILE), F32).at[0, :SSD_H].set(ssd_dt_bias[l]),
        "ssd_aneg": jnp.zeros((1, TILE), F32).at[0, :SSD_H].set(-jnp.exp(ssd_a_log[l].astype(F32))),
        "ssd_dsk": jnp.repeat(ssd_d[l].astype(F32), SSD_P).reshape(1, MIX),
        "ssd_nw": row(ssd_norm[l]),
    }
    return pp


def _layer_prompt(x2d, nb, nt_len, wts, l, pp, tm_proj=1024, tm_out=512, tm_mlp=512, tf=1024):
    proj = _proj(x2d, pp["n_mix_pre"], wts["w_in"], l, tm_proj)
    o_gla, s_gla = _gla_prompt(proj, nb, nt_len, pp["gla_wg2"], pp["gla_bg"], pp["gla_nw"], pp["gla_ones"])
    o_lru, h_lru, lru_tail = _lru_prompt(proj, nb, nt_len, pp["lru_cw"], pp["lru_cb"], pp["lru_wa"], pp["lru_ba"],
                                         pp["lru_wi"], pp["lru_bi"], pp["lru_lam"])
    o_rwkv, s_rwkv, rwkv_tail = _rwkv_prompt(proj, nb, nt_len, pp["rwkv"])
    o_ssd, s_ssd, ssd_tail = _ssd_prompt(proj, nb, nt_len, pp["ssd_cw"], pp["ssd_cb"], pp["ssd_dtb"],
                                         pp["ssd_aneg"], pp["ssd_dsk"], pp["ssd_nw"])
    x1, h2 = _outproj((o_gla, o_lru, o_rwkv, o_ssd), wts["w_out"], l, x2d, pp["n_mix_post"], pp["n_mlp_pre"], tm_out)
    x2 = _mlp(h2, wts["w_up"], wts["w_down"], l, x1, pp["n_mlp_post"], tm_mlp, tf)
    keep = SUBLANES - (CONV_W - 1)
    states = (s_gla, h_lru[:, 0, :], lru_tail[:, keep:, :], rwkv_tail[:, SUBLANES - 1, :], s_rwkv,
              ssd_tail[:, keep:, :], s_ssd)
    return x2, states


def _layer_decode(x2d, big, small, wts, l, pp, prev, tf=1024):
    sg, sr_t, ss = big
    lh, lconv, rprev, sconv = small
    n = x2d.shape[0]
    proj = _proj(x2d, pp["n_mix_pre"], wts["w_in"], l, n)
    rows, colt, rwt, o_lru, lh_new, lconv_new, sconv_new = _dec_prep(
        proj, lconv.reshape(n, -1), lh, rprev, sconv.reshape(n, -1), pp)
    sr_new, yrt = _dec_rwkv(rwt, sr_t, l, None if prev is None else prev[1])
    sg_new, ss_new, yg, ys = _dec_state(colt, rows, sg, ss, l, None if prev is None else (prev[0], prev[2]))
    o_gla, o_rwkv, o_ssd = _dec_finish(rows, yg, yrt, ys, pp)
    x1, h2 = _outproj((o_gla, o_lru, o_rwkv, o_ssd), wts["w_out"], l, x2d, pp["n_mix_post"], pp["n_mlp_pre"], n)
    x2 = _mlp(h2, wts["w_up"], wts["w_down"], l, x1, pp["n_mlp_post"], n, tf)
    rshift_new = jnp.transpose(proj[T_RWKV:T_RWKV + GROUP_TILES], (1, 0, 2)).reshape(n, RWKV_COLS)
    small_new = (lh_new, lconv_new.reshape(n, CONV_W - 1, MIX), rshift_new,
                 sconv_new.reshape(n, CONV_W - 1, SSD_CONV_DIM))
    return x2, (sg_new, sr_new, ss_new), small_new


def kernel(x_prompt, x_sample, state_gla, state_lru, cache_lru_conv, cache_rwkv_shift, state_rwkv, cache_ssd_conv, state_ssd, norm_mix_pre, norm_mix_post, norm_mlp_pre, norm_mlp_post, w_in, w_out, w_up, w_down, gla_w_gate2, gla_b_gate, gla_norm, lru_conv_w, lru_conv_b, lru_w_a, lru_b_a, lru_w_i, lru_b_i, lru_lambda, rwkv_mu, rwkv_w0, rwkv_w2, rwkv_a0, rwkv_a2, rwkv_g2, rwkv_k_k, rwkv_k_a, rwkv_r_k, rwkv_ln_w, rwkv_ln_b, ssd_conv_w, ssd_conv_b, ssd_dt_bias, ssd_a_log, ssd_d, ssd_norm):
    nb, nt_len, _ = x_prompt.shape
    nd = x_sample.shape[0]
    yp = x_prompt.reshape(nb * nt_len, D_MODEL)
    ys = x_sample.reshape(nd, D_MODEL)
    wts = _prep_weights(w_in, w_out, w_up, w_down)
    big = (state_gla, jnp.transpose(state_rwkv, (0, 2, 3, 4, 1)), state_ssd)
    new_p, small_s, big_s = [], [], None
    for l in range(DEPTH):
        pp = _prep_layer(l, norm_mix_pre, norm_mix_post, norm_mlp_pre, norm_mlp_post,
                         gla_w_gate2, gla_b_gate, gla_norm, lru_conv_w, lru_conv_b, lru_w_a, lru_b_a, lru_w_i,
                         lru_b_i, lru_lambda, rwkv_mu, rwkv_w0, rwkv_w2, rwkv_a0, rwkv_a2, rwkv_g2, rwkv_k_k,
                         rwkv_k_a, rwkv_r_k, rwkv_ln_w, rwkv_ln_b, ssd_conv_w, ssd_conv_b, ssd_dt_bias, ssd_a_log,
                         ssd_d, ssd_norm)
        yp, st_p = _layer_prompt(yp, nb, nt_len, wts, l, pp)
        small = (state_lru[l], cache_lru_conv[l], cache_rwkv_shift[l], cache_ssd_conv[l])
        ys, big_s, sm = _layer_decode(ys, big, small, wts, l, pp, big_s)
        new_p.append(st_p)
        small_s.append(sm)

    def stack(sts, i):
        return jnp.stack([s[i] for s in sts], axis=0)

    gla_s, rwkv_s, ssd_s = big_s
    rwkv_s = jnp.transpose(rwkv_s, (0, 4, 1, 2, 3))
    return (yp.reshape(nb, nt_len, D_MODEL), ys.reshape(nd, 1, D_MODEL),
            stack(new_p, 0), gla_s, stack(new_p, 1), stack(small_s, 0),
            stack(new_p, 2), stack(small_s, 1), stack(new_p, 3), stack(small_s, 2),
            stack(new_p, 4), rwkv_s, stack(new_p, 5), stack(small_s, 3),
            stack(new_p, 6), ssd_s)
```

```python
import functools
import math

import jax
import jax.numpy as jnp
from jax import lax
from jax.experimental import pallas as pl
from jax.experimental.pallas import tpu as pltpu

F32 = jnp.float32
BF16 = jnp.bfloat16

D_MODEL = 2048
D_FF = 4 * D_MODEL
DEPTH = 2
EPS = 1e-6
MIX = D_MODEL // 4
CONV_W = 4
GLA_H, GLA_DK, GLA_DV, GLA_RANK, GLA_TEMP, GLA_C = 4, 64, 128, 16, 16.0, 16
LRU_C = 8.0
RWKV_H, RWKV_D = 8, 64
RWKV_DECAY = math.exp(-0.5)
RWKV_LN_EPS = 64e-5
RWKV_C = 32
SSD_H, SSD_P, SSD_G, SSD_N, SSD_C = 8, 64, 2, 128, 64
GLA_COLS = 2 * GLA_H * GLA_DK + 2 * MIX + GLA_RANK
LRU_COLS = 2 * MIX
RWKV_COLS = 3 * MIX + 64 + 64 + 128
SSD_CONV_DIM = MIX + 2 * SSD_G * SSD_N
SSD_COLS = MIX + SSD_CONV_DIM + SSD_H

LANES = 128
SUBLANES = 8
TILE = 2 * LANES
GROUP_TILES = 7
T_GLA, T_RWKV, T_SSD, T_LRU = 0, 7, 14, 21
N_TILES = 25
VMEM_LIMIT = 56 * 1024 * 1024

NN = (((1,), (0,)), ((), ()))
NT = (((1,), (1,)), ((), ()))
TN = (((0,), (0,)), ((), ()))


def _split(x, n):
    if x.dtype == BF16:
        return [x]
    parts, r = [], x
    for i in range(n):
        p = r.astype(BF16)
        parts.append(p)
        if i + 1 < n:
            r = r - p.astype(F32)
    return parts


def _mm(a, b, dn=NN, pa=1, pb=1):
    aa, bb = _split(a, pa), _split(b, pb)
    acc = None
    for i, x in enumerate(aa):
        for j, y in enumerate(bb):
            if i + j >= max(len(aa), len(bb)):
                continue
            t = lax.dot_general(x, y, dn, preferred_element_type=F32)
            acc = t if acc is None else acc + t
    return acc


def _iota(shape, dim):
    return lax.broadcasted_iota(jnp.int32, shape, dim)


def _roll0(x, s):
    n = x.shape[0]
    s = s % n
    return x if s == 0 else pltpu.roll(x, s, 0)


def _rms(x, w):
    ms = jnp.mean(x * x, axis=-1, keepdims=True)
    return x * lax.rsqrt(ms + EPS) * w


def _sigmoid(x):
    return jax.nn.sigmoid(x)


def _silu(x):
    return x * jax.nn.sigmoid(x)


def _softplus(x):
    return jnp.maximum(x, 0.0) + jnp.log1p(jnp.exp(-jnp.abs(x)))


def _log_sigmoid(x):
    return jnp.minimum(x, 0.0) - jnp.log1p(jnp.exp(-jnp.abs(x)))


def _gelu_tanh(x):
    c = math.sqrt(2.0 / math.pi)
    return x * (0.5 * (1.0 + jnp.tanh(c * (x + 0.044715 * (x * x * x)))))


def _neg_expm1(x):
    return -jnp.tanh(0.5 * x) * (jnp.exp(x) + 1.0)


def _chunk_cumsum(x, chunk, pos):
    d = 1
    while d < chunk:
        x = x + jnp.where(pos >= d, _roll0(x, d), 0.0)
        d *= 2
    return x


def _chunk_last_bcast(x, chunk, pos):
    n = x.shape[0]
    y = jnp.where(pos == chunk - 1, x, 0.0)
    d = 1
    while d < chunk:
        y = y + _roll0(y, n - d)
        d *= 2
    return y


def _shift_rows(x, carry, i, row8):
    xs = _roll0(x, i)
    cs = _roll0(carry, i)
    top = jnp.where(row8 < i, cs, xs[:SUBLANES])
    return jnp.concatenate([top, xs[SUBLANES:]], axis=0)


def _causal_conv(x, carry, w_ref, b_ref):
    row8 = _iota((SUBLANES, x.shape[1]), 0)
    y = b_ref[...] + x * w_ref[CONV_W - 1:CONV_W, :]
    for i in range(1, CONV_W):
        y = y + _shift_rows(x, carry, i, row8) * w_ref[CONV_W - 1 - i:CONV_W - i, :]
    return y


def _params(*sem):
    return pltpu.CompilerParams(dimension_semantics=sem, vmem_limit_bytes=VMEM_LIMIT)


def _const_spec(shape):
    nd = len(shape)
    return pl.BlockSpec(shape, lambda *_: (0,) * nd)


PROJ_TILES_PER_STEP = 5


def _proj_kernel(x_ref, nw_ref, w_ref, o_ref, h_ref):
    @pl.when(pl.program_id(1) == 0)
    def _():
        h_ref[...] = _rms(x_ref[...], nw_ref[...]).astype(BF16)

    res = lax.dot_general(h_ref[...], w_ref[...], NT, preferred_element_type=F32)
    for j in range(PROJ_TILES_PER_STEP):
        o_ref[j] = res[:, j * TILE:(j + 1) * TILE]


def _proj(x2d, nw, w, l, tm):
    m = x2d.shape[0]
    tn = PROJ_TILES_PER_STEP * TILE
    return pl.pallas_call(
        _proj_kernel,
        grid=(m // tm, N_TILES // PROJ_TILES_PER_STEP),
        in_specs=[pl.BlockSpec((tm, D_MODEL), lambda i, j: (i, 0)),
                  _const_spec((1, D_MODEL)),
                  pl.BlockSpec((None, tn, D_MODEL), lambda i, j: (l, j, 0))],
        out_specs=pl.BlockSpec((PROJ_TILES_PER_STEP, tm, TILE), lambda i, j: (j, i, 0)),
        out_shape=jax.ShapeDtypeStruct((N_TILES, m, TILE), F32),
        scratch_shapes=[pltpu.VMEM((tm, D_MODEL), BF16)],
        compiler_params=_params("parallel", "arbitrary"),
        name="norm_proj",
    )(x2d, nw, w)


OUTPROJ_SUB_ROWS = 128

def _outproj_kernel(m0, m1, m2, m3, w_ref, x_ref, n1_ref, n2_ref, x1_ref, h2_ref):
    tm = x_ref.shape[0]
    sub = min(tm, OUTPROJ_SUB_ROWS)
    for r0 in range(0, tm, sub):
        rs = slice(r0, r0 + sub)
        mix = jnp.concatenate([m0[rs, :], m1[rs, :], m2[rs, :], m3[rs, :]], axis=1)
        acc = jnp.dot(mix, w_ref[...], preferred_element_type=F32)
        x1 = x_ref[rs, :] + _rms(acc, n1_ref[...])
        x1_ref[rs, :] = x1
        h2_ref[rs, :] = _rms(x1, n2_ref[...]).astype(BF16)


def _outproj(mix, w_out, l, x2d, n1, n2, tm):
    m = x2d.shape[0]
    row = lambda i: (i, 0)
    return pl.pallas_call(
        _outproj_kernel,
        grid=(m // tm,),
        in_specs=[pl.BlockSpec((tm, MIX), row)] * 4 + [
            pl.BlockSpec((None, D_MODEL, D_MODEL), lambda i: (l, 0, 0)), pl.BlockSpec((tm, D_MODEL), row),
            _const_spec((1, D_MODEL)), _const_spec((1, D_MODEL))],
        out_specs=[pl.BlockSpec((tm, D_MODEL), row), pl.BlockSpec((tm, D_MODEL), row)],
        out_shape=[jax.ShapeDtypeStruct((m, D_MODEL), F32), jax.ShapeDtypeStruct((m, D_MODEL), BF16)],
        compiler_params=_params("parallel"),
        name="out_proj",
    )(*mix, w_out, x2d, n1, n2)


def _mlp_kernel(h_ref, wu_ref, wd_ref, x1_ref, nw_ref, o_ref, acc_ref):
    f = pl.program_id(1)

    @pl.when(f == 0)
    def _():
        acc_ref[...] = jnp.zeros_like(acc_ref)

    u = jnp.maximum(jnp.dot(h_ref[...], wu_ref[...], preferred_element_type=F32), 0.0)
    acc_ref[...] += jnp.dot((u * u).astype(BF16), wd_ref[...], preferred_element_type=F32)

    @pl.when(f == pl.num_programs(1) - 1)
    def _():
        o_ref[...] = x1_ref[...] + _rms(acc_ref[...], nw_ref[...])


def _mlp(h2, w_up, w_down, l, x1, nw, tm, tf):
    m = h2.shape[0]
    return pl.pallas_call(
        _mlp_kernel,
        grid=(m // tm, D_FF // tf),
        in_specs=[pl.BlockSpec((tm, D_MODEL), lambda i, f: (i, 0)),
                  pl.BlockSpec((None, D_MODEL, tf), lambda i, f: (l, 0, f)),
                  pl.BlockSpec((None, tf, D_MODEL), lambda i, f: (l, f, 0)),
                  pl.BlockSpec((tm, D_MODEL), lambda i, f: (i, 0)),
                  _const_spec((1, D_MODEL))],
        out_specs=pl.BlockSpec((tm, D_MODEL), lambda i, f: (i, 0)),
        out_shape=jax.ShapeDtypeStruct((m, D_MODEL), F32),
        scratch_shapes=[pltpu.VMEM((tm, D_MODEL), F32)],
        compiler_params=_params("parallel", "arbitrary"),
        name="mlp",
    )(h2, w_up, w_down, x1, nw)


GLA_TB = 256


def _gla_gates(ad, wg2_ref, bg_ref):
    x = _mm(ad, wg2_ref[...], pa=2) + bg_ref[...]
    return _log_sigmoid(x) * (1.0 / GLA_TEMP)


def _gla_kernel(z_ref, wg2_ref, bg_ref, nw_ref, ones_ref, o_ref, s_ref, st_ref, y_ref):
    tb = GLA_TB
    t = pl.program_id(1)

    @pl.when(t == 0)
    def _():
        st_ref[...] = jnp.zeros_like(st_ref)

    q = z_ref[0] * (GLA_DK ** -0.5)
    k = z_ref[1]
    v = jnp.concatenate([z_ref[2], z_ref[3]], axis=1)
    gate = jnp.concatenate([z_ref[4], z_ref[5]], axis=1)
    g = _gla_gates(z_ref[6], wg2_ref, bg_ref)
    pos = _iota((tb, TILE), 0) & (GLA_C - 1)
    cum = _chunk_cumsum(g, GLA_C, pos)

    y = None
    for j in range(GLA_C):
        if j == 0:
            term = q * k
            vj = v
        else:
            e = jnp.exp(jnp.where(pos >= j, cum - _roll0(cum, j), -jnp.inf))
            term = q * _roll0(k, j) * e
            vj = _roll0(v, j)
        sc = _mm(term, ones_ref[...])
        y = sc * vj if y is None else y + sc * vj
    y_ref[...] = y

    qh = q * jnp.exp(cum)
    last = _chunk_last_bcast(cum, GLA_C, pos)
    kt = k * jnp.exp(last - cum)
    dec = jnp.exp(last)
    kss = [slice(h * GLA_DK, (h + 1) * GLA_DK) for h in range(GLA_H)]
    vss = [slice(h * GLA_DV, (h + 1) * GLA_DV) for h in range(GLA_H)]
    nch = tb // GLA_C
    ds = [[_mm(v[c * GLA_C:(c + 1) * GLA_C, vss[h]], kt[c * GLA_C:(c + 1) * GLA_C, kss[h]], TN)
           for h in range(GLA_H)] for c in range(nch)]
    st = [st_ref[h] for h in range(GLA_H)]
    for c in range(nch):
        rows = slice(c * GLA_C, (c + 1) * GLA_C)
        for h in range(GLA_H):
            y_ref[rows, vss[h]] += _mm(qh[rows, kss[h]], st[h], NT)
            st[h] = st[h] * dec[c * GLA_C:c * GLA_C + 1, kss[h]] + ds[c][h]
    for h in range(GLA_H):
        st_ref[h] = st[h]

    yy = y_ref[...]
    outs = []
    for h in range(GLA_H):
        vs = slice(h * GLA_DV, (h + 1) * GLA_DV)
        outs.append(_rms(yy[:, vs], nw_ref[...]))
    o_ref[...] = (jnp.concatenate(outs, axis=1) * _silu(gate)).astype(BF16)

    @pl.when(t == pl.num_programs(1) - 1)
    def _():
        for h in range(GLA_H):
            s_ref[0, h] = st_ref[h].T


def _gla_prompt(proj, nb, nt_len, wg2, bg, nw, ones):
    tb = GLA_TB
    nt = nt_len // tb
    return pl.pallas_call(
        _gla_kernel,
        grid=(nb, nt),
        in_specs=[pl.BlockSpec((GROUP_TILES, tb, TILE), lambda b, t: (T_GLA // GROUP_TILES, b * nt + t, 0)),
                  _const_spec((TILE, TILE)), _const_spec((1, TILE)), _const_spec((1, GLA_DV)),
                  _const_spec((TILE, MIX))],
        out_specs=[pl.BlockSpec((tb, MIX), lambda b, t: (b * nt + t, 0)),
                   pl.BlockSpec((1, GLA_H, GLA_DK, GLA_DV), lambda b, t: (b, 0, 0, 0))],
        out_shape=[jax.ShapeDtypeStruct((nb * nt_len, MIX), BF16),
                   jax.ShapeDtypeStruct((nb, GLA_H, GLA_DK, GLA_DV), F32)],
        scratch_shapes=[pltpu.VMEM((GLA_H, GLA_DV, GLA_DK), F32), pltpu.VMEM((tb, MIX), F32)],
        compiler_params=_params("parallel", "arbitrary"),
        name="gla_prompt",
    )(proj, wg2, bg, nw, ones)


LRU_TB = 256


def _lru_gates(xc, wa_ref, ba_ref, wi_ref, bi_ref, lam_ref):
    ra, ri = [], []
    for j in range(2):
        xs = xc[:, j * TILE:(j + 1) * TILE]
        ra.append(_mm(xs, wa_ref[j], pa=2))
        ri.append(_mm(xs, wi_ref[j], pa=2))
    r = _sigmoid(jnp.concatenate(ra, axis=1) + ba_ref[...])
    i = _sigmoid(jnp.concatenate(ri, axis=1) + bi_ref[...])
    log_a = -LRU_C * r * _softplus(-lam_ref[...])
    a = jnp.exp(log_a)
    mult = jnp.sqrt(_neg_expm1(2.0 * log_a))
    return a, mult, i


def _lru_kernel(x0_ref, x1_ref, g0_ref, g1_ref, cw_ref, cb_ref, wa_ref, ba_ref, wi_ref, bi_ref, lam_ref,
                o_ref, h_ref, tail_ref, carry_ref, hc_ref):
    tb = LRU_TB
    t = pl.program_id(1)

    @pl.when(t == 0)
    def _():
        carry_ref[...] = jnp.zeros_like(carry_ref)
        hc_ref[...] = jnp.zeros_like(hc_ref)

    xb = jnp.concatenate([x0_ref[...], x1_ref[...]], axis=1)
    gb = jnp.concatenate([g0_ref[...], g1_ref[...]], axis=1)
    xc = _causal_conv(xb, carry_ref[...], cw_ref, cb_ref)
    carry_ref[...] = xb[tb - SUBLANES:]
    a, mult, i = _lru_gates(xc, wa_ref, ba_ref, wi_ref, bi_ref, lam_ref)
    row = _iota((tb, MIX), 0)
    mult = jnp.where((row == 0) & (t == 0), 1.0, mult)
    b = mult * i * xc
    pos8 = row & (SUBLANES - 1)

    def roll8(x, d):
        return pltpu.roll(x.reshape(tb // SUBLANES, SUBLANES, MIX), d, 1).reshape(tb, MIX)

    d = 1
    while d < SUBLANES:
        m = pos8 >= d
        b = jnp.where(m, a * roll8(b, d) + b, b)
        a = jnp.where(m, a * roll8(a, d), a)
        d *= 2
    hl = hc_ref[0:1, :]
    groups = []
    for gidx in range(tb // SUBLANES):
        rs = slice(gidx * SUBLANES, (gidx + 1) * SUBLANES)
        hg = b[rs] + a[rs] * hl
        groups.append(hg)
        hl = hg[SUBLANES - 1:SUBLANES, :]
    h = jnp.concatenate(groups, axis=0)
    hc_ref[...] = jnp.broadcast_to(hl, hc_ref.shape)
    o_ref[...] = (h * _gelu_tanh(gb)).astype(BF16)
    h_ref[0] = hl
    tail_ref[0] = xb[tb - SUBLANES:]


def _lru_prompt(proj, nb, nt_len, cw, cb, wa, ba, wi, bi, lam):
    tb = LRU_TB
    nt = nt_len // tb

    def tile(j):
        return pl.BlockSpec((None, tb, TILE), lambda b, t: (T_LRU + j, b * nt + t, 0))

    return pl.pallas_call(
        _lru_kernel,
        grid=(nb, nt),
        in_specs=[tile(0), tile(1), tile(2), tile(3),
                  _const_spec((CONV_W, MIX)), _const_spec((1, MIX)),
                  _const_spec((2, TILE, TILE)), _const_spec((1, MIX)),
                  _const_spec((2, TILE, TILE)), _const_spec((1, MIX)), _const_spec((1, MIX))],
        out_specs=[pl.BlockSpec((tb, MIX), lambda b, t: (b * nt + t, 0)),
                   pl.BlockSpec((1, 1, MIX), lambda b, t: (b, 0, 0)),
                   pl.BlockSpec((1, SUBLANES, MIX), lambda b, t: (b, 0, 0))],
        out_shape=[jax.ShapeDtypeStruct((nb * nt_len, MIX), BF16), jax.ShapeDtypeStruct((nb, 1, MIX), F32),
                   jax.ShapeDtypeStruct((nb, SUBLANES, MIX), F32)],
        scratch_shapes=[pltpu.VMEM((SUBLANES, MIX), F32), pltpu.VMEM((SUBLANES, MIX), F32)],
        compiler_params=_params("parallel", "arbitrary"),
        name="lru_prompt",
    )(proj, proj, proj, proj, cw, cb, wa, ba, wi, bi, lam)


RWKV_TB = 256
RWKV_BS = 128
RWKV_P = {"g": (1, 1), "neu": (1, 1), "app": (1, 1), "chunk": (1, 1)}


def _seg_sum(x, seg_ref):
    outs = []
    for j in range(x.shape[1] // LANES):
        outs.append(_mm(x[:, j * LANES:(j + 1) * LANES], seg_ref[...], pa=2))
    return jnp.concatenate(outs, axis=1)


def _rwkv_pointwise(z, zs, mu_ref, w0_ref, w2_ref, a0_ref, a2_ref, g2_ref, kk_ref, ka_ref, seg_ref):
    zm = z + (zs - z) * mu_ref[...]
    r = zm[:, 0:MIX]
    k = zm[:, MIX:2 * MIX]
    v = zm[:, 2 * MIX:3 * MIX]
    zw = zm[:, 3 * MIX:3 * MIX + 64]
    za = zm[:, 3 * MIX + 64:3 * MIX + 128]
    zg = zm[:, 3 * MIX + 128:3 * MIX + 256]
    lw = -RWKV_DECAY * _sigmoid(w0_ref[...] + _mm(jnp.tanh(zw), w2_ref[...], pa=2))
    a = _sigmoid(a0_ref[...] + _mm(za, a2_ref[...], pa=2))
    g = _mm(_sigmoid(zg), g2_ref[...], pa=2)
    kk = k * kk_ref[...]
    kk = kk / jnp.maximum(jnp.sqrt(_seg_sum(kk * kk, seg_ref)), 1e-12)
    k = k * (1.0 + (a - 1.0) * ka_ref[...])
    return r, lw, k, v, kk, a, g


def _rwkv_finish(y, r, k, v, g, rk_ref, lnw_ref, lnb_ref, seg_ref):
    mean = _seg_sum(y, seg_ref) * (1.0 / RWKV_D)
    yc = y - mean
    var = _seg_sum(yc * yc, seg_ref) * (1.0 / RWKV_D)
    yn = yc * lax.rsqrt(var + RWKV_LN_EPS) * lnw_ref[...] + lnb_ref[...]
    bonus = _seg_sum(r * k * rk_ref[...], seg_ref) * v
    return (yn + bonus) * g


def _rwkv_kernel(z_ref, mu_ref, w0_ref, w2_ref, a0_ref, a2_ref, g2_ref, kk_ref, ka_ref, rk_ref, lnw_ref, lnb_ref,
                 seg_ref, o_ref, s_ref, tail_ref, prev_ref, st_ref, y_ref, scat_ref):
    tb = RWKV_TB
    cc = RWKV_C
    t = pl.program_id(1)

    @pl.when(t == 0)
    def _():
        prev_ref[...] = jnp.zeros_like(prev_ref)
        st_ref[...] = jnp.zeros_like(st_ref)

    z = jnp.concatenate([z_ref[j] for j in range(GROUP_TILES)], axis=1)
    zs = _shift_rows(z, prev_ref[...], 1, _iota((SUBLANES, RWKV_COLS), 0))
    prev_ref[...] = z[tb - SUBLANES:]
    tail_ref[0] = z[tb - SUBLANES:]
    r, lw, k, v, kk, a, g = _rwkv_pointwise(z, zs, mu_ref, w0_ref, w2_ref, a0_ref, a2_ref, g2_ref, kk_ref, ka_ref,
                                            seg_ref)
    b = kk * a

    pos = _iota((tb, MIX), 0) & (cc - 1)
    lg = _chunk_cumsum(lw, cc, pos)
    lg_end = _chunk_last_bcast(lg, cc, pos)
    gam = jnp.exp(lg)
    inv = jnp.exp(-lg)
    rg = r * gam
    kg = kk * jnp.exp(lg - lw)
    bi = b * inv
    ki = k * inv
    e_end = jnp.exp(lg_end - lg)
    bt = b * e_end
    kt = k * e_end
    g_end = jnp.exp(lg_end)

    bs = RWKV_BS
    ri = _iota((bs, bs), 0)
    ci = _iota((bs, bs), 1)
    same = (ri - (ri & (cc - 1))) == (ci - (ci & (cc - 1)))
    strict = same & (ci < ri)
    incl = same & (ci <= ri)
    eye = (ri == ci).astype(F32)
    eye_d = _iota((RWKV_D, RWKV_D), 0) == _iota((RWKV_D, RWKV_D), 1)

    hs = [slice(h * RWKV_D, (h + 1) * RWKV_D) for h in range(RWKV_H)]
    mg, mn, ma, mc_ = RWKV_P["g"], RWKV_P["neu"], RWKV_P["app"], RWKV_P["chunk"]
    units = [(slice(q * bs, (q + 1) * bs), h) for q in range(tb // bs) for h in range(RWKV_H)]
    vh = [v[rb, hs[h]] for rb, h in units]
    gmat = [_mm(jnp.concatenate([kg[rb, hs[h]], rg[rb, hs[h]]], axis=0),
                jnp.concatenate([bi[rb, hs[h]], ki[rb, hs[h]]], axis=0), NT, *mg) for rb, h in units]
    amat = [jnp.where(strict, gm[:bs, :bs], 0.0) for gm in gmat]
    bmat = [jnp.where(strict, gm[:bs, bs:], 0.0) for gm in gmat]
    pb_ = [jnp.where(incl, gm[bs:, :bs], 0.0) for gm in gmat]
    pk_ = [jnp.where(incl, gm[bs:, bs:], 0.0) for gm in gmat]
    bv = [_mm(bm_, v_, NN, *ma) for bm_, v_ in zip(bmat, vh)]

    def lower_left(b):
        rb = ri & (b - 1)
        cb = ci & (b - 1)
        return ((ri - rb) == (ci - cb)) & (rb >= b // 2) & (cb < b // 2)

    x = [eye - jnp.where(lower_left(2), a_, 0.0) for a_ in amat]
    b = 4
    while b <= cc:
        m = lower_left(b)
        tx = [_mm(x_, jnp.where(m, a_, 0.0), NN, *mn) for x_, a_ in zip(x, amat)]
        x = [x_ - _mm(tx_, x_, NN, *mn) for x_, tx_ in zip(x, tx)]
        b *= 2
    wu = [-_mm(x_, jnp.concatenate([kg[rb, hs[h]], bv_], axis=1), NN, *ma)
          for x_, (rb, h), bv_ in zip(x, units, bv)]
    low = _iota((bs, LANES), 1) < RWKV_D
    zero = jnp.zeros((bs, LANES), F32)

    def dup(slab, odd):
        sw = pltpu.roll(slab, RWKV_D, 1)
        return jnp.where(low, sw, slab) if odd else jnp.where(low, slab, sw)

    def upper(slab, odd):
        return jnp.where(low, zero, slab if odd else pltpu.roll(slab, RWKV_D, 1))

    pair = [slice((h // 2) * LANES, (h // 2 + 1) * LANES) for h in range(RWKV_H)]
    rg2 = [dup(rg[rb, pair[h]], h % 2) for rb, h in units]
    ov = [upper(v[rb, pair[h]], h % 2) for rb, h in units]
    ww = [jnp.where(low, wu_, pltpu.roll(wu_, RWKV_D, 1)) for wu_ in wu]
    qy = [_mm(pb, jnp.concatenate([ww_, wu_], axis=1), NN, *ma) for pb, ww_, wu_ in zip(pb_, ww, wu)]
    pkv = [_mm(pk, v_, NN, *ma) for pk, v_ in zip(pk_, vh)]
    qt2 = [rg2_ + qy_[:, :LANES] for rg2_, qy_ in zip(rg2, qy)]
    y0 = [qy_[:, LANES + RWKV_D:] + pkv_ for qy_, pkv_ in zip(qy, pkv)]
    nch = tb // cc
    cpb = bs // cc
    s = [st_ref[h] for h in range(RWKV_H)]
    for c in range(nch):
        rows = slice(c * cc, (c + 1) * cc)
        lrows = slice((c % cpb) * cc, (c % cpb + 1) * cc)
        u0 = (c // cpb) * RWKV_H
        for h in range(RWKV_H):
            scat_ref[h, :, c * RWKV_D:(c + 1) * RWKV_D] = s[h]
        tr = [_mm(jnp.concatenate([wu[u0 + h][lrows], ov[u0 + h][lrows]], axis=0),
                  jnp.concatenate([bt[rows, hs[h]], kt[rows, hs[h]]], axis=0), TN, *mc_) for h in range(RWKV_H)]
        s = [_mm(s_, jnp.where(eye_d, g_end[c * cc:c * cc + 1, ls], 0.0) + t_[:RWKV_D], NN, *mc_) + t_[RWKV_D:]
             for s_, t_, ls in zip(s, tr, hs)]
    r5 = _iota((bs, cpb * RWKV_D), 0)
    c5 = _iota((bs, cpb * RWKV_D), 1)
    own = (r5 - (r5 & (cc - 1))) * (RWKV_D // cc) == c5 - (c5 & (RWKV_D - 1))
    for h in range(RWKV_H):
        st_ref[h] = s[h]
    for u, (rb, h) in enumerate(units):
        q = u // RWKV_H
        qexp = jnp.where(own, jnp.concatenate([qt2[u]] * (cpb * RWKV_D // LANES), axis=1), 0.0)
        y_ref[rb, hs[h]] = _mm(qexp, scat_ref[h, :, q * cpb * RWKV_D:(q + 1) * cpb * RWKV_D], NT, *mc_) + y0[u]

    o_ref[...] = _rwkv_finish(y_ref[...], r, k, v, g, rk_ref, lnw_ref, lnb_ref, seg_ref).astype(BF16)

    @pl.when(t == pl.num_programs(1) - 1)
    def _():
        s_ref[0] = st_ref[...]


def _rwkv_prompt(proj, nb, nt_len, p):
    tb = RWKV_TB
    nt = nt_len // tb
    return pl.pallas_call(
        _rwkv_kernel,
        grid=(nb, nt),
        in_specs=[pl.BlockSpec((GROUP_TILES, tb, TILE), lambda b, t: (T_RWKV // GROUP_TILES, b * nt + t, 0)),
                  _const_spec((1, RWKV_COLS)), _const_spec((1, MIX)), _const_spec((64, MIX)),
                  _const_spec((1, MIX)), _const_spec((64, MIX)), _const_spec((128, MIX)),
                  _const_spec((1, MIX)), _const_spec((1, MIX)), _const_spec((1, MIX)),
                  _const_spec((1, MIX)), _const_spec((1, MIX)), _const_spec((LANES, LANES))],
        out_specs=[pl.BlockSpec((tb, MIX), lambda b, t: (b * nt + t, 0)),
                   pl.BlockSpec((1, RWKV_H, RWKV_D, RWKV_D), lambda b, t: (b, 0, 0, 0)),
                   pl.BlockSpec((1, SUBLANES, RWKV_COLS), lambda b, t: (b, 0, 0))],
        out_shape=[jax.ShapeDtypeStruct((nb * nt_len, MIX), BF16),
                   jax.ShapeDtypeStruct((nb, RWKV_H, RWKV_D, RWKV_D), F32),
                   jax.ShapeDtypeStruct((nb, SUBLANES, RWKV_COLS), F32)],
        scratch_shapes=[pltpu.VMEM((SUBLANES, RWKV_COLS), F32), pltpu.VMEM((RWKV_H, RWKV_D, RWKV_D), F32),
                        pltpu.VMEM((tb, MIX), F32), pltpu.VMEM((RWKV_H, RWKV_D, (tb // RWKV_C) * RWKV_D), F32)],
        compiler_params=_params("parallel", "arbitrary"),
        name="rwkv_prompt",
    )(proj, p["mu"], p["w0"], p["w2"], p["a0"], p["a2"], p["g2"], p["kk"], p["ka"], p["rk"], p["lnw"], p["lnb"],
      p["seg"])


SSD_TB = 512


def _ssd_pointwise(xbc_c, dt_raw, dtb_ref, aneg_ref):
    xbc = _silu(xbc_c)
    dt = _softplus(dt_raw + dtb_ref[...])
    return xbc, dt, dt * aneg_ref[...]


def _expand_heads(col8, width):
    n = col8.shape[0]
    return jnp.concatenate([jnp.broadcast_to(col8[:, h:h + 1], (n, width)) for h in range(SSD_H)], axis=1)


def _ssd_kernel(z_ref, cw_ref, cb_ref, dtb_ref, aneg_ref, dsk_ref, nw_ref, o_ref, s_ref, tail_ref, carry_ref, st_ref,
                y_ref):
    tb = SSD_TB
    cc = SSD_C
    t = pl.program_id(1)

    @pl.when(t == 0)
    def _():
        carry_ref[...] = jnp.zeros_like(carry_ref)
        st_ref[...] = jnp.zeros_like(st_ref)

    gate = jnp.concatenate([z_ref[0], z_ref[1]], axis=1)
    xbc_raw = jnp.concatenate([z_ref[2], z_ref[3], z_ref[4], z_ref[5]], axis=1)
    xbc_c = _causal_conv(xbc_raw, carry_ref[...], cw_ref, cb_ref)
    carry_ref[...] = xbc_raw[tb - SUBLANES:]
    tail_ref[0] = xbc_raw[tb - SUBLANES:]
    xbc, dt, dta = _ssd_pointwise(xbc_c, z_ref[6], dtb_ref, aneg_ref)
    xs = xbc[:, :MIX]
    bm = xbc[:, MIX:MIX + TILE]
    cm = xbc[:, MIX + TILE:]
    pos = _iota((tb, TILE), 0) & (cc - 1)
    cum = _chunk_cumsum(dta, cc, pos)
    xdt = xs * _expand_heads(dt, SSD_P)
    tri = _iota((cc, cc), 0) >= _iota((cc, cc), 1)

    hg = SSD_H // SSD_G
    nch = tb // cc
    pre = []
    for c in range(nch):
        rows = slice(c * cc, (c + 1) * cc)
        cum_c = cum[rows, :LANES]
        cum_t = cum_c.T
        for gi in range(SSD_G):
            ns = slice(gi * SSD_N, (gi + 1) * SSD_N)
            cg = cm[rows, ns]
            bg = bm[rows, ns]
            gmat = _mm(cg, bg, NT)
            yds, xds, decs, ecs = [], [], [], []
            for hh in range(hg):
                h = gi * hg + hh
                col = cum_c[:, h:h + 1]
                lmat = jnp.exp(jnp.where(tri, col - cum_t[h:h + 1, :], -jnp.inf))
                xh = xdt[rows, h * SSD_P:(h + 1) * SSD_P]
                last = cum_c[cc - 1:cc, h:h + 1]
                yds.append(_mm(gmat * lmat, xh))
                xds.append(xh * jnp.exp(last - col))
                decs.append(jnp.broadcast_to(jnp.exp(last), (SSD_P, SSD_N)))
                ecs.append(jnp.broadcast_to(jnp.exp(col), (cc, SSD_P)))
            pre.append((cg, jnp.concatenate(yds, axis=1), jnp.concatenate(ecs, axis=1),
                        jnp.concatenate(decs, axis=0), _mm(jnp.concatenate(xds, axis=1), bg, TN)))
    for c in range(nch):
        rows = slice(c * cc, (c + 1) * cc)
        for gi in range(SSD_G):
            gs = slice(gi * hg * SSD_P, (gi + 1) * hg * SSD_P)
            cg, yd, ec, dec, ds = pre[c * SSD_G + gi]
            st = st_ref[gs, :]
            y_ref[rows, gs] = yd + _mm(cg, st, NT) * ec
            st_ref[gs, :] = st * dec + ds

    y = (y_ref[...] + dsk_ref[...] * xs) * _silu(gate)
    o_ref[...] = _rms(y, nw_ref[...]).astype(BF16)

    @pl.when(t == pl.num_programs(1) - 1)
    def _():
        for h in range(SSD_H):
            s_ref[0, h] = st_ref[h * SSD_P:(h + 1) * SSD_P, :]


def _ssd_prompt(proj, nb, nt_len, cw, cb, dtb, aneg, dsk, nw):
    tb = SSD_TB
    nt = nt_len // tb
    return pl.pallas_call(
        _ssd_kernel,
        grid=(nb, nt),
        in_specs=[pl.BlockSpec((GROUP_TILES, tb, TILE), lambda b, t: (T_SSD // GROUP_TILES, b * nt + t, 0)),
                  _const_spec((CONV_W, SSD_CONV_DIM)), _const_spec((1, SSD_CONV_DIM)),
                  _const_spec((1, TILE)), _const_spec((1, TILE)), _const_spec((1, MIX)), _const_spec((1, MIX))],
        out_specs=[pl.BlockSpec((tb, MIX), lambda b, t: (b * nt + t, 0)),
                   pl.BlockSpec((1, SSD_H, SSD_P, SSD_N), lambda b, t: (b, 0, 0, 0)),
                   pl.BlockSpec((1, SUBLANES, SSD_CONV_DIM), lambda b, t: (b, 0, 0))],
        out_shape=[jax.ShapeDtypeStruct((nb * nt_len, MIX), BF16),
                   jax.ShapeDtypeStruct((nb, SSD_H, SSD_P, SSD_N), F32),
                   jax.ShapeDtypeStruct((nb, SUBLANES, SSD_CONV_DIM), F32)],
        scratch_shapes=[pltpu.VMEM((SUBLANES, SSD_CONV_DIM), F32), pltpu.VMEM((SSD_H * SSD_P, SSD_N), F32),
                        pltpu.VMEM((tb, MIX), F32)],
        compiler_params=_params("parallel", "arbitrary"),
        name="ssd_prompt",
    )(proj, cw, cb, dtb, aneg, dsk, nw)


R_GLA_V, R_GLA_GATE, R_GLA_Q = 0, 512, 1024
R_RW_K, R_RW_R, R_RW_V, R_RW_G = 1280, 1792, 2304, 2816
R_SSD_B, R_SSD_C, R_SSD_E, R_SSD_GATE, R_SSD_X = 3328, 3584, 3840, 4864, 5376
ROW_W = 5888
C_GLA_A, C_GLA_K, C_SSD_X = 0, 256, 512
COL_W = 1024
T_RW_W, T_RW_B, T_RW_K, T_RW_KK, T_RW_R, T_RW_V = 0, 512, 1024, 1536, 2048, 2560
RWT_W = 3072
DEC_RB = 8


def _dec_prep_kernel(z_ref, lconv_ref, lh_ref, rprev_ref, sconv_ref,
                     wg2_ref, bg_ref,
                     lcw_ref, lcb_ref, wa_ref, ba_ref, wi_ref, bi_ref, lam_ref,
                     mu_ref, w0_ref, w2_ref, a0_ref, a2_ref, g2_ref, kk_ref, ka_ref, seg_ref,
                     scw_ref, scb_ref, dtb_ref, aneg_ref,
                     rows_ref, colt_ref, rwt_ref, olru_ref, lh_out_ref, lconv_out_ref, sconv_out_ref):
    k = z_ref[T_GLA + 1]
    g = _gla_gates(z_ref[T_GLA + 6], wg2_ref, bg_ref)
    rows_ref[:, R_GLA_V:R_GLA_V + TILE] = z_ref[T_GLA + 2]
    rows_ref[:, R_GLA_V + TILE:R_GLA_V + MIX] = z_ref[T_GLA + 3]
    rows_ref[:, R_GLA_GATE:R_GLA_GATE + TILE] = z_ref[T_GLA + 4]
    rows_ref[:, R_GLA_GATE + TILE:R_GLA_GATE + MIX] = z_ref[T_GLA + 5]
    rows_ref[:, R_GLA_Q:R_GLA_Q + TILE] = z_ref[T_GLA + 0] * (GLA_DK ** -0.5)
    cols = [jnp.exp(g), k]

    xb = jnp.concatenate([z_ref[T_LRU + 0], z_ref[T_LRU + 1]], axis=1)
    gb = jnp.concatenate([z_ref[T_LRU + 2], z_ref[T_LRU + 3]], axis=1)
    xc = lcb_ref[...] + xb * lcw_ref[CONV_W - 1:CONV_W, :]
    for i in range(CONV_W - 1):
        xc = xc + lconv_ref[:, i * MIX:(i + 1) * MIX] * lcw_ref[i:i + 1, :]
    a, mult, gi = _lru_gates(xc, wa_ref, ba_ref, wi_ref, bi_ref, lam_ref)
    h = a * lh_ref[...] + mult * gi * xc
    lh_out_ref[...] = h
    olru_ref[...] = (h * _gelu_tanh(gb)).astype(BF16)
    lconv_out_ref[:, 0:2 * MIX] = lconv_ref[:, MIX:3 * MIX]
    lconv_out_ref[:, 2 * MIX:3 * MIX] = xb

    z = jnp.concatenate([z_ref[T_RWKV + j] for j in range(GROUP_TILES)], axis=1)
    r, lw, kmod, v, kk, a7, g7 = _rwkv_pointwise(z, rprev_ref[...], mu_ref, w0_ref, w2_ref, a0_ref, a2_ref, g2_ref,
                                                 kk_ref, ka_ref, seg_ref)
    rows_ref[:, R_RW_K:R_RW_K + MIX] = kmod
    rows_ref[:, R_RW_R:R_RW_R + MIX] = r
    rows_ref[:, R_RW_V:R_RW_V + MIX] = v
    rows_ref[:, R_RW_G:R_RW_G + MIX] = g7
    off = 0
    for vec in (jnp.exp(lw), kk * a7, kmod, kk, r, v):
        for j in range(MIX // LANES):
            rwt_ref[off:off + LANES, :] = vec[:, j * LANES:(j + 1) * LANES].T
            off += LANES

    xbc_raw = jnp.concatenate([z_ref[T_SSD + 2], z_ref[T_SSD + 3], z_ref[T_SSD + 4], z_ref[T_SSD + 5]], axis=1)
    xbc_c = scb_ref[...] + xbc_raw * scw_ref[CONV_W - 1:CONV_W, :]
    for i in range(CONV_W - 1):
        xbc_c = xbc_c + sconv_ref[:, i * SSD_CONV_DIM:(i + 1) * SSD_CONV_DIM] * scw_ref[i:i + 1, :]
    xbc, dt, dta = _ssd_pointwise(xbc_c, z_ref[T_SSD + 6], dtb_ref, aneg_ref)
    xs = xbc[:, :MIX]
    rows_ref[:, R_SSD_B:R_SSD_B + TILE] = xbc[:, MIX:MIX + TILE]
    rows_ref[:, R_SSD_C:R_SSD_C + TILE] = xbc[:, MIX + TILE:]
    rows_ref[:, R_SSD_E:R_SSD_E + SSD_H * LANES] = _expand_heads(jnp.exp(dta), LANES)
    rows_ref[:, R_SSD_GATE:R_SSD_GATE + TILE] = z_ref[T_SSD + 0]
    rows_ref[:, R_SSD_GATE + TILE:R_SSD_GATE + MIX] = z_ref[T_SSD + 1]
    rows_ref[:, R_SSD_X:R_SSD_X + MIX] = xs
    cols.append(xs * _expand_heads(dt, SSD_P))
    sconv_out_ref[:, 0:2 * SSD_CONV_DIM] = sconv_ref[:, SSD_CONV_DIM:3 * SSD_CONV_DIM]
    sconv_out_ref[:, 2 * SSD_CONV_DIM:3 * SSD_CONV_DIM] = xbc_raw

    off = 0
    for cvec in cols:
        for j in range(cvec.shape[1] // LANES):
            colt_ref[off:off + LANES, :] = cvec[:, j * LANES:(j + 1) * LANES].T
            off += LANES


def _dec_prep(zdec, lconv, lh, rprev, sconv, pp):
    n = zdec.shape[1]
    args = [zdec, lconv, lh, rprev, sconv,
            pp["gla_wg2"], pp["gla_bg"],
            pp["lru_cw"], pp["lru_cb"], pp["lru_wa"], pp["lru_ba"], pp["lru_wi"], pp["lru_bi"], pp["lru_lam"],
            pp["rwkv"]["mu"], pp["rwkv"]["w0"], pp["rwkv"]["w2"], pp["rwkv"]["a0"], pp["rwkv"]["a2"],
            pp["rwkv"]["g2"], pp["rwkv"]["kk"], pp["rwkv"]["ka"], pp["rwkv"]["seg"],
            pp["ssd_cw"], pp["ssd_cb"], pp["ssd_dtb"], pp["ssd_aneg"]]
    return pl.pallas_call(
        _dec_prep_kernel,
        out_shape=[jax.ShapeDtypeStruct((n, ROW_W), F32), jax.ShapeDtypeStruct((COL_W, n), F32),
                   jax.ShapeDtypeStruct((RWT_W, n), F32),
                   jax.ShapeDtypeStruct((n, MIX), BF16), jax.ShapeDtypeStruct((n, MIX), F32),
                   jax.ShapeDtypeStruct((n, 3 * MIX), F32), jax.ShapeDtypeStruct((n, 3 * SSD_CONV_DIM), F32)],
        compiler_params=pltpu.CompilerParams(vmem_limit_bytes=VMEM_LIMIT),
        name="decode_prep",
    )(*args)


def _dec_rwkv_kernel(w_ref, b_ref, k_ref, kk_ref, r_ref, v_ref, s_ref, *rest, layer, fill_others):
    s_out, y_ref = rest[-2:]
    if fill_others:
        for j in range(DEPTH):
            if j != layer:
                s_out[j] = jnp.zeros(s_out.shape[1:], F32)
        s_out = s_out.at[layer]
    w = w_ref[...]
    b = b_ref[...]
    k = k_ref[...]
    kk = kk_ref[...]
    r = r_ref[...]

    def body(vi, carry):
        s = s_ref[vi]
        sa = jnp.sum(s * kk, axis=0, keepdims=True)
        s = s * w - sa * b + v_ref[pl.ds(vi, 1), :] * k
        s_out[vi] = s
        y_ref[pl.ds(vi, 1), :] = jnp.sum(s * r, axis=0, keepdims=True)
        return carry

    lax.fori_loop(0, RWKV_D, body, 0, unroll=8)


def _dec_rwkv(rwt, sr_t, l, prev):
    n = rwt.shape[1]
    hb = MIX // RWKV_D

    def vec(off):
        return pl.BlockSpec((RWKV_D, n), lambda h: (off // RWKV_D + h, 0))

    sspec = pl.BlockSpec((None, None, RWKV_D, RWKV_D, n), lambda h: (l, h, 0, 0, 0))
    extra = [] if prev is None else [prev]
    first = prev is None
    ospec = pl.BlockSpec((DEPTH, None, RWKV_D, RWKV_D, n), lambda h: (0, h, 0, 0, 0)) if first else sspec
    return pl.pallas_call(
        functools.partial(_dec_rwkv_kernel, layer=l, fill_others=first),
        grid=(hb,),
        in_specs=[vec(T_RW_W), vec(T_RW_B), vec(T_RW_K), vec(T_RW_KK), vec(T_RW_R), vec(T_RW_V), sspec]
        + [pl.BlockSpec(memory_space=pl.ANY)] * len(extra),
        out_specs=[ospec, pl.BlockSpec((RWKV_D, n), lambda h: (h, 0))],
        out_shape=[jax.ShapeDtypeStruct(sr_t.shape, F32), jax.ShapeDtypeStruct((MIX, n), F32)],
        input_output_aliases={7: 0} if extra else {},
        compiler_params=_params("parallel"),
        name="decode_rwkv",
    )(rwt, rwt, rwt, rwt, rwt, rwt, sr_t, *extra)


def _dec_state_kernel(colt_ref, rows_ref, sg_ref, ss_ref, *rest, layer, fill_others):
    sg_out, ss_out, yg_ref, ys_ref = rest[-4:]
    if fill_others:
        for j in range(DEPTH):
            if j != layer:
                sg_out[j] = jnp.zeros(sg_out.shape[1:], F32)
                ss_out[j] = jnp.zeros(ss_out.shape[1:], F32)
        sg_out = sg_out.at[layer]
        ss_out = ss_out.at[layer]
    pid = pl.program_id(0)
    n = colt_ref.shape[1]
    lane = _iota((1, n), 1)
    rid = _iota((DEC_RB, LANES), 0)
    rows = rows_ref[...]
    hg = SSD_H // SSD_G
    yg = [jnp.zeros((DEC_RB, GLA_DV), F32) for _ in range(GLA_H)]
    ys = [jnp.zeros((DEC_RB, SSD_P), F32) for _ in range(SSD_H)]
    for i in range(DEC_RB):
        sel = lane == pid * DEC_RB + i
        col = jnp.sum(jnp.where(sel, colt_ref[...], 0.0), axis=1, keepdims=True)
        rv = rows[i:i + 1, :]
        mine = rid == i
        for h in range(GLA_H):
            al = col[C_GLA_A + h * GLA_DK:C_GLA_A + (h + 1) * GLA_DK]
            kc = col[C_GLA_K + h * GLA_DK:C_GLA_K + (h + 1) * GLA_DK]
            vr = rv[:, R_GLA_V + h * GLA_DV:R_GLA_V + (h + 1) * GLA_DV]
            s = al * sg_ref[i, h] + kc * vr
            sg_out[i, h] = s
            q8 = rows[:, R_GLA_Q + h * GLA_DK:R_GLA_Q + (h + 1) * GLA_DK]
            yg[h] = jnp.where(mine, _mm(q8, s, NN, pa=2, pb=2), yg[h])
        for h in range(SSD_H):
            gi = h // hg
            e = rv[:, R_SSD_E + h * LANES:R_SSD_E + (h + 1) * LANES]
            bn = rv[:, R_SSD_B + gi * SSD_N:R_SSD_B + (gi + 1) * SSD_N]
            xc = col[C_SSD_X + h * SSD_P:C_SSD_X + (h + 1) * SSD_P]
            s = ss_ref[i, h] * e + xc * bn
            ss_out[i, h] = s
            c8 = rows[:, R_SSD_C + gi * SSD_N:R_SSD_C + (gi + 1) * SSD_N]
            ys[h] = jnp.where(mine[:, :SSD_P], _mm(c8, s, NT, pa=2, pb=2), ys[h])
    yg_ref[...] = jnp.concatenate(yg, axis=1)
    ys_ref[...] = jnp.concatenate(ys, axis=1)


def _dec_state(colt, rows, sg, ss, l, prev):
    n = rows.shape[0]
    blk = lambda i: (l, i, 0, 0, 0)
    specs = [pl.BlockSpec((None, DEC_RB, GLA_H, GLA_DK, GLA_DV), blk),
             pl.BlockSpec((None, DEC_RB, SSD_H, SSD_P, SSD_N), blk)]
    extra = [] if prev is None else list(prev)
    first = prev is None
    ospecs = specs
    if first:
        all_layers = lambda i: (0, i, 0, 0, 0)
        ospecs = [pl.BlockSpec((DEPTH, DEC_RB, GLA_H, GLA_DK, GLA_DV), all_layers),
                  pl.BlockSpec((DEPTH, DEC_RB, SSD_H, SSD_P, SSD_N), all_layers)]
    rowspec = pl.BlockSpec((DEC_RB, MIX), lambda i: (i, 0))
    return pl.pallas_call(
        functools.partial(_dec_state_kernel, layer=l, fill_others=first),
        grid=(n // DEC_RB,),
        in_specs=[_const_spec((COL_W, n)), pl.BlockSpec((DEC_RB, ROW_W), lambda i: (i, 0))] + specs
        + [pl.BlockSpec(memory_space=pl.ANY)] * len(extra),
        out_specs=ospecs + [rowspec, rowspec],
        out_shape=[jax.ShapeDtypeStruct(sg.shape, F32), jax.ShapeDtypeStruct(ss.shape, F32),
                   jax.ShapeDtypeStruct((n, MIX), F32), jax.ShapeDtypeStruct((n, MIX), F32)],
        input_output_aliases={4 + j: j for j in range(len(extra))},
        compiler_params=_params("parallel"),
        name="decode_state",
    )(colt, rows, sg, ss, *extra)


def _dec_finish_kernel(rows_ref, yg_ref, yrt_ref, ys_ref, gnw_ref, rk_ref, lnw_ref, lnb_ref, seg_ref,
                       dsk_ref, snw_ref, og_ref, or_ref, os_ref):
    outs = []
    for h in range(GLA_H):
        outs.append(_rms(yg_ref[:, h * GLA_DV:(h + 1) * GLA_DV], gnw_ref[...]))
    og_ref[...] = (jnp.concatenate(outs, axis=1) * _silu(rows_ref[:, R_GLA_GATE:R_GLA_GATE + MIX])).astype(BF16)
    yr = jnp.concatenate([yrt_ref[j * LANES:(j + 1) * LANES, :].T for j in range(MIX // LANES)], axis=1)
    or_ref[...] = _rwkv_finish(yr, rows_ref[:, R_RW_R:R_RW_R + MIX], rows_ref[:, R_RW_K:R_RW_K + MIX],
                               rows_ref[:, R_RW_V:R_RW_V + MIX], rows_ref[:, R_RW_G:R_RW_G + MIX],
                               rk_ref, lnw_ref, lnb_ref, seg_ref).astype(BF16)
    y = ((ys_ref[...] + dsk_ref[...] * rows_ref[:, R_SSD_X:R_SSD_X + MIX])
         * _silu(rows_ref[:, R_SSD_GATE:R_SSD_GATE + MIX]))
    os_ref[...] = _rms(y, snw_ref[...]).astype(BF16)


def _dec_finish(rows, yg, yrt, ys, pp):
    n = rows.shape[0]
    o = jax.ShapeDtypeStruct((n, MIX), BF16)
    return pl.pallas_call(
        _dec_finish_kernel,
        out_shape=[o, o, o],
        compiler_params=pltpu.CompilerParams(vmem_limit_bytes=VMEM_LIMIT),
        name="decode_finish",
    )(rows, yg, yrt, ys, pp["gla_nw"], pp["rwkv"]["rk"], pp["rwkv"]["lnw"], pp["rwkv"]["lnb"], pp["rwkv"]["seg"],
      pp["ssd_dsk"], pp["ssd_nw"])


def _block_diag(w8):
    nb = TILE // 64
    tiled = jnp.tile(w8.reshape(2, TILE, 64), (1, 1, nb))
    idx = jnp.arange(TILE) // 64
    return jnp.where(idx[:, None] == idx[None, :], tiled, 0.0)


def _prep_weights(w_in, w_out, w_up, w_down):
    o_lru = GLA_COLS
    o_rwkv = GLA_COLS + LRU_COLS
    o_ssd = o_rwkv + RWKV_COLS
    wt = jnp.transpose(w_in, (0, 2, 1))
    zpad = lambda n: jnp.zeros((DEPTH, n, D_MODEL), w_in.dtype)
    w_perm = jnp.concatenate([
        wt[:, 0:GLA_COLS], zpad(GROUP_TILES * TILE - GLA_COLS),
        wt[:, o_rwkv:o_rwkv + RWKV_COLS],
        wt[:, o_ssd:o_ssd + SSD_COLS], zpad(GROUP_TILES * TILE - SSD_COLS),
        wt[:, o_lru:o_lru + LRU_COLS]], axis=1).astype(BF16)
    return {"w_in": w_perm,
            "w_out": w_out.astype(BF16),
            "w_up": w_up.astype(BF16),
            "w_down": w_down.astype(BF16)}


def _prep_layer(l, norm_mix_pre, norm_mix_post, norm_mlp_pre, norm_mlp_post,
                gla_w_gate2, gla_b_gate, gla_norm, lru_conv_w, lru_conv_b, lru_w_a, lru_b_a, lru_w_i, lru_b_i,
                lru_lambda, rwkv_mu, rwkv_w0, rwkv_w2, rwkv_a0, rwkv_a2, rwkv_g2, rwkv_k_k, rwkv_k_a, rwkv_r_k,
                rwkv_ln_w, rwkv_ln_b, ssd_conv_w, ssd_conv_b, ssd_dt_bias, ssd_a_log, ssd_d, ssd_norm):
    row = lambda a: a.reshape(1, -1).astype(F32)
    ii = jnp.arange(TILE)[:, None]
    jj = jnp.arange(MIX)[None, :]
    seg_i = jnp.arange(LANES)
    pp = {
        "n_mix_pre": row(norm_mix_pre[l]), "n_mix_post": row(norm_mix_post[l]),
        "n_mlp_pre": row(norm_mlp_pre[l]), "n_mlp_post": row(norm_mlp_post[l]),
        "gla_wg2": jnp.pad(gla_w_gate2[l], ((0, TILE - GLA_RANK), (0, 0))).astype(BF16),
        "gla_bg": row(gla_b_gate[l]),
        "gla_nw": row(gla_norm[l]),
        "gla_ones": (ii // GLA_DK == jj // GLA_DV).astype(BF16),
        "lru_cw": lru_conv_w[l].astype(F32), "lru_cb": row(lru_conv_b[l]),
        "lru_wa": _block_diag(lru_w_a[l]).astype(BF16),
        "lru_wi": _block_diag(lru_w_i[l]).astype(BF16),
        "lru_ba": row(lru_b_a[l]), "lru_bi": row(lru_b_i[l]), "lru_lam": row(lru_lambda[l]),
        "rwkv": {
            "mu": row(rwkv_mu[l]), "w0": row(rwkv_w0[l]), "w2": rwkv_w2[l].astype(BF16),
            "a0": row(rwkv_a0[l]), "a2": rwkv_a2[l].astype(BF16), "g2": rwkv_g2[l].astype(BF16),
            "kk": row(rwkv_k_k[l]), "ka": row(rwkv_k_a[l]), "rk": row(rwkv_r_k[l]),
            "lnw": row(rwkv_ln_w[l]), "lnb": row(rwkv_ln_b[l]),
            "seg": (seg_i[:, None] // RWKV_D == seg_i[None, :] // RWKV_D).astype(BF16),
        },
        "ssd_cw": ssd_conv_w[l].astype(F32), "ssd_cb": row(ssd_conv_b[l]),
        "ssd_dtb": jnp.pad(ssd_dt_bias[l].astype(F32), (0, TILE - SSD_H)).reshape(1, TILE),
        "ssd_aneg": jnp.pad(-jnp.exp(ssd_a_log[l].astype(F32)), (0, TILE - SSD_H)).reshape(1, TILE),
        "ssd_dsk": jnp.repeat(ssd_d[l].astype(F32), SSD_P).reshape(1, MIX),
        "ssd_nw": row(ssd_norm[l]),
    }
    return pp


def _layer_prompt(x2d, nb, nt_len, wts, l, pp, tm_proj=1024, tm_out=512, tm_mlp=512, tf=1024):
    proj = _proj(x2d, pp["n_mix_pre"], wts["w_in"], l, tm_proj)
    o_gla, s_gla = _gla_prompt(proj, nb, nt_len, pp["gla_wg2"], pp["gla_bg"], pp["gla_nw"], pp["gla_ones"])
    o_lru, h_lru, lru_tail = _lru_prompt(proj, nb, nt_len, pp["lru_cw"], pp["lru_cb"], pp["lru_wa"], pp["lru_ba"],
                                         pp["lru_wi"], pp["lru_bi"], pp["lru_lam"])
    o_rwkv, s_rwkv, rwkv_tail = _rwkv_prompt(proj, nb, nt_len, pp["rwkv"])
    o_ssd, s_ssd, ssd_tail = _ssd_prompt(proj, nb, nt_len, pp["ssd_cw"], pp["ssd_cb"], pp["ssd_dtb"],
                                         pp["ssd_aneg"], pp["ssd_dsk"], pp["ssd_nw"])
    x1, h2 = _outproj((o_gla, o_lru, o_rwkv, o_ssd), wts["w_out"], l, x2d, pp["n_mix_post"], pp["n_mlp_pre"], tm_out)
    x2 = _mlp(h2, wts["w_up"], wts["w_down"], l, x1, pp["n_mlp_post"], tm_mlp, tf)
    keep = SUBLANES - (CONV_W - 1)
    states = (s_gla, h_lru[:, 0, :], lru_tail[:, keep:, :], rwkv_tail[:, SUBLANES - 1, :], s_rwkv,
              ssd_tail[:, keep:, :], s_ssd)
    return x2, states


def _layer_decode(x2d, big, small, wts, l, pp, prev, tf=1024):
    sg, sr_t, ss = big
    lh, lconv, rprev, sconv = small
    n = x2d.shape[0]
    proj = _proj(x2d, pp["n_mix_pre"], wts["w_in"], l, n)
    rows, colt, rwt, o_lru, lh_new, lconv_new, sconv_new = _dec_prep(
        proj, lconv.reshape(n, -1), lh, rprev, sconv.reshape(n, -1), pp)
    sr_new, yrt = _dec_rwkv(rwt, sr_t, l, None if prev is None else prev[1])
    sg_new, ss_new, yg, ys = _dec_state(colt, rows, sg, ss, l, None if prev is None else (prev[0], prev[2]))
    o_gla, o_rwkv, o_ssd = _dec_finish(rows, yg, yrt, ys, pp)
    x1, h2 = _outproj((o_gla, o_lru, o_rwkv, o_ssd), wts["w_out"], l, x2d, pp["n_mix_post"], pp["n_mlp_pre"], n)
    x2 = _mlp(h2, wts["w_up"], wts["w_down"], l, x1, pp["n_mlp_post"], n, tf)
    rshift_new = jnp.transpose(proj[T_RWKV:T_RWKV + GROUP_TILES], (1, 0, 2)).reshape(n, RWKV_COLS)
    small_new = (lh_new, lconv_new.reshape(n, CONV_W - 1, MIX), rshift_new,
                 sconv_new.reshape(n, CONV_W - 1, SSD_CONV_DIM))
    return x2, (sg_new, sr_new, ss_new), small_new


def kernel(x_prompt, x_sample, state_gla, state_lru, cache_lru_conv, cache_rwkv_shift, state_rwkv, cache_ssd_conv, state_ssd, norm_mix_pre, norm_mix_post, norm_mlp_pre, norm_mlp_post, w_in, w_out, w_up, w_down, gla_w_gate2, gla_b_gate, gla_norm, lru_conv_w, lru_conv_b, lru_w_a, lru_b_a, lru_w_i, lru_b_i, lru_lambda, rwkv_mu, rwkv_w0, rwkv_w2, rwkv_a0, rwkv_a2, rwkv_g2, rwkv_k_k, rwkv_k_a, rwkv_r_k, rwkv_ln_w, rwkv_ln_b, ssd_conv_w, ssd_conv_b, ssd_dt_bias, ssd_a_log, ssd_d, ssd_norm):
    nb, nt_len, _ = x_prompt.shape
    nd = x_sample.shape[0]
    yp = x_prompt.reshape(nb * nt_len, D_MODEL)
    ys = x_sample.reshape(nd, D_MODEL)
    wts = _prep_weights(w_in, w_out, w_up, w_down)
    big = (state_gla, jnp.transpose(state_rwkv, (0, 2, 3, 4, 1)), state_ssd)
    new_p, small_s, big_s = [], [], None
    for l in range(DEPTH):
        pp = _prep_layer(l, norm_mix_pre, norm_mix_post, norm_mlp_pre, norm_mlp_post,
                         gla_w_gate2, gla_b_gate, gla_norm, lru_conv_w, lru_conv_b, lru_w_a, lru_b_a, lru_w_i,
                         lru_b_i, lru_lambda, rwkv_mu, rwkv_w0, rwkv_w2, rwkv_a0, rwkv_a2, rwkv_g2, rwkv_k_k,
                         rwkv_k_a, rwkv_r_k, rwkv_ln_w, rwkv_ln_b, ssd_conv_w, ssd_conv_b, ssd_dt_bias, ssd_a_log,
                         ssd_d, ssd_norm)
        yp, st_p = _layer_prompt(yp, nb, nt_len, wts, l, pp)
        small = (state_lru[l], cache_lru_conv[l], cache_rwkv_shift[l], cache_ssd_conv[l])
        ys, big_s, sm = _layer_decode(ys, big, small, wts, l, pp, big_s)
        new_p.append(st_p)
        small_s.append(sm)

    def stack(sts, i):
        return jnp.stack([s[i] for s in sts], axis=0)

    gla_s, rwkv_s, ssd_s = big_s
    rwkv_s = jnp.transpose(rwkv_s, (0, 4, 1, 2, 3))
    return (yp.reshape(nb, nt_len, D_MODEL), ys.reshape(nd, 1, D_MODEL),
            stack(new_p, 0), gla_s, stack(new_p, 1), stack(small_s, 0),
            stack(new_p, 2), stack(small_s, 1), stack(new_p, 3), stack(small_s, 2),
            stack(new_p, 4), rwkv_s, stack(new_p, 5), stack(small_s, 3),
            stack(new_p, 6), ssd_s)
```

```python
import functools
import math

import jax
import jax.numpy as jnp
from jax import lax
from jax.experimental import pallas as pl
from jax.experimental.pallas import tpu as pltpu

F32 = jnp.float32
BF16 = jnp.bfloat16

D_MODEL = 2048
D_FF = 4 * D_MODEL
DEPTH = 2
EPS = 1e-6
MIX = D_MODEL // 4
CONV_W = 4
GLA_H, GLA_DK, GLA_DV, GLA_RANK, GLA_TEMP, GLA_C = 4, 64, 128, 16, 16.0, 16
LRU_C = 8.0
RWKV_H, RWKV_D = 8, 64
RWKV_DECAY = math.exp(-0.5)
RWKV_LN_EPS = 64e-5
RWKV_C = 32
SSD_H, SSD_P, SSD_G, SSD_N, SSD_C = 8, 64, 2, 128, 64
GLA_COLS = 2 * GLA_H * GLA_DK + 2 * MIX + GLA_RANK
LRU_COLS = 2 * MIX
RWKV_COLS = 3 * MIX + 64 + 64 + 128
SSD_CONV_DIM = MIX + 2 * SSD_G * SSD_N
SSD_COLS = MIX + SSD_CONV_DIM + SSD_H

LANES = 128
SUBLANES = 8
TILE = 2 * LANES
GROUP_TILES = 7
T_GLA, T_RWKV, T_SSD, T_LRU = 0, 7, 14, 21
N_TILES = 25
VMEM_LIMIT = 56 * 1024 * 1024

NN = (((1,), (0,)), ((), ()))
NT = (((1,), (1,)), ((), ()))
TN = (((0,), (0,)), ((), ()))


def _split(x, n):
    if x.dtype == BF16:
        return [x]
    parts, r = [], x
    for i in range(n):
        p = r.astype(BF16)
        parts.append(p)
        if i + 1 < n:
            r = r - p.astype(F32)
    return parts


def _mm(a, b, dn=NN, pa=1, pb=1):
    aa, bb = _split(a, pa), _split(b, pb)
    acc = None
    for i, x in enumerate(aa):
        for j, y in enumerate(bb):
            if i + j >= max(len(aa), len(bb)):
                continue
            t = lax.dot_general(x, y, dn, preferred_element_type=F32)
            acc = t if acc is None else acc + t
    return acc


def _iota(shape, dim):
    return lax.broadcasted_iota(jnp.int32, shape, dim)


def _roll0(x, s):
    n = x.shape[0]
    s = s % n
    return x if s == 0 else pltpu.roll(x, s, 0)


def _rms(x, w):
    ms = jnp.mean(x * x, axis=-1, keepdims=True)
    return x * lax.rsqrt(ms + EPS) * w


def _sigmoid(x):
    return jax.nn.sigmoid(x)


def _silu(x):
    return x * jax.nn.sigmoid(x)


def _softplus(x):
    return jnp.maximum(x, 0.0) + jnp.log1p(jnp.exp(-jnp.abs(x)))


def _log_sigmoid(x):
    return jnp.minimum(x, 0.0) - jnp.log1p(jnp.exp(-jnp.abs(x)))


def _gelu_tanh(x):
    c = math.sqrt(2.0 / math.pi)
    return x * (0.5 * (1.0 + jnp.tanh(c * (x + 0.044715 * (x * x * x)))))


def _neg_expm1(x):
    return -jnp.tanh(0.5 * x) * (jnp.exp(x) + 1.0)


def _chunk_cumsum(x, chunk, pos):
    d = 1
    while d < chunk:
        x = x + jnp.where(pos >= d, _roll0(x, d), 0.0)
        d *= 2
    return x


def _chunk_last_bcast(x, chunk, pos):
    n = x.shape[0]
    y = jnp.where(pos == chunk - 1, x, 0.0)
    d = 1
    while d < chunk:
        y = y + _roll0(y, n - d)
        d *= 2
    return y


def _shift_rows(x, carry, i, row8):
    xs = _roll0(x, i)
    cs = _roll0(carry, i)
    top = jnp.where(row8 < i, cs, xs[:SUBLANES])
    return jnp.concatenate([top, xs[SUBLANES:]], axis=0)


def _causal_conv(x, carry, w_ref, b_ref):
    row8 = _iota((SUBLANES, x.shape[1]), 0)
    y = b_ref[...] + x * w_ref[CONV_W - 1:CONV_W, :]
    for i in range(1, CONV_W):
        y = y + _shift_rows(x, carry, i, row8) * w_ref[CONV_W - 1 - i:CONV_W - i, :]
    return y


def _params(*sem):
    return pltpu.CompilerParams(dimension_semantics=sem, vmem_limit_bytes=VMEM_LIMIT)


def _const_spec(shape):
    nd = len(shape)
    return pl.BlockSpec(shape, lambda *_: (0,) * nd)


PROJ_TILES_PER_STEP = 5


MLP_CAST_BLOCKS = 32


def _proj_kernel(x_ref, nw_ref, w_ref, *rest):
    ncast = (len(rest) - 2) // 2
    o_ref, h_ref = rest[ncast], rest[-1]
    for src, dst in zip(rest[:ncast], rest[ncast + 1:-1]):
        dst[...] = src[...].astype(BF16)

    @pl.when(pl.program_id(1) == 0)
    def _():
        h_ref[...] = _rms(x_ref[...], nw_ref[...]).astype(BF16)

    res = lax.dot_general(h_ref[...], w_ref[...], NT, preferred_element_type=F32)
    for j in range(PROJ_TILES_PER_STEP):
        o_ref[j] = res[:, j * TILE:(j + 1) * TILE]


def _proj(x2d, nw, w, l, tm, cast_w=None):
    m = x2d.shape[0]
    tn = PROJ_TILES_PER_STEP * TILE
    nj = N_TILES // PROJ_TILES_PER_STEP
    in_specs = [pl.BlockSpec((tm, D_MODEL), lambda i, j: (i, 0)),
                _const_spec((1, D_MODEL)),
                pl.BlockSpec((None, tn, D_MODEL), lambda i, j: (l, j, 0))]
    out_specs = [pl.BlockSpec((PROJ_TILES_PER_STEP, tm, TILE), lambda i, j: (j, i, 0))]
    out_shape = [jax.ShapeDtypeStruct((N_TILES, m, TILE), F32)]
    args = [x2d, nw, w]
    if cast_w is not None:
        nblk = MLP_CAST_BLOCKS
        while nblk > (m // tm) * nj:
            nblk //= 2
        blk = lambda i, j: jnp.minimum(i * nj + j, nblk - 1)
        for cw in cast_w:
            rows, cols = cw.shape[1] // nblk, cw.shape[2]
            in_specs.append(pl.BlockSpec((None, rows, cols), lambda i, j: (l, blk(i, j), 0)))
            out_specs.append(pl.BlockSpec((rows, cols), lambda i, j: (blk(i, j), 0)))
            out_shape.append(jax.ShapeDtypeStruct(cw.shape[1:], BF16))
        args += list(cast_w)
    res = pl.pallas_call(
        _proj_kernel,
        grid=(m // tm, nj),
        in_specs=in_specs,
        out_specs=out_specs,
        out_shape=out_shape,
        scratch_shapes=[pltpu.VMEM((tm, D_MODEL), BF16)],
        compiler_params=_params("arbitrary", "arbitrary"),
        name="norm_proj",
    )(*args)
    return res[0] if cast_w is None else res


OUTPROJ_SUB_ROWS = 128

def _outproj_kernel(m0, m1, m2, m3, w_ref, x_ref, n1_ref, n2_ref, *rest):
    if len(rest) == 2:
        x1_ref, h2_ref = rest
    else:
        wd_ref, x1_ref, h2_ref, wd_out = rest
        wd_out[...] = wd_ref[...].astype(BF16)
    tm = x_ref.shape[0]
    sub = min(tm, OUTPROJ_SUB_ROWS)
    for r0 in range(0, tm, sub):
        rs = slice(r0, r0 + sub)
        mix = jnp.concatenate([m0[rs, :], m1[rs, :], m2[rs, :], m3[rs, :]], axis=1)
        acc = jnp.dot(mix, w_ref[...], preferred_element_type=F32)
        x1 = x_ref[rs, :] + _rms(acc, n1_ref[...])
        x1_ref[rs, :] = x1
        h2_ref[rs, :] = _rms(x1, n2_ref[...]).astype(BF16)


def _outproj(mix, w_out, x2d, n1, n2, tm, cast_w=None, l=0):
    m = x2d.shape[0]
    row = lambda i: (i, 0)
    in_specs = [pl.BlockSpec((tm, MIX), row)] * 4 + [
        _const_spec((D_MODEL, D_MODEL)), pl.BlockSpec((tm, D_MODEL), row),
        _const_spec((1, D_MODEL)), _const_spec((1, D_MODEL))]
    out_specs = [pl.BlockSpec((tm, D_MODEL), row), pl.BlockSpec((tm, D_MODEL), row)]
    out_shape = [jax.ShapeDtypeStruct((m, D_MODEL), F32), jax.ShapeDtypeStruct((m, D_MODEL), BF16)]
    args = [*mix, w_out, x2d, n1, n2]
    if cast_w is not None:
        nblk = m // tm
        rows, cols = cast_w.shape[1] // nblk, cast_w.shape[2]
        in_specs.append(pl.BlockSpec((None, rows, cols), lambda i: (l, i, 0)))
        out_specs.append(pl.BlockSpec((rows, cols), row))
        out_shape.append(jax.ShapeDtypeStruct(cast_w.shape[1:], BF16))
        args.append(cast_w)
    return pl.pallas_call(
        _outproj_kernel,
        grid=(m // tm,),
        in_specs=in_specs,
        out_specs=out_specs,
        out_shape=out_shape,
        compiler_params=_params("parallel"),
        name="out_proj",
    )(*args)


def _mlp_kernel(h_ref, wu_ref, wd_ref, x1_ref, nw_ref, o_ref, acc_ref):
    f = pl.program_id(1)

    @pl.when(f == 0)
    def _():
        acc_ref[...] = jnp.zeros_like(acc_ref)

    u = jnp.maximum(jnp.dot(h_ref[...], wu_ref[...], preferred_element_type=F32), 0.0)
    acc_ref[...] += jnp.dot((u * u).astype(BF16), wd_ref[...], preferred_element_type=F32)

    @pl.when(f == pl.num_programs(1) - 1)
    def _():
        o_ref[...] = x1_ref[...] + _rms(acc_ref[...], nw_ref[...])


def _mlp(h2, w_up, w_down, x1, nw, tm, tf):
    m = h2.shape[0]
    return pl.pallas_call(
        _mlp_kernel,
        grid=(m // tm, D_FF // tf),
        in_specs=[pl.BlockSpec((tm, D_MODEL), lambda i, f: (i, 0)),
                  pl.BlockSpec((D_MODEL, tf), lambda i, f: (0, f)),
                  pl.BlockSpec((tf, D_MODEL), lambda i, f: (f, 0)),
                  pl.BlockSpec((tm, D_MODEL), lambda i, f: (i, 0)),
                  _const_spec((1, D_MODEL))],
        out_specs=pl.BlockSpec((tm, D_MODEL), lambda i, f: (i, 0)),
        out_shape=jax.ShapeDtypeStruct((m, D_MODEL), F32),
        scratch_shapes=[pltpu.VMEM((tm, D_MODEL), F32)],
        compiler_params=_params("parallel", "arbitrary"),
        name="mlp",
    )(h2, w_up, w_down, x1, nw)


GLA_TB = 256


def _gla_gates(ad, wg2_ref, bg_ref):
    x = _mm(ad, wg2_ref[...], pa=2) + bg_ref[...]
    return _log_sigmoid(x) * (1.0 / GLA_TEMP)


def _gla_kernel(z_ref, wg2_ref, bg_ref, nw_ref, ones_ref, o_ref, s_ref, st_ref, y_ref):
    tb = GLA_TB
    t = pl.program_id(1)

    @pl.when(t == 0)
    def _():
        st_ref[...] = jnp.zeros_like(st_ref)

    q = z_ref[0] * (GLA_DK ** -0.5)
    k = z_ref[1]
    v = jnp.concatenate([z_ref[2], z_ref[3]], axis=1)
    gate = jnp.concatenate([z_ref[4], z_ref[5]], axis=1)
    g = _gla_gates(z_ref[6], wg2_ref, bg_ref)
    pos = _iota((tb, TILE), 0) & (GLA_C - 1)
    cum = _chunk_cumsum(g, GLA_C, pos)

    y = None
    for j in range(GLA_C):
        if j == 0:
            term = q * k
            vj = v
        else:
            e = jnp.exp(jnp.where(pos >= j, cum - _roll0(cum, j), -jnp.inf))
            term = q * _roll0(k, j) * e
            vj = _roll0(v, j)
        sc = _mm(term, ones_ref[...])
        y = sc * vj if y is None else y + sc * vj
    y_ref[...] = y

    qh = q * jnp.exp(cum)
    last = _chunk_last_bcast(cum, GLA_C, pos)
    kt = k * jnp.exp(last - cum)
    dec = jnp.exp(last)
    kss = [slice(h * GLA_DK, (h + 1) * GLA_DK) for h in range(GLA_H)]
    vss = [slice(h * GLA_DV, (h + 1) * GLA_DV) for h in range(GLA_H)]
    nch = tb // GLA_C
    ds = [[_mm(v[c * GLA_C:(c + 1) * GLA_C, vss[h]], kt[c * GLA_C:(c + 1) * GLA_C, kss[h]], TN)
           for h in range(GLA_H)] for c in range(nch)]
    st = [st_ref[h] for h in range(GLA_H)]
    for c in range(nch):
        rows = slice(c * GLA_C, (c + 1) * GLA_C)
        for h in range(GLA_H):
            y_ref[rows, vss[h]] += _mm(qh[rows, kss[h]], st[h], NT)
            st[h] = st[h] * dec[c * GLA_C:c * GLA_C + 1, kss[h]] + ds[c][h]
    for h in range(GLA_H):
        st_ref[h] = st[h]

    yy = y_ref[...]
    outs = []
    for h in range(GLA_H):
        vs = slice(h * GLA_DV, (h + 1) * GLA_DV)
        outs.append(_rms(yy[:, vs], nw_ref[...]))
    o_ref[...] = (jnp.concatenate(outs, axis=1) * _silu(gate)).astype(BF16)

    @pl.when(t == pl.num_programs(1) - 1)
    def _():
        for h in range(GLA_H):
            s_ref[0, h] = st_ref[h].T


def _gla_prompt(proj, nb, nt_len, wg2, bg, nw, ones):
    tb = GLA_TB
    nt = nt_len // tb
    return pl.pallas_call(
        _gla_kernel,
        grid=(nb, nt),
        in_specs=[pl.BlockSpec((GROUP_TILES, tb, TILE), lambda b, t: (T_GLA // GROUP_TILES, b * nt + t, 0)),
                  _const_spec((TILE, TILE)), _const_spec((1, TILE)), _const_spec((1, GLA_DV)),
                  _const_spec((TILE, MIX))],
        out_specs=[pl.BlockSpec((tb, MIX), lambda b, t: (b * nt + t, 0)),
                   pl.BlockSpec((1, GLA_H, GLA_DK, GLA_DV), lambda b, t: (b, 0, 0, 0))],
        out_shape=[jax.ShapeDtypeStruct((nb * nt_len, MIX), BF16),
                   jax.ShapeDtypeStruct((nb, GLA_H, GLA_DK, GLA_DV), F32)],
        scratch_shapes=[pltpu.VMEM((GLA_H, GLA_DV, GLA_DK), F32), pltpu.VMEM((tb, MIX), F32)],
        compiler_params=_params("parallel", "arbitrary"),
        name="gla_prompt",
    )(proj, wg2, bg, nw, ones)


LRU_TB = 256


def _lru_gates(xc, wa_ref, ba_ref, wi_ref, bi_ref, lam_ref):
    ra, ri = [], []
    for j in range(2):
        xs = xc[:, j * TILE:(j + 1) * TILE]
        ra.append(_mm(xs, wa_ref[j], pa=2))
        ri.append(_mm(xs, wi_ref[j], pa=2))
    r = _sigmoid(jnp.concatenate(ra, axis=1) + ba_ref[...])
    i = _sigmoid(jnp.concatenate(ri, axis=1) + bi_ref[...])
    log_a = -LRU_C * r * _softplus(-lam_ref[...])
    a = jnp.exp(log_a)
    mult = jnp.sqrt(_neg_expm1(2.0 * log_a))
    return a, mult, i


def _lru_kernel(x0_ref, x1_ref, g0_ref, g1_ref, cw_ref, cb_ref, wa_ref, ba_ref, wi_ref, bi_ref, lam_ref,
                o_ref, h_ref, tail_ref, carry_ref, hc_ref):
    tb = LRU_TB
    t = pl.program_id(1)

    @pl.when(t == 0)
    def _():
        carry_ref[...] = jnp.zeros_like(carry_ref)
        hc_ref[...] = jnp.zeros_like(hc_ref)

    xb = jnp.concatenate([x0_ref[...], x1_ref[...]], axis=1)
    gb = jnp.concatenate([g0_ref[...], g1_ref[...]], axis=1)
    xc = _causal_conv(xb, carry_ref[...], cw_ref, cb_ref)
    carry_ref[...] = xb[tb - SUBLANES:]
    a, mult, i = _lru_gates(xc, wa_ref, ba_ref, wi_ref, bi_ref, lam_ref)
    row = _iota((tb, MIX), 0)
    mult = jnp.where((row == 0) & (t == 0), 1.0, mult)
    b = mult * i * xc
    pos8 = row & (SUBLANES - 1)

    def roll8(x, d):
        return pltpu.roll(x.reshape(tb // SUBLANES, SUBLANES, MIX), d, 1).reshape(tb, MIX)

    d = 1
    while d < SUBLANES:
        m = pos8 >= d
        b = jnp.where(m, a * roll8(b, d) + b, b)
        a = jnp.where(m, a * roll8(a, d), a)
        d *= 2
    hl = hc_ref[0:1, :]
    groups = []
    for gidx in range(tb // SUBLANES):
        rs = slice(gidx * SUBLANES, (gidx + 1) * SUBLANES)
        hg = b[rs] + a[rs] * hl
        groups.append(hg)
        hl = hg[SUBLANES - 1:SUBLANES, :]
    h = jnp.concatenate(groups, axis=0)
    hc_ref[...] = jnp.broadcast_to(hl, hc_ref.shape)
    o_ref[...] = (h * _gelu_tanh(gb)).astype(BF16)
    h_ref[0] = hl
    tail_ref[0] = xb[tb - SUBLANES:]


def _lru_prompt(proj, nb, nt_len, cw, cb, wa, ba, wi, bi, lam):
    tb = LRU_TB
    nt = nt_len // tb

    def tile(j):
        return pl.BlockSpec((None, tb, TILE), lambda b, t: (T_LRU + j, b * nt + t, 0))

    return pl.pallas_call(
        _lru_kernel,
        grid=(nb, nt),
        in_specs=[tile(0), tile(1), tile(2), tile(3),
                  _const_spec((CONV_W, MIX)), _const_spec((1, MIX)),
                  _const_spec((2, TILE, TILE)), _const_spec((1, MIX)),
                  _const_spec((2, TILE, TILE)), _const_spec((1, MIX)), _const_spec((1, MIX))],
        out_specs=[pl.BlockSpec((tb, MIX), lambda b, t: (b * nt + t, 0)),
                   pl.BlockSpec((1, 1, MIX), lambda b, t: (b, 0, 0)),
                   pl.BlockSpec((1, SUBLANES, MIX), lambda b, t: (b, 0, 0))],
        out_shape=[jax.ShapeDtypeStruct((nb * nt_len, MIX), BF16), jax.ShapeDtypeStruct((nb, 1, MIX), F32),
                   jax.ShapeDtypeStruct((nb, SUBLANES, MIX), F32)],
        scratch_shapes=[pltpu.VMEM((SUBLANES, MIX), F32), pltpu.VMEM((SUBLANES, MIX), F32)],
        compiler_params=_params("parallel", "arbitrary"),
        name="lru_prompt",
    )(proj, proj, proj, proj, cw, cb, wa, ba, wi, bi, lam)


RWKV_TB = 256
RWKV_BS = 128
RWKV_P = {"g": (1, 1), "neu": (1, 1), "app": (1, 1), "chunk": (1, 1)}


def _seg_sum(x, seg_ref):
    outs = []
    for j in range(x.shape[1] // LANES):
        outs.append(_mm(x[:, j * LANES:(j + 1) * LANES], seg_ref[...], pa=2))
    return jnp.concatenate(outs, axis=1)


def _rwkv_pointwise(z, zs, mu_ref, w0_ref, w2_ref, a0_ref, a2_ref, g2_ref, kk_ref, ka_ref, seg_ref):
    zm = z + (zs - z) * mu_ref[...]
    r = zm[:, 0:MIX]
    k = zm[:, MIX:2 * MIX]
    v = zm[:, 2 * MIX:3 * MIX]
    zw = zm[:, 3 * MIX:3 * MIX + 64]
    za = zm[:, 3 * MIX + 64:3 * MIX + 128]
    zg = zm[:, 3 * MIX + 128:3 * MIX + 256]
    lw = -RWKV_DECAY * _sigmoid(w0_ref[...] + _mm(jnp.tanh(zw), w2_ref[...], pa=2))
    a = _sigmoid(a0_ref[...] + _mm(za, a2_ref[...], pa=2))
    g = _mm(_sigmoid(zg), g2_ref[...], pa=2)
    kk = k * kk_ref[...]
    kk = kk / jnp.maximum(jnp.sqrt(_seg_sum(kk * kk, seg_ref)), 1e-12)
    k = k * (1.0 + (a - 1.0) * ka_ref[...])
    return r, lw, k, v, kk, a, g


def _rwkv_finish(y, r, k, v, g, rk_ref, lnw_ref, lnb_ref, seg_ref):
    mean = _seg_sum(y, seg_ref) * (1.0 / RWKV_D)
    yc = y - mean
    var = _seg_sum(yc * yc, seg_ref) * (1.0 / RWKV_D)
    yn = yc * lax.rsqrt(var + RWKV_LN_EPS) * lnw_ref[...] + lnb_ref[...]
    bonus = _seg_sum(r * k * rk_ref[...], seg_ref) * v
    return (yn + bonus) * g


def _rwkv_kernel(z_ref, mu_ref, w0_ref, w2_ref, a0_ref, a2_ref, g2_ref, kk_ref, ka_ref, rk_ref, lnw_ref, lnb_ref,
                 seg_ref, o_ref, s_ref, tail_ref, prev_ref, st_ref, y_ref, scat_ref):
    tb = RWKV_TB
    cc = RWKV_C
    t = pl.program_id(1)

    @pl.when(t == 0)
    def _():
        prev_ref[...] = jnp.zeros_like(prev_ref)
        st_ref[...] = jnp.zeros_like(st_ref)

    z = jnp.concatenate([z_ref[j] for j in range(GROUP_TILES)], axis=1)
    zs = _shift_rows(z, prev_ref[...], 1, _iota((SUBLANES, RWKV_COLS), 0))
    prev_ref[...] = z[tb - SUBLANES:]
    tail_ref[0] = z[tb - SUBLANES:]
    r, lw, k, v, kk, a, g = _rwkv_pointwise(z, zs, mu_ref, w0_ref, w2_ref, a0_ref, a2_ref, g2_ref, kk_ref, ka_ref,
                                            seg_ref)
    b = kk * a

    pos = _iota((tb, MIX), 0) & (cc - 1)
    lg = _chunk_cumsum(lw, cc, pos)
    lg_end = _chunk_last_bcast(lg, cc, pos)
    gam = jnp.exp(lg)
    inv = jnp.exp(-lg)
    rg = r * gam
    kg = kk * jnp.exp(lg - lw)
    bi = b * inv
    ki = k * inv
    e_end = jnp.exp(lg_end - lg)
    bt = b * e_end
    kt = k * e_end
    g_end = jnp.exp(lg_end)

    bs = RWKV_BS
    ri = _iota((bs, bs), 0)
    ci = _iota((bs, bs), 1)
    same = (ri - (ri & (cc - 1))) == (ci - (ci & (cc - 1)))
    strict = same & (ci < ri)
    incl = same & (ci <= ri)
    eye = (ri == ci).astype(F32)
    eye_d = _iota((RWKV_D, RWKV_D), 0) == _iota((RWKV_D, RWKV_D), 1)

    hs = [slice(h * RWKV_D, (h + 1) * RWKV_D) for h in range(RWKV_H)]
    mg, mn, ma, mc_ = RWKV_P["g"], RWKV_P["neu"], RWKV_P["app"], RWKV_P["chunk"]
    units = [(slice(q * bs, (q + 1) * bs), h) for q in range(tb // bs) for h in range(RWKV_H)]
    vh = [v[rb, hs[h]] for rb, h in units]
    gmat = [_mm(jnp.concatenate([kg[rb, hs[h]], rg[rb, hs[h]]], axis=0),
                jnp.concatenate([bi[rb, hs[h]], ki[rb, hs[h]]], axis=0), NT, *mg) for rb, h in units]
    amat = [jnp.where(strict, gm[:bs, :bs], 0.0) for gm in gmat]
    bmat = [jnp.where(strict, gm[:bs, bs:], 0.0) for gm in gmat]
    pb_ = [jnp.where(incl, gm[bs:, :bs], 0.0) for gm in gmat]
    pk_ = [jnp.where(incl, gm[bs:, bs:], 0.0) for gm in gmat]
    bv = [_mm(bm_, v_, NN, *ma) for bm_, v_ in zip(bmat, vh)]

    def lower_left(b):
        rb = ri & (b - 1)
        cb = ci & (b - 1)
        return ((ri - rb) == (ci - cb)) & (rb >= b // 2) & (cb < b // 2)

    x = [eye - jnp.where(lower_left(2), a_, 0.0) for a_ in amat]
    b = 4
    while b <= cc:
        m = lower_left(b)
        tx = [_mm(x_, jnp.where(m, a_, 0.0), NN, *mn) for x_, a_ in zip(x, amat)]
        x = [x_ - _mm(tx_, x_, NN, *mn) for x_, tx_ in zip(x, tx)]
        b *= 2
    wu = [-_mm(x_, jnp.concatenate([kg[rb, hs[h]], bv_], axis=1), NN, *ma)
          for x_, (rb, h), bv_ in zip(x, units, bv)]
    low = _iota((bs, LANES), 1) < RWKV_D
    zero = jnp.zeros((bs, LANES), F32)

    def dup(slab, odd):
        sw = pltpu.roll(slab, RWKV_D, 1)
        return jnp.where(low, sw, slab) if odd else jnp.where(low, slab, sw)

    def upper(slab, odd):
        return jnp.where(low, zero, slab if odd else pltpu.roll(slab, RWKV_D, 1))

    pair = [slice((h // 2) * LANES, (h // 2 + 1) * LANES) for h in range(RWKV_H)]
    rg2 = [dup(rg[rb, pair[h]], h % 2) for rb, h in units]
    ov = [upper(v[rb, pair[h]], h % 2) for rb, h in units]
    ww = [jnp.where(low, wu_, pltpu.roll(wu_, RWKV_D, 1)) for wu_ in wu]
    qy = [_mm(pb, jnp.concatenate([ww_, wu_], axis=1), NN, *ma) for pb, ww_, wu_ in zip(pb_, ww, wu)]
    pkv = [_mm(pk, v_, NN, *ma) for pk, v_ in zip(pk_, vh)]
    qt2 = [rg2_ + qy_[:, :LANES] for rg2_, qy_ in zip(rg2, qy)]
    y0 = [qy_[:, LANES + RWKV_D:] + pkv_ for qy_, pkv_ in zip(qy, pkv)]
    nch = tb // cc
    cpb = bs // cc
    s = [st_ref[h] for h in range(RWKV_H)]
    for c in range(nch):
        rows = slice(c * cc, (c + 1) * cc)
        lrows = slice((c % cpb) * cc, (c % cpb + 1) * cc)
        u0 = (c // cpb) * RWKV_H
        for h in range(RWKV_H):
            scat_ref[h, :, c * RWKV_D:(c + 1) * RWKV_D] = s[h]
        tr = [_mm(jnp.concatenate([wu[u0 + h][lrows], ov[u0 + h][lrows]], axis=0),
                  jnp.concatenate([bt[rows, hs[h]], kt[rows, hs[h]]], axis=0), TN, *mc_) for h in range(RWKV_H)]
        s = [_mm(s_, jnp.where(eye_d, g_end[c * cc:c * cc + 1, ls], 0.0) + t_[:RWKV_D], NN, *mc_) + t_[RWKV_D:]
             for s_, t_, ls in zip(s, tr, hs)]
    r5 = _iota((bs, cpb * RWKV_D), 0)
    c5 = _iota((bs, cpb * RWKV_D), 1)
    own = (r5 - (r5 & (cc - 1))) * (RWKV_D // cc) == c5 - (c5 & (RWKV_D - 1))
    for h in range(RWKV_H):
        st_ref[h] = s[h]
    for u, (rb, h) in enumerate(units):
        q = u // RWKV_H
        qexp = jnp.where(own, jnp.concatenate([qt2[u]] * (cpb * RWKV_D // LANES), axis=1), 0.0)
        y_ref[rb, hs[h]] = _mm(qexp, scat_ref[h, :, q * cpb * RWKV_D:(q + 1) * cpb * RWKV_D], NT, *mc_) + y0[u]

    o_ref[...] = _rwkv_finish(y_ref[...], r, k, v, g, rk_ref, lnw_ref, lnb_ref, seg_ref).astype(BF16)

    @pl.when(t == pl.num_programs(1) - 1)
    def _():
        s_ref[0] = st_ref[...]


def _rwkv_prompt(proj, nb, nt_len, p):
    tb = RWKV_TB
    nt = nt_len // tb
    return pl.pallas_call(
        _rwkv_kernel,
        grid=(nb, nt),
        in_specs=[pl.BlockSpec((GROUP_TILES, tb, TILE), lambda b, t: (T_RWKV // GROUP_TILES, b * nt + t, 0)),
                  _const_spec((1, RWKV_COLS)), _const_spec((1, MIX)), _const_spec((64, MIX)),
                  _const_spec((1, MIX)), _const_spec((64, MIX)), _const_spec((128, MIX)),
                  _const_spec((1, MIX)), _const_spec((1, MIX)), _const_spec((1, MIX)),
                  _const_spec((1, MIX)), _const_spec((1, MIX)), _const_spec((LANES, LANES))],
        out_specs=[pl.BlockSpec((tb, MIX), lambda b, t: (b * nt + t, 0)),
                   pl.BlockSpec((1, RWKV_H, RWKV_D, RWKV_D), lambda b, t: (b, 0, 0, 0)),
                   pl.BlockSpec((1, SUBLANES, RWKV_COLS), lambda b, t: (b, 0, 0))],
        out_shape=[jax.ShapeDtypeStruct((nb * nt_len, MIX), BF16),
                   jax.ShapeDtypeStruct((nb, RWKV_H, RWKV_D, RWKV_D), F32),
                   jax.ShapeDtypeStruct((nb, SUBLANES, RWKV_COLS), F32)],
        scratch_shapes=[pltpu.VMEM((SUBLANES, RWKV_COLS), F32), pltpu.VMEM((RWKV_H, RWKV_D, RWKV_D), F32),
                        pltpu.VMEM((tb, MIX), F32), pltpu.VMEM((RWKV_H, RWKV_D, (tb // RWKV_C) * RWKV_D), F32)],
        compiler_params=_params("parallel", "arbitrary"),
        name="rwkv_prompt",
    )(proj, p["mu"], p["w0"], p["w2"], p["a0"], p["a2"], p["g2"], p["kk"], p["ka"], p["rk"], p["lnw"], p["lnb"],
      p["seg"])


SSD_TB = 512


def _ssd_pointwise(xbc_c, dt_raw, dtb_ref, aneg_ref):
    xbc = _silu(xbc_c)
    dt = _softplus(dt_raw + dtb_ref[...])
    return xbc, dt, dt * aneg_ref[...]


def _expand_heads(col8, width):
    n = col8.shape[0]
    return jnp.concatenate([jnp.broadcast_to(col8[:, h:h + 1], (n, width)) for h in range(SSD_H)], axis=1)


def _ssd_kernel(z_ref, cw_ref, cb_ref, dtb_ref, aneg_ref, dsk_ref, nw_ref, o_ref, s_ref, tail_ref, carry_ref, st_ref,
                y_ref):
    tb = SSD_TB
    cc = SSD_C
    t = pl.program_id(1)

    @pl.when(t == 0)
    def _():
        carry_ref[...] = jnp.zeros_like(carry_ref)
        st_ref[...] = jnp.zeros_like(st_ref)

    gate = jnp.concatenate([z_ref[0], z_ref[1]], axis=1)
    xbc_raw = jnp.concatenate([z_ref[2], z_ref[3], z_ref[4], z_ref[5]], axis=1)
    xbc_c = _causal_conv(xbc_raw, carry_ref[...], cw_ref, cb_ref)
    carry_ref[...] = xbc_raw[tb - SUBLANES:]
    tail_ref[0] = xbc_raw[tb - SUBLANES:]
    xbc, dt, dta = _ssd_pointwise(xbc_c, z_ref[6], dtb_ref, aneg_ref)
    xs = xbc[:, :MIX]
    bm = xbc[:, MIX:MIX + TILE]
    cm = xbc[:, MIX + TILE:]
    pos = _iota((tb, TILE), 0) & (cc - 1)
    cum = _chunk_cumsum(dta, cc, pos)
    xdt = xs * _expand_heads(dt, SSD_P)
    tri = _iota((cc, cc), 0) >= _iota((cc, cc), 1)

    hg = SSD_H // SSD_G
    nch = tb // cc
    pre = []
    for c in range(nch):
        rows = slice(c * cc, (c + 1) * cc)
        cum_c = cum[rows, :LANES]
        cum_t = cum_c.T
        for gi in range(SSD_G):
            ns = slice(gi * SSD_N, (gi + 1) * SSD_N)
            cg = cm[rows, ns]
            bg = bm[rows, ns]
            gmat = _mm(cg, bg, NT)
            yds, xds, decs, ecs = [], [], [], []
            for hh in range(hg):
                h = gi * hg + hh
                col = cum_c[:, h:h + 1]
                lmat = jnp.exp(jnp.where(tri, col - cum_t[h:h + 1, :], -jnp.inf))
                xh = xdt[rows, h * SSD_P:(h + 1) * SSD_P]
                last = cum_c[cc - 1:cc, h:h + 1]
                yds.append(_mm(gmat * lmat, xh))
                xds.append(xh * jnp.exp(last - col))
                decs.append(jnp.broadcast_to(jnp.exp(last), (SSD_P, SSD_N)))
                ecs.append(jnp.broadcast_to(jnp.exp(col), (cc, SSD_P)))
            pre.append((cg, jnp.concatenate(yds, axis=1), jnp.concatenate(ecs, axis=1),
                        jnp.concatenate(decs, axis=0), _mm(jnp.concatenate(xds, axis=1), bg, TN)))
    for c in range(nch):
        rows = slice(c * cc, (c + 1) * cc)
        for gi in range(SSD_G):
            gs = slice(gi * hg * SSD_P, (gi + 1) * hg * SSD_P)
            cg, yd, ec, dec, ds = pre[c * SSD_G + gi]
            st = st_ref[gs, :]
            y_ref[rows, gs] = yd + _mm(cg, st, NT) * ec
            st_ref[gs, :] = st * dec + ds

    y = (y_ref[...] + dsk_ref[...] * xs) * _silu(gate)
    o_ref[...] = _rms(y, nw_ref[...]).astype(BF16)

    @pl.when(t == pl.num_programs(1) - 1)
    def _():
        for h in range(SSD_H):
            s_ref[0, h] = st_ref[h * SSD_P:(h + 1) * SSD_P, :]


def _ssd_prompt(proj, nb, nt_len, cw, cb, dtb, aneg, dsk, nw):
    tb = SSD_TB
    nt = nt_len // tb
    return pl.pallas_call(
        _ssd_kernel,
        grid=(nb, nt),
        in_specs=[pl.BlockSpec((GROUP_TILES, tb, TILE), lambda b, t: (T_SSD // GROUP_TILES, b * nt + t, 0)),
                  _const_spec((CONV_W, SSD_CONV_DIM)), _const_spec((1, SSD_CONV_DIM)),
                  _const_spec((1, TILE)), _const_spec((1, TILE)), _const_spec((1, MIX)), _const_spec((1, MIX))],
        out_specs=[pl.BlockSpec((tb, MIX), lambda b, t: (b * nt + t, 0)),
                   pl.BlockSpec((1, SSD_H, SSD_P, SSD_N), lambda b, t: (b, 0, 0, 0)),
                   pl.BlockSpec((1, SUBLANES, SSD_CONV_DIM), lambda b, t: (b, 0, 0))],
        out_shape=[jax.ShapeDtypeStruct((nb * nt_len, MIX), BF16),
                   jax.ShapeDtypeStruct((nb, SSD_H, SSD_P, SSD_N), F32),
                   jax.ShapeDtypeStruct((nb, SUBLANES, SSD_CONV_DIM), F32)],
        scratch_shapes=[pltpu.VMEM((SUBLANES, SSD_CONV_DIM), F32), pltpu.VMEM((SSD_H * SSD_P, SSD_N), F32),
                        pltpu.VMEM((tb, MIX), F32)],
        compiler_params=_params("parallel", "arbitrary"),
        name="ssd_prompt",
    )(proj, cw, cb, dtb, aneg, dsk, nw)


R_GLA_V, R_GLA_GATE, R_GLA_Q = 0, 512, 1024
R_RW_K, R_RW_R, R_RW_V, R_RW_G = 1280, 1792, 2304, 2816
R_SSD_B, R_SSD_C, R_SSD_E, R_SSD_GATE, R_SSD_X = 3328, 3584, 3840, 4864, 5376
ROW_W = 5888
C_GLA_A, C_GLA_K, C_SSD_X = 0, 256, 512
COL_W = 1024
T_RW_W, T_RW_B, T_RW_K, T_RW_KK, T_RW_R, T_RW_V = 0, 512, 1024, 1536, 2048, 2560
RWT_W = 3072
DEC_RB = 8


def _dec_prep_kernel(z_ref, lconv_ref, lh_ref, rprev_ref, sconv_ref,
                     wg2_ref, bg_ref,
                     lcw_ref, lcb_ref, wa_ref, ba_ref, wi_ref, bi_ref, lam_ref,
                     mu_ref, w0_ref, w2_ref, a0_ref, a2_ref, g2_ref, kk_ref, ka_ref, seg_ref,
                     scw_ref, scb_ref, dtb_ref, aneg_ref,
                     rows_ref, colt_ref, rwt_ref, olru_ref, lh_out_ref, lconv_out_ref, sconv_out_ref):
    k = z_ref[T_GLA + 1]
    g = _gla_gates(z_ref[T_GLA + 6], wg2_ref, bg_ref)
    rows_ref[:, R_GLA_V:R_GLA_V + TILE] = z_ref[T_GLA + 2]
    rows_ref[:, R_GLA_V + TILE:R_GLA_V + MIX] = z_ref[T_GLA + 3]
    rows_ref[:, R_GLA_GATE:R_GLA_GATE + TILE] = z_ref[T_GLA + 4]
    rows_ref[:, R_GLA_GATE + TILE:R_GLA_GATE + MIX] = z_ref[T_GLA + 5]
    rows_ref[:, R_GLA_Q:R_GLA_Q + TILE] = z_ref[T_GLA + 0] * (GLA_DK ** -0.5)
    cols = [jnp.exp(g), k]

    xb = jnp.concatenate([z_ref[T_LRU + 0], z_ref[T_LRU + 1]], axis=1)
    gb = jnp.concatenate([z_ref[T_LRU + 2], z_ref[T_LRU + 3]], axis=1)
    xc = lcb_ref[...] + xb * lcw_ref[CONV_W - 1:CONV_W, :]
    for i in range(CONV_W - 1):
        xc = xc + lconv_ref[:, i * MIX:(i + 1) * MIX] * lcw_ref[i:i + 1, :]
    a, mult, gi = _lru_gates(xc, wa_ref, ba_ref, wi_ref, bi_ref, lam_ref)
    h = a * lh_ref[...] + mult * gi * xc
    lh_out_ref[...] = h
    olru_ref[...] = (h * _gelu_tanh(gb)).astype(BF16)
    lconv_out_ref[:, 0:2 * MIX] = lconv_ref[:, MIX:3 * MIX]
    lconv_out_ref[:, 2 * MIX:3 * MIX] = xb

    z = jnp.concatenate([z_ref[T_RWKV + j] for j in range(GROUP_TILES)], axis=1)
    r, lw, kmod, v, kk, a7, g7 = _rwkv_pointwise(z, rprev_ref[...], mu_ref, w0_ref, w2_ref, a0_ref, a2_ref, g2_ref,
                                                 kk_ref, ka_ref, seg_ref)
    rows_ref[:, R_RW_K:R_RW_K + MIX] = kmod
    rows_ref[:, R_RW_R:R_RW_R + MIX] = r
    rows_ref[:, R_RW_V:R_RW_V + MIX] = v
    rows_ref[:, R_RW_G:R_RW_G + MIX] = g7
    off = 0
    for vec in (jnp.exp(lw), kk * a7, kmod, kk, r, v):
        for j in range(MIX // LANES):
            rwt_ref[off:off + LANES, :] = vec[:, j * LANES:(j + 1) * LANES].T
            off += LANES

    xbc_raw = jnp.concatenate([z_ref[T_SSD + 2], z_ref[T_SSD + 3], z_ref[T_SSD + 4], z_ref[T_SSD + 5]], axis=1)
    xbc_c = scb_ref[...] + xbc_raw * scw_ref[CONV_W - 1:CONV_W, :]
    for i in range(CONV_W - 1):
        xbc_c = xbc_c + sconv_ref[:, i * SSD_CONV_DIM:(i + 1) * SSD_CONV_DIM] * scw_ref[i:i + 1, :]
    xbc, dt, dta = _ssd_pointwise(xbc_c, z_ref[T_SSD + 6], dtb_ref, aneg_ref)
    xs = xbc[:, :MIX]
    rows_ref[:, R_SSD_B:R_SSD_B + TILE] = xbc[:, MIX:MIX + TILE]
    rows_ref[:, R_SSD_C:R_SSD_C + TILE] = xbc[:, MIX + TILE:]
    rows_ref[:, R_SSD_E:R_SSD_E + SSD_H * LANES] = _expand_heads(jnp.exp(dta), LANES)
    rows_ref[:, R_SSD_GATE:R_SSD_GATE + TILE] = z_ref[T_SSD + 0]
    rows_ref[:, R_SSD_GATE + TILE:R_SSD_GATE + MIX] = z_ref[T_SSD + 1]
    rows_ref[:, R_SSD_X:R_SSD_X + MIX] = xs
    cols.append(xs * _expand_heads(dt, SSD_P))
    sconv_out_ref[:, 0:2 * SSD_CONV_DIM] = sconv_ref[:, SSD_CONV_DIM:3 * SSD_CONV_DIM]
    sconv_out_ref[:, 2 * SSD_CONV_DIM:3 * SSD_CONV_DIM] = xbc_raw

    off = 0
    for cvec in cols:
        for j in range(cvec.shape[1] // LANES):
            colt_ref[off:off + LANES, :] = cvec[:, j * LANES:(j + 1) * LANES].T
            off += LANES


def _dec_prep(zdec, lconv, lh, rprev, sconv, pp):
    n = zdec.shape[1]
    args = [zdec, lconv, lh, rprev, sconv,
            pp["gla_wg2"], pp["gla_bg"],
            pp["lru_cw"], pp["lru_cb"], pp["lru_wa"], pp["lru_ba"], pp["lru_wi"], pp["lru_bi"], pp["lru_lam"],
            pp["rwkv"]["mu"], pp["rwkv"]["w0"], pp["rwkv"]["w2"], pp["rwkv"]["a0"], pp["rwkv"]["a2"],
            pp["rwkv"]["g2"], pp["rwkv"]["kk"], pp["rwkv"]["ka"], pp["rwkv"]["seg"],
            pp["ssd_cw"], pp["ssd_cb"], pp["ssd_dtb"], pp["ssd_aneg"]]
    return pl.pallas_call(
        _dec_prep_kernel,
        out_shape=[jax.ShapeDtypeStruct((n, ROW_W), F32), jax.ShapeDtypeStruct((COL_W, n), F32),
                   jax.ShapeDtypeStruct((RWT_W, n), F32),
                   jax.ShapeDtypeStruct((n, MIX), BF16), jax.ShapeDtypeStruct((n, MIX), F32),
                   jax.ShapeDtypeStruct((n, 3 * MIX), F32), jax.ShapeDtypeStruct((n, 3 * SSD_CONV_DIM), F32)],
        compiler_params=pltpu.CompilerParams(vmem_limit_bytes=VMEM_LIMIT),
        name="decode_prep",
    )(*args)


def _dec_rwkv_kernel(w_ref, b_ref, k_ref, kk_ref, r_ref, v_ref, s_ref, *rest, layer, fill_others):
    s_out, y_ref = rest[-2:]
    if fill_others:
        for j in range(DEPTH):
            if j != layer:
                s_out[j] = jnp.zeros(s_out.shape[1:], F32)
        s_out = s_out.at[layer]
    w = w_ref[...]
    b = b_ref[...]
    k = k_ref[...]
    kk = kk_ref[...]
    r = r_ref[...]

    def body(vi, carry):
        s = s_ref[vi]
        sa = jnp.sum(s * kk, axis=0, keepdims=True)
        s = s * w - sa * b + v_ref[pl.ds(vi, 1), :] * k
        s_out[vi] = s
        y_ref[pl.ds(vi, 1), :] = jnp.sum(s * r, axis=0, keepdims=True)
        return carry

    lax.fori_loop(0, RWKV_D, body, 0, unroll=8)


def _dec_rwkv(rwt, sr_t, l, prev):
    n = rwt.shape[1]
    hb = MIX // RWKV_D

    def vec(off):
        return pl.BlockSpec((RWKV_D, n), lambda h: (off // RWKV_D + h, 0))

    sspec = pl.BlockSpec((None, None, RWKV_D, RWKV_D, n), lambda h: (l, h, 0, 0, 0))
    extra = [] if prev is None else [prev]
    first = prev is None
    ospec = pl.BlockSpec((DEPTH, None, RWKV_D, RWKV_D, n), lambda h: (0, h, 0, 0, 0)) if first else sspec
    return pl.pallas_call(
        functools.partial(_dec_rwkv_kernel, layer=l, fill_others=first),
        grid=(hb,),
        in_specs=[vec(T_RW_W), vec(T_RW_B), vec(T_RW_K), vec(T_RW_KK), vec(T_RW_R), vec(T_RW_V), sspec]
        + [pl.BlockSpec(memory_space=pl.ANY)] * len(extra),
        out_specs=[ospec, pl.BlockSpec((RWKV_D, n), lambda h: (h, 0))],
        out_shape=[jax.ShapeDtypeStruct(sr_t.shape, F32), jax.ShapeDtypeStruct((MIX, n), F32)],
        input_output_aliases={7: 0} if extra else {},
        compiler_params=_params("parallel"),
        name="decode_rwkv",
    )(rwt, rwt, rwt, rwt, rwt, rwt, sr_t, *extra)


def _dec_state_kernel(colt_ref, rows_ref, sg_ref, ss_ref, *rest, layer, fill_others):
    sg_out, ss_out, yg_ref, ys_ref = rest[-4:]
    if fill_others:
        for j in range(DEPTH):
            if j != layer:
                sg_out[j] = jnp.zeros(sg_out.shape[1:], F32)
                ss_out[j] = jnp.zeros(ss_out.shape[1:], F32)
        sg_out = sg_out.at[layer]
        ss_out = ss_out.at[layer]
    pid = pl.program_id(0)
    n = colt_ref.shape[1]
    lane = _iota((1, n), 1)
    rid = _iota((DEC_RB, LANES), 0)
    rows = rows_ref[...]
    hg = SSD_H // SSD_G
    yg = [jnp.zeros((DEC_RB, GLA_DV), F32) for _ in range(GLA_H)]
    ys = [jnp.zeros((DEC_RB, SSD_P), F32) for _ in range(SSD_H)]
    for i in range(DEC_RB):
        sel = lane == pid * DEC_RB + i
        col = jnp.sum(jnp.where(sel, colt_ref[...], 0.0), axis=1, keepdims=True)
        rv = rows[i:i + 1, :]
        mine = rid == i
        for h in range(GLA_H):
            al = col[C_GLA_A + h * GLA_DK:C_GLA_A + (h + 1) * GLA_DK]
            kc = col[C_GLA_K + h * GLA_DK:C_GLA_K + (h + 1) * GLA_DK]
            vr = rv[:, R_GLA_V + h * GLA_DV:R_GLA_V + (h + 1) * GLA_DV]
            s = al * sg_ref[i, h] + kc * vr
            sg_out[i, h] = s
            q8 = rows[:, R_GLA_Q + h * GLA_DK:R_GLA_Q + (h + 1) * GLA_DK]
            yg[h] = jnp.where(mine, _mm(q8, s, NN, pa=2, pb=2), yg[h])
        for h in range(SSD_H):
            gi = h // hg
            e = rv[:, R_SSD_E + h * LANES:R_SSD_E + (h + 1) * LANES]
            bn = rv[:, R_SSD_B + gi * SSD_N:R_SSD_B + (gi + 1) * SSD_N]
            xc = col[C_SSD_X + h * SSD_P:C_SSD_X + (h + 1) * SSD_P]
            s = ss_ref[i, h] * e + xc * bn
            ss_out[i, h] = s
            c8 = rows[:, R_SSD_C + gi * SSD_N:R_SSD_C + (gi + 1) * SSD_N]
            ys[h] = jnp.where(mine[:, :SSD_P], _mm(c8, s, NT, pa=2, pb=2), ys[h])
    yg_ref[...] = jnp.concatenate(yg, axis=1)
    ys_ref[...] = jnp.concatenate(ys, axis=1)


def _dec_state(colt, rows, sg, ss, l, prev):
    n = rows.shape[0]
    blk = lambda i: (l, i, 0, 0, 0)
    specs = [pl.BlockSpec((None, DEC_RB, GLA_H, GLA_DK, GLA_DV), blk),
             pl.BlockSpec((None, DEC_RB, SSD_H, SSD_P, SSD_N), blk)]
    extra = [] if prev is None else list(prev)
    first = prev is None
    ospecs = specs
    if first:
        all_layers = lambda i: (0, i, 0, 0, 0)
        ospecs = [pl.BlockSpec((DEPTH, DEC_RB, GLA_H, GLA_DK, GLA_DV), all_layers),
                  pl.BlockSpec((DEPTH, DEC_RB, SSD_H, SSD_P, SSD_N), all_layers)]
    rowspec = pl.BlockSpec((DEC_RB, MIX), lambda i: (i, 0))
    return pl.pallas_call(
        functools.partial(_dec_state_kernel, layer=l, fill_others=first),
        grid=(n // DEC_RB,),
        in_specs=[_const_spec((COL_W, n)), pl.BlockSpec((DEC_RB, ROW_W), lambda i: (i, 0))] + specs
        + [pl.BlockSpec(memory_space=pl.ANY)] * len(extra),
        out_specs=ospecs + [rowspec, rowspec],
        out_shape=[jax.ShapeDtypeStruct(sg.shape, F32), jax.ShapeDtypeStruct(ss.shape, F32),
                   jax.ShapeDtypeStruct((n, MIX), F32), jax.ShapeDtypeStruct((n, MIX), F32)],
        input_output_aliases={4 + j: j for j in range(len(extra))},
        compiler_params=_params("parallel"),
        name="decode_state",
    )(colt, rows, sg, ss, *extra)


def _dec_finish_kernel(rows_ref, yg_ref, yrt_ref, ys_ref, gnw_ref, rk_ref, lnw_ref, lnb_ref, seg_ref,
                       dsk_ref, snw_ref, og_ref, or_ref, os_ref):
    outs = []
    for h in range(GLA_H):
        outs.append(_rms(yg_ref[:, h * GLA_DV:(h + 1) * GLA_DV], gnw_ref[...]))
    og_ref[...] = (jnp.concatenate(outs, axis=1) * _silu(rows_ref[:, R_GLA_GATE:R_GLA_GATE + MIX])).astype(BF16)
    yr = jnp.concatenate([yrt_ref[j * LANES:(j + 1) * LANES, :].T for j in range(MIX // LANES)], axis=1)
    or_ref[...] = _rwkv_finish(yr, rows_ref[:, R_RW_R:R_RW_R + MIX], rows_ref[:, R_RW_K:R_RW_K + MIX],
                               rows_ref[:, R_RW_V:R_RW_V + MIX], rows_ref[:, R_RW_G:R_RW_G + MIX],
                               rk_ref, lnw_ref, lnb_ref, seg_ref).astype(BF16)
    y = ((ys_ref[...] + dsk_ref[...] * rows_ref[:, R_SSD_X:R_SSD_X + MIX])
         * _silu(rows_ref[:, R_SSD_GATE:R_SSD_GATE + MIX]))
    os_ref[...] = _rms(y, snw_ref[...]).astype(BF16)


def _dec_finish(rows, yg, yrt, ys, pp):
    n = rows.shape[0]
    o = jax.ShapeDtypeStruct((n, MIX), BF16)
    return pl.pallas_call(
        _dec_finish_kernel,
        out_shape=[o, o, o],
        compiler_params=pltpu.CompilerParams(vmem_limit_bytes=VMEM_LIMIT),
        name="decode_finish",
    )(rows, yg, yrt, ys, pp["gla_nw"], pp["rwkv"]["rk"], pp["rwkv"]["lnw"], pp["rwkv"]["lnb"], pp["rwkv"]["seg"],
      pp["ssd_dsk"], pp["ssd_nw"])


def _block_diag(w8):
    nb = TILE // 64
    tiled = jnp.tile(w8.reshape(2, TILE, 64), (1, 1, nb))
    idx = jnp.arange(TILE) // 64
    return jnp.where(idx[:, None] == idx[None, :], tiled, 0.0)


def _prep_weights(w_in, w_out, w_up, w_down):
    o_lru = GLA_COLS
    o_rwkv = GLA_COLS + LRU_COLS
    o_ssd = o_rwkv + RWKV_COLS
    wt = jnp.transpose(w_in, (0, 2, 1))
    zpad = lambda n: jnp.zeros((DEPTH, n, D_MODEL), w_in.dtype)
    w_perm = jnp.concatenate([
        wt[:, 0:GLA_COLS], zpad(GROUP_TILES * TILE - GLA_COLS),
        wt[:, o_rwkv:o_rwkv + RWKV_COLS],
        wt[:, o_ssd:o_ssd + SSD_COLS], zpad(GROUP_TILES * TILE - SSD_COLS),
        wt[:, o_lru:o_lru + LRU_COLS]], axis=1).astype(BF16)
    return {"w_in": w_perm,
            "w_out": w_out,
            "w_up": w_up, "w_down": w_down}


def _prep_layer(l, norm_mix_pre, norm_mix_post, norm_mlp_pre, norm_mlp_post,
                gla_w_gate2, gla_b_gate, gla_norm, lru_conv_w, lru_conv_b, lru_w_a, lru_b_a, lru_w_i, lru_b_i,
                lru_lambda, rwkv_mu, rwkv_w0, rwkv_w2, rwkv_a0, rwkv_a2, rwkv_g2, rwkv_k_k, rwkv_k_a, rwkv_r_k,
                rwkv_ln_w, rwkv_ln_b, ssd_conv_w, ssd_conv_b, ssd_dt_bias, ssd_a_log, ssd_d, ssd_norm):
    row = lambda a: a.reshape(1, -1).astype(F32)
    ii = jnp.arange(TILE)[:, None]
    jj = jnp.arange(MIX)[None, :]
    seg_i = jnp.arange(LANES)
    pp = {
        "n_mix_pre": row(norm_mix_pre[l]), "n_mix_post": row(norm_mix_post[l]),
        "n_mlp_pre": row(norm_mlp_pre[l]), "n_mlp_post": row(norm_mlp_post[l]),
        "gla_wg2": jnp.pad(gla_w_gate2[l], ((0, TILE - GLA_RANK), (0, 0))).astype(BF16),
        "gla_bg": row(gla_b_gate[l]),
        "gla_nw": row(gla_norm[l]),
        "gla_ones": (ii // GLA_DK == jj // GLA_DV).astype(BF16),
        "lru_cw": lru_conv_w[l].astype(F32), "lru_cb": row(lru_conv_b[l]),
        "lru_wa": _block_diag(lru_w_a[l]).astype(BF16),
        "lru_wi": _block_diag(lru_w_i[l]).astype(BF16),
        "lru_ba": row(lru_b_a[l]), "lru_bi": row(lru_b_i[l]), "lru_lam": row(lru_lambda[l]),
        "rwkv": {
            "mu": row(rwkv_mu[l]), "w0": row(rwkv_w0[l]), "w2": rwkv_w2[l].astype(BF16),
            "a0": row(rwkv_a0[l]), "a2": rwkv_a2[l].astype(BF16), "g2": rwkv_g2[l].astype(BF16),
            "kk": row(rwkv_k_k[l]), "ka": row(rwkv_k_a[l]), "rk": row(rwkv_r_k[l]),
            "lnw": row(rwkv_ln_w[l]), "lnb": row(rwkv_ln_b[l]),
            "seg": (seg_i[:, None] // RWKV_D == seg_i[None, :] // RWKV_D).astype(BF16),
        },
        "ssd_cw": ssd_conv_w[l].astype(F32), "ssd_cb": row(ssd_conv_b[l]),
        "ssd_dtb": jnp.pad(ssd_dt_bias[l].astype(F32), (0, TILE - SSD_H)).reshape(1, TILE),
        "ssd_aneg": jnp.pad(-jnp.exp(ssd_a_log[l].astype(F32)), (0, TILE - SSD_H)).reshape(1, TILE),
        "ssd_dsk": jnp.repeat(ssd_d[l].astype(F32), SSD_P).reshape(1, MIX),
        "ssd_nw": row(ssd_norm[l]),
    }
    return pp


def _layer_prompt(x2d, nb, nt_len, wts, l, pp, tm_proj=1024, tm_out=512, tm_mlp=512, tf=1024):
    proj, wu_bf, wo_bf = _proj(x2d, pp["n_mix_pre"], wts["w_in"], l, tm_proj, (wts["w_up"], wts["w_out"]))
    o_gla, s_gla = _gla_prompt(proj, nb, nt_len, pp["gla_wg2"], pp["gla_bg"], pp["gla_nw"], pp["gla_ones"])
    o_lru, h_lru, lru_tail = _lru_prompt(proj, nb, nt_len, pp["lru_cw"], pp["lru_cb"], pp["lru_wa"], pp["lru_ba"],
                                         pp["lru_wi"], pp["lru_bi"], pp["lru_lam"])
    o_rwkv, s_rwkv, rwkv_tail = _rwkv_prompt(proj, nb, nt_len, pp["rwkv"])
    o_ssd, s_ssd, ssd_tail = _ssd_prompt(proj, nb, nt_len, pp["ssd_cw"], pp["ssd_cb"], pp["ssd_dtb"],
                                         pp["ssd_aneg"], pp["ssd_dsk"], pp["ssd_nw"])
    x1, h2, wd_bf = _outproj((o_gla, o_lru, o_rwkv, o_ssd), wo_bf, x2d, pp["n_mix_post"], pp["n_mlp_pre"], tm_out,
                             wts["w_down"], l)
    x2 = _mlp(h2, wu_bf, wd_bf, x1, pp["n_mlp_post"], tm_mlp, tf)
    keep = SUBLANES - (CONV_W - 1)
    states = (s_gla, h_lru[:, 0, :], lru_tail[:, keep:, :], rwkv_tail[:, SUBLANES - 1, :], s_rwkv,
              ssd_tail[:, keep:, :], s_ssd)
    return x2, states, (wu_bf, wd_bf, wo_bf)


def _layer_decode(x2d, big, small, wts, mlp_w, l, pp, prev, tf=1024):
    sg, sr_t, ss = big
    lh, lconv, rprev, sconv = small
    n = x2d.shape[0]
    proj = _proj(x2d, pp["n_mix_pre"], wts["w_in"], l, n)
    rows, colt, rwt, o_lru, lh_new, lconv_new, sconv_new = _dec_prep(
        proj, lconv.reshape(n, -1), lh, rprev, sconv.reshape(n, -1), pp)
    sr_new, yrt = _dec_rwkv(rwt, sr_t, l, None if prev is None else prev[1])
    sg_new, ss_new, yg, ys = _dec_state(colt, rows, sg, ss, l, None if prev is None else (prev[0], prev[2]))
    o_gla, o_rwkv, o_ssd = _dec_finish(rows, yg, yrt, ys, pp)
    x1, h2 = _outproj((o_gla, o_lru, o_rwkv, o_ssd), mlp_w[2], x2d, pp["n_mix_post"], pp["n_mlp_pre"], n)
    x2 = _mlp(h2, mlp_w[0], mlp_w[1], x1, pp["n_mlp_post"], n, tf)
    rshift_new = jnp.transpose(proj[T_RWKV:T_RWKV + GROUP_TILES], (1, 0, 2)).reshape(n, RWKV_COLS)
    small_new = (lh_new, lconv_new.reshape(n, CONV_W - 1, MIX), rshift_new,
                 sconv_new.reshape(n, CONV_W - 1, SSD_CONV_DIM))
    return x2, (sg_new, sr_new, ss_new), small_new


def kernel(x_prompt, x_sample, state_gla, state_lru, cache_lru_conv, cache_rwkv_shift, state_rwkv, cache_ssd_conv, state_ssd, norm_mix_pre, norm_mix_post, norm_mlp_pre, norm_mlp_post, w_in, w_out, w_up, w_down, gla_w_gate2, gla_b_gate, gla_norm, lru_conv_w, lru_conv_b, lru_w_a, lru_b_a, lru_w_i, lru_b_i, lru_lambda, rwkv_mu, rwkv_w0, rwkv_w2, rwkv_a0, rwkv_a2, rwkv_g2, rwkv_k_k, rwkv_k_a, rwkv_r_k, rwkv_ln_w, rwkv_ln_b, ssd_conv_w, ssd_conv_b, ssd_dt_bias, ssd_a_log, ssd_d, ssd_norm):
    nb, nt_len, _ = x_prompt.shape
    nd = x_sample.shape[0]
    yp = x_prompt.reshape(nb * nt_len, D_MODEL)
    ys = x_sample.reshape(nd, D_MODEL)
    wts = _prep_weights(w_in, w_out, w_up, w_down)
    big = (state_gla, jnp.transpose(state_rwkv, (0, 2, 3, 4, 1)), state_ssd)
    new_p, small_s, big_s = [], [], None
    for l in range(DEPTH):
        pp = _prep_layer(l, norm_mix_pre, norm_mix_post, norm_mlp_pre, norm_mlp_post,
                         gla_w_gate2, gla_b_gate, gla_norm, lru_conv_w, lru_conv_b, lru_w_a, lru_b_a, lru_w_i,
                         lru_b_i, lru_lambda, rwkv_mu, rwkv_w0, rwkv_w2, rwkv_a0, rwkv_a2, rwkv_g2, rwkv_k_k,
                         rwkv_k_a, rwkv_r_k, rwkv_ln_w, rwkv_ln_b, ssd_conv_w, ssd_conv_b, ssd_dt_bias, ssd_a_log,
                         ssd_d, ssd_norm)
        yp, st_p, mlp_w = _layer_prompt(yp, nb, nt_len, wts, l, pp)
        small = (state_lru[l], cache_lru_conv[l], cache_rwkv_shift[l], cache_ssd_conv[l])
        ys, big_s, sm = _layer_decode(ys, big, small, wts, mlp_w, l, pp, big_s)
        new_p.append(st_p)
        small_s.append(sm)

    def stack(sts, i):
        return jnp.stack([s[i] for s in sts], axis=0)

    gla_s, rwkv_s, ssd_s = big_s
    rwkv_s = jnp.transpose(rwkv_s, (0, 4, 1, 2, 3))
    return (yp.reshape(nb, nt_len, D_MODEL), ys.reshape(nd, 1, D_MODEL),
            stack(new_p, 0), gla_s, stack(new_p, 1), stack(small_s, 0),
            stack(new_p, 2), stack(small_s, 1), stack(new_p, 3), stack(small_s, 2),
            stack(new_p, 4), rwkv_s, stack(new_p, 5), stack(small_s, 3),
            stack(new_p, 6), ssd_s)
```

```python
import functools
import math

import jax
import jax.numpy as jnp
from jax import lax
from jax.experimental import pallas as pl
from jax.experimental.pallas import tpu as pltpu

F32 = jnp.float32
BF16 = jnp.bfloat16

D_MODEL = 2048
D_FF = 4 * D_MODEL
DEPTH = 2
EPS = 1e-6
MIX = D_MODEL // 4
CONV_W = 4
GLA_H, GLA_DK, GLA_DV, GLA_RANK, GLA_TEMP, GLA_C = 4, 64, 128, 16, 16.0, 16
LRU_C = 8.0
RWKV_H, RWKV_D = 8, 64
RWKV_DECAY = math.exp(-0.5)
RWKV_LN_EPS = 64e-5
RWKV_C = 32
SSD_H, SSD_P, SSD_G, SSD_N, SSD_C = 8, 64, 2, 128, 64
GLA_COLS = 2 * GLA_H * GLA_DK + 2 * MIX + GLA_RANK
LRU_COLS = 2 * MIX
RWKV_COLS = 3 * MIX + 64 + 64 + 128
SSD_CONV_DIM = MIX + 2 * SSD_G * SSD_N
SSD_COLS = MIX + SSD_CONV_DIM + SSD_H

LANES = 128
SUBLANES = 8
TILE = 2 * LANES
GROUP_TILES = 7
WIDE_TILES = 6
T_RWKV, T_LRU, T_MISC, T_GLA, T_SSD = 0, 7, 11, 12, 18
N_TILES = 24
MISC_DT = GLA_RANK
VMEM_LIMIT = 56 * 1024 * 1024

NN = (((1,), (0,)), ((), ()))
NT = (((1,), (1,)), ((), ()))
TN = (((0,), (0,)), ((), ()))


def _split(x, n):
    if x.dtype == BF16:
        return [x]
    parts, r = [], x
    for i in range(n):
        p = r.astype(BF16)
        parts.append(p)
        if i + 1 < n:
            r = r - p.astype(F32)
    return parts


def _mm(a, b, dn=NN, pa=1, pb=1):
    aa, bb = _split(a, pa), _split(b, pb)
    acc = None
    for i, x in enumerate(aa):
        for j, y in enumerate(bb):
            if i + j >= max(len(aa), len(bb)):
                continue
            t = lax.dot_general(x, y, dn, preferred_element_type=F32)
            acc = t if acc is None else acc + t
    return acc


def _iota(shape, dim):
    return lax.broadcasted_iota(jnp.int32, shape, dim)


def _roll0(x, s):
    n = x.shape[0]
    s = s % n
    return x if s == 0 else pltpu.roll(x, s, 0)


def _rms(x, w):
    ms = jnp.mean(x * x, axis=-1, keepdims=True)
    return x * lax.rsqrt(ms + EPS) * w


def _sigmoid(x):
    return jax.nn.sigmoid(x)


def _silu(x):
    return x * jax.nn.sigmoid(x)


def _softplus(x):
    return jnp.maximum(x, 0.0) + jnp.log1p(jnp.exp(-jnp.abs(x)))


def _log_sigmoid(x):
    return jnp.minimum(x, 0.0) - jnp.log1p(jnp.exp(-jnp.abs(x)))


def _gelu_tanh(x):
    c = math.sqrt(2.0 / math.pi)
    return x * (0.5 * (1.0 + jnp.tanh(c * (x + 0.044715 * (x * x * x)))))


def _neg_expm1(x):
    return -jnp.tanh(0.5 * x) * (jnp.exp(x) + 1.0)


def _chunk_cumsum(x, chunk, pos):
    d = 1
    while d < chunk:
        x = x + jnp.where(pos >= d, _roll0(x, d), 0.0)
        d *= 2
    return x


def _chunk_last_bcast(x, chunk, pos):
    n = x.shape[0]
    y = jnp.where(pos == chunk - 1, x, 0.0)
    d = 1
    while d < chunk:
        y = y + _roll0(y, n - d)
        d *= 2
    return y


def _shift_rows(x, carry, i, row8):
    xs = _roll0(x, i)
    cs = _roll0(carry, i)
    top = jnp.where(row8 < i, cs, xs[:SUBLANES])
    return jnp.concatenate([top, xs[SUBLANES:]], axis=0)


def _causal_conv(x, carry, w_ref, b_ref):
    row8 = _iota((SUBLANES, x.shape[1]), 0)
    y = b_ref[...] + x * w_ref[CONV_W - 1:CONV_W, :]
    for i in range(1, CONV_W):
        y = y + _shift_rows(x, carry, i, row8) * w_ref[CONV_W - 1 - i:CONV_W - i, :]
    return y


def _params(*sem):
    return pltpu.CompilerParams(dimension_semantics=sem, vmem_limit_bytes=VMEM_LIMIT)


def _const_spec(shape):
    nd = len(shape)
    return pl.BlockSpec(shape, lambda *_: (0,) * nd)


PROJ_TILES_PER_STEP = 6


MLP_CAST_BLOCKS = 32


def _proj_kernel(x_ref, nw_ref, w_ref, *rest):
    ncast = (len(rest) - 2) // 2
    o_ref, h_ref = rest[ncast], rest[-1]
    for src, dst in zip(rest[:ncast], rest[ncast + 1:-1]):
        dst[...] = src[...].astype(BF16)

    @pl.when(pl.program_id(1) == 0)
    def _():
        h_ref[...] = _rms(x_ref[...], nw_ref[...]).astype(BF16)

    res = lax.dot_general(h_ref[...], w_ref[...], NT, preferred_element_type=F32)
    for j in range(PROJ_TILES_PER_STEP):
        o_ref[j] = res[:, j * TILE:(j + 1) * TILE]


def _proj(x2d, nw, w, l, tm, cast_w=None):
    m = x2d.shape[0]
    tn = PROJ_TILES_PER_STEP * TILE
    nj = N_TILES // PROJ_TILES_PER_STEP
    in_specs = [pl.BlockSpec((tm, D_MODEL), lambda i, j: (i, 0)),
                _const_spec((1, D_MODEL)),
                pl.BlockSpec((None, tn, D_MODEL), lambda i, j: (l, j, 0))]
    out_specs = [pl.BlockSpec((PROJ_TILES_PER_STEP, tm, TILE), lambda i, j: (j, i, 0))]
    out_shape = [jax.ShapeDtypeStruct((N_TILES, m, TILE), F32)]
    args = [x2d, nw, w]
    if cast_w is not None:
        nblk = MLP_CAST_BLOCKS
        while nblk > (m // tm) * nj:
            nblk //= 2
        blk = lambda i, j: jnp.minimum(i * nj + j, nblk - 1)
        for cw in cast_w:
            rows, cols = cw.shape[1] // nblk, cw.shape[2]
            in_specs.append(pl.BlockSpec((None, rows, cols), lambda i, j: (l, blk(i, j), 0)))
            out_specs.append(pl.BlockSpec((rows, cols), lambda i, j: (blk(i, j), 0)))
            out_shape.append(jax.ShapeDtypeStruct(cw.shape[1:], BF16))
        args += list(cast_w)
    res = pl.pallas_call(
        _proj_kernel,
        grid=(m // tm, nj),
        in_specs=in_specs,
        out_specs=out_specs,
        out_shape=out_shape,
        scratch_shapes=[pltpu.VMEM((tm, D_MODEL), BF16)],
        compiler_params=_params("arbitrary", "arbitrary"),
        name="norm_proj",
    )(*args)
    return res[0] if cast_w is None else res


OUTPROJ_SUB_ROWS = 128

def _outproj_kernel(m0, m1, m2, m3, w_ref, x_ref, n1_ref, n2_ref, *rest):
    if len(rest) == 2:
        x1_ref, h2_ref = rest
    else:
        wd_ref, x1_ref, h2_ref, wd_out = rest
        wd_out[...] = wd_ref[...].astype(BF16)
    tm = x_ref.shape[0]
    sub = min(tm, OUTPROJ_SUB_ROWS)
    for r0 in range(0, tm, sub):
        rs = slice(r0, r0 + sub)
        mix = jnp.concatenate([m0[rs, :], m1[rs, :], m2[rs, :], m3[rs, :]], axis=1)
        acc = jnp.dot(mix, w_ref[...], preferred_element_type=F32)
        x1 = x_ref[rs, :] + _rms(acc, n1_ref[...])
        x1_ref[rs, :] = x1
        h2_ref[rs, :] = _rms(x1, n2_ref[...]).astype(BF16)


def _outproj(mix, w_out, x2d, n1, n2, tm, cast_w=None, l=0):
    m = x2d.shape[0]
    row = lambda i: (i, 0)
    in_specs = [pl.BlockSpec((tm, MIX), row)] * 4 + [
        _const_spec((D_MODEL, D_MODEL)), pl.BlockSpec((tm, D_MODEL), row),
        _const_spec((1, D_MODEL)), _const_spec((1, D_MODEL))]
    out_specs = [pl.BlockSpec((tm, D_MODEL), row), pl.BlockSpec((tm, D_MODEL), row)]
    out_shape = [jax.ShapeDtypeStruct((m, D_MODEL), F32), jax.ShapeDtypeStruct((m, D_MODEL), BF16)]
    args = [*mix, w_out, x2d, n1, n2]
    if cast_w is not None:
        nblk = m // tm
        rows, cols = cast_w.shape[1] // nblk, cast_w.shape[2]
        in_specs.append(pl.BlockSpec((None, rows, cols), lambda i: (l, i, 0)))
        out_specs.append(pl.BlockSpec((rows, cols), row))
        out_shape.append(jax.ShapeDtypeStruct(cast_w.shape[1:], BF16))
        args.append(cast_w)
    return pl.pallas_call(
        _outproj_kernel,
        grid=(m // tm,),
        in_specs=in_specs,
        out_specs=out_specs,
        out_shape=out_shape,
        compiler_params=_params("parallel"),
        name="out_proj",
    )(*args)


def _mlp_kernel(h_ref, wu_ref, wd_ref, x1_ref, nw_ref, o_ref, acc_ref):
    f = pl.program_id(1)

    @pl.when(f == 0)
    def _():
        acc_ref[...] = jnp.zeros_like(acc_ref)

    u = jnp.maximum(jnp.dot(h_ref[...], wu_ref[...], preferred_element_type=F32), 0.0)
    acc_ref[...] += jnp.dot((u * u).astype(BF16), wd_ref[...], preferred_element_type=F32)

    @pl.when(f == pl.num_programs(1) - 1)
    def _():
        o_ref[...] = x1_ref[...] + _rms(acc_ref[...], nw_ref[...])


def _mlp(h2, w_up, w_down, x1, nw, tm, tf):
    m = h2.shape[0]
    return pl.pallas_call(
        _mlp_kernel,
        grid=(m // tm, D_FF // tf),
        in_specs=[pl.BlockSpec((tm, D_MODEL), lambda i, f: (i, 0)),
                  pl.BlockSpec((D_MODEL, tf), lambda i, f: (0, f)),
                  pl.BlockSpec((tf, D_MODEL), lambda i, f: (f, 0)),
                  pl.BlockSpec((tm, D_MODEL), lambda i, f: (i, 0)),
                  _const_spec((1, D_MODEL))],
        out_specs=pl.BlockSpec((tm, D_MODEL), lambda i, f: (i, 0)),
        out_shape=jax.ShapeDtypeStruct((m, D_MODEL), F32),
        scratch_shapes=[pltpu.VMEM((tm, D_MODEL), F32)],
        compiler_params=_params("parallel", "arbitrary"),
        name="mlp",
    )(h2, w_up, w_down, x1, nw)


GLA_TB = 256


def _gla_gates(ad, wg2_ref, bg_ref):
    x = _mm(ad, wg2_ref[...], pa=2) + bg_ref[...]
    return _log_sigmoid(x) * (1.0 / GLA_TEMP)


def _gla_kernel(z_ref, misc_ref, wg2_ref, bg_ref, nw_ref, ones_ref, o_ref, s_ref, st_ref, y_ref):
    tb = GLA_TB
    t = pl.program_id(1)

    @pl.when(t == 0)
    def _():
        st_ref[...] = jnp.zeros_like(st_ref)

    q = z_ref[0] * (GLA_DK ** -0.5)
    k = z_ref[1]
    v = jnp.concatenate([z_ref[2], z_ref[3]], axis=1)
    gate = jnp.concatenate([z_ref[4], z_ref[5]], axis=1)
    g = _gla_gates(misc_ref[...], wg2_ref, bg_ref)
    pos = _iota((tb, TILE), 0) & (GLA_C - 1)
    cum = _chunk_cumsum(g, GLA_C, pos)

    y = None
    for j in range(GLA_C):
        if j == 0:
            term = q * k
            vj = v
        else:
            e = jnp.exp(jnp.where(pos >= j, cum - _roll0(cum, j), -jnp.inf))
            term = q * _roll0(k, j) * e
            vj = _roll0(v, j)
        sc = _mm(term, ones_ref[...])
        y = sc * vj if y is None else y + sc * vj
    y_ref[...] = y

    qh = q * jnp.exp(cum)
    last = _chunk_last_bcast(cum, GLA_C, pos)
    kt = k * jnp.exp(last - cum)
    dec = jnp.exp(last)
    kss = [slice(h * GLA_DK, (h + 1) * GLA_DK) for h in range(GLA_H)]
    vss = [slice(h * GLA_DV, (h + 1) * GLA_DV) for h in range(GLA_H)]
    nch = tb // GLA_C
    ds = [[_mm(v[c * GLA_C:(c + 1) * GLA_C, vss[h]], kt[c * GLA_C:(c + 1) * GLA_C, kss[h]], TN)
           for h in range(GLA_H)] for c in range(nch)]
    st = [st_ref[h] for h in range(GLA_H)]
    for c in range(nch):
        rows = slice(c * GLA_C, (c + 1) * GLA_C)
        for h in range(GLA_H):
            y_ref[rows, vss[h]] += _mm(qh[rows, kss[h]], st[h], NT)
            st[h] = st[h] * dec[c * GLA_C:c * GLA_C + 1, kss[h]] + ds[c][h]
    for h in range(GLA_H):
        st_ref[h] = st[h]

    yy = y_ref[...]
    outs = []
    for h in range(GLA_H):
        vs = slice(h * GLA_DV, (h + 1) * GLA_DV)
        outs.append(_rms(yy[:, vs], nw_ref[...]))
    o_ref[...] = (jnp.concatenate(outs, axis=1) * _silu(gate)).astype(BF16)

    @pl.when(t == pl.num_programs(1) - 1)
    def _():
        for h in range(GLA_H):
            s_ref[0, h] = st_ref[h].T


def _gla_prompt(proj, nb, nt_len, wg2, bg, nw, ones):
    tb = GLA_TB
    nt = nt_len // tb
    return pl.pallas_call(
        _gla_kernel,
        grid=(nb, nt),
        in_specs=[pl.BlockSpec((WIDE_TILES, tb, TILE), lambda b, t: (T_GLA // WIDE_TILES, b * nt + t, 0)),
                  pl.BlockSpec((None, tb, TILE), lambda b, t: (T_MISC, b * nt + t, 0)),
                  _const_spec((TILE, TILE)), _const_spec((1, TILE)), _const_spec((1, GLA_DV)),
                  _const_spec((TILE, MIX))],
        out_specs=[pl.BlockSpec((tb, MIX), lambda b, t: (b * nt + t, 0)),
                   pl.BlockSpec((1, GLA_H, GLA_DK, GLA_DV), lambda b, t: (b, 0, 0, 0))],
        out_shape=[jax.ShapeDtypeStruct((nb * nt_len, MIX), BF16),
                   jax.ShapeDtypeStruct((nb, GLA_H, GLA_DK, GLA_DV), F32)],
        scratch_shapes=[pltpu.VMEM((GLA_H, GLA_DV, GLA_DK), F32), pltpu.VMEM((tb, MIX), F32)],
        compiler_params=_params("parallel", "arbitrary"),
        name="gla_prompt",
    )(proj, proj, wg2, bg, nw, ones)


LRU_TB = 256


def _lru_gates(xc, wa_ref, ba_ref, wi_ref, bi_ref, lam_ref):
    ra, ri = [], []
    for j in range(2):
        xs = xc[:, j * TILE:(j + 1) * TILE]
        ra.append(_mm(xs, wa_ref[j], pa=2))
        ri.append(_mm(xs, wi_ref[j], pa=2))
    r = _sigmoid(jnp.concatenate(ra, axis=1) + ba_ref[...])
    i = _sigmoid(jnp.concatenate(ri, axis=1) + bi_ref[...])
    log_a = -LRU_C * r * _softplus(-lam_ref[...])
    a = jnp.exp(log_a)
    mult = jnp.sqrt(_neg_expm1(2.0 * log_a))
    return a, mult, i


def _lru_kernel(x0_ref, x1_ref, g0_ref, g1_ref, cw_ref, cb_ref, wa_ref, ba_ref, wi_ref, bi_ref, lam_ref,
                o_ref, h_ref, tail_ref, carry_ref, hc_ref):
    tb = LRU_TB
    t = pl.program_id(1)

    @pl.when(t == 0)
    def _():
        carry_ref[...] = jnp.zeros_like(carry_ref)
        hc_ref[...] = jnp.zeros_like(hc_ref)

    xb = jnp.concatenate([x0_ref[...], x1_ref[...]], axis=1)
    gb = jnp.concatenate([g0_ref[...], g1_ref[...]], axis=1)
    xc = _causal_conv(xb, carry_ref[...], cw_ref, cb_ref)
    carry_ref[...] = xb[tb - SUBLANES:]
    a, mult, i = _lru_gates(xc, wa_ref, ba_ref, wi_ref, bi_ref, lam_ref)
    row = _iota((tb, MIX), 0)
    mult = jnp.where((row == 0) & (t == 0), 1.0, mult)
    b = mult * i * xc
    pos8 = row & (SUBLANES - 1)

    def roll8(x, d):
        return pltpu.roll(x.reshape(tb // SUBLANES, SUBLANES, MIX), d, 1).reshape(tb, MIX)

    d = 1
    while d < SUBLANES:
        m = pos8 >= d
        b = jnp.where(m, a * roll8(b, d) + b, b)
        a = jnp.where(m, a * roll8(a, d), a)
        d *= 2
    hl = hc_ref[0:1, :]
    groups = []
    for gidx in range(tb // SUBLANES):
        rs = slice(gidx * SUBLANES, (gidx + 1) * SUBLANES)
        hg = b[rs] + a[rs] * hl
        groups.append(hg)
        hl = hg[SUBLANES - 1:SUBLANES, :]
    h = jnp.concatenate(groups, axis=0)
    hc_ref[...] = jnp.broadcast_to(hl, hc_ref.shape)
    o_ref[...] = (h * _gelu_tanh(gb)).astype(BF16)
    h_ref[0] = hl
    tail_ref[0] = xb[tb - SUBLANES:]


def _lru_prompt(proj, nb, nt_len, cw, cb, wa, ba, wi, bi, lam):
    tb = LRU_TB
    nt = nt_len // tb

    def tile(j):
        return pl.BlockSpec((None, tb, TILE), lambda b, t: (T_LRU + j, b * nt + t, 0))

    return pl.pallas_call(
        _lru_kernel,
        grid=(nb, nt),
        in_specs=[tile(0), tile(1), tile(2), tile(3),
                  _const_spec((CONV_W, MIX)), _const_spec((1, MIX)),
                  _const_spec((2, TILE, TILE)), _const_spec((1, MIX)),
                  _const_spec((2, TILE, TILE)), _const_spec((1, MIX)), _const_spec((1, MIX))],
        out_specs=[pl.BlockSpec((tb, MIX), lambda b, t: (b * nt + t, 0)),
                   pl.BlockSpec((1, 1, MIX), lambda b, t: (b, 0, 0)),
                   pl.BlockSpec((1, SUBLANES, MIX), lambda b, t: (b, 0, 0))],
        out_shape=[jax.ShapeDtypeStruct((nb * nt_len, MIX), BF16), jax.ShapeDtypeStruct((nb, 1, MIX), F32),
                   jax.ShapeDtypeStruct((nb, SUBLANES, MIX), F32)],
        scratch_shapes=[pltpu.VMEM((SUBLANES, MIX), F32), pltpu.VMEM((SUBLANES, MIX), F32)],
        compiler_params=_params("parallel", "arbitrary"),
        name="lru_prompt",
    )(proj, proj, proj, proj, cw, cb, wa, ba, wi, bi, lam)


RWKV_TB = 256
RWKV_BS = 128
RWKV_P = {"g": (1, 1), "neu": (1, 1), "app": (1, 1), "chunk": (1, 1)}


def _seg_sum(x, seg_ref):
    outs = []
    for j in range(x.shape[1] // LANES):
        outs.append(_mm(x[:, j * LANES:(j + 1) * LANES], seg_ref[...], pa=2))
    return jnp.concatenate(outs, axis=1)


def _rwkv_pointwise(z, zs, mu_ref, w0_ref, w2_ref, a0_ref, a2_ref, g2_ref, kk_ref, ka_ref, seg_ref):
    zm = z + (zs - z) * mu_ref[...]
    r = zm[:, 0:MIX]
    k = zm[:, MIX:2 * MIX]
    v = zm[:, 2 * MIX:3 * MIX]
    zw = zm[:, 3 * MIX:3 * MIX + 64]
    za = zm[:, 3 * MIX + 64:3 * MIX + 128]
    zg = zm[:, 3 * MIX + 128:3 * MIX + 256]
    lw = -RWKV_DECAY * _sigmoid(w0_ref[...] + _mm(jnp.tanh(zw), w2_ref[...], pa=2))
    a = _sigmoid(a0_ref[...] + _mm(za, a2_ref[...], pa=2))
    g = _mm(_sigmoid(zg), g2_ref[...], pa=2)
    kk = k * kk_ref[...]
    kk = kk / jnp.maximum(jnp.sqrt(_seg_sum(kk * kk, seg_ref)), 1e-12)
    k = k * (1.0 + (a - 1.0) * ka_ref[...])
    return r, lw, k, v, kk, a, g


def _rwkv_finish(y, r, k, v, g, rk_ref, lnw_ref, lnb_ref, seg_ref):
    mean = _seg_sum(y, seg_ref) * (1.0 / RWKV_D)
    yc = y - mean
    var = _seg_sum(yc * yc, seg_ref) * (1.0 / RWKV_D)
    yn = yc * lax.rsqrt(var + RWKV_LN_EPS) * lnw_ref[...] + lnb_ref[...]
    bonus = _seg_sum(r * k * rk_ref[...], seg_ref) * v
    return (yn + bonus) * g


def _rwkv_kernel(z_ref, mu_ref, w0_ref, w2_ref, a0_ref, a2_ref, g2_ref, kk_ref, ka_ref, rk_ref, lnw_ref, lnb_ref,
                 seg_ref, wsrc_ref, o_ref, s_ref, tail_ref, wdst_ref, prev_ref, st_ref, y_ref, scat_ref):
    wdst_ref[...] = wsrc_ref[...].astype(BF16)
    tb = RWKV_TB
    cc = RWKV_C
    t = pl.program_id(1)

    @pl.when(t == 0)
    def _():
        prev_ref[...] = jnp.zeros_like(prev_ref)
        st_ref[...] = jnp.zeros_like(st_ref)

    z = jnp.concatenate([z_ref[j] for j in range(GROUP_TILES)], axis=1)
    zs = _shift_rows(z, prev_ref[...], 1, _iota((SUBLANES, RWKV_COLS), 0))
    prev_ref[...] = z[tb - SUBLANES:]
    tail_ref[0] = z[tb - SUBLANES:]
    r, lw, k, v, kk, a, g = _rwkv_pointwise(z, zs, mu_ref, w0_ref, w2_ref, a0_ref, a2_ref, g2_ref, kk_ref, ka_ref,
                                            seg_ref)
    b = kk * a

    pos = _iota((tb, MIX), 0) & (cc - 1)
    lg = _chunk_cumsum(lw, cc, pos)
    lg_end = _chunk_last_bcast(lg, cc, pos)
    gam = jnp.exp(lg)
    inv = jnp.exp(-lg)
    rg = r * gam
    kg = kk * jnp.exp(lg - lw)
    bi = b * inv
    ki = k * inv
    e_end = jnp.exp(lg_end - lg)
    bt = b * e_end
    kt = k * e_end
    g_end = jnp.exp(lg_end)

    bs = RWKV_BS
    ri = _iota((bs, bs), 0)
    ci = _iota((bs, bs), 1)
    same = (ri - (ri & (cc - 1))) == (ci - (ci & (cc - 1)))
    strict = same & (ci < ri)
    incl = same & (ci <= ri)
    eye = (ri == ci).astype(F32)
    eye_d = _iota((RWKV_D, RWKV_D), 0) == _iota((RWKV_D, RWKV_D), 1)

    hs = [slice(h * RWKV_D, (h + 1) * RWKV_D) for h in range(RWKV_H)]
    mg, mn, ma, mc_ = RWKV_P["g"], RWKV_P["neu"], RWKV_P["app"], RWKV_P["chunk"]
    units = [(slice(q * bs, (q + 1) * bs), h) for q in range(tb // bs) for h in range(RWKV_H)]
    vh = [v[rb, hs[h]] for rb, h in units]
    gmat = [_mm(jnp.concatenate([kg[rb, hs[h]], rg[rb, hs[h]]], axis=0),
                jnp.concatenate([bi[rb, hs[h]], ki[rb, hs[h]]], axis=0), NT, *mg) for rb, h in units]
    amat = [jnp.where(strict, gm[:bs, :bs], 0.0) for gm in gmat]
    bmat = [jnp.where(strict, gm[:bs, bs:], 0.0) for gm in gmat]
    pb_ = [jnp.where(incl, gm[bs:, :bs], 0.0) for gm in gmat]
    pk_ = [jnp.where(incl, gm[bs:, bs:], 0.0) for gm in gmat]
    bv = [_mm(bm_, v_, NN, *ma) for bm_, v_ in zip(bmat, vh)]

    def lower_left(b):
        rb = ri & (b - 1)
        cb = ci & (b - 1)
        return ((ri - rb) == (ci - cb)) & (rb >= b // 2) & (cb < b // 2)

    x = [eye - jnp.where(lower_left(2), a_, 0.0) for a_ in amat]
    b = 4
    while b <= cc:
        m = lower_left(b)
        tx = [_mm(x_, jnp.where(m, a_, 0.0), NN, *mn) for x_, a_ in zip(x, amat)]
        x = [x_ - _mm(tx_, x_, NN, *mn) for x_, tx_ in zip(x, tx)]
        b *= 2
    wu = [-_mm(x_, jnp.concatenate([kg[rb, hs[h]], bv_], axis=1), NN, *ma)
          for x_, (rb, h), bv_ in zip(x, units, bv)]
    low = _iota((bs, LANES), 1) < RWKV_D
    zero = jnp.zeros((bs, LANES), F32)

    def dup(slab, odd):
        sw = pltpu.roll(slab, RWKV_D, 1)
        return jnp.where(low, sw, slab) if odd else jnp.where(low, slab, sw)

    def upper(slab, odd):
        return jnp.where(low, zero, slab if odd else pltpu.roll(slab, RWKV_D, 1))

    pair = [slice((h // 2) * LANES, (h // 2 + 1) * LANES) for h in range(RWKV_H)]
    rg2 = [dup(rg[rb, pair[h]], h % 2) for rb, h in units]
    ov = [upper(v[rb, pair[h]], h % 2) for rb, h in units]
    ww = [jnp.where(low, wu_, pltpu.roll(wu_, RWKV_D, 1)) for wu_ in wu]
    qy = [_mm(pb, jnp.concatenate([ww_, wu_], axis=1), NN, *ma) for pb, ww_, wu_ in zip(pb_, ww, wu)]
    pkv = [_mm(pk, v_, NN, *ma) for pk, v_ in zip(pk_, vh)]
    qt2 = [rg2_ + qy_[:, :LANES] for rg2_, qy_ in zip(rg2, qy)]
    y0 = [qy_[:, LANES + RWKV_D:] + pkv_ for qy_, pkv_ in zip(qy, pkv)]
    nch = tb // cc
    cpb = bs // cc
    s = [st_ref[h] for h in range(RWKV_H)]
    for c in range(nch):
        rows = slice(c * cc, (c + 1) * cc)
        lrows = slice((c % cpb) * cc, (c % cpb + 1) * cc)
        u0 = (c // cpb) * RWKV_H
        for h in range(RWKV_H):
            scat_ref[h, :, c * RWKV_D:(c + 1) * RWKV_D] = s[h]
        tr = [_mm(jnp.concatenate([wu[u0 + h][lrows], ov[u0 + h][lrows]], axis=0),
                  jnp.concatenate([bt[rows, hs[h]], kt[rows, hs[h]]], axis=0), TN, *mc_) for h in range(RWKV_H)]
        s = [_mm(s_, jnp.where(eye_d, g_end[c * cc:c * cc + 1, ls], 0.0) + t_[:RWKV_D], NN, *mc_) + t_[RWKV_D:]
             for s_, t_, ls in zip(s, tr, hs)]
    r5 = _iota((bs, cpb * RWKV_D), 0)
    c5 = _iota((bs, cpb * RWKV_D), 1)
    own = (r5 - (r5 & (cc - 1))) * (RWKV_D // cc) == c5 - (c5 & (RWKV_D - 1))
    for h in range(RWKV_H):
        st_ref[h] = s[h]
    for u, (rb, h) in enumerate(units):
        q = u // RWKV_H
        qexp = jnp.where(own, jnp.concatenate([qt2[u]] * (cpb * RWKV_D // LANES), axis=1), 0.0)
        y_ref[rb, hs[h]] = _mm(qexp, scat_ref[h, :, q * cpb * RWKV_D:(q + 1) * cpb * RWKV_D], NT, *mc_) + y0[u]

    o_ref[...] = _rwkv_finish(y_ref[...], r, k, v, g, rk_ref, lnw_ref, lnb_ref, seg_ref).astype(BF16)

    @pl.when(t == pl.num_programs(1) - 1)
    def _():
        s_ref[0] = st_ref[...]


def _rwkv_prompt(proj, nb, nt_len, p, cast_w, l):
    tb = RWKV_TB
    nt = nt_len // tb
    crow, ccol = cast_w.shape[1] // (nb * nt), cast_w.shape[2]
    return pl.pallas_call(
        _rwkv_kernel,
        grid=(nb, nt),
        in_specs=[pl.BlockSpec((GROUP_TILES, tb, TILE), lambda b, t: (T_RWKV // GROUP_TILES, b * nt + t, 0)),
                  _const_spec((1, RWKV_COLS)), _const_spec((1, MIX)), _const_spec((64, MIX)),
                  _const_spec((1, MIX)), _const_spec((64, MIX)), _const_spec((128, MIX)),
                  _const_spec((1, MIX)), _const_spec((1, MIX)), _const_spec((1, MIX)),
                  _const_spec((1, MIX)), _const_spec((1, MIX)), _const_spec((LANES, LANES)),
                  pl.BlockSpec((None, crow, ccol), lambda b, t: (l, b * nt + t, 0))],
        out_specs=[pl.BlockSpec((tb, MIX), lambda b, t: (b * nt + t, 0)),
                   pl.BlockSpec((1, RWKV_H, RWKV_D, RWKV_D), lambda b, t: (b, 0, 0, 0)),
                   pl.BlockSpec((1, SUBLANES, RWKV_COLS), lambda b, t: (b, 0, 0)),
                   pl.BlockSpec((crow, ccol), lambda b, t: (b * nt + t, 0))],
        out_shape=[jax.ShapeDtypeStruct((nb * nt_len, MIX), BF16),
                   jax.ShapeDtypeStruct((nb, RWKV_H, RWKV_D, RWKV_D), F32),
                   jax.ShapeDtypeStruct((nb, SUBLANES, RWKV_COLS), F32),
                   jax.ShapeDtypeStruct(cast_w.shape[1:], BF16)],
        scratch_shapes=[pltpu.VMEM((SUBLANES, RWKV_COLS), F32), pltpu.VMEM((RWKV_H, RWKV_D, RWKV_D), F32),
                        pltpu.VMEM((tb, MIX), F32), pltpu.VMEM((RWKV_H, RWKV_D, (tb // RWKV_C) * RWKV_D), F32)],
        compiler_params=_params("parallel", "arbitrary"),
        name="rwkv_prompt",
    )(proj, p["mu"], p["w0"], p["w2"], p["a0"], p["a2"], p["g2"], p["kk"], p["ka"], p["rk"], p["lnw"], p["lnb"],
      p["seg"], cast_w)


SSD_TB = 512


def _ssd_pointwise(xbc_c, dt_raw, dtb_ref, aneg_ref):
    xbc = _silu(xbc_c)
    dt = _softplus(dt_raw + dtb_ref[...])
    return xbc, dt, dt * aneg_ref[...]


def _expand_heads(misc, width):
    n = misc.shape[0]
    return jnp.concatenate([jnp.broadcast_to(misc[:, MISC_DT + h:MISC_DT + h + 1], (n, width))
                            for h in range(SSD_H)], axis=1)


def _ssd_kernel(z_ref, misc_ref, cw_ref, cb_ref, dtb_ref, aneg_ref, dsk_ref, nw_ref, o_ref, s_ref, tail_ref, carry_ref,
                st_ref, y_ref):
    tb = SSD_TB
    cc = SSD_C
    t = pl.program_id(1)

    @pl.when(t == 0)
    def _():
        carry_ref[...] = jnp.zeros_like(carry_ref)
        st_ref[...] = jnp.zeros_like(st_ref)

    gate = jnp.concatenate([z_ref[0], z_ref[1]], axis=1)
    xbc_raw = jnp.concatenate([z_ref[2], z_ref[3], z_ref[4], z_ref[5]], axis=1)
    xbc_c = _causal_conv(xbc_raw, carry_ref[...], cw_ref, cb_ref)
    carry_ref[...] = xbc_raw[tb - SUBLANES:]
    tail_ref[0] = xbc_raw[tb - SUBLANES:]
    xbc, dt, dta = _ssd_pointwise(xbc_c, misc_ref[...], dtb_ref, aneg_ref)
    xs = xbc[:, :MIX]
    bm = xbc[:, MIX:MIX + TILE]
    cm = xbc[:, MIX + TILE:]
    pos = _iota((tb, TILE), 0) & (cc - 1)
    cum = _chunk_cumsum(dta, cc, pos)
    xdt = xs * _expand_heads(dt, SSD_P)
    tri = _iota((cc, cc), 0) >= _iota((cc, cc), 1)

    hg = SSD_H // SSD_G
    nch = tb // cc
    pre = []
    for c in range(nch):
        rows = slice(c * cc, (c + 1) * cc)
        cum_c = cum[rows, :LANES]
        cum_t = cum_c.T
        for gi in range(SSD_G):
            ns = slice(gi * SSD_N, (gi + 1) * SSD_N)
            cg = cm[rows, ns]
            bg = bm[rows, ns]
            gmat = _mm(cg, bg, NT)
            yds, xds, decs, ecs = [], [], [], []
            for hh in range(hg):
                h = gi * hg + hh
                hc = MISC_DT + h
                col = cum_c[:, hc:hc + 1]
                lmat = jnp.exp(jnp.where(tri, col - cum_t[hc:hc + 1, :], -jnp.inf))
                xh = xdt[rows, h * SSD_P:(h + 1) * SSD_P]
                last = cum_c[cc - 1:cc, hc:hc + 1]
                yds.append(_mm(gmat * lmat, xh))
                xds.append(xh * jnp.exp(last - col))
                decs.append(jnp.broadcast_to(jnp.exp(last), (SSD_P, SSD_N)))
                ecs.append(jnp.broadcast_to(jnp.exp(col), (cc, SSD_P)))
            pre.append((cg, jnp.concatenate(yds, axis=1), jnp.concatenate(ecs, axis=1),
                        jnp.concatenate(decs, axis=0), _mm(jnp.concatenate(xds, axis=1), bg, TN)))
    for c in range(nch):
        rows = slice(c * cc, (c + 1) * cc)
        for gi in range(SSD_G):
            gs = slice(gi * hg * SSD_P, (gi + 1) * hg * SSD_P)
            cg, yd, ec, dec, ds = pre[c * SSD_G + gi]
            st = st_ref[gs, :]
            y_ref[rows, gs] = yd + _mm(cg, st, NT) * ec
            st_ref[gs, :] = st * dec + ds

    y = (y_ref[...] + dsk_ref[...] * xs) * _silu(gate)
    o_ref[...] = _rms(y, nw_ref[...]).astype(BF16)

    @pl.when(t == pl.num_programs(1) - 1)
    def _():
        for h in range(SSD_H):
            s_ref[0, h] = st_ref[h * SSD_P:(h + 1) * SSD_P, :]


def _ssd_prompt(proj, nb, nt_len, cw, cb, dtb, aneg, dsk, nw):
    tb = SSD_TB
    nt = nt_len // tb
    return pl.pallas_call(
        _ssd_kernel,
        grid=(nb, nt),
        in_specs=[pl.BlockSpec((WIDE_TILES, tb, TILE), lambda b, t: (T_SSD // WIDE_TILES, b * nt + t, 0)),
                  pl.BlockSpec((None, tb, TILE), lambda b, t: (T_MISC, b * nt + t, 0)),
                  _const_spec((CONV_W, SSD_CONV_DIM)), _const_spec((1, SSD_CONV_DIM)),
                  _const_spec((1, TILE)), _const_spec((1, TILE)), _const_spec((1, MIX)), _const_spec((1, MIX))],
        out_specs=[pl.BlockSpec((tb, MIX), lambda b, t: (b * nt + t, 0)),
                   pl.BlockSpec((1, SSD_H, SSD_P, SSD_N), lambda b, t: (b, 0, 0, 0)),
                   pl.BlockSpec((1, SUBLANES, SSD_CONV_DIM), lambda b, t: (b, 0, 0))],
        out_shape=[jax.ShapeDtypeStruct((nb * nt_len, MIX), BF16),
                   jax.ShapeDtypeStruct((nb, SSD_H, SSD_P, SSD_N), F32),
                   jax.ShapeDtypeStruct((nb, SUBLANES, SSD_CONV_DIM), F32)],
        scratch_shapes=[pltpu.VMEM((SUBLANES, SSD_CONV_DIM), F32), pltpu.VMEM((SSD_H * SSD_P, SSD_N), F32),
                        pltpu.VMEM((tb, MIX), F32)],
        compiler_params=_params("parallel", "arbitrary"),
        name="ssd_prompt",
    )(proj, proj, cw, cb, dtb, aneg, dsk, nw)


R_GLA_V, R_GLA_GATE, R_GLA_Q = 0, 512, 1024
R_RW_K, R_RW_R, R_RW_V, R_RW_G = 1280, 1792, 2304, 2816
R_SSD_B, R_SSD_C, R_SSD_E, R_SSD_GATE, R_SSD_X = 3328, 3584, 3840, 4864, 5376
ROW_W = 5888
C_GLA_A, C_GLA_K, C_SSD_X = 0, 256, 512
COL_W = 1024
T_RW_W, T_RW_B, T_RW_K, T_RW_KK, T_RW_R, T_RW_V = 0, 512, 1024, 1536, 2048, 2560
RWT_W = 3072
DEC_RB = 8


def _dec_prep_kernel(z_ref, lconv_ref, lh_ref, rprev_ref, sconv_ref,
                     wg2_ref, bg_ref,
                     lcw_ref, lcb_ref, wa_ref, ba_ref, wi_ref, bi_ref, lam_ref,
                     mu_ref, w0_ref, w2_ref, a0_ref, a2_ref, g2_ref, kk_ref, ka_ref, seg_ref,
                     scw_ref, scb_ref, dtb_ref, aneg_ref,
                     rows_ref, colt_ref, rwt_ref, olru_ref, lh_out_ref, lconv_out_ref, sconv_out_ref):
    k = z_ref[T_GLA + 1]
    g = _gla_gates(z_ref[T_MISC], wg2_ref, bg_ref)
    rows_ref[:, R_GLA_V:R_GLA_V + TILE] = z_ref[T_GLA + 2]
    rows_ref[:, R_GLA_V + TILE:R_GLA_V + MIX] = z_ref[T_GLA + 3]
    rows_ref[:, R_GLA_GATE:R_GLA_GATE + TILE] = z_ref[T_GLA + 4]
    rows_ref[:, R_GLA_GATE + TILE:R_GLA_GATE + MIX] = z_ref[T_GLA + 5]
    rows_ref[:, R_GLA_Q:R_GLA_Q + TILE] = z_ref[T_GLA + 0] * (GLA_DK ** -0.5)
    cols = [jnp.exp(g), k]

    xb = jnp.concatenate([z_ref[T_LRU + 0], z_ref[T_LRU + 1]], axis=1)
    gb = jnp.concatenate([z_ref[T_LRU + 2], z_ref[T_LRU + 3]], axis=1)
    xc = lcb_ref[...] + xb * lcw_ref[CONV_W - 1:CONV_W, :]
    for i in range(CONV_W - 1):
        xc = xc + lconv_ref[:, i * MIX:(i + 1) * MIX] * lcw_ref[i:i + 1, :]
    a, mult, gi = _lru_gates(xc, wa_ref, ba_ref, wi_ref, bi_ref, lam_ref)
    h = a * lh_ref[...] + mult * gi * xc
    lh_out_ref[...] = h
    olru_ref[...] = (h * _gelu_tanh(gb)).astype(BF16)
    lconv_out_ref[:, 0:2 * MIX] = lconv_ref[:, MIX:3 * MIX]
    lconv_out_ref[:, 2 * MIX:3 * MIX] = xb

    z = jnp.concatenate([z_ref[T_RWKV + j] for j in range(GROUP_TILES)], axis=1)
    r, lw, kmod, v, kk, a7, g7 = _rwkv_pointwise(z, rprev_ref[...], mu_ref, w0_ref, w2_ref, a0_ref, a2_ref, g2_ref,
                                                 kk_ref, ka_ref, seg_ref)
    rows_ref[:, R_RW_K:R_RW_K + MIX] = kmod
    rows_ref[:, R_RW_R:R_RW_R + MIX] = r
    rows_ref[:, R_RW_V:R_RW_V + MIX] = v
    rows_ref[:, R_RW_G:R_RW_G + MIX] = g7
    off = 0
    for vec in (jnp.exp(lw), kk * a7, kmod, kk, r, v):
        for j in range(MIX // LANES):
            rwt_ref[off:off + LANES, :] = vec[:, j * LANES:(j + 1) * LANES].T
            off += LANES

    xbc_raw = jnp.concatenate([z_ref[T_SSD + 2], z_ref[T_SSD + 3], z_ref[T_SSD + 4], z_ref[T_SSD + 5]], axis=1)
    xbc_c = scb_ref[...] + xbc_raw * scw_ref[CONV_W - 1:CONV_W, :]
    for i in range(CONV_W - 1):
        xbc_c = xbc_c + sconv_ref[:, i * SSD_CONV_DIM:(i + 1) * SSD_CONV_DIM] * scw_ref[i:i + 1, :]
    xbc, dt, dta = _ssd_pointwise(xbc_c, z_ref[T_MISC], dtb_ref, aneg_ref)
    xs = xbc[:, :MIX]
    rows_ref[:, R_SSD_B:R_SSD_B + TILE] = xbc[:, MIX:MIX + TILE]
    rows_ref[:, R_SSD_C:R_SSD_C + TILE] = xbc[:, MIX + TILE:]
    rows_ref[:, R_SSD_E:R_SSD_E + SSD_H * LANES] = _expand_heads(jnp.exp(dta), LANES)
    rows_ref[:, R_SSD_GATE:R_SSD_GATE + TILE] = z_ref[T_SSD + 0]
    rows_ref[:, R_SSD_GATE + TILE:R_SSD_GATE + MIX] = z_ref[T_SSD + 1]
    rows_ref[:, R_SSD_X:R_SSD_X + MIX] = xs
    cols.append(xs * _expand_heads(dt, SSD_P))
    sconv_out_ref[:, 0:2 * SSD_CONV_DIM] = sconv_ref[:, SSD_CONV_DIM:3 * SSD_CONV_DIM]
    sconv_out_ref[:, 2 * SSD_CONV_DIM:3 * SSD_CONV_DIM] = xbc_raw

    off = 0
    for cvec in cols:
        for j in range(cvec.shape[1] // LANES):
            colt_ref[off:off + LANES, :] = cvec[:, j * LANES:(j + 1) * LANES].T
            off += LANES


def _dec_prep(zdec, lconv, lh, rprev, sconv, pp):
    n = zdec.shape[1]
    args = [zdec, lconv, lh, rprev, sconv,
            pp["gla_wg2"], pp["gla_bg"],
            pp["lru_cw"], pp["lru_cb"], pp["lru_wa"], pp["lru_ba"], pp["lru_wi"], pp["lru_bi"], pp["lru_lam"],
            pp["rwkv"]["mu"], pp["rwkv"]["w0"], pp["rwkv"]["w2"], pp["rwkv"]["a0"], pp["rwkv"]["a2"],
            pp["rwkv"]["g2"], pp["rwkv"]["kk"], pp["rwkv"]["ka"], pp["rwkv"]["seg"],
            pp["ssd_cw"], pp["ssd_cb"], pp["ssd_dtb"], pp["ssd_aneg"]]
    return pl.pallas_call(
        _dec_prep_kernel,
        out_shape=[jax.ShapeDtypeStruct((n, ROW_W), F32), jax.ShapeDtypeStruct((COL_W, n), F32),
                   jax.ShapeDtypeStruct((RWT_W, n), F32),
                   jax.ShapeDtypeStruct((n, MIX), BF16), jax.ShapeDtypeStruct((n, MIX), F32),
                   jax.ShapeDtypeStruct((n, 3 * MIX), F32), jax.ShapeDtypeStruct((n, 3 * SSD_CONV_DIM), F32)],
        compiler_params=pltpu.CompilerParams(vmem_limit_bytes=VMEM_LIMIT),
        name="decode_prep",
    )(*args)


def _dec_rwkv_kernel(w_ref, b_ref, k_ref, kk_ref, r_ref, v_ref, s_ref, *rest, layer, fill_others):
    s_out, y_ref = rest[-2:]
    if fill_others:
        for j in range(DEPTH):
            if j != layer:
                s_out[j] = jnp.zeros(s_out.shape[1:], F32)
        s_out = s_out.at[layer]
    w = w_ref[...]
    b = b_ref[...]
    k = k_ref[...]
    kk = kk_ref[...]
    r = r_ref[...]

    def body(vi, carry):
        s = s_ref[vi]
        sa = jnp.sum(s * kk, axis=0, keepdims=True)
        s = s * w - sa * b + v_ref[pl.ds(vi, 1), :] * k
        s_out[vi] = s
        y_ref[pl.ds(vi, 1), :] = jnp.sum(s * r, axis=0, keepdims=True)
        return carry

    lax.fori_loop(0, RWKV_D, body, 0, unroll=8)


def _dec_rwkv(rwt, sr_t, l, prev):
    n = rwt.shape[1]
    hb = MIX // RWKV_D

    def vec(off):
        return pl.BlockSpec((RWKV_D, n), lambda h: (off // RWKV_D + h, 0))

    sspec = pl.BlockSpec((None, None, RWKV_D, RWKV_D, n), lambda h: (l, h, 0, 0, 0))
    extra = [] if prev is None else [prev]
    first = prev is None
    ospec = pl.BlockSpec((DEPTH, None, RWKV_D, RWKV_D, n), lambda h: (0, h, 0, 0, 0)) if first else sspec
    return pl.pallas_call(
        functools.partial(_dec_rwkv_kernel, layer=l, fill_others=first),
        grid=(hb,),
        in_specs=[vec(T_RW_W), vec(T_RW_B), vec(T_RW_K), vec(T_RW_KK), vec(T_RW_R), vec(T_RW_V), sspec]
        + [pl.BlockSpec(memory_space=pl.ANY)] * len(extra),
        out_specs=[ospec, pl.BlockSpec((RWKV_D, n), lambda h: (h, 0))],
        out_shape=[jax.ShapeDtypeStruct(sr_t.shape, F32), jax.ShapeDtypeStruct((MIX, n), F32)],
        input_output_aliases={7: 0} if extra else {},
        compiler_params=_params("parallel"),
        name="decode_rwkv",
    )(rwt, rwt, rwt, rwt, rwt, rwt, sr_t, *extra)


def _dec_state_kernel(colt_ref, rows_ref, sg_ref, ss_ref, *rest, layer, fill_others):
    sg_out, ss_out, yg_ref, ys_ref = rest[-4:]
    if fill_others:
        for j in range(DEPTH):
            if j != layer:
                sg_out[j] = jnp.zeros(sg_out.shape[1:], F32)
                ss_out[j] = jnp.zeros(ss_out.shape[1:], F32)
        sg_out = sg_out.at[layer]
        ss_out = ss_out.at[layer]
    pid = pl.program_id(0)
    n = colt_ref.shape[1]
    lane = _iota((1, n), 1)
    rid = _iota((DEC_RB, LANES), 0)
    rows = rows_ref[...]
    hg = SSD_H // SSD_G
    yg = [jnp.zeros((DEC_RB, GLA_DV), F32) for _ in range(GLA_H)]
    ys = [jnp.zeros((DEC_RB, SSD_P), F32) for _ in range(SSD_H)]
    for i in range(DEC_RB):
        sel = lane == pid * DEC_RB + i
        col = jnp.sum(jnp.where(sel, colt_ref[...], 0.0), axis=1, keepdims=True)
        rv = rows[i:i + 1, :]
        mine = rid == i
        for h in range(GLA_H):
            al = col[C_GLA_A + h * GLA_DK:C_GLA_A + (h + 1) * GLA_DK]
            kc = col[C_GLA_K + h * GLA_DK:C_GLA_K + (h + 1) * GLA_DK]
            vr = rv[:, R_GLA_V + h * GLA_DV:R_GLA_V + (h + 1) * GLA_DV]
            s = al * sg_ref[i, h] + kc * vr
            sg_out[i, h] = s
            q8 = rows[:, R_GLA_Q + h * GLA_DK:R_GLA_Q + (h + 1) * GLA_DK]
            yg[h] = jnp.where(mine, _mm(q8, s, NN, pa=2, pb=2), yg[h])
        for h in range(SSD_H):
            gi = h // hg
            e = rv[:, R_SSD_E + h * LANES:R_SSD_E + (h + 1) * LANES]
            bn = rv[:, R_SSD_B + gi * SSD_N:R_SSD_B + (gi + 1) * SSD_N]
            xc = col[C_SSD_X + h * SSD_P:C_SSD_X + (h + 1) * SSD_P]
            s = ss_ref[i, h] * e + xc * bn
            ss_out[i, h] = s
            c8 = rows[:, R_SSD_C + gi * SSD_N:R_SSD_C + (gi + 1) * SSD_N]
            ys[h] = jnp.where(mine[:, :SSD_P], _mm(c8, s, NT, pa=2, pb=2), ys[h])
    yg_ref[...] = jnp.concatenate(yg, axis=1)
    ys_ref[...] = jnp.concatenate(ys, axis=1)


def _dec_state(colt, rows, sg, ss, l, prev):
    n = rows.shape[0]
    blk = lambda i: (l, i, 0, 0, 0)
    specs = [pl.BlockSpec((None, DEC_RB, GLA_H, GLA_DK, GLA_DV), blk),
             pl.BlockSpec((None, DEC_RB, SSD_H, SSD_P, SSD_N), blk)]
    extra = [] if prev is None else list(prev)
    first = prev is None
    ospecs = specs
    if first:
        all_layers = lambda i: (0, i, 0, 0, 0)
        ospecs = [pl.BlockSpec((DEPTH, DEC_RB, GLA_H, GLA_DK, GLA_DV), all_layers),
                  pl.BlockSpec((DEPTH, DEC_RB, SSD_H, SSD_P, SSD_N), all_layers)]
    rowspec = pl.BlockSpec((DEC_RB, MIX), lambda i: (i, 0))
    return pl.pallas_call(
        functools.partial(_dec_state_kernel, layer=l, fill_others=first),
        grid=(n // DEC_RB,),
        in_specs=[_const_spec((COL_W, n)), pl.BlockSpec((DEC_RB, ROW_W), lambda i: (i, 0))] + specs
        + [pl.BlockSpec(memory_space=pl.ANY)] * len(extra),
        out_specs=ospecs + [rowspec, rowspec],
        out_shape=[jax.ShapeDtypeStruct(sg.shape, F32), jax.ShapeDtypeStruct(ss.shape, F32),
                   jax.ShapeDtypeStruct((n, MIX), F32), jax.ShapeDtypeStruct((n, MIX), F32)],
        input_output_aliases={4 + j: j for j in range(len(extra))},
        compiler_params=_params("parallel"),
        name="decode_state",
    )(colt, rows, sg, ss, *extra)


def _dec_finish_kernel(rows_ref, yg_ref, yrt_ref, ys_ref, gnw_ref, rk_ref, lnw_ref, lnb_ref, seg_ref,
                       dsk_ref, snw_ref, og_ref, or_ref, os_ref):
    outs = []
    for h in range(GLA_H):
        outs.append(_rms(yg_ref[:, h * GLA_DV:(h + 1) * GLA_DV], gnw_ref[...]))
    og_ref[...] = (jnp.concatenate(outs, axis=1) * _silu(rows_ref[:, R_GLA_GATE:R_GLA_GATE + MIX])).astype(BF16)
    yr = jnp.concatenate([yrt_ref[j * LANES:(j + 1) * LANES, :].T for j in range(MIX // LANES)], axis=1)
    or_ref[...] = _rwkv_finish(yr, rows_ref[:, R_RW_R:R_RW_R + MIX], rows_ref[:, R_RW_K:R_RW_K + MIX],
                               rows_ref[:, R_RW_V:R_RW_V + MIX], rows_ref[:, R_RW_G:R_RW_G + MIX],
                               rk_ref, lnw_ref, lnb_ref, seg_ref).astype(BF16)
    y = ((ys_ref[...] + dsk_ref[...] * rows_ref[:, R_SSD_X:R_SSD_X + MIX])
         * _silu(rows_ref[:, R_SSD_GATE:R_SSD_GATE + MIX]))
    os_ref[...] = _rms(y, snw_ref[...]).astype(BF16)


def _dec_finish(rows, yg, yrt, ys, pp):
    n = rows.shape[0]
    o = jax.ShapeDtypeStruct((n, MIX), BF16)
    return pl.pallas_call(
        _dec_finish_kernel,
        out_shape=[o, o, o],
        compiler_params=pltpu.CompilerParams(vmem_limit_bytes=VMEM_LIMIT),
        name="decode_finish",
    )(rows, yg, yrt, ys, pp["gla_nw"], pp["rwkv"]["rk"], pp["rwkv"]["lnw"], pp["rwkv"]["lnb"], pp["rwkv"]["seg"],
      pp["ssd_dsk"], pp["ssd_nw"])


def _block_diag(w8):
    nb = TILE // 64
    tiled = jnp.tile(w8.reshape(2, TILE, 64), (1, 1, nb))
    idx = jnp.arange(TILE) // 64
    return jnp.where(idx[:, None] == idx[None, :], tiled, 0.0)


def _prep_weights(w_in, w_out, w_up, w_down):
    o_lru = GLA_COLS
    o_rwkv = GLA_COLS + LRU_COLS
    o_ssd = o_rwkv + RWKV_COLS
    wide = WIDE_TILES * TILE
    wt = jnp.transpose(w_in, (0, 2, 1))
    zpad = lambda n: jnp.zeros((DEPTH, n, D_MODEL), w_in.dtype)
    w_perm = jnp.concatenate([
        wt[:, o_rwkv:o_rwkv + RWKV_COLS],
        wt[:, o_lru:o_lru + LRU_COLS],
        wt[:, wide:GLA_COLS], wt[:, o_ssd + wide:o_ssd + SSD_COLS], zpad(TILE - GLA_RANK - SSD_H),
        wt[:, 0:wide],
        wt[:, o_ssd:o_ssd + wide]], axis=1).astype(BF16)
    return {"w_in": w_perm,
            "w_out": w_out,
            "w_up": w_up, "w_down": w_down}


def _prep_layer(l, norm_mix_pre, norm_mix_post, norm_mlp_pre, norm_mlp_post,
                gla_w_gate2, gla_b_gate, gla_norm, lru_conv_w, lru_conv_b, lru_w_a, lru_b_a, lru_w_i, lru_b_i,
                lru_lambda, rwkv_mu, rwkv_w0, rwkv_w2, rwkv_a0, rwkv_a2, rwkv_g2, rwkv_k_k, rwkv_k_a, rwkv_r_k,
                rwkv_ln_w, rwkv_ln_b, ssd_conv_w, ssd_conv_b, ssd_dt_bias, ssd_a_log, ssd_d, ssd_norm):
    row = lambda a: a.reshape(1, -1).astype(F32)
    ii = jnp.arange(TILE)[:, None]
    jj = jnp.arange(MIX)[None, :]
    seg_i = jnp.arange(LANES)
    pp = {
        "n_mix_pre": row(norm_mix_pre[l]), "n_mix_post": row(norm_mix_post[l]),
        "n_mlp_pre": row(norm_mlp_pre[l]), "n_mlp_post": row(norm_mlp_post[l]),
        "gla_wg2": jnp.pad(gla_w_gate2[l], ((0, TILE - GLA_RANK), (0, 0))).astype(BF16),
        "gla_bg": row(gla_b_gate[l]),
        "gla_nw": row(gla_norm[l]),
        "gla_ones": (ii // GLA_DK == jj // GLA_DV).astype(BF16),
        "lru_cw": lru_conv_w[l].astype(F32), "lru_cb": row(lru_conv_b[l]),
        "lru_wa": _block_diag(lru_w_a[l]).astype(BF16),
        "lru_wi": _block_diag(lru_w_i[l]).astype(BF16),
        "lru_ba": row(lru_b_a[l]), "lru_bi": row(lru_b_i[l]), "lru_lam": row(lru_lambda[l]),
        "rwkv": {
            "mu": row(rwkv_mu[l]), "w0": row(rwkv_w0[l]), "w2": rwkv_w2[l].astype(BF16),
            "a0": row(rwkv_a0[l]), "a2": rwkv_a2[l].astype(BF16), "g2": rwkv_g2[l].astype(BF16),
            "kk": row(rwkv_k_k[l]), "ka": row(rwkv_k_a[l]), "rk": row(rwkv_r_k[l]),
            "lnw": row(rwkv_ln_w[l]), "lnb": row(rwkv_ln_b[l]),
            "seg": (seg_i[:, None] // RWKV_D == seg_i[None, :] // RWKV_D).astype(BF16),
        },
        "ssd_cw": ssd_conv_w[l].astype(F32), "ssd_cb": row(ssd_conv_b[l]),
        "ssd_dtb": jnp.pad(ssd_dt_bias[l].astype(F32), (MISC_DT, TILE - MISC_DT - SSD_H)).reshape(1, TILE),
        "ssd_aneg": jnp.pad(-jnp.exp(ssd_a_log[l].astype(F32)), (MISC_DT, TILE - MISC_DT - SSD_H)).reshape(1, TILE),
        "ssd_dsk": jnp.repeat(ssd_d[l].astype(F32), SSD_P).reshape(1, MIX),
        "ssd_nw": row(ssd_norm[l]),
    }
    return pp


def _layer_prompt(x2d, nb, nt_len, wts, l, pp, tm_proj=1024, tm_out=512, tm_mlp=512, tf=1024):
    proj, wo_bf = _proj(x2d, pp["n_mix_pre"], wts["w_in"], l, tm_proj, (wts["w_out"],))
    o_gla, s_gla = _gla_prompt(proj, nb, nt_len, pp["gla_wg2"], pp["gla_bg"], pp["gla_nw"], pp["gla_ones"])
    o_lru, h_lru, lru_tail = _lru_prompt(proj, nb, nt_len, pp["lru_cw"], pp["lru_cb"], pp["lru_wa"], pp["lru_ba"],
                                         pp["lru_wi"], pp["lru_bi"], pp["lru_lam"])
    o_rwkv, s_rwkv, rwkv_tail, wu_bf = _rwkv_prompt(proj, nb, nt_len, pp["rwkv"], wts["w_up"], l)
    o_ssd, s_ssd, ssd_tail = _ssd_prompt(proj, nb, nt_len, pp["ssd_cw"], pp["ssd_cb"], pp["ssd_dtb"],
                                         pp["ssd_aneg"], pp["ssd_dsk"], pp["ssd_nw"])
    x1, h2, wd_bf = _outproj((o_gla, o_lru, o_rwkv, o_ssd), wo_bf, x2d, pp["n_mix_post"], pp["n_mlp_pre"], tm_out,
                             wts["w_down"], l)
    x2 = _mlp(h2, wu_bf, wd_bf, x1, pp["n_mlp_post"], tm_mlp, tf)
    keep = SUBLANES - (CONV_W - 1)
    states = (s_gla, h_lru[:, 0, :], lru_tail[:, keep:, :], rwkv_tail[:, SUBLANES - 1, :], s_rwkv,
              ssd_tail[:, keep:, :], s_ssd)
    return x2, states, (wu_bf, wd_bf, wo_bf)


def _layer_decode(x2d, big, small, wts, mlp_w, l, pp, prev, tf=1024):
    sg, sr_t, ss = big
    lh, lconv, rprev, sconv = small
    n = x2d.shape[0]
    proj = _proj(x2d, pp["n_mix_pre"], wts["w_in"], l, n)
    rows, colt, rwt, o_lru, lh_new, lconv_new, sconv_new = _dec_prep(
        proj, lconv.reshape(n, -1), lh, rprev, sconv.reshape(n, -1), pp)
    sr_new, yrt = _dec_rwkv(rwt, sr_t, l, None if prev is None else prev[1])
    sg_new, ss_new, yg, ys = _dec_state(colt, rows, sg, ss, l, None if prev is None else (prev[0], prev[2]))
    o_gla, o_rwkv, o_ssd = _dec_finish(rows, yg, yrt, ys, pp)
    x1, h2 = _outproj((o_gla, o_lru, o_rwkv, o_ssd), mlp_w[2], x2d, pp["n_mix_post"], pp["n_mlp_pre"], n)
    x2 = _mlp(h2, mlp_w[0], mlp_w[1], x1, pp["n_mlp_post"], n, tf)
    rshift_new = jnp.transpose(proj[T_RWKV:T_RWKV + GROUP_TILES], (1, 0, 2)).reshape(n, RWKV_COLS)
    small_new = (lh_new, lconv_new.reshape(n, CONV_W - 1, MIX), rshift_new,
                 sconv_new.reshape(n, CONV_W - 1, SSD_CONV_DIM))
    return x2, (sg_new, sr_new, ss_new), small_new


def kernel(x_prompt, x_sample, state_gla, state_lru, cache_lru_conv, cache_rwkv_shift, state_rwkv, cache_ssd_conv, state_ssd, norm_mix_pre, norm_mix_post, norm_mlp_pre, norm_mlp_post, w_in, w_out, w_up, w_down, gla_w_gate2, gla_b_gate, gla_norm, lru_conv_w, lru_conv_b, lru_w_a, lru_b_a, lru_w_i, lru_b_i, lru_lambda, rwkv_mu, rwkv_w0, rwkv_w2, rwkv_a0, rwkv_a2, rwkv_g2, rwkv_k_k, rwkv_k_a, rwkv_r_k, rwkv_ln_w, rwkv_ln_b, ssd_conv_w, ssd_conv_b, ssd_dt_bias, ssd_a_log, ssd_d, ssd_norm):
    nb, nt_len, _ = x_prompt.shape
    nd = x_sample.shape[0]
    yp = x_prompt.reshape(nb * nt_len, D_MODEL)
    ys = x_sample.reshape(nd, D_MODEL)
    wts = _prep_weights(w_in, w_out, w_up, w_down)
    big = (state_gla, jnp.transpose(state_rwkv, (0, 2, 3, 4, 1)), state_ssd)
    new_p, small_s, big_s = [], [], None
    for l in range(DEPTH):
        pp = _prep_layer(l, norm_mix_pre, norm_mix_post, norm_mlp_pre, norm_mlp_post,
                         gla_w_gate2, gla_b_gate, gla_norm, lru_conv_w, lru_conv_b, lru_w_a, lru_b_a, lru_w_i,
                         lru_b_i, lru_lambda, rwkv_mu, rwkv_w0, rwkv_w2, rwkv_a0, rwkv_a2, rwkv_g2, rwkv_k_k,
                         rwkv_k_a, rwkv_r_k, rwkv_ln_w, rwkv_ln_b, ssd_conv_w, ssd_conv_b, ssd_dt_bias, ssd_a_log,
                         ssd_d, ssd_norm)
        yp, st_p, mlp_w = _layer_prompt(yp, nb, nt_len, wts, l, pp)
        small = (state_lru[l], cache_lru_conv[l], cache_rwkv_shift[l], cache_ssd_conv[l])
        ys, big_s, sm = _layer_decode(ys, big, small, wts, mlp_w, l, pp, big_s)
        new_p.append(st_p)
        small_s.append(sm)

    def stack(sts, i):
        return jnp.stack([s[i] for s in sts], axis=0)

    gla_s, rwkv_s, ssd_s = big_s
    rwkv_s = jnp.transpose(rwkv_s, (0, 4, 1, 2, 3))
    return (yp.reshape(nb, nt_len, D_MODEL), ys.reshape(nd, 1, D_MODEL),
            stack(new_p, 0), gla_s, stack(new_p, 1), stack(small_s, 0),
            stack(new_p, 2), stack(small_s, 1), stack(new_p, 3), stack(small_s, 2),
            stack(new_p, 4), rwkv_s, stack(new_p, 5), stack(small_s, 3),
            stack(new_p, 6), ssd_s)
```

```python
import functools
import math

import jax
import jax.numpy as jnp
from jax import lax
from jax.experimental import pallas as pl
from jax.experimental.pallas import tpu as pltpu

F32 = jnp.float32
BF16 = jnp.bfloat16

D_MODEL = 2048
D_FF = 4 * D_MODEL
DEPTH = 2
EPS = 1e-6
MIX = D_MODEL // 4
CONV_W = 4
GLA_H, GLA_DK, GLA_DV, GLA_RANK, GLA_TEMP, GLA_C = 4, 64, 128, 16, 16.0, 16
LRU_C = 8.0
RWKV_H, RWKV_D = 8, 64
RWKV_DECAY = math.exp(-0.5)
RWKV_LN_EPS = 64e-5
RWKV_C = 32
SSD_H, SSD_P, SSD_G, SSD_N, SSD_C = 8, 64, 2, 128, 64
GLA_COLS = 2 * GLA_H * GLA_DK + 2 * MIX + GLA_RANK
LRU_COLS = 2 * MIX
RWKV_COLS = 3 * MIX + 64 + 64 + 128
SSD_CONV_DIM = MIX + 2 * SSD_G * SSD_N
SSD_COLS = MIX + SSD_CONV_DIM + SSD_H

LANES = 128
SUBLANES = 8
TILE = 2 * LANES
GROUP_TILES = 7
WIDE_TILES = 6
T_RWKV, T_LRU, T_MISC, T_GLA, T_SSD = 0, 7, 11, 12, 18
N_TILES = 24
MISC_DT = GLA_RANK
VMEM_LIMIT = 56 * 1024 * 1024

NN = (((1,), (0,)), ((), ()))
NT = (((1,), (1,)), ((), ()))
TN = (((0,), (0,)), ((), ()))


def _split(x, n):
    if x.dtype == BF16:
        return [x]
    parts, r = [], x
    for i in range(n):
        p = r.astype(BF16)
        parts.append(p)
        if i + 1 < n:
            r = r - p.astype(F32)
    return parts


def _mm(a, b, dn=NN, pa=1, pb=1):
    aa, bb = _split(a, pa), _split(b, pb)
    acc = None
    for i, x in enumerate(aa):
        for j, y in enumerate(bb):
            if i + j >= max(len(aa), len(bb)):
                continue
            t = lax.dot_general(x, y, dn, preferred_element_type=F32)
            acc = t if acc is None else acc + t
    return acc


def _iota(shape, dim):
    return lax.broadcasted_iota(jnp.int32, shape, dim)


def _roll0(x, s):
    n = x.shape[0]
    s = s % n
    return x if s == 0 else pltpu.roll(x, s, 0)


def _rms(x, w):
    ms = jnp.mean(x * x, axis=-1, keepdims=True)
    return x * lax.rsqrt(ms + EPS) * w


def _sigmoid(x):
    return jax.nn.sigmoid(x)


def _silu(x):
    return x * jax.nn.sigmoid(x)


def _softplus(x):
    return jnp.maximum(x, 0.0) + jnp.log1p(jnp.exp(-jnp.abs(x)))


def _log_sigmoid(x):
    return jnp.minimum(x, 0.0) - jnp.log1p(jnp.exp(-jnp.abs(x)))


def _gelu_tanh(x):
    c = math.sqrt(2.0 / math.pi)
    return x * (0.5 * (1.0 + jnp.tanh(c * (x + 0.044715 * (x * x * x)))))


def _neg_expm1(x):
    return -jnp.tanh(0.5 * x) * (jnp.exp(x) + 1.0)


def _chunk_cumsum(x, chunk, pos):
    d = 1
    while d < chunk:
        x = x + jnp.where(pos >= d, _roll0(x, d), 0.0)
        d *= 2
    return x


def _chunk_last_bcast(x, chunk, pos):
    n = x.shape[0]
    y = jnp.where(pos == chunk - 1, x, 0.0)
    d = 1
    while d < chunk:
        y = y + _roll0(y, n - d)
        d *= 2
    return y


def _shift_rows(x, carry, i, row8):
    xs = _roll0(x, i)
    cs = _roll0(carry, i)
    top = jnp.where(row8 < i, cs, xs[:SUBLANES])
    return jnp.concatenate([top, xs[SUBLANES:]], axis=0)


def _causal_conv(x, carry, w_ref, b_ref):
    row8 = _iota((SUBLANES, x.shape[1]), 0)
    y = b_ref[...] + x * w_ref[CONV_W - 1:CONV_W, :]
    for i in range(1, CONV_W):
        y = y + _shift_rows(x, carry, i, row8) * w_ref[CONV_W - 1 - i:CONV_W - i, :]
    return y


def _params(*sem):
    return pltpu.CompilerParams(dimension_semantics=sem, vmem_limit_bytes=VMEM_LIMIT)


def _const_spec(shape):
    nd = len(shape)
    return pl.BlockSpec(shape, lambda *_: (0,) * nd)


PROJ_TILES_PER_STEP = 6


MLP_CAST_BLOCKS = 32


def _proj_kernel(x_ref, nw_ref, w_ref, *rest):
    ncast = (len(rest) - 2) // 2
    o_ref, h_ref = rest[ncast], rest[-1]
    for src, dst in zip(rest[:ncast], rest[ncast + 1:-1]):
        dst[...] = src[...].astype(BF16)

    @pl.when(pl.program_id(1) == 0)
    def _():
        h_ref[...] = _rms(x_ref[...], nw_ref[...]).astype(BF16)

    res = lax.dot_general(h_ref[...], w_ref[...], NT, preferred_element_type=F32)
    for j in range(PROJ_TILES_PER_STEP):
        o_ref[j] = res[:, j * TILE:(j + 1) * TILE]


def _proj(x2d, nw, w, l, tm, cast_w=None):
    m = x2d.shape[0]
    tn = PROJ_TILES_PER_STEP * TILE
    nj = N_TILES // PROJ_TILES_PER_STEP
    in_specs = [pl.BlockSpec((tm, D_MODEL), lambda i, j: (i, 0)),
                _const_spec((1, D_MODEL)),
                pl.BlockSpec((None, tn, D_MODEL), lambda i, j: (l, j, 0))]
    out_specs = [pl.BlockSpec((PROJ_TILES_PER_STEP, tm, TILE), lambda i, j: (j, i, 0))]
    out_shape = [jax.ShapeDtypeStruct((N_TILES, m, TILE), F32)]
    args = [x2d, nw, w]
    if cast_w is not None:
        nblk = MLP_CAST_BLOCKS
        while nblk > (m // tm) * nj:
            nblk //= 2
        blk = lambda i, j: jnp.minimum(i * nj + j, nblk - 1)
        for cw in cast_w:
            rows, cols = cw.shape[1] // nblk, cw.shape[2]
            in_specs.append(pl.BlockSpec((None, rows, cols), lambda i, j: (l, blk(i, j), 0)))
            out_specs.append(pl.BlockSpec((rows, cols), lambda i, j: (blk(i, j), 0)))
            out_shape.append(jax.ShapeDtypeStruct(cw.shape[1:], BF16))
        args += list(cast_w)
    res = pl.pallas_call(
        _proj_kernel,
        grid=(m // tm, nj),
        in_specs=in_specs,
        out_specs=out_specs,
        out_shape=out_shape,
        scratch_shapes=[pltpu.VMEM((tm, D_MODEL), BF16)],
        compiler_params=_params("arbitrary", "arbitrary"),
        name="norm_proj",
    )(*args)
    return res[0] if cast_w is None else res


OUTPROJ_SUB_ROWS = 128

def _outproj_kernel(m0, m1, m2, m3, w_ref, x_ref, n1_ref, n2_ref, *rest):
    if len(rest) == 2:
        x1_ref, h2_ref = rest
    else:
        wd_ref, x1_ref, h2_ref, wd_out = rest
        wd_out[...] = wd_ref[...].astype(BF16)
    tm = x_ref.shape[0]
    sub = min(tm, OUTPROJ_SUB_ROWS)
    for r0 in range(0, tm, sub):
        rs = slice(r0, r0 + sub)
        mix = jnp.concatenate([m0[rs, :], m1[rs, :], m2[rs, :], m3[rs, :]], axis=1)
        acc = jnp.dot(mix, w_ref[...], preferred_element_type=F32)
        x1 = x_ref[rs, :] + _rms(acc, n1_ref[...])
        x1_ref[rs, :] = x1
        h2_ref[rs, :] = _rms(x1, n2_ref[...]).astype(BF16)


def _outproj(mix, w_out, x2d, n1, n2, tm, cast_w=None, l=0):
    m = x2d.shape[0]
    row = lambda i: (i, 0)
    in_specs = [pl.BlockSpec((tm, MIX), row)] * 4 + [
        _const_spec((D_MODEL, D_MODEL)), pl.BlockSpec((tm, D_MODEL), row),
        _const_spec((1, D_MODEL)), _const_spec((1, D_MODEL))]
    out_specs = [pl.BlockSpec((tm, D_MODEL), row), pl.BlockSpec((tm, D_MODEL), row)]
    out_shape = [jax.ShapeDtypeStruct((m, D_MODEL), F32), jax.ShapeDtypeStruct((m, D_MODEL), BF16)]
    args = [*mix, w_out, x2d, n1, n2]
    if cast_w is not None:
        nblk = m // tm
        rows, cols = cast_w.shape[1] // nblk, cast_w.shape[2]
        in_specs.append(pl.BlockSpec((None, rows, cols), lambda i: (l, i, 0)))
        out_specs.append(pl.BlockSpec((rows, cols), row))
        out_shape.append(jax.ShapeDtypeStruct(cast_w.shape[1:], BF16))
        args.append(cast_w)
    return pl.pallas_call(
        _outproj_kernel,
        grid=(m // tm,),
        in_specs=in_specs,
        out_specs=out_specs,
        out_shape=out_shape,
        compiler_params=_params("parallel"),
        name="out_proj",
    )(*args)


def _mlp_kernel(h_ref, wu_ref, wd_ref, x1_ref, nw_ref, o_ref, acc_ref):
    f = pl.program_id(1)

    @pl.when(f == 0)
    def _():
        acc_ref[...] = jnp.zeros_like(acc_ref)

    u = jnp.maximum(jnp.dot(h_ref[...], wu_ref[...], preferred_element_type=F32), 0.0)
    acc_ref[...] += jnp.dot((u * u).astype(BF16), wd_ref[...], preferred_element_type=F32)

    @pl.when(f == pl.num_programs(1) - 1)
    def _():
        o_ref[...] = x1_ref[...] + _rms(acc_ref[...], nw_ref[...])


def _mlp(h2, w_up, w_down, x1, nw, tm, tf):
    m = h2.shape[0]
    return pl.pallas_call(
        _mlp_kernel,
        grid=(m // tm, D_FF // tf),
        in_specs=[pl.BlockSpec((tm, D_MODEL), lambda i, f: (i, 0)),
                  pl.BlockSpec((D_MODEL, tf), lambda i, f: (0, f)),
                  pl.BlockSpec((tf, D_MODEL), lambda i, f: (f, 0)),
                  pl.BlockSpec((tm, D_MODEL), lambda i, f: (i, 0)),
                  _const_spec((1, D_MODEL))],
        out_specs=pl.BlockSpec((tm, D_MODEL), lambda i, f: (i, 0)),
        out_shape=jax.ShapeDtypeStruct((m, D_MODEL), F32),
        scratch_shapes=[pltpu.VMEM((tm, D_MODEL), F32)],
        compiler_params=_params("parallel", "arbitrary"),
        name="mlp",
    )(h2, w_up, w_down, x1, nw)


GLA_TB = 256


def _gla_gates(ad, wg2_ref, bg_ref):
    x = _mm(ad, wg2_ref[...], pa=2) + bg_ref[...]
    return _log_sigmoid(x) * (1.0 / GLA_TEMP)


def _gla_kernel(z_ref, misc_ref, wg2_ref, bg_ref, nw_ref, ones_ref, o_ref, s_ref, st_ref, y_ref):
    tb = GLA_TB
    t = pl.program_id(1)

    @pl.when(t == 0)
    def _():
        st_ref[...] = jnp.zeros_like(st_ref)

    q = z_ref[0] * (GLA_DK ** -0.5)
    k = z_ref[1]
    v = jnp.concatenate([z_ref[2], z_ref[3]], axis=1)
    gate = jnp.concatenate([z_ref[4], z_ref[5]], axis=1)
    g = _gla_gates(misc_ref[...], wg2_ref, bg_ref)
    pos = _iota((tb, TILE), 0) & (GLA_C - 1)
    cum = _chunk_cumsum(g, GLA_C, pos)

    y = None
    for j in range(GLA_C):
        if j == 0:
            term = q * k
            vj = v
        else:
            e = jnp.exp(jnp.where(pos >= j, cum - _roll0(cum, j), -jnp.inf))
            term = q * _roll0(k, j) * e
            vj = _roll0(v, j)
        sc = _mm(term, ones_ref[...])
        y = sc * vj if y is None else y + sc * vj
    y_ref[...] = y

    qh = q * jnp.exp(cum)
    last = _chunk_last_bcast(cum, GLA_C, pos)
    kt = k * jnp.exp(last - cum)
    dec = jnp.exp(last)
    kss = [slice(h * GLA_DK, (h + 1) * GLA_DK) for h in range(GLA_H)]
    vss = [slice(h * GLA_DV, (h + 1) * GLA_DV) for h in range(GLA_H)]
    nch = tb // GLA_C
    ds = [[_mm(v[c * GLA_C:(c + 1) * GLA_C, vss[h]], kt[c * GLA_C:(c + 1) * GLA_C, kss[h]], TN)
           for h in range(GLA_H)] for c in range(nch)]
    st = [st_ref[h] for h in range(GLA_H)]
    for c in range(nch):
        rows = slice(c * GLA_C, (c + 1) * GLA_C)
        for h in range(GLA_H):
            y_ref[rows, vss[h]] += _mm(qh[rows, kss[h]], st[h], NT)
            st[h] = st[h] * dec[c * GLA_C:c * GLA_C + 1, kss[h]] + ds[c][h]
    for h in range(GLA_H):
        st_ref[h] = st[h]

    yy = y_ref[...]
    outs = []
    for h in range(GLA_H):
        vs = slice(h * GLA_DV, (h + 1) * GLA_DV)
        outs.append(_rms(yy[:, vs], nw_ref[...]))
    o_ref[...] = (jnp.concatenate(outs, axis=1) * _silu(gate)).astype(BF16)

    @pl.when(t == pl.num_programs(1) - 1)
    def _():
        for h in range(GLA_H):
            s_ref[0, h] = st_ref[h].T


def _gla_prompt(proj, nb, nt_len, wg2, bg, nw, ones):
    tb = GLA_TB
    nt = nt_len // tb
    return pl.pallas_call(
        _gla_kernel,
        grid=(nb, nt),
        in_specs=[pl.BlockSpec((WIDE_TILES, tb, TILE), lambda b, t: (T_GLA // WIDE_TILES, b * nt + t, 0)),
                  pl.BlockSpec((None, tb, TILE), lambda b, t: (T_MISC, b * nt + t, 0)),
                  _const_spec((TILE, TILE)), _const_spec((1, TILE)), _const_spec((1, GLA_DV)),
                  _const_spec((TILE, MIX))],
        out_specs=[pl.BlockSpec((tb, MIX), lambda b, t: (b * nt + t, 0)),
                   pl.BlockSpec((1, GLA_H, GLA_DK, GLA_DV), lambda b, t: (b, 0, 0, 0))],
        out_shape=[jax.ShapeDtypeStruct((nb * nt_len, MIX), BF16),
                   jax.ShapeDtypeStruct((nb, GLA_H, GLA_DK, GLA_DV), F32)],
        scratch_shapes=[pltpu.VMEM((GLA_H, GLA_DV, GLA_DK), F32), pltpu.VMEM((tb, MIX), F32)],
        compiler_params=_params("parallel", "arbitrary"),
        name="gla_prompt",
    )(proj, proj, wg2, bg, nw, ones)


LRU_TB = 256


def _lru_gates(xc, wa_ref, ba_ref, wi_ref, bi_ref, lam_ref):
    ra, ri = [], []
    for j in range(2):
        xs = xc[:, j * TILE:(j + 1) * TILE]
        ra.append(_mm(xs, wa_ref[j], pa=2))
        ri.append(_mm(xs, wi_ref[j], pa=2))
    r = _sigmoid(jnp.concatenate(ra, axis=1) + ba_ref[...])
    i = _sigmoid(jnp.concatenate(ri, axis=1) + bi_ref[...])
    log_a = -LRU_C * r * _softplus(-lam_ref[...])
    a = jnp.exp(log_a)
    mult = jnp.sqrt(_neg_expm1(2.0 * log_a))
    return a, mult, i


def _lru_kernel(x0_ref, x1_ref, g0_ref, g1_ref, cw_ref, cb_ref, wa_ref, ba_ref, wi_ref, bi_ref, lam_ref,
                o_ref, h_ref, tail_ref, carry_ref, hc_ref):
    tb = LRU_TB
    t = pl.program_id(1)

    @pl.when(t == 0)
    def _():
        carry_ref[...] = jnp.zeros_like(carry_ref)
        hc_ref[...] = jnp.zeros_like(hc_ref)

    xb = jnp.concatenate([x0_ref[...], x1_ref[...]], axis=1)
    gb = jnp.concatenate([g0_ref[...], g1_ref[...]], axis=1)
    xc = _causal_conv(xb, carry_ref[...], cw_ref, cb_ref)
    carry_ref[...] = xb[tb - SUBLANES:]
    a, mult, i = _lru_gates(xc, wa_ref, ba_ref, wi_ref, bi_ref, lam_ref)
    row = _iota((tb, MIX), 0)
    mult = jnp.where((row == 0) & (t == 0), 1.0, mult)
    b = mult * i * xc
    pos8 = row & (SUBLANES - 1)

    def roll8(x, d):
        return pltpu.roll(x.reshape(tb // SUBLANES, SUBLANES, MIX), d, 1).reshape(tb, MIX)

    d = 1
    while d < SUBLANES:
        m = pos8 >= d
        b = jnp.where(m, a * roll8(b, d) + b, b)
        a = jnp.where(m, a * roll8(a, d), a)
        d *= 2
    hl = hc_ref[0:1, :]
    groups = []
    for gidx in range(tb // SUBLANES):
        rs = slice(gidx * SUBLANES, (gidx + 1) * SUBLANES)
        hg = b[rs] + a[rs] * hl
        groups.append(hg)
        hl = hg[SUBLANES - 1:SUBLANES, :]
    h = jnp.concatenate(groups, axis=0)
    hc_ref[...] = jnp.broadcast_to(hl, hc_ref.shape)
    o_ref[...] = (h * _gelu_tanh(gb)).astype(BF16)
    h_ref[0] = hl
    tail_ref[0] = xb[tb - SUBLANES:]


def _lru_prompt(proj, nb, nt_len, cw, cb, wa, ba, wi, bi, lam):
    tb = LRU_TB
    nt = nt_len // tb

    def tile(j):
        return pl.BlockSpec((None, tb, TILE), lambda b, t: (T_LRU + j, b * nt + t, 0))

    return pl.pallas_call(
        _lru_kernel,
        grid=(nb, nt),
        in_specs=[tile(0), tile(1), tile(2), tile(3),
                  _const_spec((CONV_W, MIX)), _const_spec((1, MIX)),
                  _const_spec((2, TILE, TILE)), _const_spec((1, MIX)),
                  _const_spec((2, TILE, TILE)), _const_spec((1, MIX)), _const_spec((1, MIX))],
        out_specs=[pl.BlockSpec((tb, MIX), lambda b, t: (b * nt + t, 0)),
                   pl.BlockSpec((1, 1, MIX), lambda b, t: (b, 0, 0)),
                   pl.BlockSpec((1, SUBLANES, MIX), lambda b, t: (b, 0, 0))],
        out_shape=[jax.ShapeDtypeStruct((nb * nt_len, MIX), BF16), jax.ShapeDtypeStruct((nb, 1, MIX), F32),
                   jax.ShapeDtypeStruct((nb, SUBLANES, MIX), F32)],
        scratch_shapes=[pltpu.VMEM((SUBLANES, MIX), F32), pltpu.VMEM((SUBLANES, MIX), F32)],
        compiler_params=_params("parallel", "arbitrary"),
        name="lru_prompt",
    )(proj, proj, proj, proj, cw, cb, wa, ba, wi, bi, lam)


RWKV_TB = 256
RWKV_BS = 128
RWKV_P = {"g": (1, 1), "neu": (1, 1), "app": (1, 1), "chunk": (1, 1)}


def _seg_sum(x, seg_ref):
    outs = []
    for j in range(x.shape[1] // LANES):
        outs.append(_mm(x[:, j * LANES:(j + 1) * LANES], seg_ref[...], pa=2))
    return jnp.concatenate(outs, axis=1)


def _rwkv_pointwise(z, zs, mu_ref, w0_ref, w2_ref, a0_ref, a2_ref, g2_ref, kk_ref, ka_ref, seg_ref):
    zm = z + (zs - z) * mu_ref[...]
    r = zm[:, 0:MIX]
    k = zm[:, MIX:2 * MIX]
    v = zm[:, 2 * MIX:3 * MIX]
    zw = zm[:, 3 * MIX:3 * MIX + 64]
    za = zm[:, 3 * MIX + 64:3 * MIX + 128]
    zg = zm[:, 3 * MIX + 128:3 * MIX + 256]
    lw = -RWKV_DECAY * _sigmoid(w0_ref[...] + _mm(jnp.tanh(zw), w2_ref[...], pa=2))
    a = _sigmoid(a0_ref[...] + _mm(za, a2_ref[...], pa=2))
    g = _mm(_sigmoid(zg), g2_ref[...], pa=2)
    kk = k * kk_ref[...]
    kk = kk / jnp.maximum(jnp.sqrt(_seg_sum(kk * kk, seg_ref)), 1e-12)
    k = k * (1.0 + (a - 1.0) * ka_ref[...])
    return r, lw, k, v, kk, a, g


def _rwkv_finish(y, r, k, v, g, rk_ref, lnw_ref, lnb_ref, seg_ref):
    mean = _seg_sum(y, seg_ref) * (1.0 / RWKV_D)
    yc = y - mean
    var = _seg_sum(yc * yc, seg_ref) * (1.0 / RWKV_D)
    yn = yc * lax.rsqrt(var + RWKV_LN_EPS) * lnw_ref[...] + lnb_ref[...]
    bonus = _seg_sum(r * k * rk_ref[...], seg_ref) * v
    return (yn + bonus) * g


def _rwkv_kernel(z_ref, mu_ref, w0_ref, w2_ref, a0_ref, a2_ref, g2_ref, kk_ref, ka_ref, rk_ref, lnw_ref, lnb_ref,
                 seg_ref, wsrc_ref, o_ref, s_ref, tail_ref, wdst_ref, prev_ref, st_ref, y_ref, scat_ref):
    wdst_ref[...] = wsrc_ref[...].astype(BF16)
    tb = RWKV_TB
    cc = RWKV_C
    t = pl.program_id(1)

    @pl.when(t == 0)
    def _():
        prev_ref[...] = jnp.zeros_like(prev_ref)
        st_ref[...] = jnp.zeros_like(st_ref)

    z = jnp.concatenate([z_ref[j] for j in range(GROUP_TILES)], axis=1)
    zs = _shift_rows(z, prev_ref[...], 1, _iota((SUBLANES, RWKV_COLS), 0))
    prev_ref[...] = z[tb - SUBLANES:]
    tail_ref[0] = z[tb - SUBLANES:]
    r, lw, k, v, kk, a, g = _rwkv_pointwise(z, zs, mu_ref, w0_ref, w2_ref, a0_ref, a2_ref, g2_ref, kk_ref, ka_ref,
                                            seg_ref)
    b = kk * a

    pos = _iota((tb, MIX), 0) & (cc - 1)
    lg = _chunk_cumsum(lw, cc, pos)
    lg_end = _chunk_last_bcast(lg, cc, pos)
    gam = jnp.exp(lg)
    inv = jnp.exp(-lg)
    rg = r * gam
    kg = kk * jnp.exp(lg - lw)
    bi = b * inv
    ki = k * inv
    e_end = jnp.exp(lg_end - lg)
    bt = b * e_end
    kt = k * e_end
    g_end = jnp.exp(lg_end)

    bs = RWKV_BS
    ri = _iota((bs, bs), 0)
    ci = _iota((bs, bs), 1)
    same = (ri - (ri & (cc - 1))) == (ci - (ci & (cc - 1)))
    strict = same & (ci < ri)
    incl = same & (ci <= ri)
    eye = (ri == ci).astype(F32)
    eye_d = _iota((RWKV_D, RWKV_D), 0) == _iota((RWKV_D, RWKV_D), 1)

    hs = [slice(h * RWKV_D, (h + 1) * RWKV_D) for h in range(RWKV_H)]
    mg, mn, ma, mc_ = RWKV_P["g"], RWKV_P["neu"], RWKV_P["app"], RWKV_P["chunk"]
    units = [(slice(q * bs, (q + 1) * bs), h) for q in range(tb // bs) for h in range(RWKV_H)]
    vh = [v[rb, hs[h]] for rb, h in units]
    gmat = [_mm(jnp.concatenate([kg[rb, hs[h]], rg[rb, hs[h]]], axis=0),
                jnp.concatenate([bi[rb, hs[h]], ki[rb, hs[h]]], axis=0), NT, *mg) for rb, h in units]
    amat = [jnp.where(strict, gm[:bs, :bs], 0.0) for gm in gmat]
    bmat = [jnp.where(strict, gm[:bs, bs:], 0.0) for gm in gmat]
    pb_ = [jnp.where(incl, gm[bs:, :bs], 0.0) for gm in gmat]
    pk_ = [jnp.where(incl, gm[bs:, bs:], 0.0) for gm in gmat]
    bv = [_mm(bm_, v_, NN, *ma) for bm_, v_ in zip(bmat, vh)]

    def lower_left(b):
        rb = ri & (b - 1)
        cb = ci & (b - 1)
        return ((ri - rb) == (ci - cb)) & (rb >= b // 2) & (cb < b // 2)

    x = [eye - jnp.where(lower_left(2), a_, 0.0) for a_ in amat]
    b = 4
    while b <= cc:
        m = lower_left(b)
        tx = [_mm(x_, jnp.where(m, a_, 0.0), NN, *mn) for x_, a_ in zip(x, amat)]
        x = [x_ - _mm(tx_, x_, NN, *mn) for x_, tx_ in zip(x, tx)]
        b *= 2
    wu = [-_mm(x_, jnp.concatenate([kg[rb, hs[h]], bv_], axis=1), NN, *ma)
          for x_, (rb, h), bv_ in zip(x, units, bv)]
    low = _iota((bs, LANES), 1) < RWKV_D
    zero = jnp.zeros((bs, LANES), F32)

    def dup(slab, odd):
        sw = pltpu.roll(slab, RWKV_D, 1)
        return jnp.where(low, sw, slab) if odd else jnp.where(low, slab, sw)

    def upper(slab, odd):
        return jnp.where(low, zero, slab if odd else pltpu.roll(slab, RWKV_D, 1))

    pair = [slice((h // 2) * LANES, (h // 2 + 1) * LANES) for h in range(RWKV_H)]
    rg2 = [dup(rg[rb, pair[h]], h % 2) for rb, h in units]
    ov = [upper(v[rb, pair[h]], h % 2) for rb, h in units]
    ww = [jnp.where(low, wu_, pltpu.roll(wu_, RWKV_D, 1)) for wu_ in wu]
    qy = [_mm(pb, jnp.concatenate([ww_, wu_], axis=1), NN, *ma) for pb, ww_, wu_ in zip(pb_, ww, wu)]
    pkv = [_mm(pk, v_, NN, *ma) for pk, v_ in zip(pk_, vh)]
    qt2 = [rg2_ + qy_[:, :LANES] for rg2_, qy_ in zip(rg2, qy)]
    y0 = [qy_[:, LANES + RWKV_D:] + pkv_ for qy_, pkv_ in zip(qy, pkv)]
    nch = tb // cc
    cpb = bs // cc
    s = [st_ref[h] for h in range(RWKV_H)]
    for c in range(nch):
        rows = slice(c * cc, (c + 1) * cc)
        lrows = slice((c % cpb) * cc, (c % cpb + 1) * cc)
        u0 = (c // cpb) * RWKV_H
        for h in range(RWKV_H):
            scat_ref[h, :, c * RWKV_D:(c + 1) * RWKV_D] = s[h]
        tr = [_mm(jnp.concatenate([wu[u0 + h][lrows], ov[u0 + h][lrows]], axis=0),
                  jnp.concatenate([bt[rows, hs[h]], kt[rows, hs[h]]], axis=0), TN, *mc_) for h in range(RWKV_H)]
        s = [_mm(s_, jnp.where(eye_d, g_end[c * cc:c * cc + 1, ls], 0.0) + t_[:RWKV_D], NN, *mc_) + t_[RWKV_D:]
             for s_, t_, ls in zip(s, tr, hs)]
    r5 = _iota((bs, cpb * RWKV_D), 0)
    c5 = _iota((bs, cpb * RWKV_D), 1)
    own = (r5 - (r5 & (cc - 1))) * (RWKV_D // cc) == c5 - (c5 & (RWKV_D - 1))
    for h in range(RWKV_H):
        st_ref[h] = s[h]
    for u, (rb, h) in enumerate(units):
        q = u // RWKV_H
        qexp = jnp.where(own, jnp.concatenate([qt2[u]] * (cpb * RWKV_D // LANES), axis=1), 0.0)
        y_ref[rb, hs[h]] = _mm(qexp, scat_ref[h, :, q * cpb * RWKV_D:(q + 1) * cpb * RWKV_D], NT, *mc_) + y0[u]

    o_ref[...] = _rwkv_finish(y_ref[...], r, k, v, g, rk_ref, lnw_ref, lnb_ref, seg_ref).astype(BF16)

    @pl.when(t == pl.num_programs(1) - 1)
    def _():
        s_ref[0] = st_ref[...]


def _rwkv_prompt(proj, nb, nt_len, p, cast_w, l):
    tb = RWKV_TB
    nt = nt_len // tb
    crow, ccol = cast_w.shape[1] // (nb * nt), cast_w.shape[2]
    return pl.pallas_call(
        _rwkv_kernel,
        grid=(nb, nt),
        in_specs=[pl.BlockSpec((GROUP_TILES, tb, TILE), lambda b, t: (T_RWKV // GROUP_TILES, b * nt + t, 0)),
                  _const_spec((1, RWKV_COLS)), _const_spec((1, MIX)), _const_spec((64, MIX)),
                  _const_spec((1, MIX)), _const_spec((64, MIX)), _const_spec((128, MIX)),
                  _const_spec((1, MIX)), _const_spec((1, MIX)), _const_spec((1, MIX)),
                  _const_spec((1, MIX)), _const_spec((1, MIX)), _const_spec((LANES, LANES)),
                  pl.BlockSpec((None, crow, ccol), lambda b, t: (l, b * nt + t, 0))],
        out_specs=[pl.BlockSpec((tb, MIX), lambda b, t: (b * nt + t, 0)),
                   pl.BlockSpec((1, RWKV_H, RWKV_D, RWKV_D), lambda b, t: (b, 0, 0, 0)),
                   pl.BlockSpec((1, SUBLANES, RWKV_COLS), lambda b, t: (b, 0, 0)),
                   pl.BlockSpec((crow, ccol), lambda b, t: (b * nt + t, 0))],
        out_shape=[jax.ShapeDtypeStruct((nb * nt_len, MIX), BF16),
                   jax.ShapeDtypeStruct((nb, RWKV_H, RWKV_D, RWKV_D), F32),
                   jax.ShapeDtypeStruct((nb, SUBLANES, RWKV_COLS), F32),
                   jax.ShapeDtypeStruct(cast_w.shape[1:], BF16)],
        scratch_shapes=[pltpu.VMEM((SUBLANES, RWKV_COLS), F32), pltpu.VMEM((RWKV_H, RWKV_D, RWKV_D), F32),
                        pltpu.VMEM((tb, MIX), F32), pltpu.VMEM((RWKV_H, RWKV_D, (tb // RWKV_C) * RWKV_D), F32)],
        compiler_params=_params("parallel", "arbitrary"),
        name="rwkv_prompt",
    )(proj, p["mu"], p["w0"], p["w2"], p["a0"], p["a2"], p["g2"], p["kk"], p["ka"], p["rk"], p["lnw"], p["lnb"],
      p["seg"], cast_w)


SSD_TB = 512


def _ssd_pointwise(xbc_c, dt_raw, dtb_ref, aneg_ref):
    xbc = _silu(xbc_c)
    dt = _softplus(dt_raw + dtb_ref[...])
    return xbc, dt, dt * aneg_ref[...]


def _expand_heads(misc, width):
    n = misc.shape[0]
    return jnp.concatenate([jnp.broadcast_to(misc[:, MISC_DT + h:MISC_DT + h + 1], (n, width))
                            for h in range(SSD_H)], axis=1)


def _ssd_kernel(z_ref, misc_ref, cw_ref, cb_ref, dtb_ref, aneg_ref, dsk_ref, nw_ref, o_ref, s_ref, tail_ref, carry_ref,
                st_ref, y_ref):
    tb = SSD_TB
    cc = SSD_C
    t = pl.program_id(1)

    @pl.when(t == 0)
    def _():
        carry_ref[...] = jnp.zeros_like(carry_ref)
        st_ref[...] = jnp.zeros_like(st_ref)

    gate = jnp.concatenate([z_ref[0], z_ref[1]], axis=1)
    xbc_raw = jnp.concatenate([z_ref[2], z_ref[3], z_ref[4], z_ref[5]], axis=1)
    xbc_c = _causal_conv(xbc_raw, carry_ref[...], cw_ref, cb_ref)
    carry_ref[...] = xbc_raw[tb - SUBLANES:]
    tail_ref[0] = xbc_raw[tb - SUBLANES:]
    xbc, dt, dta = _ssd_pointwise(xbc_c, misc_ref[...], dtb_ref, aneg_ref)
    xs = xbc[:, :MIX]
    bm = xbc[:, MIX:MIX + TILE]
    cm = xbc[:, MIX + TILE:]
    pos = _iota((tb, TILE), 0) & (cc - 1)
    cum = _chunk_cumsum(dta, cc, pos)
    xdt = xs * _expand_heads(dt, SSD_P)
    tri = _iota((cc, cc), 0) >= _iota((cc, cc), 1)

    hg = SSD_H // SSD_G
    nch = tb // cc
    rws = [slice(c * cc, (c + 1) * cc) for c in range(nch)]
    cum_cs = [cum[rows, :LANES] for rows in rws]
    cum_ts = [x.T for x in cum_cs]
    units = [(c, gi) for c in range(nch) for gi in range(SSD_G)]
    cgs = [cm[rws[c], gi * SSD_N:(gi + 1) * SSD_N] for c, gi in units]
    bgs = [bm[rws[c], gi * SSD_N:(gi + 1) * SSD_N] for c, gi in units]
    gmats = [_mm(cg, bg, NT) for cg, bg in zip(cgs, bgs)]
    heads = [(u, gi * hg + hh) for u, (c, gi) in enumerate(units) for hh in range(hg)]
    cols = [cum_cs[units[u][0]][:, MISC_DT + h:MISC_DT + h + 1] for u, h in heads]
    lasts = [cum_cs[units[u][0]][cc - 1:cc, MISC_DT + h:MISC_DT + h + 1] for u, h in heads]
    lmats = [jnp.exp(jnp.where(tri, col - cum_ts[units[u][0]][MISC_DT + h:MISC_DT + h + 1, :], -jnp.inf))
             for (u, h), col in zip(heads, cols)]
    xhs = [xdt[rws[units[u][0]], h * SSD_P:(h + 1) * SSD_P] for u, h in heads]
    yds = [_mm(gmats[u] * lm, xh) for (u, h), lm, xh in zip(heads, lmats, xhs)]
    xds = [xh * jnp.exp(last - col) for xh, last, col in zip(xhs, lasts, cols)]
    pre = []
    for u in range(len(units)):
        hsl = slice(u * hg, (u + 1) * hg)
        pre.append((cgs[u], jnp.concatenate(yds[hsl], axis=1),
                    jnp.concatenate([jnp.broadcast_to(jnp.exp(col), (cc, SSD_P)) for col in cols[hsl]], axis=1),
                    jnp.concatenate([jnp.broadcast_to(jnp.exp(last), (SSD_P, SSD_N)) for last in lasts[hsl]], axis=0),
                    _mm(jnp.concatenate(xds[hsl], axis=1), bgs[u], TN)))
    for c in range(nch):
        rows = slice(c * cc, (c + 1) * cc)
        for gi in range(SSD_G):
            gs = slice(gi * hg * SSD_P, (gi + 1) * hg * SSD_P)
            cg, yd, ec, dec, ds = pre[c * SSD_G + gi]
            st = st_ref[gs, :]
            y_ref[rows, gs] = yd + _mm(cg, st, NT) * ec
            st_ref[gs, :] = st * dec + ds

    y = (y_ref[...] + dsk_ref[...] * xs) * _silu(gate)
    o_ref[...] = _rms(y, nw_ref[...]).astype(BF16)

    @pl.when(t == pl.num_programs(1) - 1)
    def _():
        for h in range(SSD_H):
            s_ref[0, h] = st_ref[h * SSD_P:(h + 1) * SSD_P, :]


def _ssd_prompt(proj, nb, nt_len, cw, cb, dtb, aneg, dsk, nw):
    tb = SSD_TB
    nt = nt_len // tb
    return pl.pallas_call(
        _ssd_kernel,
        grid=(nb, nt),
        in_specs=[pl.BlockSpec((WIDE_TILES, tb, TILE), lambda b, t: (T_SSD // WIDE_TILES, b * nt + t, 0)),
                  pl.BlockSpec((None, tb, TILE), lambda b, t: (T_MISC, b * nt + t, 0)),
                  _const_spec((CONV_W, SSD_CONV_DIM)), _const_spec((1, SSD_CONV_DIM)),
                  _const_spec((1, TILE)), _const_spec((1, TILE)), _const_spec((1, MIX)), _const_spec((1, MIX))],
        out_specs=[pl.BlockSpec((tb, MIX), lambda b, t: (b * nt + t, 0)),
                   pl.BlockSpec((1, SSD_H, SSD_P, SSD_N), lambda b, t: (b, 0, 0, 0)),
                   pl.BlockSpec((1, SUBLANES, SSD_CONV_DIM), lambda b, t: (b, 0, 0))],
        out_shape=[jax.ShapeDtypeStruct((nb * nt_len, MIX), BF16),
                   jax.ShapeDtypeStruct((nb, SSD_H, SSD_P, SSD_N), F32),
                   jax.ShapeDtypeStruct((nb, SUBLANES, SSD_CONV_DIM), F32)],
        scratch_shapes=[pltpu.VMEM((SUBLANES, SSD_CONV_DIM), F32), pltpu.VMEM((SSD_H * SSD_P, SSD_N), F32),
                        pltpu.VMEM((tb, MIX), F32)],
        compiler_params=_params("parallel", "arbitrary"),
        name="ssd_prompt",
    )(proj, proj, cw, cb, dtb, aneg, dsk, nw)


R_GLA_V, R_GLA_GATE, R_GLA_Q = 0, 512, 1024
R_RW_K, R_RW_R, R_RW_V, R_RW_G = 1280, 1792, 2304, 2816
R_SSD_B, R_SSD_C, R_SSD_E, R_SSD_GATE, R_SSD_X = 3328, 3584, 3840, 4864, 5376
ROW_W = 5888
C_GLA_A, C_GLA_K, C_SSD_X = 0, 256, 512
COL_W = 1024
T_RW_W, T_RW_B, T_RW_K, T_RW_KK, T_RW_R, T_RW_V = 0, 512, 1024, 1536, 2048, 2560
RWT_W = 3072
DEC_RB = 8


def _dec_prep_kernel(z_ref, lconv_ref, lh_ref, rprev_ref, sconv_ref,
                     wg2_ref, bg_ref,
                     lcw_ref, lcb_ref, wa_ref, ba_ref, wi_ref, bi_ref, lam_ref,
                     mu_ref, w0_ref, w2_ref, a0_ref, a2_ref, g2_ref, kk_ref, ka_ref, seg_ref,
                     scw_ref, scb_ref, dtb_ref, aneg_ref,
                     rows_ref, colt_ref, rwt_ref, olru_ref, lh_out_ref, lconv_out_ref, sconv_out_ref):
    k = z_ref[T_GLA + 1]
    g = _gla_gates(z_ref[T_MISC], wg2_ref, bg_ref)
    rows_ref[:, R_GLA_V:R_GLA_V + TILE] = z_ref[T_GLA + 2]
    rows_ref[:, R_GLA_V + TILE:R_GLA_V + MIX] = z_ref[T_GLA + 3]
    rows_ref[:, R_GLA_GATE:R_GLA_GATE + TILE] = z_ref[T_GLA + 4]
    rows_ref[:, R_GLA_GATE + TILE:R_GLA_GATE + MIX] = z_ref[T_GLA + 5]
    rows_ref[:, R_GLA_Q:R_GLA_Q + TILE] = z_ref[T_GLA + 0] * (GLA_DK ** -0.5)
    cols = [jnp.exp(g), k]

    xb = jnp.concatenate([z_ref[T_LRU + 0], z_ref[T_LRU + 1]], axis=1)
    gb = jnp.concatenate([z_ref[T_LRU + 2], z_ref[T_LRU + 3]], axis=1)
    xc = lcb_ref[...] + xb * lcw_ref[CONV_W - 1:CONV_W, :]
    for i in range(CONV_W - 1):
        xc = xc + lconv_ref[:, i * MIX:(i + 1) * MIX] * lcw_ref[i:i + 1, :]
    a, mult, gi = _lru_gates(xc, wa_ref, ba_ref, wi_ref, bi_ref, lam_ref)
    h = a * lh_ref[...] + mult * gi * xc
    lh_out_ref[...] = h
    olru_ref[...] = (h * _gelu_tanh(gb)).astype(BF16)
    lconv_out_ref[:, 0:2 * MIX] = lconv_ref[:, MIX:3 * MIX]
    lconv_out_ref[:, 2 * MIX:3 * MIX] = xb

    z = jnp.concatenate([z_ref[T_RWKV + j] for j in range(GROUP_TILES)], axis=1)
    r, lw, kmod, v, kk, a7, g7 = _rwkv_pointwise(z, rprev_ref[...], mu_ref, w0_ref, w2_ref, a0_ref, a2_ref, g2_ref,
                                                 kk_ref, ka_ref, seg_ref)
    rows_ref[:, R_RW_K:R_RW_K + MIX] = kmod
    rows_ref[:, R_RW_R:R_RW_R + MIX] = r
    rows_ref[:, R_RW_V:R_RW_V + MIX] = v
    rows_ref[:, R_RW_G:R_RW_G + MIX] = g7
    off = 0
    for vec in (jnp.exp(lw), kk * a7, kmod, kk, r, v):
        for j in range(MIX // LANES):
            rwt_ref[off:off + LANES, :] = vec[:, j * LANES:(j + 1) * LANES].T
            off += LANES

    xbc_raw = jnp.concatenate([z_ref[T_SSD + 2], z_ref[T_SSD + 3], z_ref[T_SSD + 4], z_ref[T_SSD + 5]], axis=1)
    xbc_c = scb_ref[...] + xbc_raw * scw_ref[CONV_W - 1:CONV_W, :]
    for i in range(CONV_W - 1):
        xbc_c = xbc_c + sconv_ref[:, i * SSD_CONV_DIM:(i + 1) * SSD_CONV_DIM] * scw_ref[i:i + 1, :]
    xbc, dt, dta = _ssd_pointwise(xbc_c, z_ref[T_MISC], dtb_ref, aneg_ref)
    xs = xbc[:, :MIX]
    rows_ref[:, R_SSD_B:R_SSD_B + TILE] = xbc[:, MIX:MIX + TILE]
    rows_ref[:, R_SSD_C:R_SSD_C + TILE] = xbc[:, MIX + TILE:]
    rows_ref[:, R_SSD_E:R_SSD_E + SSD_H * LANES] = _expand_heads(jnp.exp(dta), LANES)
    rows_ref[:, R_SSD_GATE:R_SSD_GATE + TILE] = z_ref[T_SSD + 0]
    rows_ref[:, R_SSD_GATE + TILE:R_SSD_GATE + MIX] = z_ref[T_SSD + 1]
    rows_ref[:, R_SSD_X:R_SSD_X + MIX] = xs
    cols.append(xs * _expand_heads(dt, SSD_P))
    sconv_out_ref[:, 0:2 * SSD_CONV_DIM] = sconv_ref[:, SSD_CONV_DIM:3 * SSD_CONV_DIM]
    sconv_out_ref[:, 2 * SSD_CONV_DIM:3 * SSD_CONV_DIM] = xbc_raw

    off = 0
    for cvec in cols:
        for j in range(cvec.shape[1] // LANES):
            colt_ref[off:off + LANES, :] = cvec[:, j * LANES:(j + 1) * LANES].T
            off += LANES


def _dec_prep(zdec, lconv, lh, rprev, sconv, pp):
    n = zdec.shape[1]
    args = [zdec, lconv, lh, rprev, sconv,
            pp["gla_wg2"], pp["gla_bg"],
            pp["lru_cw"], pp["lru_cb"], pp["lru_wa"], pp["lru_ba"], pp["lru_wi"], pp["lru_bi"], pp["lru_lam"],
            pp["rwkv"]["mu"], pp["rwkv"]["w0"], pp["rwkv"]["w2"], pp["rwkv"]["a0"], pp["rwkv"]["a2"],
            pp["rwkv"]["g2"], pp["rwkv"]["kk"], pp["rwkv"]["ka"], pp["rwkv"]["seg"],
            pp["ssd_cw"], pp["ssd_cb"], pp["ssd_dtb"], pp["ssd_aneg"]]
    return pl.pallas_call(
        _dec_prep_kernel,
        out_shape=[jax.ShapeDtypeStruct((n, ROW_W), F32), jax.ShapeDtypeStruct((COL_W, n), F32),
                   jax.ShapeDtypeStruct((RWT_W, n), F32),
                   jax.ShapeDtypeStruct((n, MIX), BF16), jax.ShapeDtypeStruct((n, MIX), F32),
                   jax.ShapeDtypeStruct((n, 3 * MIX), F32), jax.ShapeDtypeStruct((n, 3 * SSD_CONV_DIM), F32)],
        compiler_params=pltpu.CompilerParams(vmem_limit_bytes=VMEM_LIMIT),
        name="decode_prep",
    )(*args)


def _dec_rwkv_kernel(w_ref, b_ref, k_ref, kk_ref, r_ref, v_ref, s_ref, *rest, layer, fill_others):
    s_out, y_ref = rest[-2:]
    if fill_others:
        for j in range(DEPTH):
            if j != layer:
                s_out[j] = jnp.zeros(s_out.shape[1:], F32)
        s_out = s_out.at[layer]
    w = w_ref[...]
    b = b_ref[...]
    k = k_ref[...]
    kk = kk_ref[...]
    r = r_ref[...]

    def body(vi, carry):
        s = s_ref[vi]
        sa = jnp.sum(s * kk, axis=0, keepdims=True)
        s = s * w - sa * b + v_ref[pl.ds(vi, 1), :] * k
        s_out[vi] = s
        y_ref[pl.ds(vi, 1), :] = jnp.sum(s * r, axis=0, keepdims=True)
        return carry

    lax.fori_loop(0, RWKV_D, body, 0, unroll=8)


def _dec_rwkv(rwt, sr_t, l, prev):
    n = rwt.shape[1]
    hb = MIX // RWKV_D

    def vec(off):
        return pl.BlockSpec((RWKV_D, n), lambda h: (off // RWKV_D + h, 0))

    sspec = pl.BlockSpec((None, None, RWKV_D, RWKV_D, n), lambda h: (l, h, 0, 0, 0))
    extra = [] if prev is None else [prev]
    first = prev is None
    ospec = pl.BlockSpec((DEPTH, None, RWKV_D, RWKV_D, n), lambda h: (0, h, 0, 0, 0)) if first else sspec
    return pl.pallas_call(
        functools.partial(_dec_rwkv_kernel, layer=l, fill_others=first),
        grid=(hb,),
        in_specs=[vec(T_RW_W), vec(T_RW_B), vec(T_RW_K), vec(T_RW_KK), vec(T_RW_R), vec(T_RW_V), sspec]
        + [pl.BlockSpec(memory_space=pl.ANY)] * len(extra),
        out_specs=[ospec, pl.BlockSpec((RWKV_D, n), lambda h: (h, 0))],
        out_shape=[jax.ShapeDtypeStruct(sr_t.shape, F32), jax.ShapeDtypeStruct((MIX, n), F32)],
        input_output_aliases={7: 0} if extra else {},
        compiler_params=_params("parallel"),
        name="decode_rwkv",
    )(rwt, rwt, rwt, rwt, rwt, rwt, sr_t, *extra)


def _dec_state_kernel(colt_ref, rows_ref, sg_ref, ss_ref, *rest, layer, fill_others):
    sg_out, ss_out, yg_ref, ys_ref = rest[-4:]
    if fill_others:
        for j in range(DEPTH):
            if j != layer:
                sg_out[j] = jnp.zeros(sg_out.shape[1:], F32)
                ss_out[j] = jnp.zeros(ss_out.shape[1:], F32)
        sg_out = sg_out.at[layer]
        ss_out = ss_out.at[layer]
    pid = pl.program_id(0)
    n = colt_ref.shape[1]
    lane = _iota((1, n), 1)
    rid = _iota((DEC_RB, LANES), 0)
    rows = rows_ref[...]
    hg = SSD_H // SSD_G
    yg = [jnp.zeros((DEC_RB, GLA_DV), F32) for _ in range(GLA_H)]
    ys = [jnp.zeros((DEC_RB, SSD_P), F32) for _ in range(SSD_H)]
    for i in range(DEC_RB):
        sel = lane == pid * DEC_RB + i
        col = jnp.sum(jnp.where(sel, colt_ref[...], 0.0), axis=1, keepdims=True)
        rv = rows[i:i + 1, :]
        mine = rid == i
        for h in range(GLA_H):
            al = col[C_GLA_A + h * GLA_DK:C_GLA_A + (h + 1) * GLA_DK]
            kc = col[C_GLA_K + h * GLA_DK:C_GLA_K + (h + 1) * GLA_DK]
            vr = rv[:, R_GLA_V + h * GLA_DV:R_GLA_V + (h + 1) * GLA_DV]
            s = al * sg_ref[i, h] + kc * vr
            sg_out[i, h] = s
            q8 = rows[:, R_GLA_Q + h * GLA_DK:R_GLA_Q + (h + 1) * GLA_DK]
            yg[h] = jnp.where(mine, _mm(q8, s, NN, pa=2, pb=2), yg[h])
        for h in range(SSD_H):
            gi = h // hg
            e = rv[:, R_SSD_E + h * LANES:R_SSD_E + (h + 1) * LANES]
            bn = rv[:, R_SSD_B + gi * SSD_N:R_SSD_B + (gi + 1) * SSD_N]
            xc = col[C_SSD_X + h * SSD_P:C_SSD_X + (h + 1) * SSD_P]
            s = ss_ref[i, h] * e + xc * bn
            ss_out[i, h] = s
            c8 = rows[:, R_SSD_C + gi * SSD_N:R_SSD_C + (gi + 1) * SSD_N]
            ys[h] = jnp.where(mine[:, :SSD_P], _mm(c8, s, NT, pa=2, pb=2), ys[h])
    yg_ref[...] = jnp.concatenate(yg, axis=1)
    ys_ref[...] = jnp.concatenate(ys, axis=1)


def _dec_state(colt, rows, sg, ss, l, prev):
    n = rows.shape[0]
    blk = lambda i: (l, i, 0, 0, 0)
    specs = [pl.BlockSpec((None, DEC_RB, GLA_H, GLA_DK, GLA_DV), blk),
             pl.BlockSpec((None, DEC_RB, SSD_H, SSD_P, SSD_N), blk)]
    extra = [] if prev is None else list(prev)
    first = prev is None
    ospecs = specs
    if first:
        all_layers = lambda i: (0, i, 0, 0, 0)
        ospecs = [pl.BlockSpec((DEPTH, DEC_RB, GLA_H, GLA_DK, GLA_DV), all_layers),
                  pl.BlockSpec((DEPTH, DEC_RB, SSD_H, SSD_P, SSD_N), all_layers)]
    rowspec = pl.BlockSpec((DEC_RB, MIX), lambda i: (i, 0))
    return pl.pallas_call(
        functools.partial(_dec_state_kernel, layer=l, fill_others=first),
        grid=(n // DEC_RB,),
        in_specs=[_const_spec((COL_W, n)), pl.BlockSpec((DEC_RB, ROW_W), lambda i: (i, 0))] + specs
        + [pl.BlockSpec(memory_space=pl.ANY)] * len(extra),
        out_specs=ospecs + [rowspec, rowspec],
        out_shape=[jax.ShapeDtypeStruct(sg.shape, F32), jax.ShapeDtypeStruct(ss.shape, F32),
                   jax.ShapeDtypeStruct((n, MIX), F32), jax.ShapeDtypeStruct((n, MIX), F32)],
        input_output_aliases={4 + j: j for j in range(len(extra))},
        compiler_params=_params("parallel"),
        name="decode_state",
    )(colt, rows, sg, ss, *extra)


def _dec_finish_kernel(rows_ref, yg_ref, yrt_ref, ys_ref, gnw_ref, rk_ref, lnw_ref, lnb_ref, seg_ref,
                       dsk_ref, snw_ref, og_ref, or_ref, os_ref):
    outs = []
    for h in range(GLA_H):
        outs.append(_rms(yg_ref[:, h * GLA_DV:(h + 1) * GLA_DV], gnw_ref[...]))
    og_ref[...] = (jnp.concatenate(outs, axis=1) * _silu(rows_ref[:, R_GLA_GATE:R_GLA_GATE + MIX])).astype(BF16)
    yr = jnp.concatenate([yrt_ref[j * LANES:(j + 1) * LANES, :].T for j in range(MIX // LANES)], axis=1)
    or_ref[...] = _rwkv_finish(yr, rows_ref[:, R_RW_R:R_RW_R + MIX], rows_ref[:, R_RW_K:R_RW_K + MIX],
                               rows_ref[:, R_RW_V:R_RW_V + MIX], rows_ref[:, R_RW_G:R_RW_G + MIX],
                               rk_ref, lnw_ref, lnb_ref, seg_ref).astype(BF16)
    y = ((ys_ref[...] + dsk_ref[...] * rows_ref[:, R_SSD_X:R_SSD_X + MIX])
         * _silu(rows_ref[:, R_SSD_GATE:R_SSD_GATE + MIX]))
    os_ref[...] = _rms(y, snw_ref[...]).astype(BF16)


def _dec_finish(rows, yg, yrt, ys, pp):
    n = rows.shape[0]
    o = jax.ShapeDtypeStruct((n, MIX), BF16)
    return pl.pallas_call(
        _dec_finish_kernel,
        out_shape=[o, o, o],
        compiler_params=pltpu.CompilerParams(vmem_limit_bytes=VMEM_LIMIT),
        name="decode_finish",
    )(rows, yg, yrt, ys, pp["gla_nw"], pp["rwkv"]["rk"], pp["rwkv"]["lnw"], pp["rwkv"]["lnb"], pp["rwkv"]["seg"],
      pp["ssd_dsk"], pp["ssd_nw"])


def _block_diag(w8):
    nb = TILE // 64
    tiled = jnp.tile(w8.reshape(2, TILE, 64), (1, 1, nb))
    idx = jnp.arange(TILE) // 64
    return jnp.where(idx[:, None] == idx[None, :], tiled, 0.0)


def _prep_weights(w_in, w_out, w_up, w_down):
    o_lru = GLA_COLS
    o_rwkv = GLA_COLS + LRU_COLS
    o_ssd = o_rwkv + RWKV_COLS
    wide = WIDE_TILES * TILE
    wt = jnp.transpose(w_in, (0, 2, 1))
    zpad = lambda n: jnp.zeros((DEPTH, n, D_MODEL), w_in.dtype)
    w_perm = jnp.concatenate([
        wt[:, o_rwkv:o_rwkv + RWKV_COLS],
        wt[:, o_lru:o_lru + LRU_COLS],
        wt[:, wide:GLA_COLS], wt[:, o_ssd + wide:o_ssd + SSD_COLS], zpad(TILE - GLA_RANK - SSD_H),
        wt[:, 0:wide],
        wt[:, o_ssd:o_ssd + wide]], axis=1).astype(BF16)
    return {"w_in": w_perm,
            "w_out": w_out,
            "w_up": w_up, "w_down": w_down}


def _prep_layer(l, norm_mix_pre, norm_mix_post, norm_mlp_pre, norm_mlp_post,
                gla_w_gate2, gla_b_gate, gla_norm, lru_conv_w, lru_conv_b, lru_w_a, lru_b_a, lru_w_i, lru_b_i,
                lru_lambda, rwkv_mu, rwkv_w0, rwkv_w2, rwkv_a0, rwkv_a2, rwkv_g2, rwkv_k_k, rwkv_k_a, rwkv_r_k,
                rwkv_ln_w, rwkv_ln_b, ssd_conv_w, ssd_conv_b, ssd_dt_bias, ssd_a_log, ssd_d, ssd_norm):
    row = lambda a: a.reshape(1, -1).astype(F32)
    ii = jnp.arange(TILE)[:, None]
    jj = jnp.arange(MIX)[None, :]
    seg_i = jnp.arange(LANES)
    pp = {
        "n_mix_pre": row(norm_mix_pre[l]), "n_mix_post": row(norm_mix_post[l]),
        "n_mlp_pre": row(norm_mlp_pre[l]), "n_mlp_post": row(norm_mlp_post[l]),
        "gla_wg2": jnp.pad(gla_w_gate2[l], ((0, TILE - GLA_RANK), (0, 0))).astype(BF16),
        "gla_bg": row(gla_b_gate[l]),
        "gla_nw": row(gla_norm[l]),
        "gla_ones": (ii // GLA_DK == jj // GLA_DV).astype(BF16),
        "lru_cw": lru_conv_w[l].astype(F32), "lru_cb": row(lru_conv_b[l]),
        "lru_wa": _block_diag(lru_w_a[l]).astype(BF16),
        "lru_wi": _block_diag(lru_w_i[l]).astype(BF16),
        "lru_ba": row(lru_b_a[l]), "lru_bi": row(lru_b_i[l]), "lru_lam": row(lru_lambda[l]),
        "rwkv": {
            "mu": row(rwkv_mu[l]), "w0": row(rwkv_w0[l]), "w2": rwkv_w2[l].astype(BF16),
            "a0": row(rwkv_a0[l]), "a2": rwkv_a2[l].astype(BF16), "g2": rwkv_g2[l].astype(BF16),
            "kk": row(rwkv_k_k[l]), "ka": row(rwkv_k_a[l]), "rk": row(rwkv_r_k[l]),
            "lnw": row(rwkv_ln_w[l]), "lnb": row(rwkv_ln_b[l]),
            "seg": (seg_i[:, None] // RWKV_D == seg_i[None, :] // RWKV_D).astype(BF16),
        },
        "ssd_cw": ssd_conv_w[l].astype(F32), "ssd_cb": row(ssd_conv_b[l]),
        "ssd_dtb": jnp.pad(ssd_dt_bias[l].astype(F32), (MISC_DT, TILE - MISC_DT - SSD_H)).reshape(1, TILE),
        "ssd_aneg": jnp.pad(-jnp.exp(ssd_a_log[l].astype(F32)), (MISC_DT, TILE - MISC_DT - SSD_H)).reshape(1, TILE),
        "ssd_dsk": jnp.repeat(ssd_d[l].astype(F32), SSD_P).reshape(1, MIX),
        "ssd_nw": row(ssd_norm[l]),
    }
    return pp


def _layer_prompt(x2d, nb, nt_len, wts, l, pp, tm_proj=1024, tm_out=512, tm_mlp=512, tf=1024):
    proj, wo_bf = _proj(x2d, pp["n_mix_pre"], wts["w_in"], l, tm_proj, (wts["w_out"],))
    o_gla, s_gla = _gla_prompt(proj, nb, nt_len, pp["gla_wg2"], pp["gla_bg"], pp["gla_nw"], pp["gla_ones"])
    o_lru, h_lru, lru_tail = _lru_prompt(proj, nb, nt_len, pp["lru_cw"], pp["lru_cb"], pp["lru_wa"], pp["lru_ba"],
                                         pp["lru_wi"], pp["lru_bi"], pp["lru_lam"])
    o_rwkv, s_rwkv, rwkv_tail, wu_bf = _rwkv_prompt(proj, nb, nt_len, pp["rwkv"], wts["w_up"], l)
    o_ssd, s_ssd, ssd_tail = _ssd_prompt(proj, nb, nt_len, pp["ssd_cw"], pp["ssd_cb"], pp["ssd_dtb"],
                                         pp["ssd_aneg"], pp["ssd_dsk"], pp["ssd_nw"])
    x1, h2, wd_bf = _outproj((o_gla, o_lru, o_rwkv, o_ssd), wo_bf, x2d, pp["n_mix_post"], pp["n_mlp_pre"], tm_out,
                             wts["w_down"], l)
    x2 = _mlp(h2, wu_bf, wd_bf, x1, pp["n_mlp_post"], tm_mlp, tf)
    keep = SUBLANES - (CONV_W - 1)
    states = (s_gla, h_lru[:, 0, :], lru_tail[:, keep:, :], rwkv_tail[:, SUBLANES - 1, :], s_rwkv,
              ssd_tail[:, keep:, :], s_ssd)
    return x2, states, (wu_bf, wd_bf, wo_bf)


def _layer_decode(x2d, big, small, wts, mlp_w, l, pp, prev, tf=1024):
    sg, sr_t, ss = big
    lh, lconv, rprev, sconv = small
    n = x2d.shape[0]
    proj = _proj(x2d, pp["n_mix_pre"], wts["w_in"], l, n)
    rows, colt, rwt, o_lru, lh_new, lconv_new, sconv_new = _dec_prep(
        proj, lconv.reshape(n, -1), lh, rprev, sconv.reshape(n, -1), pp)
    sr_new, yrt = _dec_rwkv(rwt, sr_t, l, None if prev is None else prev[1])
    sg_new, ss_new, yg, ys = _dec_state(colt, rows, sg, ss, l, None if prev is None else (prev[0], prev[2]))
    o_gla, o_rwkv, o_ssd = _dec_finish(rows, yg, yrt, ys, pp)
    x1, h2 = _outproj((o_gla, o_lru, o_rwkv, o_ssd), mlp_w[2], x2d, pp["n_mix_post"], pp["n_mlp_pre"], n)
    x2 = _mlp(h2, mlp_w[0], mlp_w[1], x1, pp["n_mlp_post"], n, tf)
    rshift_new = jnp.transpose(proj[T_RWKV:T_RWKV + GROUP_TILES], (1, 0, 2)).reshape(n, RWKV_COLS)
    small_new = (lh_new, lconv_new.reshape(n, CONV_W - 1, MIX), rshift_new,
                 sconv_new.reshape(n, CONV_W - 1, SSD_CONV_DIM))
    return x2, (sg_new, sr_new, ss_new), small_new


def kernel(x_prompt, x_sample, state_gla, state_lru, cache_lru_conv, cache_rwkv_shift, state_rwkv, cache_ssd_conv, state_ssd, norm_mix_pre, norm_mix_post, norm_mlp_pre, norm_mlp_post, w_in, w_out, w_up, w_down, gla_w_gate2, gla_b_gate, gla_norm, lru_conv_w, lru_conv_b, lru_w_a, lru_b_a, lru_w_i, lru_b_i, lru_lambda, rwkv_mu, rwkv_w0, rwkv_w2, rwkv_a0, rwkv_a2, rwkv_g2, rwkv_k_k, rwkv_k_a, rwkv_r_k, rwkv_ln_w, rwkv_ln_b, ssd_conv_w, ssd_conv_b, ssd_dt_bias, ssd_a_log, ssd_d, ssd_norm):
    nb, nt_len, _ = x_prompt.shape
    nd = x_sample.shape[0]
    yp = x_prompt.reshape(nb * nt_len, D_MODEL)
    ys = x_sample.reshape(nd, D_MODEL)
    wts = _prep_weights(w_in, w_out, w_up, w_down)
    big = (state_gla, jnp.transpose(state_rwkv, (0, 2, 3, 4, 1)), state_ssd)
    new_p, small_s, big_s = [], [], None
    for l in range(DEPTH):
        pp = _prep_layer(l, norm_mix_pre, norm_mix_post, norm_mlp_pre, norm_mlp_post,
                         gla_w_gate2, gla_b_gate, gla_norm, lru_conv_w, lru_conv_b, lru_w_a, lru_b_a, lru_w_i,
                         lru_b_i, lru_lambda, rwkv_mu, rwkv_w0, rwkv_w2, rwkv_a0, rwkv_a2, rwkv_g2, rwkv_k_k,
                         rwkv_k_a, rwkv_r_k, rwkv_ln_w, rwkv_ln_b, ssd_conv_w, ssd_conv_b, ssd_dt_bias, ssd_a_log,
                         ssd_d, ssd_norm)
        yp, st_p, mlp_w = _layer_prompt(yp, nb, nt_len, wts, l, pp)
        small = (state_lru[l], cache_lru_conv[l], cache_rwkv_shift[l], cache_ssd_conv[l])
        ys, big_s, sm = _layer_decode(ys, big, small, wts, mlp_w, l, pp, big_s)
        new_p.append(st_p)
        small_s.append(sm)

    def stack(sts, i):
        return jnp.stack([s[i] for s in sts], axis=0)

    gla_s, rwkv_s, ssd_s = big_s
    rwkv_s = jnp.transpose(rwkv_s, (0, 4, 1, 2, 3))
    return (yp.reshape(nb, nt_len, D_MODEL), ys.reshape(nd, 1, D_MODEL),
            stack(new_p, 0), gla_s, stack(new_p, 1), stack(small_s, 0),
            stack(new_p, 2), stack(small_s, 1), stack(new_p, 3), stack(small_s, 2),
            stack(new_p, 4), rwkv_s, stack(new_p, 5), stack(small_s, 3),
            stack(new_p, 6), ssd_s)
```

```python
import functools
import math

import jax
import jax.numpy as jnp
from jax import lax
from jax.experimental import pallas as pl
from jax.experimental.pallas import tpu as pltpu

F32 = jnp.float32
BF16 = jnp.bfloat16

D_MODEL = 2048
D_FF = 4 * D_MODEL
DEPTH = 2
EPS = 1e-6
MIX = D_MODEL // 4
CONV_W = 4
GLA_H, GLA_DK, GLA_DV, GLA_RANK, GLA_TEMP, GLA_C = 4, 64, 128, 16, 16.0, 16
LRU_C = 8.0
RWKV_H, RWKV_D = 8, 64
RWKV_DECAY = math.exp(-0.5)
RWKV_LN_EPS = 64e-5
RWKV_C = 32
SSD_H, SSD_P, SSD_G, SSD_N, SSD_C = 8, 64, 2, 128, 64
GLA_COLS = 2 * GLA_H * GLA_DK + 2 * MIX + GLA_RANK
LRU_COLS = 2 * MIX
RWKV_COLS = 3 * MIX + 64 + 64 + 128
SSD_CONV_DIM = MIX + 2 * SSD_G * SSD_N
SSD_COLS = MIX + SSD_CONV_DIM + SSD_H

LANES = 128
SUBLANES = 8
TILE = 2 * LANES
GROUP_TILES = 7
WIDE_TILES = 6
T_RWKV, T_LRU, T_MISC, T_GLA, T_SSD = 0, 7, 11, 12, 18
N_TILES = 24
MISC_DT = GLA_RANK
VMEM_LIMIT = 56 * 1024 * 1024

NN = (((1,), (0,)), ((), ()))
NT = (((1,), (1,)), ((), ()))
TN = (((0,), (0,)), ((), ()))


def _split(x, n):
    if x.dtype == BF16:
        return [x]
    parts, r = [], x
    for i in range(n):
        p = r.astype(BF16)
        parts.append(p)
        if i + 1 < n:
            r = r - p.astype(F32)
    return parts


def _mm(a, b, dn=NN, pa=1, pb=1):
    aa, bb = _split(a, pa), _split(b, pb)
    acc = None
    for i, x in enumerate(aa):
        for j, y in enumerate(bb):
            if i + j >= max(len(aa), len(bb)):
                continue
            t = lax.dot_general(x, y, dn, preferred_element_type=F32)
            acc = t if acc is None else acc + t
    return acc


def _iota(shape, dim):
    return lax.broadcasted_iota(jnp.int32, shape, dim)


def _roll0(x, s):
    n = x.shape[0]
    s = s % n
    return x if s == 0 else pltpu.roll(x, s, 0)


def _rms(x, w):
    ms = jnp.mean(x * x, axis=-1, keepdims=True)
    return x * lax.rsqrt(ms + EPS) * w


def _sigmoid(x):
    return jax.nn.sigmoid(x)


def _silu(x):
    return x * jax.nn.sigmoid(x)


def _softplus(x):
    return jnp.maximum(x, 0.0) + jnp.log1p(jnp.exp(-jnp.abs(x)))


def _log_sigmoid(x):
    return jnp.minimum(x, 0.0) - jnp.log1p(jnp.exp(-jnp.abs(x)))


def _gelu_tanh(x):
    c = math.sqrt(2.0 / math.pi)
    return x * (0.5 * (1.0 + jnp.tanh(c * (x + 0.044715 * (x * x * x)))))


def _neg_expm1(x):
    return -jnp.tanh(0.5 * x) * (jnp.exp(x) + 1.0)


def _chunk_cumsum(x, chunk, pos):
    d = 1
    while d < chunk:
        x = x + jnp.where(pos >= d, _roll0(x, d), 0.0)
        d *= 2
    return x


def _chunk_last_bcast(x, chunk, pos):
    n = x.shape[0]
    y = jnp.where(pos == chunk - 1, x, 0.0)
    d = 1
    while d < chunk:
        y = y + _roll0(y, n - d)
        d *= 2
    return y


def _shift_rows(x, carry, i, row8):
    xs = _roll0(x, i)
    cs = _roll0(carry, i)
    top = jnp.where(row8 < i, cs, xs[:SUBLANES])
    return jnp.concatenate([top, xs[SUBLANES:]], axis=0)


def _causal_conv(x, carry, w_ref, b_ref):
    row8 = _iota((SUBLANES, x.shape[1]), 0)
    y = b_ref[...] + x * w_ref[CONV_W - 1:CONV_W, :]
    for i in range(1, CONV_W):
        y = y + _shift_rows(x, carry, i, row8) * w_ref[CONV_W - 1 - i:CONV_W - i, :]
    return y


def _params(*sem):
    return pltpu.CompilerParams(dimension_semantics=sem, vmem_limit_bytes=VMEM_LIMIT)


def _const_spec(shape):
    nd = len(shape)
    return pl.BlockSpec(shape, lambda *_: (0,) * nd)


PROJ_TILES_PER_STEP = 6


MLP_CAST_BLOCKS = 32


def _proj_kernel(x_ref, nw_ref, w_ref, *rest):
    ncast = (len(rest) - 2) // 2
    o_ref, h_ref = rest[ncast], rest[-1]
    for src, dst in zip(rest[:ncast], rest[ncast + 1:-1]):
        dst[...] = src[...].astype(BF16)

    @pl.when(pl.program_id(1) == 0)
    def _():
        h_ref[...] = _rms(x_ref[...], nw_ref[...]).astype(BF16)

    res = lax.dot_general(h_ref[...], w_ref[...], NT, preferred_element_type=F32)
    for j in range(PROJ_TILES_PER_STEP):
        o_ref[j] = res[:, j * TILE:(j + 1) * TILE]


def _proj(x2d, nw, w, l, tm, cast_w=None):
    m = x2d.shape[0]
    tn = PROJ_TILES_PER_STEP * TILE
    nj = N_TILES // PROJ_TILES_PER_STEP
    in_specs = [pl.BlockSpec((tm, D_MODEL), lambda i, j: (i, 0)),
                _const_spec((1, D_MODEL)),
                pl.BlockSpec((None, tn, D_MODEL), lambda i, j: (l, j, 0))]
    out_specs = [pl.BlockSpec((PROJ_TILES_PER_STEP, tm, TILE), lambda i, j: (j, i, 0))]
    out_shape = [jax.ShapeDtypeStruct((N_TILES, m, TILE), F32)]
    args = [x2d, nw, w]
    if cast_w is not None:
        nblk = MLP_CAST_BLOCKS
        while nblk > (m // tm) * nj:
            nblk //= 2
        blk = lambda i, j: jnp.minimum(i * nj + j, nblk - 1)
        for cw in cast_w:
            rows, cols = cw.shape[1] // nblk, cw.shape[2]
            in_specs.append(pl.BlockSpec((None, rows, cols), lambda i, j: (l, blk(i, j), 0)))
            out_specs.append(pl.BlockSpec((rows, cols), lambda i, j: (blk(i, j), 0)))
            out_shape.append(jax.ShapeDtypeStruct(cw.shape[1:], BF16))
        args += list(cast_w)
    res = pl.pallas_call(
        _proj_kernel,
        grid=(m // tm, nj),
        in_specs=in_specs,
        out_specs=out_specs,
        out_shape=out_shape,
        scratch_shapes=[pltpu.VMEM((tm, D_MODEL), BF16)],
        compiler_params=_params("arbitrary", "arbitrary"),
        name="norm_proj",
    )(*args)
    return res[0] if cast_w is None else res


OUTPROJ_SUB_ROWS = 128

def _outproj_kernel(m0, m1, m2, m3, w_ref, x_ref, n1_ref, n2_ref, *rest):
    if len(rest) == 2:
        x1_ref, h2_ref = rest
    else:
        wd_ref, x1_ref, h2_ref, wd_out = rest
        wd_out[...] = wd_ref[...].astype(BF16)
    tm = x_ref.shape[0]
    sub = min(tm, OUTPROJ_SUB_ROWS)
    for r0 in range(0, tm, sub):
        rs = slice(r0, r0 + sub)
        mix = jnp.concatenate([m0[rs, :], m1[rs, :], m2[rs, :], m3[rs, :]], axis=1)
        acc = jnp.dot(mix, w_ref[...], preferred_element_type=F32)
        x1 = x_ref[rs, :] + _rms(acc, n1_ref[...])
        x1_ref[rs, :] = x1
        h2_ref[rs, :] = _rms(x1, n2_ref[...]).astype(BF16)


def _outproj(mix, w_out, x2d, n1, n2, tm, cast_w=None, l=0):
    m = x2d.shape[0]
    row = lambda i: (i, 0)
    in_specs = [pl.BlockSpec((tm, MIX), row)] * 4 + [
        _const_spec((D_MODEL, D_MODEL)), pl.BlockSpec((tm, D_MODEL), row),
        _const_spec((1, D_MODEL)), _const_spec((1, D_MODEL))]
    out_specs = [pl.BlockSpec((tm, D_MODEL), row), pl.BlockSpec((tm, D_MODEL), row)]
    out_shape = [jax.ShapeDtypeStruct((m, D_MODEL), F32), jax.ShapeDtypeStruct((m, D_MODEL), BF16)]
    args = [*mix, w_out, x2d, n1, n2]
    if cast_w is not None:
        nblk = m // tm
        rows, cols = cast_w.shape[1] // nblk, cast_w.shape[2]
        in_specs.append(pl.BlockSpec((None, rows, cols), lambda i: (l, i, 0)))
        out_specs.append(pl.BlockSpec((rows, cols), row))
        out_shape.append(jax.ShapeDtypeStruct(cast_w.shape[1:], BF16))
        args.append(cast_w)
    return pl.pallas_call(
        _outproj_kernel,
        grid=(m // tm,),
        in_specs=in_specs,
        out_specs=out_specs,
        out_shape=out_shape,
        compiler_params=_params("parallel"),
        name="out_proj",
    )(*args)


def _mlp_kernel(h_ref, wu_ref, wd_ref, x1_ref, nw_ref, o_ref, acc_ref):
    f = pl.program_id(1)

    @pl.when(f == 0)
    def _():
        acc_ref[...] = jnp.zeros_like(acc_ref)

    u = jnp.maximum(jnp.dot(h_ref[...], wu_ref[...], preferred_element_type=F32), 0.0)
    acc_ref[...] += jnp.dot((u * u).astype(BF16), wd_ref[...], preferred_element_type=F32)

    @pl.when(f == pl.num_programs(1) - 1)
    def _():
        o_ref[...] = x1_ref[...] + _rms(acc_ref[...], nw_ref[...])


def _mlp(h2, w_up, w_down, x1, nw, tm, tf):
    m = h2.shape[0]
    return pl.pallas_call(
        _mlp_kernel,
        grid=(m // tm, D_FF // tf),
        in_specs=[pl.BlockSpec((tm, D_MODEL), lambda i, f: (i, 0)),
                  pl.BlockSpec((D_MODEL, tf), lambda i, f: (0, f)),
                  pl.BlockSpec((tf, D_MODEL), lambda i, f: (f, 0)),
                  pl.BlockSpec((tm, D_MODEL), lambda i, f: (i, 0)),
                  _const_spec((1, D_MODEL))],
        out_specs=pl.BlockSpec((tm, D_MODEL), lambda i, f: (i, 0)),
        out_shape=jax.ShapeDtypeStruct((m, D_MODEL), F32),
        scratch_shapes=[pltpu.VMEM((tm, D_MODEL), F32)],
        compiler_params=_params("parallel", "arbitrary"),
        name="mlp",
    )(h2, w_up, w_down, x1, nw)


GLA_TB = 512


def _gla_gates(ad, wg2_ref, bg_ref):
    x = _mm(ad, wg2_ref[...], pa=2) + bg_ref[...]
    return _log_sigmoid(x) * (1.0 / GLA_TEMP)


def _gla_kernel(z_ref, misc_ref, wg2_ref, bg_ref, nw_ref, ones_ref, o_ref, s_ref, st_ref, y_ref):
    tb = GLA_TB
    t = pl.program_id(1)

    @pl.when(t == 0)
    def _():
        st_ref[...] = jnp.zeros_like(st_ref)

    q = z_ref[0] * (GLA_DK ** -0.5)
    k = z_ref[1]
    v = jnp.concatenate([z_ref[2], z_ref[3]], axis=1)
    gate = jnp.concatenate([z_ref[4], z_ref[5]], axis=1)
    g = _gla_gates(misc_ref[...], wg2_ref, bg_ref)
    pos = _iota((tb, TILE), 0) & (GLA_C - 1)
    cum = _chunk_cumsum(g, GLA_C, pos)

    y = None
    for j in range(GLA_C):
        if j == 0:
            term = q * k
            vj = v
        else:
            e = jnp.exp(jnp.where(pos >= j, cum - _roll0(cum, j), -jnp.inf))
            term = q * _roll0(k, j) * e
            vj = _roll0(v, j)
        sc = _mm(term, ones_ref[...])
        y = sc * vj if y is None else y + sc * vj
    y_ref[...] = y

    qh = q * jnp.exp(cum)
    last = _chunk_last_bcast(cum, GLA_C, pos)
    kt = k * jnp.exp(last - cum)
    dec = jnp.exp(last)
    kss = [slice(h * GLA_DK, (h + 1) * GLA_DK) for h in range(GLA_H)]
    vss = [slice(h * GLA_DV, (h + 1) * GLA_DV) for h in range(GLA_H)]
    nch = tb // GLA_C
    ds = [[_mm(v[c * GLA_C:(c + 1) * GLA_C, vss[h]], kt[c * GLA_C:(c + 1) * GLA_C, kss[h]], TN)
           for h in range(GLA_H)] for c in range(nch)]
    st = [st_ref[h] for h in range(GLA_H)]
    for c in range(nch):
        rows = slice(c * GLA_C, (c + 1) * GLA_C)
        for h in range(GLA_H):
            y_ref[rows, vss[h]] += _mm(qh[rows, kss[h]], st[h], NT)
            st[h] = st[h] * dec[c * GLA_C:c * GLA_C + 1, kss[h]] + ds[c][h]
    for h in range(GLA_H):
        st_ref[h] = st[h]

    yy = y_ref[...]
    outs = []
    for h in range(GLA_H):
        vs = slice(h * GLA_DV, (h + 1) * GLA_DV)
        outs.append(_rms(yy[:, vs], nw_ref[...]))
    o_ref[...] = (jnp.concatenate(outs, axis=1) * _silu(gate)).astype(BF16)

    @pl.when(t == pl.num_programs(1) - 1)
    def _():
        for h in range(GLA_H):
            s_ref[0, h] = st_ref[h].T


def _gla_prompt(proj, nb, nt_len, wg2, bg, nw, ones):
    tb = GLA_TB
    nt = nt_len // tb
    return pl.pallas_call(
        _gla_kernel,
        grid=(nb, nt),
        in_specs=[pl.BlockSpec((WIDE_TILES, tb, TILE), lambda b, t: (T_GLA // WIDE_TILES, b * nt + t, 0)),
                  pl.BlockSpec((None, tb, TILE), lambda b, t: (T_MISC, b * nt + t, 0)),
                  _const_spec((TILE, TILE)), _const_spec((1, TILE)), _const_spec((1, GLA_DV)),
                  _const_spec((TILE, MIX))],
        out_specs=[pl.BlockSpec((tb, MIX), lambda b, t: (b * nt + t, 0)),
                   pl.BlockSpec((1, GLA_H, GLA_DK, GLA_DV), lambda b, t: (b, 0, 0, 0))],
        out_shape=[jax.ShapeDtypeStruct((nb * nt_len, MIX), BF16),
                   jax.ShapeDtypeStruct((nb, GLA_H, GLA_DK, GLA_DV), F32)],
        scratch_shapes=[pltpu.VMEM((GLA_H, GLA_DV, GLA_DK), F32), pltpu.VMEM((tb, MIX), F32)],
        compiler_params=_params("parallel", "arbitrary"),
        name="gla_prompt",
    )(proj, proj, wg2, bg, nw, ones)


LRU_TB = 256


def _lru_gates(xc, wa_ref, ba_ref, wi_ref, bi_ref, lam_ref):
    ra, ri = [], []
    for j in range(2):
        xs = xc[:, j * TILE:(j + 1) * TILE]
        ra.append(_mm(xs, wa_ref[j], pa=2))
        ri.append(_mm(xs, wi_ref[j], pa=2))
    r = _sigmoid(jnp.concatenate(ra, axis=1) + ba_ref[...])
    i = _sigmoid(jnp.concatenate(ri, axis=1) + bi_ref[...])
    log_a = -LRU_C * r * _softplus(-lam_ref[...])
    a = jnp.exp(log_a)
    mult = jnp.sqrt(_neg_expm1(2.0 * log_a))
    return a, mult, i


def _lru_kernel(x0_ref, x1_ref, g0_ref, g1_ref, cw_ref, cb_ref, wa_ref, ba_ref, wi_ref, bi_ref, lam_ref,
                o_ref, h_ref, tail_ref, carry_ref, hc_ref):
    tb = LRU_TB
    t = pl.program_id(1)

    @pl.when(t == 0)
    def _():
        carry_ref[...] = jnp.zeros_like(carry_ref)
        hc_ref[...] = jnp.zeros_like(hc_ref)

    xb = jnp.concatenate([x0_ref[...], x1_ref[...]], axis=1)
    gb = jnp.concatenate([g0_ref[...], g1_ref[...]], axis=1)
    xc = _causal_conv(xb, carry_ref[...], cw_ref, cb_ref)
    carry_ref[...] = xb[tb - SUBLANES:]
    a, mult, i = _lru_gates(xc, wa_ref, ba_ref, wi_ref, bi_ref, lam_ref)
    row = _iota((tb, MIX), 0)
    mult = jnp.where((row == 0) & (t == 0), 1.0, mult)
    b = mult * i * xc
    pos8 = row & (SUBLANES - 1)

    def roll8(x, d):
        return pltpu.roll(x.reshape(tb // SUBLANES, SUBLANES, MIX), d, 1).reshape(tb, MIX)

    d = 1
    while d < SUBLANES:
        m = pos8 >= d
        b = jnp.where(m, a * roll8(b, d) + b, b)
        a = jnp.where(m, a * roll8(a, d), a)
        d *= 2
    hl = hc_ref[0:1, :]
    groups = []
    for gidx in range(tb // SUBLANES):
        rs = slice(gidx * SUBLANES, (gidx + 1) * SUBLANES)
        hg = b[rs] + a[rs] * hl
        groups.append(hg)
        hl = hg[SUBLANES - 1:SUBLANES, :]
    h = jnp.concatenate(groups, axis=0)
    hc_ref[...] = jnp.broadcast_to(hl, hc_ref.shape)
    o_ref[...] = (h * _gelu_tanh(gb)).astype(BF16)
    h_ref[0] = hl
    tail_ref[0] = xb[tb - SUBLANES:]


def _lru_prompt(proj, nb, nt_len, cw, cb, wa, ba, wi, bi, lam):
    tb = LRU_TB
    nt = nt_len // tb

    def tile(j):
        return pl.BlockSpec((None, tb, TILE), lambda b, t: (T_LRU + j, b * nt + t, 0))

    return pl.pallas_call(
        _lru_kernel,
        grid=(nb, nt),
        in_specs=[tile(0), tile(1), tile(2), tile(3),
                  _const_spec((CONV_W, MIX)), _const_spec((1, MIX)),
                  _const_spec((2, TILE, TILE)), _const_spec((1, MIX)),
                  _const_spec((2, TILE, TILE)), _const_spec((1, MIX)), _const_spec((1, MIX))],
        out_specs=[pl.BlockSpec((tb, MIX), lambda b, t: (b * nt + t, 0)),
                   pl.BlockSpec((1, 1, MIX), lambda b, t: (b, 0, 0)),
                   pl.BlockSpec((1, SUBLANES, MIX), lambda b, t: (b, 0, 0))],
        out_shape=[jax.ShapeDtypeStruct((nb * nt_len, MIX), BF16), jax.ShapeDtypeStruct((nb, 1, MIX), F32),
                   jax.ShapeDtypeStruct((nb, SUBLANES, MIX), F32)],
        scratch_shapes=[pltpu.VMEM((SUBLANES, MIX), F32), pltpu.VMEM((SUBLANES, MIX), F32)],
        compiler_params=_params("parallel", "arbitrary"),
        name="lru_prompt",
    )(proj, proj, proj, proj, cw, cb, wa, ba, wi, bi, lam)


RWKV_TB = 256
RWKV_BS = 128
RWKV_P = {"g": (1, 1), "neu": (1, 1), "app": (1, 1), "chunk": (1, 1)}


def _seg_sum(x, seg_ref):
    outs = []
    for j in range(x.shape[1] // LANES):
        outs.append(_mm(x[:, j * LANES:(j + 1) * LANES], seg_ref[...], pa=2))
    return jnp.concatenate(outs, axis=1)


def _rwkv_pointwise(z, zs, mu_ref, w0_ref, w2_ref, a0_ref, a2_ref, g2_ref, kk_ref, ka_ref, seg_ref):
    zm = z + (zs - z) * mu_ref[...]
    r = zm[:, 0:MIX]
    k = zm[:, MIX:2 * MIX]
    v = zm[:, 2 * MIX:3 * MIX]
    zw = zm[:, 3 * MIX:3 * MIX + 64]
    za = zm[:, 3 * MIX + 64:3 * MIX + 128]
    zg = zm[:, 3 * MIX + 128:3 * MIX + 256]
    lw = -RWKV_DECAY * _sigmoid(w0_ref[...] + _mm(jnp.tanh(zw), w2_ref[...], pa=2))
    a = _sigmoid(a0_ref[...] + _mm(za, a2_ref[...], pa=2))
    g = _mm(_sigmoid(zg), g2_ref[...], pa=2)
    kk = k * kk_ref[...]
    kk = kk / jnp.maximum(jnp.sqrt(_seg_sum(kk * kk, seg_ref)), 1e-12)
    k = k * (1.0 + (a - 1.0) * ka_ref[...])
    return r, lw, k, v, kk, a, g


def _rwkv_finish(y, r, k, v, g, rk_ref, lnw_ref, lnb_ref, seg_ref):
    mean = _seg_sum(y, seg_ref) * (1.0 / RWKV_D)
    yc = y - mean
    var = _seg_sum(yc * yc, seg_ref) * (1.0 / RWKV_D)
    yn = yc * lax.rsqrt(var + RWKV_LN_EPS) * lnw_ref[...] + lnb_ref[...]
    bonus = _seg_sum(r * k * rk_ref[...], seg_ref) * v
    return (yn + bonus) * g


def _rwkv_kernel(z_ref, mu_ref, w0_ref, w2_ref, a0_ref, a2_ref, g2_ref, kk_ref, ka_ref, rk_ref, lnw_ref, lnb_ref,
                 seg_ref, wsrc_ref, o_ref, s_ref, tail_ref, wdst_ref, prev_ref, st_ref, y_ref, scat_ref):
    wdst_ref[...] = wsrc_ref[...].astype(BF16)
    tb = RWKV_TB
    cc = RWKV_C
    t = pl.program_id(1)

    @pl.when(t == 0)
    def _():
        prev_ref[...] = jnp.zeros_like(prev_ref)
        st_ref[...] = jnp.zeros_like(st_ref)

    z = jnp.concatenate([z_ref[j] for j in range(GROUP_TILES)], axis=1)
    zs = _shift_rows(z, prev_ref[...], 1, _iota((SUBLANES, RWKV_COLS), 0))
    prev_ref[...] = z[tb - SUBLANES:]
    tail_ref[0] = z[tb - SUBLANES:]
    r, lw, k, v, kk, a, g = _rwkv_pointwise(z, zs, mu_ref, w0_ref, w2_ref, a0_ref, a2_ref, g2_ref, kk_ref, ka_ref,
                                            seg_ref)
    b = kk * a

    pos = _iota((tb, MIX), 0) & (cc - 1)
    lg = _chunk_cumsum(lw, cc, pos)
    lg_end = _chunk_last_bcast(lg, cc, pos)
    gam = jnp.exp(lg)
    inv = jnp.exp(-lg)
    rg = r * gam
    kg = kk * jnp.exp(lg - lw)
    bi = b * inv
    ki = k * inv
    e_end = jnp.exp(lg_end - lg)
    bt = b * e_end
    kt = k * e_end
    g_end = jnp.exp(lg_end)

    bs = RWKV_BS
    ri = _iota((bs, bs), 0)
    ci = _iota((bs, bs), 1)
    same = (ri - (ri & (cc - 1))) == (ci - (ci & (cc - 1)))
    strict = same & (ci < ri)
    incl = same & (ci <= ri)
    eye = (ri == ci).astype(F32)
    eye_d = _iota((RWKV_D, RWKV_D), 0) == _iota((RWKV_D, RWKV_D), 1)

    hs = [slice(h * RWKV_D, (h + 1) * RWKV_D) for h in range(RWKV_H)]
    mg, mn, ma, mc_ = RWKV_P["g"], RWKV_P["neu"], RWKV_P["app"], RWKV_P["chunk"]
    units = [(slice(q * bs, (q + 1) * bs), h) for q in range(tb // bs) for h in range(RWKV_H)]
    vh = [v[rb, hs[h]] for rb, h in units]
    gmat = [_mm(jnp.concatenate([kg[rb, hs[h]], rg[rb, hs[h]]], axis=0),
                jnp.concatenate([bi[rb, hs[h]], ki[rb, hs[h]]], axis=0), NT, *mg) for rb, h in units]
    amat = [jnp.where(strict, gm[:bs, :bs], 0.0) for gm in gmat]
    bmat = [jnp.where(strict, gm[:bs, bs:], 0.0) for gm in gmat]
    pb_ = [jnp.where(incl, gm[bs:, :bs], 0.0) for gm in gmat]
    pk_ = [jnp.where(incl, gm[bs:, bs:], 0.0) for gm in gmat]
    bv = [_mm(bm_, v_, NN, *ma) for bm_, v_ in zip(bmat, vh)]

    def lower_left(b):
        rb = ri & (b - 1)
        cb = ci & (b - 1)
        return ((ri - rb) == (ci - cb)) & (rb >= b // 2) & (cb < b // 2)

    x = [eye - jnp.where(lower_left(2), a_, 0.0) for a_ in amat]
    b = 4
    while b <= cc:
        m = lower_left(b)
        tx = [_mm(x_, jnp.where(m, a_, 0.0), NN, *mn) for x_, a_ in zip(x, amat)]
        x = [x_ - _mm(tx_, x_, NN, *mn) for x_, tx_ in zip(x, tx)]
        b *= 2
    wu = [-_mm(x_, jnp.concatenate([kg[rb, hs[h]], bv_], axis=1), NN, *ma)
          for x_, (rb, h), bv_ in zip(x, units, bv)]
    low = _iota((bs, LANES), 1) < RWKV_D
    zero = jnp.zeros((bs, LANES), F32)

    def dup(slab, odd):
        sw = pltpu.roll(slab, RWKV_D, 1)
        return jnp.where(low, sw, slab) if odd else jnp.where(low, slab, sw)

    def upper(slab, odd):
        return jnp.where(low, zero, slab if odd else pltpu.roll(slab, RWKV_D, 1))

    pair = [slice((h // 2) * LANES, (h // 2 + 1) * LANES) for h in range(RWKV_H)]
    rg2 = [dup(rg[rb, pair[h]], h % 2) for rb, h in units]
    ov = [upper(v[rb, pair[h]], h % 2) for rb, h in units]
    ww = [jnp.where(low, wu_, pltpu.roll(wu_, RWKV_D, 1)) for wu_ in wu]
    qy = [_mm(pb, jnp.concatenate([ww_, wu_], axis=1), NN, *ma) for pb, ww_, wu_ in zip(pb_, ww, wu)]
    pkv = [_mm(pk, v_, NN, *ma) for pk, v_ in zip(pk_, vh)]
    qt2 = [rg2_ + qy_[:, :LANES] for rg2_, qy_ in zip(rg2, qy)]
    y0 = [qy_[:, LANES + RWKV_D:] + pkv_ for qy_, pkv_ in zip(qy, pkv)]
    nch = tb // cc
    cpb = bs // cc
    s = [st_ref[h] for h in range(RWKV_H)]
    for c in range(nch):
        rows = slice(c * cc, (c + 1) * cc)
        lrows = slice((c % cpb) * cc, (c % cpb + 1) * cc)
        u0 = (c // cpb) * RWKV_H
        for h in range(RWKV_H):
            scat_ref[h, :, c * RWKV_D:(c + 1) * RWKV_D] = s[h]
        tr = [_mm(jnp.concatenate([wu[u0 + h][lrows], ov[u0 + h][lrows]], axis=0),
                  jnp.concatenate([bt[rows, hs[h]], kt[rows, hs[h]]], axis=0), TN, *mc_) for h in range(RWKV_H)]
        s = [_mm(s_, jnp.where(eye_d, g_end[c * cc:c * cc + 1, ls], 0.0) + t_[:RWKV_D], NN, *mc_) + t_[RWKV_D:]
             for s_, t_, ls in zip(s, tr, hs)]
    r5 = _iota((bs, cpb * RWKV_D), 0)
    c5 = _iota((bs, cpb * RWKV_D), 1)
    own = (r5 - (r5 & (cc - 1))) * (RWKV_D // cc) == c5 - (c5 & (RWKV_D - 1))
    for h in range(RWKV_H):
        st_ref[h] = s[h]
    for u, (rb, h) in enumerate(units):
        q = u // RWKV_H
        qexp = jnp.where(own, jnp.concatenate([qt2[u]] * (cpb * RWKV_D // LANES), axis=1), 0.0)
        y_ref[rb, hs[h]] = _mm(qexp, scat_ref[h, :, q * cpb * RWKV_D:(q + 1) * cpb * RWKV_D], NT, *mc_) + y0[u]

    o_ref[...] = _rwkv_finish(y_ref[...], r, k, v, g, rk_ref, lnw_ref, lnb_ref, seg_ref).astype(BF16)

    @pl.when(t == pl.num_programs(1) - 1)
    def _():
        s_ref[0] = st_ref[...]


def _rwkv_prompt(proj, nb, nt_len, p, cast_w, l):
    tb = RWKV_TB
    nt = nt_len // tb
    crow, ccol = cast_w.shape[1] // (nb * nt), cast_w.shape[2]
    return pl.pallas_call(
        _rwkv_kernel,
        grid=(nb, nt),
        in_specs=[pl.BlockSpec((GROUP_TILES, tb, TILE), lambda b, t: (T_RWKV // GROUP_TILES, b * nt + t, 0)),
                  _const_spec((1, RWKV_COLS)), _const_spec((1, MIX)), _const_spec((64, MIX)),
                  _const_spec((1, MIX)), _const_spec((64, MIX)), _const_spec((128, MIX)),
                  _const_spec((1, MIX)), _const_spec((1, MIX)), _const_spec((1, MIX)),
                  _const_spec((1, MIX)), _const_spec((1, MIX)), _const_spec((LANES, LANES)),
                  pl.BlockSpec((None, crow, ccol), lambda b, t: (l, b * nt + t, 0))],
        out_specs=[pl.BlockSpec((tb, MIX), lambda b, t: (b * nt + t, 0)),
                   pl.BlockSpec((1, RWKV_H, RWKV_D, RWKV_D), lambda b, t: (b, 0, 0, 0)),
                   pl.BlockSpec((1, SUBLANES, RWKV_COLS), lambda b, t: (b, 0, 0)),
                   pl.BlockSpec((crow, ccol), lambda b, t: (b * nt + t, 0))],
        out_shape=[jax.ShapeDtypeStruct((nb * nt_len, MIX), BF16),
                   jax.ShapeDtypeStruct((nb, RWKV_H, RWKV_D, RWKV_D), F32),
                   jax.ShapeDtypeStruct((nb, SUBLANES, RWKV_COLS), F32),
                   jax.ShapeDtypeStruct(cast_w.shape[1:], BF16)],
        scratch_shapes=[pltpu.VMEM((SUBLANES, RWKV_COLS), F32), pltpu.VMEM((RWKV_H, RWKV_D, RWKV_D), F32),
                        pltpu.VMEM((tb, MIX), F32), pltpu.VMEM((RWKV_H, RWKV_D, (tb // RWKV_C) * RWKV_D), F32)],
        compiler_params=_params("parallel", "arbitrary"),
        name="rwkv_prompt",
    )(proj, p["mu"], p["w0"], p["w2"], p["a0"], p["a2"], p["g2"], p["kk"], p["ka"], p["rk"], p["lnw"], p["lnb"],
      p["seg"], cast_w)


SSD_TB = 512


def _ssd_pointwise(xbc_c, dt_raw, dtb_ref, aneg_ref):
    xbc = _silu(xbc_c)
    dt = _softplus(dt_raw + dtb_ref[...])
    return xbc, dt, dt * aneg_ref[...]


def _expand_heads(misc, width):
    n = misc.shape[0]
    return jnp.concatenate([jnp.broadcast_to(misc[:, MISC_DT + h:MISC_DT + h + 1], (n, width))
                            for h in range(SSD_H)], axis=1)


def _ssd_kernel(z_ref, misc_ref, cw_ref, cb_ref, dtb_ref, aneg_ref, dsk_ref, nw_ref, o_ref, s_ref, tail_ref, carry_ref,
                st_ref, y_ref):
    tb = SSD_TB
    cc = SSD_C
    t = pl.program_id(1)

    @pl.when(t == 0)
    def _():
        carry_ref[...] = jnp.zeros_like(carry_ref)
        st_ref[...] = jnp.zeros_like(st_ref)

    gate = jnp.concatenate([z_ref[0], z_ref[1]], axis=1)
    xbc_raw = jnp.concatenate([z_ref[2], z_ref[3], z_ref[4], z_ref[5]], axis=1)
    xbc_c = _causal_conv(xbc_raw, carry_ref[...], cw_ref, cb_ref)
    carry_ref[...] = xbc_raw[tb - SUBLANES:]
    tail_ref[0] = xbc_raw[tb - SUBLANES:]
    xbc, dt, dta = _ssd_pointwise(xbc_c, misc_ref[...], dtb_ref, aneg_ref)
    xs = xbc[:, :MIX]
    bm = xbc[:, MIX:MIX + TILE]
    cm = xbc[:, MIX + TILE:]
    pos = _iota((tb, TILE), 0) & (cc - 1)
    cum = _chunk_cumsum(dta, cc, pos)
    xdt = xs * _expand_heads(dt, SSD_P)
    tri = _iota((cc, cc), 0) >= _iota((cc, cc), 1)

    hg = SSD_H // SSD_G
    nch = tb // cc
    rws = [slice(c * cc, (c + 1) * cc) for c in range(nch)]
    cum_cs = [cum[rows, :LANES] for rows in rws]
    cum_ts = [x.T for x in cum_cs]
    units = [(c, gi) for c in range(nch) for gi in range(SSD_G)]
    cgs = [cm[rws[c], gi * SSD_N:(gi + 1) * SSD_N] for c, gi in units]
    bgs = [bm[rws[c], gi * SSD_N:(gi + 1) * SSD_N] for c, gi in units]
    gmats = [_mm(cg, bg, NT) for cg, bg in zip(cgs, bgs)]
    heads = [(u, gi * hg + hh) for u, (c, gi) in enumerate(units) for hh in range(hg)]
    cols = [cum_cs[units[u][0]][:, MISC_DT + h:MISC_DT + h + 1] for u, h in heads]
    lasts = [cum_cs[units[u][0]][cc - 1:cc, MISC_DT + h:MISC_DT + h + 1] for u, h in heads]
    lmats = [jnp.exp(jnp.where(tri, col - cum_ts[units[u][0]][MISC_DT + h:MISC_DT + h + 1, :], -jnp.inf))
             for (u, h), col in zip(heads, cols)]
    xhs = [xdt[rws[units[u][0]], h * SSD_P:(h + 1) * SSD_P] for u, h in heads]
    yds = [_mm(gmats[u] * lm, xh) for (u, h), lm, xh in zip(heads, lmats, xhs)]
    xds = [xh * jnp.exp(last - col) for xh, last, col in zip(xhs, lasts, cols)]
    pre = []
    for u in range(len(units)):
        hsl = slice(u * hg, (u + 1) * hg)
        pre.append((cgs[u], jnp.concatenate(yds[hsl], axis=1),
                    jnp.concatenate([jnp.broadcast_to(jnp.exp(col), (cc, SSD_P)) for col in cols[hsl]], axis=1),
                    jnp.concatenate([jnp.broadcast_to(jnp.exp(last), (SSD_P, SSD_N)) for last in lasts[hsl]], axis=0),
                    _mm(jnp.concatenate(xds[hsl], axis=1), bgs[u], TN)))
    for c in range(nch):
        rows = slice(c * cc, (c + 1) * cc)
        for gi in range(SSD_G):
            gs = slice(gi * hg * SSD_P, (gi + 1) * hg * SSD_P)
            cg, yd, ec, dec, ds = pre[c * SSD_G + gi]
            st = st_ref[gs, :]
            y_ref[rows, gs] = yd + _mm(cg, st, NT) * ec
            st_ref[gs, :] = st * dec + ds

    y = (y_ref[...] + dsk_ref[...] * xs) * _silu(gate)
    o_ref[...] = _rms(y, nw_ref[...]).astype(BF16)

    @pl.when(t == pl.num_programs(1) - 1)
    def _():
        for h in range(SSD_H):
            s_ref[0, h] = st_ref[h * SSD_P:(h + 1) * SSD_P, :]


def _ssd_prompt(proj, nb, nt_len, cw, cb, dtb, aneg, dsk, nw):
    tb = SSD_TB
    nt = nt_len // tb
    return pl.pallas_call(
        _ssd_kernel,
        grid=(nb, nt),
        in_specs=[pl.BlockSpec((WIDE_TILES, tb, TILE), lambda b, t: (T_SSD // WIDE_TILES, b * nt + t, 0)),
                  pl.BlockSpec((None, tb, TILE), lambda b, t: (T_MISC, b * nt + t, 0)),
                  _const_spec((CONV_W, SSD_CONV_DIM)), _const_spec((1, SSD_CONV_DIM)),
                  _const_spec((1, TILE)), _const_spec((1, TILE)), _const_spec((1, MIX)), _const_spec((1, MIX))],
        out_specs=[pl.BlockSpec((tb, MIX), lambda b, t: (b * nt + t, 0)),
                   pl.BlockSpec((1, SSD_H, SSD_P, SSD_N), lambda b, t: (b, 0, 0, 0)),
                   pl.BlockSpec((1, SUBLANES, SSD_CONV_DIM), lambda b, t: (b, 0, 0))],
        out_shape=[jax.ShapeDtypeStruct((nb * nt_len, MIX), BF16),
                   jax.ShapeDtypeStruct((nb, SSD_H, SSD_P, SSD_N), F32),
                   jax.ShapeDtypeStruct((nb, SUBLANES, SSD_CONV_DIM), F32)],
        scratch_shapes=[pltpu.VMEM((SUBLANES, SSD_CONV_DIM), F32), pltpu.VMEM((SSD_H * SSD_P, SSD_N), F32),
                        pltpu.VMEM((tb, MIX), F32)],
        compiler_params=_params("parallel", "arbitrary"),
        name="ssd_prompt",
    )(proj, proj, cw, cb, dtb, aneg, dsk, nw)


R_GLA_V, R_GLA_GATE, R_GLA_Q = 0, 512, 1024
R_RW_K, R_RW_R, R_RW_V, R_RW_G = 1280, 1792, 2304, 2816
R_SSD_B, R_SSD_C, R_SSD_E, R_SSD_GATE, R_SSD_X = 3328, 3584, 3840, 4864, 5376
ROW_W = 5888
C_GLA_A, C_GLA_K, C_SSD_X = 0, 256, 512
COL_W = 1024
T_RW_W, T_RW_B, T_RW_K, T_RW_KK, T_RW_R, T_RW_V = 0, 512, 1024, 1536, 2048, 2560
RWT_W = 3072
DEC_RB = 8


def _dec_prep_kernel(z_ref, lconv_ref, lh_ref, rprev_ref, sconv_ref,
                     wg2_ref, bg_ref,
                     lcw_ref, lcb_ref, wa_ref, ba_ref, wi_ref, bi_ref, lam_ref,
                     mu_ref, w0_ref, w2_ref, a0_ref, a2_ref, g2_ref, kk_ref, ka_ref, seg_ref,
                     scw_ref, scb_ref, dtb_ref, aneg_ref,
                     rows_ref, colt_ref, rwt_ref, olru_ref, lh_out_ref, lconv_out_ref, sconv_out_ref):
    k = z_ref[T_GLA + 1]
    g = _gla_gates(z_ref[T_MISC], wg2_ref, bg_ref)
    rows_ref[:, R_GLA_V:R_GLA_V + TILE] = z_ref[T_GLA + 2]
    rows_ref[:, R_GLA_V + TILE:R_GLA_V + MIX] = z_ref[T_GLA + 3]
    rows_ref[:, R_GLA_GATE:R_GLA_GATE + TILE] = z_ref[T_GLA + 4]
    rows_ref[:, R_GLA_GATE + TILE:R_GLA_GATE + MIX] = z_ref[T_GLA + 5]
    rows_ref[:, R_GLA_Q:R_GLA_Q + TILE] = z_ref[T_GLA + 0] * (GLA_DK ** -0.5)
    cols = [jnp.exp(g), k]

    xb = jnp.concatenate([z_ref[T_LRU + 0], z_ref[T_LRU + 1]], axis=1)
    gb = jnp.concatenate([z_ref[T_LRU + 2], z_ref[T_LRU + 3]], axis=1)
    xc = lcb_ref[...] + xb * lcw_ref[CONV_W - 1:CONV_W, :]
    for i in range(CONV_W - 1):
        xc = xc + lconv_ref[:, i * MIX:(i + 1) * MIX] * lcw_ref[i:i + 1, :]
    a, mult, gi = _lru_gates(xc, wa_ref, ba_ref, wi_ref, bi_ref, lam_ref)
    h = a * lh_ref[...] + mult * gi * xc
    lh_out_ref[...] = h
    olru_ref[...] = (h * _gelu_tanh(gb)).astype(BF16)
    lconv_out_ref[:, 0:2 * MIX] = lconv_ref[:, MIX:3 * MIX]
    lconv_out_ref[:, 2 * MIX:3 * MIX] = xb

    z = jnp.concatenate([z_ref[T_RWKV + j] for j in range(GROUP_TILES)], axis=1)
    r, lw, kmod, v, kk, a7, g7 = _rwkv_pointwise(z, rprev_ref[...], mu_ref, w0_ref, w2_ref, a0_ref, a2_ref, g2_ref,
                                                 kk_ref, ka_ref, seg_ref)
    rows_ref[:, R_RW_K:R_RW_K + MIX] = kmod
    rows_ref[:, R_RW_R:R_RW_R + MIX] = r
    rows_ref[:, R_RW_V:R_RW_V + MIX] = v
    rows_ref[:, R_RW_G:R_RW_G + MIX] = g7
    off = 0
    for vec in (jnp.exp(lw), kk * a7, kmod, kk, r, v):
        for j in range(MIX // LANES):
            rwt_ref[off:off + LANES, :] = vec[:, j * LANES:(j + 1) * LANES].T
            off += LANES

    xbc_raw = jnp.concatenate([z_ref[T_SSD + 2], z_ref[T_SSD + 3], z_ref[T_SSD + 4], z_ref[T_SSD + 5]], axis=1)
    xbc_c = scb_ref[...] + xbc_raw * scw_ref[CONV_W - 1:CONV_W, :]
    for i in range(CONV_W - 1):
        xbc_c = xbc_c + sconv_ref[:, i * SSD_CONV_DIM:(i + 1) * SSD_CONV_DIM] * scw_ref[i:i + 1, :]
    xbc, dt, dta = _ssd_pointwise(xbc_c, z_ref[T_MISC], dtb_ref, aneg_ref)
    xs = xbc[:, :MIX]
    rows_ref[:, R_SSD_B:R_SSD_B + TILE] = xbc[:, MIX:MIX + TILE]
    rows_ref[:, R_SSD_C:R_SSD_C + TILE] = xbc[:, MIX + TILE:]
    rows_ref[:, R_SSD_E:R_SSD_E + SSD_H * LANES] = _expand_heads(jnp.exp(dta), LANES)
    rows_ref[:, R_SSD_GATE:R_SSD_GATE + TILE] = z_ref[T_SSD + 0]
    rows_ref[:, R_SSD_GATE + TILE:R_SSD_GATE + MIX] = z_ref[T_SSD + 1]
    rows_ref[:, R_SSD_X:R_SSD_X + MIX] = xs
    cols.append(xs * _expand_heads(dt, SSD_P))
    sconv_out_ref[:, 0:2 * SSD_CONV_DIM] = sconv_ref[:, SSD_CONV_DIM:3 * SSD_CONV_DIM]
    sconv_out_ref[:, 2 * SSD_CONV_DIM:3 * SSD_CONV_DIM] = xbc_raw

    off = 0
    for cvec in cols:
        for j in range(cvec.shape[1] // LANES):
            colt_ref[off:off + LANES, :] = cvec[:, j * LANES:(j + 1) * LANES].T
            off += LANES


def _dec_prep(zdec, lconv, lh, rprev, sconv, pp):
    n = zdec.shape[1]
    args = [zdec, lconv, lh, rprev, sconv,
            pp["gla_wg2"], pp["gla_bg"],
            pp["lru_cw"], pp["lru_cb"], pp["lru_wa"], pp["lru_ba"], pp["lru_wi"], pp["lru_bi"], pp["lru_lam"],
            pp["rwkv"]["mu"], pp["rwkv"]["w0"], pp["rwkv"]["w2"], pp["rwkv"]["a0"], pp["rwkv"]["a2"],
            pp["rwkv"]["g2"], pp["rwkv"]["kk"], pp["rwkv"]["ka"], pp["rwkv"]["seg"],
            pp["ssd_cw"], pp["ssd_cb"], pp["ssd_dtb"], pp["ssd_aneg"]]
    return pl.pallas_call(
        _dec_prep_kernel,
        out_shape=[jax.ShapeDtypeStruct((n, ROW_W), F32), jax.ShapeDtypeStruct((COL_W, n), F32),
                   jax.ShapeDtypeStruct((RWT_W, n), F32),
                   jax.ShapeDtypeStruct((n, MIX), BF16), jax.ShapeDtypeStruct((n, MIX), F32),
                   jax.ShapeDtypeStruct((n, 3 * MIX), F32), jax.ShapeDtypeStruct((n, 3 * SSD_CONV_DIM), F32)],
        compiler_params=pltpu.CompilerParams(vmem_limit_bytes=VMEM_LIMIT),
        name="decode_prep",
    )(*args)


def _dec_rwkv_kernel(w_ref, b_ref, k_ref, kk_ref, r_ref, v_ref, s_ref, *rest, layer, fill_others):
    s_out, y_ref = rest[-2:]
    if fill_others:
        for j in range(DEPTH):
            if j != layer:
                s_out[j] = jnp.zeros(s_out.shape[1:], F32)
        s_out = s_out.at[layer]
    w = w_ref[...]
    b = b_ref[...]
    k = k_ref[...]
    kk = kk_ref[...]
    r = r_ref[...]

    def body(vi, carry):
        s = s_ref[vi]
        sa = jnp.sum(s * kk, axis=0, keepdims=True)
        s = s * w - sa * b + v_ref[pl.ds(vi, 1), :] * k
        s_out[vi] = s
        y_ref[pl.ds(vi, 1), :] = jnp.sum(s * r, axis=0, keepdims=True)
        return carry

    lax.fori_loop(0, RWKV_D, body, 0, unroll=8)


def _dec_rwkv(rwt, sr_t, l, prev):
    n = rwt.shape[1]
    hb = MIX // RWKV_D

    def vec(off):
        return pl.BlockSpec((RWKV_D, n), lambda h: (off // RWKV_D + h, 0))

    sspec = pl.BlockSpec((None, None, RWKV_D, RWKV_D, n), lambda h: (l, h, 0, 0, 0))
    extra = [] if prev is None else [prev]
    first = prev is None
    ospec = pl.BlockSpec((DEPTH, None, RWKV_D, RWKV_D, n), lambda h: (0, h, 0, 0, 0)) if first else sspec
    return pl.pallas_call(
        functools.partial(_dec_rwkv_kernel, layer=l, fill_others=first),
        grid=(hb,),
        in_specs=[vec(T_RW_W), vec(T_RW_B), vec(T_RW_K), vec(T_RW_KK), vec(T_RW_R), vec(T_RW_V), sspec]
        + [pl.BlockSpec(memory_space=pl.ANY)] * len(extra),
        out_specs=[ospec, pl.BlockSpec((RWKV_D, n), lambda h: (h, 0))],
        out_shape=[jax.ShapeDtypeStruct(sr_t.shape, F32), jax.ShapeDtypeStruct((MIX, n), F32)],
        input_output_aliases={7: 0} if extra else {},
        compiler_params=_params("parallel"),
        name="decode_rwkv",
    )(rwt, rwt, rwt, rwt, rwt, rwt, sr_t, *extra)


def _dec_state_kernel(colt_ref, rows_ref, sg_ref, ss_ref, *rest, layer, fill_others):
    sg_out, ss_out, yg_ref, ys_ref = rest[-4:]
    if fill_others:
        for j in range(DEPTH):
            if j != layer:
                sg_out[j] = jnp.zeros(sg_out.shape[1:], F32)
                ss_out[j] = jnp.zeros(ss_out.shape[1:], F32)
        sg_out = sg_out.at[layer]
        ss_out = ss_out.at[layer]
    pid = pl.program_id(0)
    n = colt_ref.shape[1]
    lane = _iota((1, n), 1)
    rid = _iota((DEC_RB, LANES), 0)
    rows = rows_ref[...]
    hg = SSD_H // SSD_G
    yg = [jnp.zeros((DEC_RB, GLA_DV), F32) for _ in range(GLA_H)]
    ys = [jnp.zeros((DEC_RB, SSD_P), F32) for _ in range(SSD_H)]
    for i in range(DEC_RB):
        sel = lane == pid * DEC_RB + i
        col = jnp.sum(jnp.where(sel, colt_ref[...], 0.0), axis=1, keepdims=True)
        rv = rows[i:i + 1, :]
        mine = rid == i
        for h in range(GLA_H):
            al = col[C_GLA_A + h * GLA_DK:C_GLA_A + (h + 1) * GLA_DK]
            kc = col[C_GLA_K + h * GLA_DK:C_GLA_K + (h + 1) * GLA_DK]
            vr = rv[:, R_GLA_V + h * GLA_DV:R_GLA_V + (h + 1) * GLA_DV]
            s = al * sg_ref[i, h] + kc * vr
            sg_out[i, h] = s
            q8 = rows[:, R_GLA_Q + h * GLA_DK:R_GLA_Q + (h + 1) * GLA_DK]
            yg[h] = jnp.where(mine, _mm(q8, s, NN, pa=2, pb=2), yg[h])
        for h in range(SSD_H):
            gi = h // hg
            e = rv[:, R_SSD_E + h * LANES:R_SSD_E + (h + 1) * LANES]
            bn = rv[:, R_SSD_B + gi * SSD_N:R_SSD_B + (gi + 1) * SSD_N]
            xc = col[C_SSD_X + h * SSD_P:C_SSD_X + (h + 1) * SSD_P]
            s = ss_ref[i, h] * e + xc * bn
            ss_out[i, h] = s
            c8 = rows[:, R_SSD_C + gi * SSD_N:R_SSD_C + (gi + 1) * SSD_N]
            ys[h] = jnp.where(mine[:, :SSD_P], _mm(c8, s, NT, pa=2, pb=2), ys[h])
    yg_ref[...] = jnp.concatenate(yg, axis=1)
    ys_ref[...] = jnp.concatenate(ys, axis=1)


def _dec_state(colt, rows, sg, ss, l, prev):
    n = rows.shape[0]
    blk = lambda i: (l, i, 0, 0, 0)
    specs = [pl.BlockSpec((None, DEC_RB, GLA_H, GLA_DK, GLA_DV), blk),
             pl.BlockSpec((None, DEC_RB, SSD_H, SSD_P, SSD_N), blk)]
    extra = [] if prev is None else list(prev)
    first = prev is None
    ospecs = specs
    if first:
        all_layers = lambda i: (0, i, 0, 0, 0)
        ospecs = [pl.BlockSpec((DEPTH, DEC_RB, GLA_H, GLA_DK, GLA_DV), all_layers),
                  pl.BlockSpec((DEPTH, DEC_RB, SSD_H, SSD_P, SSD_N), all_layers)]
    rowspec = pl.BlockSpec((DEC_RB, MIX), lambda i: (i, 0))
    return pl.pallas_call(
        functools.partial(_dec_state_kernel, layer=l, fill_others=first),
        grid=(n // DEC_RB,),
        in_specs=[_const_spec((COL_W, n)), pl.BlockSpec((DEC_RB, ROW_W), lambda i: (i, 0))] + specs
        + [pl.BlockSpec(memory_space=pl.ANY)] * len(extra),
        out_specs=ospecs + [rowspec, rowspec],
        out_shape=[jax.ShapeDtypeStruct(sg.shape, F32), jax.ShapeDtypeStruct(ss.shape, F32),
                   jax.ShapeDtypeStruct((n, MIX), F32), jax.ShapeDtypeStruct((n, MIX), F32)],
        input_output_aliases={4 + j: j for j in range(len(extra))},
        compiler_params=_params("parallel"),
        name="decode_state",
    )(colt, rows, sg, ss, *extra)


def _dec_finish_kernel(rows_ref, yg_ref, yrt_ref, ys_ref, gnw_ref, rk_ref, lnw_ref, lnb_ref, seg_ref,
                       dsk_ref, snw_ref, og_ref, or_ref, os_ref):
    outs = []
    for h in range(GLA_H):
        outs.append(_rms(yg_ref[:, h * GLA_DV:(h + 1) * GLA_DV], gnw_ref[...]))
    og_ref[...] = (jnp.concatenate(outs, axis=1) * _silu(rows_ref[:, R_GLA_GATE:R_GLA_GATE + MIX])).astype(BF16)
    yr = jnp.concatenate([yrt_ref[j * LANES:(j + 1) * LANES, :].T for j in range(MIX // LANES)], axis=1)
    or_ref[...] = _rwkv_finish(yr, rows_ref[:, R_RW_R:R_RW_R + MIX], rows_ref[:, R_RW_K:R_RW_K + MIX],
                               rows_ref[:, R_RW_V:R_RW_V + MIX], rows_ref[:, R_RW_G:R_RW_G + MIX],
                               rk_ref, lnw_ref, lnb_ref, seg_ref).astype(BF16)
    y = ((ys_ref[...] + dsk_ref[...] * rows_ref[:, R_SSD_X:R_SSD_X + MIX])
         * _silu(rows_ref[:, R_SSD_GATE:R_SSD_GATE + MIX]))
    os_ref[...] = _rms(y, snw_ref[...]).astype(BF16)


def _dec_finish(rows, yg, yrt, ys, pp):
    n = rows.shape[0]
    o = jax.ShapeDtypeStruct((n, MIX), BF16)
    return pl.pallas_call(
        _dec_finish_kernel,
        out_shape=[o, o, o],
        compiler_params=pltpu.CompilerParams(vmem_limit_bytes=VMEM_LIMIT),
        name="decode_finish",
    )(rows, yg, yrt, ys, pp["gla_nw"], pp["rwkv"]["rk"], pp["rwkv"]["lnw"], pp["rwkv"]["lnb"], pp["rwkv"]["seg"],
      pp["ssd_dsk"], pp["ssd_nw"])


def _block_diag(w8):
    nb = TILE // 64
    tiled = jnp.tile(w8.reshape(2, TILE, 64), (1, 1, nb))
    idx = jnp.arange(TILE) // 64
    return jnp.where(idx[:, None] == idx[None, :], tiled, 0.0)


def _prep_weights(w_in, w_out, w_up, w_down):
    o_lru = GLA_COLS
    o_rwkv = GLA_COLS + LRU_COLS
    o_ssd = o_rwkv + RWKV_COLS
    wide = WIDE_TILES * TILE
    wt = jnp.transpose(w_in, (0, 2, 1))
    zpad = lambda n: jnp.zeros((DEPTH, n, D_MODEL), w_in.dtype)
    w_perm = jnp.concatenate([
        wt[:, o_rwkv:o_rwkv + RWKV_COLS],
        wt[:, o_lru:o_lru + LRU_COLS],
        wt[:, wide:GLA_COLS], wt[:, o_ssd + wide:o_ssd + SSD_COLS], zpad(TILE - GLA_RANK - SSD_H),
        wt[:, 0:wide],
        wt[:, o_ssd:o_ssd + wide]], axis=1).astype(BF16)
    return {"w_in": w_perm,
            "w_out": w_out,
            "w_up": w_up, "w_down": w_down}


def _prep_layer(l, norm_mix_pre, norm_mix_post, norm_mlp_pre, norm_mlp_post,
                gla_w_gate2, gla_b_gate, gla_norm, lru_conv_w, lru_conv_b, lru_w_a, lru_b_a, lru_w_i, lru_b_i,
                lru_lambda, rwkv_mu, rwkv_w0, rwkv_w2, rwkv_a0, rwkv_a2, rwkv_g2, rwkv_k_k, rwkv_k_a, rwkv_r_k,
                rwkv_ln_w, rwkv_ln_b, ssd_conv_w, ssd_conv_b, ssd_dt_bias, ssd_a_log, ssd_d, ssd_norm):
    row = lambda a: a.reshape(1, -1).astype(F32)
    ii = jnp.arange(TILE)[:, None]
    jj = jnp.arange(MIX)[None, :]
    seg_i = jnp.arange(LANES)
    pp = {
        "n_mix_pre": row(norm_mix_pre[l]), "n_mix_post": row(norm_mix_post[l]),
        "n_mlp_pre": row(norm_mlp_pre[l]), "n_mlp_post": row(norm_mlp_post[l]),
        "gla_wg2": jnp.pad(gla_w_gate2[l], ((0, TILE - GLA_RANK), (0, 0))).astype(BF16),
        "gla_bg": row(gla_b_gate[l]),
        "gla_nw": row(gla_norm[l]),
        "gla_ones": (ii // GLA_DK == jj // GLA_DV).astype(BF16),
        "lru_cw": lru_conv_w[l].astype(F32), "lru_cb": row(lru_conv_b[l]),
        "lru_wa": _block_diag(lru_w_a[l]).astype(BF16),
        "lru_wi": _block_diag(lru_w_i[l]).astype(BF16),
        "lru_ba": row(lru_b_a[l]), "lru_bi": row(lru_b_i[l]), "lru_lam": row(lru_lambda[l]),
        "rwkv": {
            "mu": row(rwkv_mu[l]), "w0": row(rwkv_w0[l]), "w2": rwkv_w2[l].astype(BF16),
            "a0": row(rwkv_a0[l]), "a2": rwkv_a2[l].astype(BF16), "g2": rwkv_g2[l].astype(BF16),
            "kk": row(rwkv_k_k[l]), "ka": row(rwkv_k_a[l]), "rk": row(rwkv_r_k[l]),
            "lnw": row(rwkv_ln_w[l]), "lnb": row(rwkv_ln_b[l]),
            "seg": (seg_i[:, None] // RWKV_D == seg_i[None, :] // RWKV_D).astype(BF16),
        },
        "ssd_cw": ssd_conv_w[l].astype(F32), "ssd_cb": row(ssd_conv_b[l]),
        "ssd_dtb": jnp.pad(ssd_dt_bias[l].astype(F32), (MISC_DT, TILE - MISC_DT - SSD_H)).reshape(1, TILE),
        "ssd_aneg": jnp.pad(-jnp.exp(ssd_a_log[l].astype(F32)), (MISC_DT, TILE - MISC_DT - SSD_H)).reshape(1, TILE),
        "ssd_dsk": jnp.repeat(ssd_d[l].astype(F32), SSD_P).reshape(1, MIX),
        "ssd_nw": row(ssd_norm[l]),
    }
    return pp


def _layer_prompt(x2d, nb, nt_len, wts, l, pp, tm_proj=1024, tm_out=512, tm_mlp=512, tf=1024):
    proj, wo_bf = _proj(x2d, pp["n_mix_pre"], wts["w_in"], l, tm_proj, (wts["w_out"],))
    o_gla, s_gla = _gla_prompt(proj, nb, nt_len, pp["gla_wg2"], pp["gla_bg"], pp["gla_nw"], pp["gla_ones"])
    o_lru, h_lru, lru_tail = _lru_prompt(proj, nb, nt_len, pp["lru_cw"], pp["lru_cb"], pp["lru_wa"], pp["lru_ba"],
                                         pp["lru_wi"], pp["lru_bi"], pp["lru_lam"])
    o_rwkv, s_rwkv, rwkv_tail, wu_bf = _rwkv_prompt(proj, nb, nt_len, pp["rwkv"], wts["w_up"], l)
    o_ssd, s_ssd, ssd_tail = _ssd_prompt(proj, nb, nt_len, pp["ssd_cw"], pp["ssd_cb"], pp["ssd_dtb"],
                                         pp["ssd_aneg"], pp["ssd_dsk"], pp["ssd_nw"])
    x1, h2, wd_bf = _outproj((o_gla, o_lru, o_rwkv, o_ssd), wo_bf, x2d, pp["n_mix_post"], pp["n_mlp_pre"], tm_out,
                             wts["w_down"], l)
    x2 = _mlp(h2, wu_bf, wd_bf, x1, pp["n_mlp_post"], tm_mlp, tf)
    keep = SUBLANES - (CONV_W - 1)
    states = (s_gla, h_lru[:, 0, :], lru_tail[:, keep:, :], rwkv_tail[:, SUBLANES - 1, :], s_rwkv,
              ssd_tail[:, keep:, :], s_ssd)
    return x2, states, (wu_bf, wd_bf, wo_bf)


def _layer_decode(x2d, big, small, wts, mlp_w, l, pp, prev, tf=1024):
    sg, sr_t, ss = big
    lh, lconv, rprev, sconv = small
    n = x2d.shape[0]
    proj = _proj(x2d, pp["n_mix_pre"], wts["w_in"], l, n)
    rows, colt, rwt, o_lru, lh_new, lconv_new, sconv_new = _dec_prep(
        proj, lconv.reshape(n, -1), lh, rprev, sconv.reshape(n, -1), pp)
    sr_new, yrt = _dec_rwkv(rwt, sr_t, l, None if prev is None else prev[1])
    sg_new, ss_new, yg, ys = _dec_state(colt, rows, sg, ss, l, None if prev is None else (prev[0], prev[2]))
    o_gla, o_rwkv, o_ssd = _dec_finish(rows, yg, yrt, ys, pp)
    x1, h2 = _outproj((o_gla, o_lru, o_rwkv, o_ssd), mlp_w[2], x2d, pp["n_mix_post"], pp["n_mlp_pre"], n)
    x2 = _mlp(h2, mlp_w[0], mlp_w[1], x1, pp["n_mlp_post"], n, tf)
    rshift_new = jnp.transpose(proj[T_RWKV:T_RWKV + GROUP_TILES], (1, 0, 2)).reshape(n, RWKV_COLS)
    small_new = (lh_new, lconv_new.reshape(n, CONV_W - 1, MIX), rshift_new,
                 sconv_new.reshape(n, CONV_W - 1, SSD_CONV_DIM))
    return x2, (sg_new, sr_new, ss_new), small_new


def kernel(x_prompt, x_sample, state_gla, state_lru, cache_lru_conv, cache_rwkv_shift, state_rwkv, cache_ssd_conv, state_ssd, norm_mix_pre, norm_mix_post, norm_mlp_pre, norm_mlp_post, w_in, w_out, w_up, w_down, gla_w_gate2, gla_b_gate, gla_norm, lru_conv_w, lru_conv_b, lru_w_a, lru_b_a, lru_w_i, lru_b_i, lru_lambda, rwkv_mu, rwkv_w0, rwkv_w2, rwkv_a0, rwkv_a2, rwkv_g2, rwkv_k_k, rwkv_k_a, rwkv_r_k, rwkv_ln_w, rwkv_ln_b, ssd_conv_w, ssd_conv_b, ssd_dt_bias, ssd_a_log, ssd_d, ssd_norm):
    nb, nt_len, _ = x_prompt.shape
    nd = x_sample.shape[0]
    yp = x_prompt.reshape(nb * nt_len, D_MODEL)
    ys = x_sample.reshape(nd, D_MODEL)
    wts = _prep_weights(w_in, w_out, w_up, w_down)
    big = (state_gla, jnp.transpose(state_rwkv, (0, 2, 3, 4, 1)), state_ssd)
    new_p, small_s, big_s = [], [], None
    for l in range(DEPTH):
        pp = _prep_layer(l, norm_mix_pre, norm_mix_post, norm_mlp_pre, norm_mlp_post,
                         gla_w_gate2, gla_b_gate, gla_norm, lru_conv_w, lru_conv_b, lru_w_a, lru_b_a, lru_w_i,
                         lru_b_i, lru_lambda, rwkv_mu, rwkv_w0, rwkv_w2, rwkv_a0, rwkv_a2, rwkv_g2, rwkv_k_k,
                         rwkv_k_a, rwkv_r_k, rwkv_ln_w, rwkv_ln_b, ssd_conv_w, ssd_conv_b, ssd_dt_bias, ssd_a_log,
                         ssd_d, ssd_norm)
        yp, st_p, mlp_w = _layer_prompt(yp, nb, nt_len, wts, l, pp)
        small = (state_lru[l], cache_lru_conv[l], cache_rwkv_shift[l], cache_ssd_conv[l])
        ys, big_s, sm = _layer_decode(ys, big, small, wts, mlp_w, l, pp, big_s)
        new_p.append(st_p)
        small_s.append(sm)

    def stack(sts, i):
        return jnp.stack([s[i] for s in sts], axis=0)

    gla_s, rwkv_s, ssd_s = big_s
    rwkv_s = jnp.transpose(rwkv_s, (0, 4, 1, 2, 3))
    return (yp.reshape(nb, nt_len, D_MODEL), ys.reshape(nd, 1, D_MODEL),
            stack(new_p, 0), gla_s, stack(new_p, 1), stack(small_s, 0),
            stack(new_p, 2), stack(small_s, 1), stack(new_p, 3), stack(small_s, 2),
            stack(new_p, 4), rwkv_s, stack(new_p, 5), stack(small_s, 3),
            stack(new_p, 6), ssd_s)
```

```python
import functools
import math

import jax
import jax.numpy as jnp
from jax import lax
from jax.experimental import pallas as pl
from jax.experimental.pallas import tpu as pltpu

F32 = jnp.float32
BF16 = jnp.bfloat16

D_MODEL = 2048
D_FF = 4 * D_MODEL
DEPTH = 2
EPS = 1e-6
MIX = D_MODEL // 4
CONV_W = 4
GLA_H, GLA_DK, GLA_DV, GLA_RANK, GLA_TEMP, GLA_C = 4, 64, 128, 16, 16.0, 16
LRU_C = 8.0
RWKV_H, RWKV_D = 8, 64
RWKV_DECAY = math.exp(-0.5)
RWKV_LN_EPS = 64e-5
RWKV_C = 32
SSD_H, SSD_P, SSD_G, SSD_N, SSD_C = 8, 64, 2, 128, 64
GLA_COLS = 2 * GLA_H * GLA_DK + 2 * MIX + GLA_RANK
LRU_COLS = 2 * MIX
RWKV_COLS = 3 * MIX + 64 + 64 + 128
SSD_CONV_DIM = MIX + 2 * SSD_G * SSD_N
SSD_COLS = MIX + SSD_CONV_DIM + SSD_H

LANES = 128
SUBLANES = 8
TILE = 2 * LANES
GROUP_TILES = 7
WIDE_TILES = 6
T_RWKV, T_LRU, T_MISC, T_GLA, T_SSD = 0, 7, 11, 12, 18
N_TILES = 24
MISC_DT = GLA_RANK
VMEM_LIMIT = 56 * 1024 * 1024

NN = (((1,), (0,)), ((), ()))
NT = (((1,), (1,)), ((), ()))
TN = (((0,), (0,)), ((), ()))


def _split(x, n):
    if x.dtype == BF16:
        return [x]
    parts, r = [], x
    for i in range(n):
        p = r.astype(BF16)
        parts.append(p)
        if i + 1 < n:
            r = r - p.astype(F32)
    return parts


def _mm(a, b, dn=NN, pa=1, pb=1):
    aa, bb = _split(a, pa), _split(b, pb)
    acc = None
    for i, x in enumerate(aa):
        for j, y in enumerate(bb):
            if i + j >= max(len(aa), len(bb)):
                continue
            t = lax.dot_general(x, y, dn, preferred_element_type=F32)
            acc = t if acc is None else acc + t
    return acc


def _iota(shape, dim):
    return lax.broadcasted_iota(jnp.int32, shape, dim)


def _roll0(x, s):
    n = x.shape[0]
    s = s % n
    return x if s == 0 else pltpu.roll(x, s, 0)


def _rms(x, w):
    ms = jnp.mean(x * x, axis=-1, keepdims=True)
    return x * lax.rsqrt(ms + EPS) * w


def _sigmoid(x):
    return jax.nn.sigmoid(x)


def _silu(x):
    return x * jax.nn.sigmoid(x)


def _softplus(x):
    return jnp.maximum(x, 0.0) + jnp.log1p(jnp.exp(-jnp.abs(x)))


def _log_sigmoid(x):
    return jnp.minimum(x, 0.0) - jnp.log1p(jnp.exp(-jnp.abs(x)))


def _gelu_tanh(x):
    c = math.sqrt(2.0 / math.pi)
    return x * (0.5 * (1.0 + jnp.tanh(c * (x + 0.044715 * (x * x * x)))))


def _neg_expm1(x):
    return -jnp.tanh(0.5 * x) * (jnp.exp(x) + 1.0)


def _chunk_cumsum(x, chunk, pos):
    d = 1
    while d < chunk:
        x = x + jnp.where(pos >= d, _roll0(x, d), 0.0)
        d *= 2
    return x


def _chunk_last_bcast(x, chunk, pos):
    n = x.shape[0]
    y = jnp.where(pos == chunk - 1, x, 0.0)
    d = 1
    while d < chunk:
        y = y + _roll0(y, n - d)
        d *= 2
    return y


def _shift_rows(x, carry, i, row8):
    xs = _roll0(x, i)
    cs = _roll0(carry, i)
    top = jnp.where(row8 < i, cs, xs[:SUBLANES])
    return jnp.concatenate([top, xs[SUBLANES:]], axis=0)


def _causal_conv(x, carry, w_ref, b_ref):
    row8 = _iota((SUBLANES, x.shape[1]), 0)
    y = b_ref[...] + x * w_ref[CONV_W - 1:CONV_W, :]
    for i in range(1, CONV_W):
        y = y + _shift_rows(x, carry, i, row8) * w_ref[CONV_W - 1 - i:CONV_W - i, :]
    return y


def _params(*sem):
    return pltpu.CompilerParams(dimension_semantics=sem, vmem_limit_bytes=VMEM_LIMIT)


def _const_spec(shape):
    nd = len(shape)
    return pl.BlockSpec(shape, lambda *_: (0,) * nd)


PROJ_TILES_PER_STEP = 6


MLP_CAST_BLOCKS = 32


def _proj_kernel(x_ref, nw_ref, w_ref, *rest):
    ncast = (len(rest) - 2) // 2
    o_ref, h_ref = rest[ncast], rest[-1]
    for src, dst in zip(rest[:ncast], rest[ncast + 1:-1]):
        dst[...] = src[...].astype(BF16)

    @pl.when(pl.program_id(1) == 0)
    def _():
        h_ref[...] = _rms(x_ref[...], nw_ref[...]).astype(BF16)

    res = lax.dot_general(h_ref[...], w_ref[...], NT, preferred_element_type=F32)
    for j in range(PROJ_TILES_PER_STEP):
        o_ref[j] = res[:, j * TILE:(j + 1) * TILE]


def _proj(x2d, nw, w, l, tm, cast_w=None):
    m = x2d.shape[0]
    tn = PROJ_TILES_PER_STEP * TILE
    nj = N_TILES // PROJ_TILES_PER_STEP
    in_specs = [pl.BlockSpec((tm, D_MODEL), lambda i, j: (i, 0)),
                _const_spec((1, D_MODEL)),
                pl.BlockSpec((None, tn, D_MODEL), lambda i, j: (l, j, 0))]
    out_specs = [pl.BlockSpec((PROJ_TILES_PER_STEP, tm, TILE), lambda i, j: (j, i, 0))]
    out_shape = [jax.ShapeDtypeStruct((N_TILES, m, TILE), F32)]
    args = [x2d, nw, w]
    if cast_w is not None:
        nblk = MLP_CAST_BLOCKS
        while nblk > (m // tm) * nj:
            nblk //= 2
        blk = lambda i, j: jnp.minimum(i * nj + j, nblk - 1)
        for cw in cast_w:
            rows, cols = cw.shape[1] // nblk, cw.shape[2]
            in_specs.append(pl.BlockSpec((None, rows, cols), lambda i, j: (l, blk(i, j), 0)))
            out_specs.append(pl.BlockSpec((rows, cols), lambda i, j: (blk(i, j), 0)))
            out_shape.append(jax.ShapeDtypeStruct(cw.shape[1:], BF16))
        args += list(cast_w)
    res = pl.pallas_call(
        _proj_kernel,
        grid=(m // tm, nj),
        in_specs=in_specs,
        out_specs=out_specs,
        out_shape=out_shape,
        scratch_shapes=[pltpu.VMEM((tm, D_MODEL), BF16)],
        compiler_params=_params("arbitrary", "arbitrary"),
        name="norm_proj",
    )(*args)
    return res[0] if cast_w is None else res


OUTPROJ_SUB_ROWS = 128

def _outproj_kernel(m0, m1, m2, m3, w_ref, x_ref, n1_ref, n2_ref, *rest):
    if len(rest) == 2:
        x1_ref, h2_ref = rest
    else:
        wd_ref, x1_ref, h2_ref, wd_out = rest
        wd_out[...] = wd_ref[...].astype(BF16)
    tm = x_ref.shape[0]
    sub = min(tm, OUTPROJ_SUB_ROWS)
    for r0 in range(0, tm, sub):
        rs = slice(r0, r0 + sub)
        mix = jnp.concatenate([m0[rs, :], m1[rs, :], m2[rs, :], m3[rs, :]], axis=1)
        acc = jnp.dot(mix, w_ref[...], preferred_element_type=F32)
        x1 = x_ref[rs, :] + _rms(acc, n1_ref[...])
        x1_ref[rs, :] = x1
        h2_ref[rs, :] = _rms(x1, n2_ref[...]).astype(BF16)


def _outproj(mix, w_out, x2d, n1, n2, tm, cast_w=None, l=0):
    m = x2d.shape[0]
    row = lambda i: (i, 0)
    in_specs = [pl.BlockSpec((tm, MIX), row)] * 4 + [
        _const_spec((D_MODEL, D_MODEL)), pl.BlockSpec((tm, D_MODEL), row),
        _const_spec((1, D_MODEL)), _const_spec((1, D_MODEL))]
    out_specs = [pl.BlockSpec((tm, D_MODEL), row), pl.BlockSpec((tm, D_MODEL), row)]
    out_shape = [jax.ShapeDtypeStruct((m, D_MODEL), F32), jax.ShapeDtypeStruct((m, D_MODEL), BF16)]
    args = [*mix, w_out, x2d, n1, n2]
    if cast_w is not None:
        nblk = m // tm
        rows, cols = cast_w.shape[1] // nblk, cast_w.shape[2]
        in_specs.append(pl.BlockSpec((None, rows, cols), lambda i: (l, i, 0)))
        out_specs.append(pl.BlockSpec((rows, cols), row))
        out_shape.append(jax.ShapeDtypeStruct(cast_w.shape[1:], BF16))
        args.append(cast_w)
    return pl.pallas_call(
        _outproj_kernel,
        grid=(m // tm,),
        in_specs=in_specs,
        out_specs=out_specs,
        out_shape=out_shape,
        compiler_params=_params("parallel"),
        name="out_proj",
    )(*args)


def _mlp_kernel(h_ref, wu_ref, wd_ref, x1_ref, nw_ref, o_ref, acc_ref):
    f = pl.program_id(1)

    @pl.when(f == 0)
    def _():
        acc_ref[...] = jnp.zeros_like(acc_ref)

    u = jnp.maximum(jnp.dot(h_ref[...], wu_ref[...], preferred_element_type=F32), 0.0)
    acc_ref[...] += jnp.dot((u * u).astype(BF16), wd_ref[...], preferred_element_type=F32)

    @pl.when(f == pl.num_programs(1) - 1)
    def _():
        o_ref[...] = x1_ref[...] + _rms(acc_ref[...], nw_ref[...])


def _mlp(h2, w_up, w_down, x1, nw, tm, tf):
    m = h2.shape[0]
    return pl.pallas_call(
        _mlp_kernel,
        grid=(m // tm, D_FF // tf),
        in_specs=[pl.BlockSpec((tm, D_MODEL), lambda i, f: (i, 0)),
                  pl.BlockSpec((D_MODEL, tf), lambda i, f: (0, f)),
                  pl.BlockSpec((tf, D_MODEL), lambda i, f: (f, 0)),
                  pl.BlockSpec((tm, D_MODEL), lambda i, f: (i, 0)),
                  _const_spec((1, D_MODEL))],
        out_specs=pl.BlockSpec((tm, D_MODEL), lambda i, f: (i, 0)),
        out_shape=jax.ShapeDtypeStruct((m, D_MODEL), F32),
        scratch_shapes=[pltpu.VMEM((tm, D_MODEL), F32)],
        compiler_params=_params("parallel", "arbitrary"),
        name="mlp",
    )(h2, w_up, w_down, x1, nw)


GLA_TB = 512


def _gla_gates(ad, wg2_ref, bg_ref):
    x = _mm(ad, wg2_ref[...], pa=2) + bg_ref[...]
    return _log_sigmoid(x) * (1.0 / GLA_TEMP)


def _gla_kernel(z_ref, misc_ref, wg2_ref, bg_ref, nw_ref, ones_ref, o_ref, s_ref, st_ref, y_ref):
    tb = GLA_TB
    t = pl.program_id(1)

    @pl.when(t == 0)
    def _():
        st_ref[...] = jnp.zeros_like(st_ref)

    q = z_ref[0] * (GLA_DK ** -0.5)
    k = z_ref[1]
    v = jnp.concatenate([z_ref[2], z_ref[3]], axis=1)
    gate = jnp.concatenate([z_ref[4], z_ref[5]], axis=1)
    g = _gla_gates(misc_ref[...], wg2_ref, bg_ref)
    pos = _iota((tb, TILE), 0) & (GLA_C - 1)
    cum = _chunk_cumsum(g, GLA_C, pos)

    y = None
    for j in range(GLA_C):
        if j == 0:
            term = q * k
            vj = v
        else:
            e = jnp.exp(jnp.where(pos >= j, cum - _roll0(cum, j), -jnp.inf))
            term = q * _roll0(k, j) * e
            vj = _roll0(v, j)
        sc = _mm(term, ones_ref[...])
        y = sc * vj if y is None else y + sc * vj
    y_ref[...] = y

    qh = q * jnp.exp(cum)
    last = _chunk_last_bcast(cum, GLA_C, pos)
    kt = k * jnp.exp(last - cum)
    dec = jnp.exp(last)
    kss = [slice(h * GLA_DK, (h + 1) * GLA_DK) for h in range(GLA_H)]
    vss = [slice(h * GLA_DV, (h + 1) * GLA_DV) for h in range(GLA_H)]
    nch = tb // GLA_C
    ds = [[_mm(v[c * GLA_C:(c + 1) * GLA_C, vss[h]], kt[c * GLA_C:(c + 1) * GLA_C, kss[h]], TN)
           for h in range(GLA_H)] for c in range(nch)]
    st = [st_ref[h] for h in range(GLA_H)]
    for c in range(nch):
        rows = slice(c * GLA_C, (c + 1) * GLA_C)
        for h in range(GLA_H):
            y_ref[rows, vss[h]] += _mm(qh[rows, kss[h]], st[h], NT)
            st[h] = st[h] * dec[c * GLA_C:c * GLA_C + 1, kss[h]] + ds[c][h]
    for h in range(GLA_H):
        st_ref[h] = st[h]

    yy = y_ref[...]
    outs = []
    for h in range(GLA_H):
        vs = slice(h * GLA_DV, (h + 1) * GLA_DV)
        outs.append(_rms(yy[:, vs], nw_ref[...]))
    o_ref[...] = (jnp.concatenate(outs, axis=1) * _silu(gate)).astype(BF16)

    @pl.when(t == pl.num_programs(1) - 1)
    def _():
        for h in range(GLA_H):
            s_ref[0, h] = st_ref[h].T


def _gla_prompt(proj, nb, nt_len, wg2, bg, nw, ones):
    tb = GLA_TB
    nt = nt_len // tb
    return pl.pallas_call(
        _gla_kernel,
        grid=(nb, nt),
        in_specs=[pl.BlockSpec((WIDE_TILES, tb, TILE), lambda b, t: (T_GLA // WIDE_TILES, b * nt + t, 0)),
                  pl.BlockSpec((None, tb, TILE), lambda b, t: (T_MISC, b * nt + t, 0)),
                  _const_spec((TILE, TILE)), _const_spec((1, TILE)), _const_spec((1, GLA_DV)),
                  _const_spec((TILE, MIX))],
        out_specs=[pl.BlockSpec((tb, MIX), lambda b, t: (b * nt + t, 0)),
                   pl.BlockSpec((1, GLA_H, GLA_DK, GLA_DV), lambda b, t: (b, 0, 0, 0))],
        out_shape=[jax.ShapeDtypeStruct((nb * nt_len, MIX), BF16),
                   jax.ShapeDtypeStruct((nb, GLA_H, GLA_DK, GLA_DV), F32)],
        scratch_shapes=[pltpu.VMEM((GLA_H, GLA_DV, GLA_DK), F32), pltpu.VMEM((tb, MIX), F32)],
        compiler_params=_params("parallel", "arbitrary"),
        name="gla_prompt",
    )(proj, proj, wg2, bg, nw, ones)


LRU_TB = 256


def _lru_gates(xc, wa_ref, ba_ref, wi_ref, bi_ref, lam_ref):
    ra, ri = [], []
    for j in range(2):
        xs = xc[:, j * TILE:(j + 1) * TILE]
        ra.append(_mm(xs, wa_ref[j], pa=2))
        ri.append(_mm(xs, wi_ref[j], pa=2))
    r = _sigmoid(jnp.concatenate(ra, axis=1) + ba_ref[...])
    i = _sigmoid(jnp.concatenate(ri, axis=1) + bi_ref[...])
    log_a = -LRU_C * r * _softplus(-lam_ref[...])
    a = jnp.exp(log_a)
    mult = jnp.sqrt(_neg_expm1(2.0 * log_a))
    return a, mult, i


def _lru_kernel(x0_ref, x1_ref, g0_ref, g1_ref, cw_ref, cb_ref, wa_ref, ba_ref, wi_ref, bi_ref, lam_ref,
                o_ref, h_ref, tail_ref, carry_ref, hc_ref):
    tb = LRU_TB
    t = pl.program_id(1)

    @pl.when(t == 0)
    def _():
        carry_ref[...] = jnp.zeros_like(carry_ref)
        hc_ref[...] = jnp.zeros_like(hc_ref)

    xb = jnp.concatenate([x0_ref[...], x1_ref[...]], axis=1)
    gb = jnp.concatenate([g0_ref[...], g1_ref[...]], axis=1)
    xc = _causal_conv(xb, carry_ref[...], cw_ref, cb_ref)
    carry_ref[...] = xb[tb - SUBLANES:]
    a, mult, i = _lru_gates(xc, wa_ref, ba_ref, wi_ref, bi_ref, lam_ref)
    row = _iota((tb, MIX), 0)
    mult = jnp.where((row == 0) & (t == 0), 1.0, mult)
    b = mult * i * xc
    pos8 = row & (SUBLANES - 1)

    def roll8(x, d):
        return pltpu.roll(x.reshape(tb // SUBLANES, SUBLANES, MIX), d, 1).reshape(tb, MIX)

    d = 1
    while d < SUBLANES:
        m = pos8 >= d
        b = jnp.where(m, a * roll8(b, d) + b, b)
        a = jnp.where(m, a * roll8(a, d), a)
        d *= 2
    hl = hc_ref[0:1, :]
    groups = []
    for gidx in range(tb // SUBLANES):
        rs = slice(gidx * SUBLANES, (gidx + 1) * SUBLANES)
        hg = b[rs] + a[rs] * hl
        groups.append(hg)
        hl = hg[SUBLANES - 1:SUBLANES, :]
    h = jnp.concatenate(groups, axis=0)
    hc_ref[...] = jnp.broadcast_to(hl, hc_ref.shape)
    o_ref[...] = (h * _gelu_tanh(gb)).astype(BF16)
    h_ref[0] = hl
    tail_ref[0] = xb[tb - SUBLANES:]


def _lru_prompt(proj, nb, nt_len, cw, cb, wa, ba, wi, bi, lam):
    tb = LRU_TB
    nt = nt_len // tb

    def tile(j):
        return pl.BlockSpec((None, tb, TILE), lambda b, t: (T_LRU + j, b * nt + t, 0))

    return pl.pallas_call(
        _lru_kernel,
        grid=(nb, nt),
        in_specs=[tile(0), tile(1), tile(2), tile(3),
                  _const_spec((CONV_W, MIX)), _const_spec((1, MIX)),
                  _const_spec((2, TILE, TILE)), _const_spec((1, MIX)),
                  _const_spec((2, TILE, TILE)), _const_spec((1, MIX)), _const_spec((1, MIX))],
        out_specs=[pl.BlockSpec((tb, MIX), lambda b, t: (b * nt + t, 0)),
                   pl.BlockSpec((1, 1, MIX), lambda b, t: (b, 0, 0)),
                   pl.BlockSpec((1, SUBLANES, MIX), lambda b, t: (b, 0, 0))],
        out_shape=[jax.ShapeDtypeStruct((nb * nt_len, MIX), BF16), jax.ShapeDtypeStruct((nb, 1, MIX), F32),
                   jax.ShapeDtypeStruct((nb, SUBLANES, MIX), F32)],
        scratch_shapes=[pltpu.VMEM((SUBLANES, MIX), F32), pltpu.VMEM((SUBLANES, MIX), F32)],
        compiler_params=_params("parallel", "arbitrary"),
        name="lru_prompt",
    )(proj, proj, proj, proj, cw, cb, wa, ba, wi, bi, lam)


RWKV_TB = 512
RWKV_BS = 128
RWKV_P = {"g": (1, 1), "neu": (1, 1), "app": (1, 1), "chunk": (1, 1)}


def _seg_sum(x, seg_ref):
    outs = []
    for j in range(x.shape[1] // LANES):
        outs.append(_mm(x[:, j * LANES:(j + 1) * LANES], seg_ref[...], pa=2))
    return jnp.concatenate(outs, axis=1)


def _rwkv_pointwise(z, zs, mu_ref, w0_ref, w2_ref, a0_ref, a2_ref, g2_ref, kk_ref, ka_ref, seg_ref):
    zm = z + (zs - z) * mu_ref[...]
    r = zm[:, 0:MIX]
    k = zm[:, MIX:2 * MIX]
    v = zm[:, 2 * MIX:3 * MIX]
    zw = zm[:, 3 * MIX:3 * MIX + 64]
    za = zm[:, 3 * MIX + 64:3 * MIX + 128]
    zg = zm[:, 3 * MIX + 128:3 * MIX + 256]
    lw = -RWKV_DECAY * _sigmoid(w0_ref[...] + _mm(jnp.tanh(zw), w2_ref[...], pa=2))
    a = _sigmoid(a0_ref[...] + _mm(za, a2_ref[...], pa=2))
    g = _mm(_sigmoid(zg), g2_ref[...], pa=2)
    kk = k * kk_ref[...]
    kk = kk / jnp.maximum(jnp.sqrt(_seg_sum(kk * kk, seg_ref)), 1e-12)
    k = k * (1.0 + (a - 1.0) * ka_ref[...])
    return r, lw, k, v, kk, a, g


def _rwkv_finish(y, r, k, v, g, rk_ref, lnw_ref, lnb_ref, seg_ref):
    mean = _seg_sum(y, seg_ref) * (1.0 / RWKV_D)
    yc = y - mean
    var = _seg_sum(yc * yc, seg_ref) * (1.0 / RWKV_D)
    yn = yc * lax.rsqrt(var + RWKV_LN_EPS) * lnw_ref[...] + lnb_ref[...]
    bonus = _seg_sum(r * k * rk_ref[...], seg_ref) * v
    return (yn + bonus) * g


def _rwkv_kernel(z_ref, mu_ref, w0_ref, w2_ref, a0_ref, a2_ref, g2_ref, kk_ref, ka_ref, rk_ref, lnw_ref, lnb_ref,
                 seg_ref, wsrc_ref, o_ref, s_ref, tail_ref, wdst_ref, prev_ref, st_ref, y_ref, scat_ref):
    wdst_ref[...] = wsrc_ref[...].astype(BF16)
    tb = RWKV_TB
    cc = RWKV_C
    t = pl.program_id(1)

    @pl.when(t == 0)
    def _():
        prev_ref[...] = jnp.zeros_like(prev_ref)
        st_ref[...] = jnp.zeros_like(st_ref)

    z = jnp.concatenate([z_ref[j] for j in range(GROUP_TILES)], axis=1)
    zs = _shift_rows(z, prev_ref[...], 1, _iota((SUBLANES, RWKV_COLS), 0))
    prev_ref[...] = z[tb - SUBLANES:]
    tail_ref[0] = z[tb - SUBLANES:]
    r, lw, k, v, kk, a, g = _rwkv_pointwise(z, zs, mu_ref, w0_ref, w2_ref, a0_ref, a2_ref, g2_ref, kk_ref, ka_ref,
                                            seg_ref)
    b = kk * a

    pos = _iota((tb, MIX), 0) & (cc - 1)
    lg = _chunk_cumsum(lw, cc, pos)
    lg_end = _chunk_last_bcast(lg, cc, pos)
    gam = jnp.exp(lg)
    inv = jnp.exp(-lg)
    rg = r * gam
    kg = kk * jnp.exp(lg - lw)
    bi = b * inv
    ki = k * inv
    e_end = jnp.exp(lg_end - lg)
    bt = b * e_end
    kt = k * e_end
    g_end = jnp.exp(lg_end)

    bs = RWKV_BS
    ri = _iota((bs, bs), 0)
    ci = _iota((bs, bs), 1)
    same = (ri - (ri & (cc - 1))) == (ci - (ci & (cc - 1)))
    strict = same & (ci < ri)
    incl = same & (ci <= ri)
    eye = (ri == ci).astype(F32)
    eye_d = _iota((RWKV_D, RWKV_D), 0) == _iota((RWKV_D, RWKV_D), 1)

    hs = [slice(h * RWKV_D, (h + 1) * RWKV_D) for h in range(RWKV_H)]
    mg, mn, ma, mc_ = RWKV_P["g"], RWKV_P["neu"], RWKV_P["app"], RWKV_P["chunk"]
    units = [(slice(q * bs, (q + 1) * bs), h) for q in range(tb // bs) for h in range(RWKV_H)]
    vh = [v[rb, hs[h]] for rb, h in units]
    gmat = [_mm(jnp.concatenate([kg[rb, hs[h]], rg[rb, hs[h]]], axis=0),
                jnp.concatenate([bi[rb, hs[h]], ki[rb, hs[h]]], axis=0), NT, *mg) for rb, h in units]
    amat = [jnp.where(strict, gm[:bs, :bs], 0.0) for gm in gmat]
    bmat = [jnp.where(strict, gm[:bs, bs:], 0.0) for gm in gmat]
    pb_ = [jnp.where(incl, gm[bs:, :bs], 0.0) for gm in gmat]
    pk_ = [jnp.where(incl, gm[bs:, bs:], 0.0) for gm in gmat]
    bv = [_mm(bm_, v_, NN, *ma) for bm_, v_ in zip(bmat, vh)]

    def lower_left(b):
        rb = ri & (b - 1)
        cb = ci & (b - 1)
        return ((ri - rb) == (ci - cb)) & (rb >= b // 2) & (cb < b // 2)

    x = [eye - jnp.where(lower_left(2), a_, 0.0) for a_ in amat]
    b = 4
    while b <= cc:
        m = lower_left(b)
        tx = [_mm(x_, jnp.where(m, a_, 0.0), NN, *mn) for x_, a_ in zip(x, amat)]
        x = [x_ - _mm(tx_, x_, NN, *mn) for x_, tx_ in zip(x, tx)]
        b *= 2
    wu = [-_mm(x_, jnp.concatenate([kg[rb, hs[h]], bv_], axis=1), NN, *ma)
          for x_, (rb, h), bv_ in zip(x, units, bv)]
    low = _iota((bs, LANES), 1) < RWKV_D
    zero = jnp.zeros((bs, LANES), F32)

    def dup(slab, odd):
        sw = pltpu.roll(slab, RWKV_D, 1)
        return jnp.where(low, sw, slab) if odd else jnp.where(low, slab, sw)

    def upper(slab, odd):
        return jnp.where(low, zero, slab if odd else pltpu.roll(slab, RWKV_D, 1))

    pair = [slice((h // 2) * LANES, (h // 2 + 1) * LANES) for h in range(RWKV_H)]
    rg2 = [dup(rg[rb, pair[h]], h % 2) for rb, h in units]
    ov = [upper(v[rb, pair[h]], h % 2) for rb, h in units]
    ww = [jnp.where(low, wu_, pltpu.roll(wu_, RWKV_D, 1)) for wu_ in wu]
    qy = [_mm(pb, jnp.concatenate([ww_, wu_], axis=1), NN, *ma) for pb, ww_, wu_ in zip(pb_, ww, wu)]
    pkv = [_mm(pk, v_, NN, *ma) for pk, v_ in zip(pk_, vh)]
    qt2 = [rg2_ + qy_[:, :LANES] for rg2_, qy_ in zip(rg2, qy)]
    y0 = [qy_[:, LANES + RWKV_D:] + pkv_ for qy_, pkv_ in zip(qy, pkv)]
    nch = tb // cc
    cpb = bs // cc
    s = [st_ref[h] for h in range(RWKV_H)]
    for c in range(nch):
        rows = slice(c * cc, (c + 1) * cc)
        lrows = slice((c % cpb) * cc, (c % cpb + 1) * cc)
        u0 = (c // cpb) * RWKV_H
        for h in range(RWKV_H):
            scat_ref[h, :, c * RWKV_D:(c + 1) * RWKV_D] = s[h]
        tr = [_mm(jnp.concatenate([wu[u0 + h][lrows], ov[u0 + h][lrows]], axis=0),
                  jnp.concatenate([bt[rows, hs[h]], kt[rows, hs[h]]], axis=0), TN, *mc_) for h in range(RWKV_H)]
        s = [_mm(s_, jnp.where(eye_d, g_end[c * cc:c * cc + 1, ls], 0.0) + t_[:RWKV_D], NN, *mc_) + t_[RWKV_D:]
             for s_, t_, ls in zip(s, tr, hs)]
    r5 = _iota((bs, cpb * RWKV_D), 0)
    c5 = _iota((bs, cpb * RWKV_D), 1)
    own = (r5 - (r5 & (cc - 1))) * (RWKV_D // cc) == c5 - (c5 & (RWKV_D - 1))
    for h in range(RWKV_H):
        st_ref[h] = s[h]
    for u, (rb, h) in enumerate(units):
        q = u // RWKV_H
        qexp = jnp.where(own, jnp.concatenate([qt2[u]] * (cpb * RWKV_D // LANES), axis=1), 0.0)
        y_ref[rb, hs[h]] = _mm(qexp, scat_ref[h, :, q * cpb * RWKV_D:(q + 1) * cpb * RWKV_D], NT, *mc_) + y0[u]

    o_ref[...] = _rwkv_finish(y_ref[...], r, k, v, g, rk_ref, lnw_ref, lnb_ref, seg_ref).astype(BF16)

    @pl.when(t == pl.num_programs(1) - 1)
    def _():
        s_ref[0] = st_ref[...]


def _rwkv_prompt(proj, nb, nt_len, p, cast_w, l):
    tb = RWKV_TB
    nt = nt_len // tb
    crow, ccol = cast_w.shape[1] // (nb * nt), cast_w.shape[2]
    return pl.pallas_call(
        _rwkv_kernel,
        grid=(nb, nt),
        in_specs=[pl.BlockSpec((GROUP_TILES, tb, TILE), lambda b, t: (T_RWKV // GROUP_TILES, b * nt + t, 0)),
                  _const_spec((1, RWKV_COLS)), _const_spec((1, MIX)), _const_spec((64, MIX)),
                  _const_spec((1, MIX)), _const_spec((64, MIX)), _const_spec((128, MIX)),
                  _const_spec((1, MIX)), _const_spec((1, MIX)), _const_spec((1, MIX)),
                  _const_spec((1, MIX)), _const_spec((1, MIX)), _const_spec((LANES, LANES)),
                  pl.BlockSpec((None, crow, ccol), lambda b, t: (l, b * nt + t, 0))],
        out_specs=[pl.BlockSpec((tb, MIX), lambda b, t: (b * nt + t, 0)),
                   pl.BlockSpec((1, RWKV_H, RWKV_D, RWKV_D), lambda b, t: (b, 0, 0, 0)),
                   pl.BlockSpec((1, SUBLANES, RWKV_COLS), lambda b, t: (b, 0, 0)),
                   pl.BlockSpec((crow, ccol), lambda b, t: (b * nt + t, 0))],
        out_shape=[jax.ShapeDtypeStruct((nb * nt_len, MIX), BF16),
                   jax.ShapeDtypeStruct((nb, RWKV_H, RWKV_D, RWKV_D), F32),
                   jax.ShapeDtypeStruct((nb, SUBLANES, RWKV_COLS), F32),
                   jax.ShapeDtypeStruct(cast_w.shape[1:], BF16)],
        scratch_shapes=[pltpu.VMEM((SUBLANES, RWKV_COLS), F32), pltpu.VMEM((RWKV_H, RWKV_D, RWKV_D), F32),
                        pltpu.VMEM((tb, MIX), F32), pltpu.VMEM((RWKV_H, RWKV_D, (tb // RWKV_C) * RWKV_D), F32)],
        compiler_params=_params("parallel", "arbitrary"),
        name="rwkv_prompt",
    )(proj, p["mu"], p["w0"], p["w2"], p["a0"], p["a2"], p["g2"], p["kk"], p["ka"], p["rk"], p["lnw"], p["lnb"],
      p["seg"], cast_w)


SSD_TB = 512


def _ssd_pointwise(xbc_c, dt_raw, dtb_ref, aneg_ref):
    xbc = _silu(xbc_c)
    dt = _softplus(dt_raw + dtb_ref[...])
    return xbc, dt, dt * aneg_ref[...]


def _expand_heads(misc, width):
    n = misc.shape[0]
    return jnp.concatenate([jnp.broadcast_to(misc[:, MISC_DT + h:MISC_DT + h + 1], (n, width))
                            for h in range(SSD_H)], axis=1)


def _ssd_kernel(z_ref, misc_ref, cw_ref, cb_ref, dtb_ref, aneg_ref, dsk_ref, nw_ref, o_ref, s_ref, tail_ref, carry_ref,
                st_ref, y_ref):
    tb = SSD_TB
    cc = SSD_C
    t = pl.program_id(1)

    @pl.when(t == 0)
    def _():
        carry_ref[...] = jnp.zeros_like(carry_ref)
        st_ref[...] = jnp.zeros_like(st_ref)

    gate = jnp.concatenate([z_ref[0], z_ref[1]], axis=1)
    xbc_raw = jnp.concatenate([z_ref[2], z_ref[3], z_ref[4], z_ref[5]], axis=1)
    xbc_c = _causal_conv(xbc_raw, carry_ref[...], cw_ref, cb_ref)
    carry_ref[...] = xbc_raw[tb - SUBLANES:]
    tail_ref[0] = xbc_raw[tb - SUBLANES:]
    xbc, dt, dta = _ssd_pointwise(xbc_c, misc_ref[...], dtb_ref, aneg_ref)
    xs = xbc[:, :MIX]
    bm = xbc[:, MIX:MIX + TILE]
    cm = xbc[:, MIX + TILE:]
    pos = _iota((tb, TILE), 0) & (cc - 1)
    cum = _chunk_cumsum(dta, cc, pos)
    xdt = xs * _expand_heads(dt, SSD_P)
    tri = _iota((cc, cc), 0) >= _iota((cc, cc), 1)

    hg = SSD_H // SSD_G
    nch = tb // cc
    rws = [slice(c * cc, (c + 1) * cc) for c in range(nch)]
    cum_cs = [cum[rows, :LANES] for rows in rws]
    cum_ts = [x.T for x in cum_cs]
    units = [(c, gi) for c in range(nch) for gi in range(SSD_G)]
    cgs = [cm[rws[c], gi * SSD_N:(gi + 1) * SSD_N] for c, gi in units]
    bgs = [bm[rws[c], gi * SSD_N:(gi + 1) * SSD_N] for c, gi in units]
    gmats = [_mm(cg, bg, NT) for cg, bg in zip(cgs, bgs)]
    heads = [(u, gi * hg + hh) for u, (c, gi) in enumerate(units) for hh in range(hg)]
    cols = [cum_cs[units[u][0]][:, MISC_DT + h:MISC_DT + h + 1] for u, h in heads]
    lasts = [cum_cs[units[u][0]][cc - 1:cc, MISC_DT + h:MISC_DT + h + 1] for u, h in heads]
    lmats = [jnp.exp(jnp.where(tri, col - cum_ts[units[u][0]][MISC_DT + h:MISC_DT + h + 1, :], -jnp.inf))
             for (u, h), col in zip(heads, cols)]
    xhs = [xdt[rws[units[u][0]], h * SSD_P:(h + 1) * SSD_P] for u, h in heads]
    yds = [_mm(gmats[u] * lm, xh) for (u, h), lm, xh in zip(heads, lmats, xhs)]
    xds = [xh * jnp.exp(last - col) for xh, last, col in zip(xhs, lasts, cols)]
    pre = []
    for u in range(len(units)):
        hsl = slice(u * hg, (u + 1) * hg)
        pre.append((cgs[u], jnp.concatenate(yds[hsl], axis=1),
                    jnp.concatenate([jnp.broadcast_to(jnp.exp(col), (cc, SSD_P)) for col in cols[hsl]], axis=1),
                    jnp.concatenate([jnp.broadcast_to(jnp.exp(last), (SSD_P, SSD_N)) for last in lasts[hsl]], axis=0),
                    _mm(jnp.concatenate(xds[hsl], axis=1), bgs[u], TN)))
    for c in range(nch):
        rows = slice(c * cc, (c + 1) * cc)
        for gi in range(SSD_G):
            gs = slice(gi * hg * SSD_P, (gi + 1) * hg * SSD_P)
            cg, yd, ec, dec, ds = pre[c * SSD_G + gi]
            st = st_ref[gs, :]
            y_ref[rows, gs] = yd + _mm(cg, st, NT) * ec
            st_ref[gs, :] = st * dec + ds

    y = (y_ref[...] + dsk_ref[...] * xs) * _silu(gate)
    o_ref[...] = _rms(y, nw_ref[...]).astype(BF16)

    @pl.when(t == pl.num_programs(1) - 1)
    def _():
        for h in range(SSD_H):
            s_ref[0, h] = st_ref[h * SSD_P:(h + 1) * SSD_P, :]


def _ssd_prompt(proj, nb, nt_len, cw, cb, dtb, aneg, dsk, nw):
    tb = SSD_TB
    nt = nt_len // tb
    return pl.pallas_call(
        _ssd_kernel,
        grid=(nb, nt),
        in_specs=[pl.BlockSpec((WIDE_TILES, tb, TILE), lambda b, t: (T_SSD // WIDE_TILES, b * nt + t, 0)),
                  pl.BlockSpec((None, tb, TILE), lambda b, t: (T_MISC, b * nt + t, 0)),
                  _const_spec((CONV_W, SSD_CONV_DIM)), _const_spec((1, SSD_CONV_DIM)),
                  _const_spec((1, TILE)), _const_spec((1, TILE)), _const_spec((1, MIX)), _const_spec((1, MIX))],
        out_specs=[pl.BlockSpec((tb, MIX), lambda b, t: (b * nt + t, 0)),
                   pl.BlockSpec((1, SSD_H, SSD_P, SSD_N), lambda b, t: (b, 0, 0, 0)),
                   pl.BlockSpec((1, SUBLANES, SSD_CONV_DIM), lambda b, t: (b, 0, 0))],
        out_shape=[jax.ShapeDtypeStruct((nb * nt_len, MIX), BF16),
                   jax.ShapeDtypeStruct((nb, SSD_H, SSD_P, SSD_N), F32),
                   jax.ShapeDtypeStruct((nb, SUBLANES, SSD_CONV_DIM), F32)],
        scratch_shapes=[pltpu.VMEM((SUBLANES, SSD_CONV_DIM), F32), pltpu.VMEM((SSD_H * SSD_P, SSD_N), F32),
                        pltpu.VMEM((tb, MIX), F32)],
        compiler_params=_params("parallel", "arbitrary"),
        name="ssd_prompt",
    )(proj, proj, cw, cb, dtb, aneg, dsk, nw)


R_GLA_V, R_GLA_GATE, R_GLA_Q = 0, 512, 1024
R_RW_K, R_RW_R, R_RW_V, R_RW_G = 1280, 1792, 2304, 2816
R_SSD_B, R_SSD_C, R_SSD_E, R_SSD_GATE, R_SSD_X = 3328, 3584, 3840, 4864, 5376
ROW_W = 5888
C_GLA_A, C_GLA_K, C_SSD_X = 0, 256, 512
COL_W = 1024
T_RW_W, T_RW_B, T_RW_K, T_RW_KK, T_RW_R, T_RW_V = 0, 512, 1024, 1536, 2048, 2560
RWT_W = 3072
DEC_RB = 8


def _dec_prep_kernel(z_ref, lconv_ref, lh_ref, rprev_ref, sconv_ref,
                     wg2_ref, bg_ref,
                     lcw_ref, lcb_ref, wa_ref, ba_ref, wi_ref, bi_ref, lam_ref,
                     mu_ref, w0_ref, w2_ref, a0_ref, a2_ref, g2_ref, kk_ref, ka_ref, seg_ref,
                     scw_ref, scb_ref, dtb_ref, aneg_ref,
                     rows_ref, colt_ref, rwt_ref, olru_ref, lh_out_ref, lconv_out_ref, sconv_out_ref):
    k = z_ref[T_GLA + 1]
    g = _gla_gates(z_ref[T_MISC], wg2_ref, bg_ref)
    rows_ref[:, R_GLA_V:R_GLA_V + TILE] = z_ref[T_GLA + 2]
    rows_ref[:, R_GLA_V + TILE:R_GLA_V + MIX] = z_ref[T_GLA + 3]
    rows_ref[:, R_GLA_GATE:R_GLA_GATE + TILE] = z_ref[T_GLA + 4]
    rows_ref[:, R_GLA_GATE + TILE:R_GLA_GATE + MIX] = z_ref[T_GLA + 5]
    rows_ref[:, R_GLA_Q:R_GLA_Q + TILE] = z_ref[T_GLA + 0] * (GLA_DK ** -0.5)
    cols = [jnp.exp(g), k]

    xb = jnp.concatenate([z_ref[T_LRU + 0], z_ref[T_LRU + 1]], axis=1)
    gb = jnp.concatenate([z_ref[T_LRU + 2], z_ref[T_LRU + 3]], axis=1)
    xc = lcb_ref[...] + xb * lcw_ref[CONV_W - 1:CONV_W, :]
    for i in range(CONV_W - 1):
        xc = xc + lconv_ref[:, i * MIX:(i + 1) * MIX] * lcw_ref[i:i + 1, :]
    a, mult, gi = _lru_gates(xc, wa_ref, ba_ref, wi_ref, bi_ref, lam_ref)
    h = a * lh_ref[...] + mult * gi * xc
    lh_out_ref[...] = h
    olru_ref[...] = (h * _gelu_tanh(gb)).astype(BF16)
    lconv_out_ref[:, 0:2 * MIX] = lconv_ref[:, MIX:3 * MIX]
    lconv_out_ref[:, 2 * MIX:3 * MIX] = xb

    z = jnp.concatenate([z_ref[T_RWKV + j] for j in range(GROUP_TILES)], axis=1)
    r, lw, kmod, v, kk, a7, g7 = _rwkv_pointwise(z, rprev_ref[...], mu_ref, w0_ref, w2_ref, a0_ref, a2_ref, g2_ref,
                                                 kk_ref, ka_ref, seg_ref)
    rows_ref[:, R_RW_K:R_RW_K + MIX] = kmod
    rows_ref[:, R_RW_R:R_RW_R + MIX] = r
    rows_ref[:, R_RW_V:R_RW_V + MIX] = v
    rows_ref[:, R_RW_G:R_RW_G + MIX] = g7
    off = 0
    for vec in (jnp.exp(lw), kk * a7, kmod, kk, r, v):
        for j in range(MIX // LANES):
            rwt_ref[off:off + LANES, :] = vec[:, j * LANES:(j + 1) * LANES].T
            off += LANES

    xbc_raw = jnp.concatenate([z_ref[T_SSD + 2], z_ref[T_SSD + 3], z_ref[T_SSD + 4], z_ref[T_SSD + 5]], axis=1)
    xbc_c = scb_ref[...] + xbc_raw * scw_ref[CONV_W - 1:CONV_W, :]
    for i in range(CONV_W - 1):
        xbc_c = xbc_c + sconv_ref[:, i * SSD_CONV_DIM:(i + 1) * SSD_CONV_DIM] * scw_ref[i:i + 1, :]
    xbc, dt, dta = _ssd_pointwise(xbc_c, z_ref[T_MISC], dtb_ref, aneg_ref)
    xs = xbc[:, :MIX]
    rows_ref[:, R_SSD_B:R_SSD_B + TILE] = xbc[:, MIX:MIX + TILE]
    rows_ref[:, R_SSD_C:R_SSD_C + TILE] = xbc[:, MIX + TILE:]
    rows_ref[:, R_SSD_E:R_SSD_E + SSD_H * LANES] = _expand_heads(jnp.exp(dta), LANES)
    rows_ref[:, R_SSD_GATE:R_SSD_GATE + TILE] = z_ref[T_SSD + 0]
    rows_ref[:, R_SSD_GATE + TILE:R_SSD_GATE + MIX] = z_ref[T_SSD + 1]
    rows_ref[:, R_SSD_X:R_SSD_X + MIX] = xs
    cols.append(xs * _expand_heads(dt, SSD_P))
    sconv_out_ref[:, 0:2 * SSD_CONV_DIM] = sconv_ref[:, SSD_CONV_DIM:3 * SSD_CONV_DIM]
    sconv_out_ref[:, 2 * SSD_CONV_DIM:3 * SSD_CONV_DIM] = xbc_raw

    off = 0
    for cvec in cols:
        for j in range(cvec.shape[1] // LANES):
            colt_ref[off:off + LANES, :] = cvec[:, j * LANES:(j + 1) * LANES].T
            off += LANES


def _dec_prep(zdec, lconv, lh, rprev, sconv, pp):
    n = zdec.shape[1]
    args = [zdec, lconv, lh, rprev, sconv,
            pp["gla_wg2"], pp["gla_bg"],
            pp["lru_cw"], pp["lru_cb"], pp["lru_wa"], pp["lru_ba"], pp["lru_wi"], pp["lru_bi"], pp["lru_lam"],
            pp["rwkv"]["mu"], pp["rwkv"]["w0"], pp["rwkv"]["w2"], pp["rwkv"]["a0"], pp["rwkv"]["a2"],
            pp["rwkv"]["g2"], pp["rwkv"]["kk"], pp["rwkv"]["ka"], pp["rwkv"]["seg"],
            pp["ssd_cw"], pp["ssd_cb"], pp["ssd_dtb"], pp["ssd_aneg"]]
    return pl.pallas_call(
        _dec_prep_kernel,
        out_shape=[jax.ShapeDtypeStruct((n, ROW_W), F32), jax.ShapeDtypeStruct((COL_W, n), F32),
                   jax.ShapeDtypeStruct((RWT_W, n), F32),
                   jax.ShapeDtypeStruct((n, MIX), BF16), jax.ShapeDtypeStruct((n, MIX), F32),
                   jax.ShapeDtypeStruct((n, 3 * MIX), F32), jax.ShapeDtypeStruct((n, 3 * SSD_CONV_DIM), F32)],
        compiler_params=pltpu.CompilerParams(vmem_limit_bytes=VMEM_LIMIT),
        name="decode_prep",
    )(*args)


def _dec_rwkv_kernel(w_ref, b_ref, k_ref, kk_ref, r_ref, v_ref, s_ref, *rest, layer, fill_others):
    s_out, y_ref = rest[-2:]
    if fill_others:
        for j in range(DEPTH):
            if j != layer:
                s_out[j] = jnp.zeros(s_out.shape[1:], F32)
        s_out = s_out.at[layer]
    w = w_ref[...]
    b = b_ref[...]
    k = k_ref[...]
    kk = kk_ref[...]
    r = r_ref[...]

    def body(vi, carry):
        s = s_ref[vi]
        sa = jnp.sum(s * kk, axis=0, keepdims=True)
        s = s * w - sa * b + v_ref[pl.ds(vi, 1), :] * k
        s_out[vi] = s
        y_ref[pl.ds(vi, 1), :] = jnp.sum(s * r, axis=0, keepdims=True)
        return carry

    lax.fori_loop(0, RWKV_D, body, 0, unroll=8)


def _dec_rwkv(rwt, sr_t, l, prev):
    n = rwt.shape[1]
    hb = MIX // RWKV_D

    def vec(off):
        return pl.BlockSpec((RWKV_D, n), lambda h: (off // RWKV_D + h, 0))

    sspec = pl.BlockSpec((None, None, RWKV_D, RWKV_D, n), lambda h: (l, h, 0, 0, 0))
    extra = [] if prev is None else [prev]
    first = prev is None
    ospec = pl.BlockSpec((DEPTH, None, RWKV_D, RWKV_D, n), lambda h: (0, h, 0, 0, 0)) if first else sspec
    return pl.pallas_call(
        functools.partial(_dec_rwkv_kernel, layer=l, fill_others=first),
        grid=(hb,),
        in_specs=[vec(T_RW_W), vec(T_RW_B), vec(T_RW_K), vec(T_RW_KK), vec(T_RW_R), vec(T_RW_V), sspec]
        + [pl.BlockSpec(memory_space=pl.ANY)] * len(extra),
        out_specs=[ospec, pl.BlockSpec((RWKV_D, n), lambda h: (h, 0))],
        out_shape=[jax.ShapeDtypeStruct(sr_t.shape, F32), jax.ShapeDtypeStruct((MIX, n), F32)],
        input_output_aliases={7: 0} if extra else {},
        compiler_params=_params("parallel"),
        name="decode_rwkv",
    )(rwt, rwt, rwt, rwt, rwt, rwt, sr_t, *extra)


def _dec_state_kernel(colt_ref, rows_ref, sg_ref, ss_ref, *rest, layer, fill_others):
    sg_out, ss_out, yg_ref, ys_ref = rest[-4:]
    if fill_others:
        for j in range(DEPTH):
            if j != layer:
                sg_out[j] = jnp.zeros(sg_out.shape[1:], F32)
                ss_out[j] = jnp.zeros(ss_out.shape[1:], F32)
        sg_out = sg_out.at[layer]
        ss_out = ss_out.at[layer]
    pid = pl.program_id(0)
    n = colt_ref.shape[1]
    lane = _iota((1, n), 1)
    rid = _iota((DEC_RB, LANES), 0)
    rows = rows_ref[...]
    hg = SSD_H // SSD_G
    yg = [jnp.zeros((DEC_RB, GLA_DV), F32) for _ in range(GLA_H)]
    ys = [jnp.zeros((DEC_RB, SSD_P), F32) for _ in range(SSD_H)]
    for i in range(DEC_RB):
        sel = lane == pid * DEC_RB + i
        col = jnp.sum(jnp.where(sel, colt_ref[...], 0.0), axis=1, keepdims=True)
        rv = rows[i:i + 1, :]
        mine = rid == i
        for h in range(GLA_H):
            al = col[C_GLA_A + h * GLA_DK:C_GLA_A + (h + 1) * GLA_DK]
            kc = col[C_GLA_K + h * GLA_DK:C_GLA_K + (h + 1) * GLA_DK]
            vr = rv[:, R_GLA_V + h * GLA_DV:R_GLA_V + (h + 1) * GLA_DV]
            s = al * sg_ref[i, h] + kc * vr
            sg_out[i, h] = s
            q8 = rows[:, R_GLA_Q + h * GLA_DK:R_GLA_Q + (h + 1) * GLA_DK]
            yg[h] = jnp.where(mine, _mm(q8, s, NN, pa=2, pb=2), yg[h])
        for h in range(SSD_H):
            gi = h // hg
            e = rv[:, R_SSD_E + h * LANES:R_SSD_E + (h + 1) * LANES]
            bn = rv[:, R_SSD_B + gi * SSD_N:R_SSD_B + (gi + 1) * SSD_N]
            xc = col[C_SSD_X + h * SSD_P:C_SSD_X + (h + 1) * SSD_P]
            s = ss_ref[i, h] * e + xc * bn
            ss_out[i, h] = s
            c8 = rows[:, R_SSD_C + gi * SSD_N:R_SSD_C + (gi + 1) * SSD_N]
            ys[h] = jnp.where(mine[:, :SSD_P], _mm(c8, s, NT, pa=2, pb=2), ys[h])
    yg_ref[...] = jnp.concatenate(yg, axis=1)
    ys_ref[...] = jnp.concatenate(ys, axis=1)


def _dec_state(colt, rows, sg, ss, l, prev):
    n = rows.shape[0]
    blk = lambda i: (l, i, 0, 0, 0)
    specs = [pl.BlockSpec((None, DEC_RB, GLA_H, GLA_DK, GLA_DV), blk),
             pl.BlockSpec((None, DEC_RB, SSD_H, SSD_P, SSD_N), blk)]
    extra = [] if prev is None else list(prev)
    first = prev is None
    ospecs = specs
    if first:
        all_layers = lambda i: (0, i, 0, 0, 0)
        ospecs = [pl.BlockSpec((DEPTH, DEC_RB, GLA_H, GLA_DK, GLA_DV), all_layers),
                  pl.BlockSpec((DEPTH, DEC_RB, SSD_H, SSD_P, SSD_N), all_layers)]
    rowspec = pl.BlockSpec((DEC_RB, MIX), lambda i: (i, 0))
    return pl.pallas_call(
        functools.partial(_dec_state_kernel, layer=l, fill_others=first),
        grid=(n // DEC_RB,),
        in_specs=[_const_spec((COL_W, n)), pl.BlockSpec((DEC_RB, ROW_W), lambda i: (i, 0))] + specs
        + [pl.BlockSpec(memory_space=pl.ANY)] * len(extra),
        out_specs=ospecs + [rowspec, rowspec],
        out_shape=[jax.ShapeDtypeStruct(sg.shape, F32), jax.ShapeDtypeStruct(ss.shape, F32),
                   jax.ShapeDtypeStruct((n, MIX), F32), jax.ShapeDtypeStruct((n, MIX), F32)],
        input_output_aliases={4 + j: j for j in range(len(extra))},
        compiler_params=_params("parallel"),
        name="decode_state",
    )(colt, rows, sg, ss, *extra)


def _dec_finish_kernel(rows_ref, yg_ref, yrt_ref, ys_ref, gnw_ref, rk_ref, lnw_ref, lnb_ref, seg_ref,
                       dsk_ref, snw_ref, og_ref, or_ref, os_ref):
    outs = []
    for h in range(GLA_H):
        outs.append(_rms(yg_ref[:, h * GLA_DV:(h + 1) * GLA_DV], gnw_ref[...]))
    og_ref[...] = (jnp.concatenate(outs, axis=1) * _silu(rows_ref[:, R_GLA_GATE:R_GLA_GATE + MIX])).astype(BF16)
    yr = jnp.concatenate([yrt_ref[j * LANES:(j + 1) * LANES, :].T for j in range(MIX // LANES)], axis=1)
    or_ref[...] = _rwkv_finish(yr, rows_ref[:, R_RW_R:R_RW_R + MIX], rows_ref[:, R_RW_K:R_RW_K + MIX],
                               rows_ref[:, R_RW_V:R_RW_V + MIX], rows_ref[:, R_RW_G:R_RW_G + MIX],
                               rk_ref, lnw_ref, lnb_ref, seg_ref).astype(BF16)
    y = ((ys_ref[...] + dsk_ref[...] * rows_ref[:, R_SSD_X:R_SSD_X + MIX])
         * _silu(rows_ref[:, R_SSD_GATE:R_SSD_GATE + MIX]))
    os_ref[...] = _rms(y, snw_ref[...]).astype(BF16)


def _dec_finish(rows, yg, yrt, ys, pp):
    n = rows.shape[0]
    o = jax.ShapeDtypeStruct((n, MIX), BF16)
    return pl.pallas_call(
        _dec_finish_kernel,
        out_shape=[o, o, o],
        compiler_params=pltpu.CompilerParams(vmem_limit_bytes=VMEM_LIMIT),
        name="decode_finish",
    )(rows, yg, yrt, ys, pp["gla_nw"], pp["rwkv"]["rk"], pp["rwkv"]["lnw"], pp["rwkv"]["lnb"], pp["rwkv"]["seg"],
      pp["ssd_dsk"], pp["ssd_nw"])


def _block_diag(w8):
    nb = TILE // 64
    tiled = jnp.tile(w8.reshape(2, TILE, 64), (1, 1, nb))
    idx = jnp.arange(TILE) // 64
    return jnp.where(idx[:, None] == idx[None, :], tiled, 0.0)


def _prep_weights(w_in, w_out, w_up, w_down):
    o_lru = GLA_COLS
    o_rwkv = GLA_COLS + LRU_COLS
    o_ssd = o_rwkv + RWKV_COLS
    wide = WIDE_TILES * TILE
    wt = jnp.transpose(w_in, (0, 2, 1))
    zpad = lambda n: jnp.zeros((DEPTH, n, D_MODEL), w_in.dtype)
    w_perm = jnp.concatenate([
        wt[:, o_rwkv:o_rwkv + RWKV_COLS],
        wt[:, o_lru:o_lru + LRU_COLS],
        wt[:, wide:GLA_COLS], wt[:, o_ssd + wide:o_ssd + SSD_COLS], zpad(TILE - GLA_RANK - SSD_H),
        wt[:, 0:wide],
        wt[:, o_ssd:o_ssd + wide]], axis=1).astype(BF16)
    return {"w_in": w_perm,
            "w_out": w_out,
            "w_up": w_up, "w_down": w_down}


def _prep_layer(l, norm_mix_pre, norm_mix_post, norm_mlp_pre, norm_mlp_post,
                gla_w_gate2, gla_b_gate, gla_norm, lru_conv_w, lru_conv_b, lru_w_a, lru_b_a, lru_w_i, lru_b_i,
                lru_lambda, rwkv_mu, rwkv_w0, rwkv_w2, rwkv_a0, rwkv_a2, rwkv_g2, rwkv_k_k, rwkv_k_a, rwkv_r_k,
                rwkv_ln_w, rwkv_ln_b, ssd_conv_w, ssd_conv_b, ssd_dt_bias, ssd_a_log, ssd_d, ssd_norm):
    row = lambda a: a.reshape(1, -1).astype(F32)
    ii = jnp.arange(TILE)[:, None]
    jj = jnp.arange(MIX)[None, :]
    seg_i = jnp.arange(LANES)
    pp = {
        "n_mix_pre": row(norm_mix_pre[l]), "n_mix_post": row(norm_mix_post[l]),
        "n_mlp_pre": row(norm_mlp_pre[l]), "n_mlp_post": row(norm_mlp_post[l]),
        "gla_wg2": jnp.pad(gla_w_gate2[l], ((0, TILE - GLA_RANK), (0, 0))).astype(BF16),
        "gla_bg": row(gla_b_gate[l]),
        "gla_nw": row(gla_norm[l]),
        "gla_ones": (ii // GLA_DK == jj // GLA_DV).astype(BF16),
        "lru_cw": lru_conv_w[l].astype(F32), "lru_cb": row(lru_conv_b[l]),
        "lru_wa": _block_diag(lru_w_a[l]).astype(BF16),
        "lru_wi": _block_diag(lru_w_i[l]).astype(BF16),
        "lru_ba": row(lru_b_a[l]), "lru_bi": row(lru_b_i[l]), "lru_lam": row(lru_lambda[l]),
        "rwkv": {
            "mu": row(rwkv_mu[l]), "w0": row(rwkv_w0[l]), "w2": rwkv_w2[l].astype(BF16),
            "a0": row(rwkv_a0[l]), "a2": rwkv_a2[l].astype(BF16), "g2": rwkv_g2[l].astype(BF16),
            "kk": row(rwkv_k_k[l]), "ka": row(rwkv_k_a[l]), "rk": row(rwkv_r_k[l]),
            "lnw": row(rwkv_ln_w[l]), "lnb": row(rwkv_ln_b[l]),
            "seg": (seg_i[:, None] // RWKV_D == seg_i[None, :] // RWKV_D).astype(BF16),
        },
        "ssd_cw": ssd_conv_w[l].astype(F32), "ssd_cb": row(ssd_conv_b[l]),
        "ssd_dtb": jnp.pad(ssd_dt_bias[l].astype(F32), (MISC_DT, TILE - MISC_DT - SSD_H)).reshape(1, TILE),
        "ssd_aneg": jnp.pad(-jnp.exp(ssd_a_log[l].astype(F32)), (MISC_DT, TILE - MISC_DT - SSD_H)).reshape(1, TILE),
        "ssd_dsk": jnp.repeat(ssd_d[l].astype(F32), SSD_P).reshape(1, MIX),
        "ssd_nw": row(ssd_norm[l]),
    }
    return pp


def _layer_prompt(x2d, nb, nt_len, wts, l, pp, tm_proj=1024, tm_out=512, tm_mlp=512, tf=1024):
    proj, wo_bf = _proj(x2d, pp["n_mix_pre"], wts["w_in"], l, tm_proj, (wts["w_out"],))
    o_gla, s_gla = _gla_prompt(proj, nb, nt_len, pp["gla_wg2"], pp["gla_bg"], pp["gla_nw"], pp["gla_ones"])
    o_lru, h_lru, lru_tail = _lru_prompt(proj, nb, nt_len, pp["lru_cw"], pp["lru_cb"], pp["lru_wa"], pp["lru_ba"],
                                         pp["lru_wi"], pp["lru_bi"], pp["lru_lam"])
    o_rwkv, s_rwkv, rwkv_tail, wu_bf = _rwkv_prompt(proj, nb, nt_len, pp["rwkv"], wts["w_up"], l)
    o_ssd, s_ssd, ssd_tail = _ssd_prompt(proj, nb, nt_len, pp["ssd_cw"], pp["ssd_cb"], pp["ssd_dtb"],
                                         pp["ssd_aneg"], pp["ssd_dsk"], pp["ssd_nw"])
    x1, h2, wd_bf = _outproj((o_gla, o_lru, o_rwkv, o_ssd), wo_bf, x2d, pp["n_mix_post"], pp["n_mlp_pre"], tm_out,
                             wts["w_down"], l)
    x2 = _mlp(h2, wu_bf, wd_bf, x1, pp["n_mlp_post"], tm_mlp, tf)
    keep = SUBLANES - (CONV_W - 1)
    states = (s_gla, h_lru[:, 0, :], lru_tail[:, keep:, :], rwkv_tail[:, SUBLANES - 1, :], s_rwkv,
              ssd_tail[:, keep:, :], s_ssd)
    return x2, states, (wu_bf, wd_bf, wo_bf)


def _layer_decode(x2d, big, small, wts, mlp_w, l, pp, prev, tf=1024):
    sg, sr_t, ss = big
    lh, lconv, rprev, sconv = small
    n = x2d.shape[0]
    proj = _proj(x2d, pp["n_mix_pre"], wts["w_in"], l, n)
    rows, colt, rwt, o_lru, lh_new, lconv_new, sconv_new = _dec_prep(
        proj, lconv.reshape(n, -1), lh, rprev, sconv.reshape(n, -1), pp)
    sr_new, yrt = _dec_rwkv(rwt, sr_t, l, None if prev is None else prev[1])
    sg_new, ss_new, yg, ys = _dec_state(colt, rows, sg, ss, l, None if prev is None else (prev[0], prev[2]))
    o_gla, o_rwkv, o_ssd = _dec_finish(rows, yg, yrt, ys, pp)
    x1, h2 = _outproj((o_gla, o_lru, o_rwkv, o_ssd), mlp_w[2], x2d, pp["n_mix_post"], pp["n_mlp_pre"], n)
    x2 = _mlp(h2, mlp_w[0], mlp_w[1], x1, pp["n_mlp_post"], n, tf)
    rshift_new = jnp.transpose(proj[T_RWKV:T_RWKV + GROUP_TILES], (1, 0, 2)).reshape(n, RWKV_COLS)
    small_new = (lh_new, lconv_new.reshape(n, CONV_W - 1, MIX), rshift_new,
                 sconv_new.reshape(n, CONV_W - 1, SSD_CONV_DIM))
    return x2, (sg_new, sr_new, ss_new), small_new


def kernel(x_prompt, x_sample, state_gla, state_lru, cache_lru_conv, cache_rwkv_shift, state_rwkv, cache_ssd_conv, state_ssd, norm_mix_pre, norm_mix_post, norm_mlp_pre, norm_mlp_post, w_in, w_out, w_up, w_down, gla_w_gate2, gla_b_gate, gla_norm, lru_conv_w, lru_conv_b, lru_w_a, lru_b_a, lru_w_i, lru_b_i, lru_lambda, rwkv_mu, rwkv_w0, rwkv_w2, rwkv_a0, rwkv_a2, rwkv_g2, rwkv_k_k, rwkv_k_a, rwkv_r_k, rwkv_ln_w, rwkv_ln_b, ssd_conv_w, ssd_conv_b, ssd_dt_bias, ssd_a_log, ssd_d, ssd_norm):
    nb, nt_len, _ = x_prompt.shape
    nd = x_sample.shape[0]
    yp = x_prompt.reshape(nb * nt_len, D_MODEL)
    ys = x_sample.reshape(nd, D_MODEL)
    wts = _prep_weights(w_in, w_out, w_up, w_down)
    big = (state_gla, jnp.transpose(state_rwkv, (0, 2, 3, 4, 1)), state_ssd)
    new_p, small_s, big_s = [], [], None
    for l in range(DEPTH):
        pp = _prep_layer(l, norm_mix_pre, norm_mix_post, norm_mlp_pre, norm_mlp_post,
                         gla_w_gate2, gla_b_gate, gla_norm, lru_conv_w, lru_conv_b, lru_w_a, lru_b_a, lru_w_i,
                         lru_b_i, lru_lambda, rwkv_mu, rwkv_w0, rwkv_w2, rwkv_a0, rwkv_a2, rwkv_g2, rwkv_k_k,
                         rwkv_k_a, rwkv_r_k, rwkv_ln_w, rwkv_ln_b, ssd_conv_w, ssd_conv_b, ssd_dt_bias, ssd_a_log,
                         ssd_d, ssd_norm)
        yp, st_p, mlp_w = _layer_prompt(yp, nb, nt_len, wts, l, pp)
        small = (state_lru[l], cache_lru_conv[l], cache_rwkv_shift[l], cache_ssd_conv[l])
        ys, big_s, sm = _layer_decode(ys, big, small, wts, mlp_w, l, pp, big_s)
        new_p.append(st_p)
        small_s.append(sm)

    def stack(sts, i):
        return jnp.stack([s[i] for s in sts], axis=0)

    gla_s, rwkv_s, ssd_s = big_s
    rwkv_s = jnp.transpose(rwkv_s, (0, 4, 1, 2, 3))
    return (yp.reshape(nb, nt_len, D_MODEL), ys.reshape(nd, 1, D_MODEL),
            stack(new_p, 0), gla_s, stack(new_p, 1), stack(small_s, 0),
            stack(new_p, 2), stack(small_s, 1), stack(new_p, 3), stack(small_s, 2),
            stack(new_p, 4), rwkv_s, stack(new_p, 5), stack(small_s, 3),
            stack(new_p, 6), ssd_s)
```
